```python
import math
import jax, jax.numpy as jnp
from jax import lax
import numpy as np

D_MODEL = 1024
BATCH = 8
SEQ = 4096
DEPTH = 4

N_META = 16
CHUNK = 64
META_PAD = CHUNK - N_META
CONV_K = 4
D_FF = 2816
LRU_WIDTH = D_MODEL // 4
LRU_HEADS = 4
LRU_BLOCK = LRU_WIDTH // LRU_HEADS
LRU_C = 8.0
SSD_HEADS = 8
SSD_HEADDIM = 64
SSD_INNER = SSD_HEADS * SSD_HEADDIM
SSD_GROUPS = 2
SSD_HPG = SSD_HEADS // SSD_GROUPS
SSD_STATE = 64
SSD_CONV_DIM = SSD_INNER + 2 * SSD_GROUPS * SSD_STATE
DN_HEADS = 4
DN_DK = 64
DN_DV = 64
DN_CONV_DIM = DN_HEADS * (2 * DN_DK + DN_DV)
MIX_WIDTH = LRU_WIDTH + SSD_INNER + DN_HEADS * DN_DV
IN_SIZES = (LRU_WIDTH, LRU_WIDTH,
            SSD_INNER, SSD_CONV_DIM, SSD_HEADS,
            DN_CONV_DIM, DN_HEADS * DN_DV, DN_HEADS, DN_HEADS)
D_IN = sum(IN_SIZES)
IN_OFFSETS = [int(s) for s in np.cumsum(IN_SIZES)[:-1]]
DEEPNORM_ALPHA = (2 * DEPTH) ** 0.25
DEEPNORM_BETA = (8 * DEPTH) ** -0.25
FFN_RES = 0.5
LN_EPS = 1e-5
RMS_EPS = 1e-6

kernel_name = 'hymba_deepnorm_lru_ssd_deltanet_macaron'

F32 = jnp.float32


def _layer_norm(x, g, b):
    xf = x.astype(F32)
    mu = jnp.mean(xf, axis=-1, keepdims=True)
    var = jnp.mean(jnp.square(xf - mu), axis=-1, keepdims=True)
    return ((xf - mu) * lax.rsqrt(var + LN_EPS) * g.astype(F32) + b.astype(F32)).astype(x.dtype)


def _rms_norm(x, w):
    xf = x.astype(F32)
    return xf * lax.rsqrt(jnp.mean(xf * xf, axis=-1, keepdims=True) + RMS_EPS) * w.astype(F32)


def _l2norm(x):
    return x * lax.rsqrt(jnp.sum(x * x, axis=-1, keepdims=True) + RMS_EPS)


def _causal_conv(x, w, b=None):
    c = x.shape[-1]
    y = lax.conv_general_dilated(x, w[:, None, :].astype(x.dtype), window_strides=(1,),
                                 padding=[(CONV_K - 1, 0)], dimension_numbers=('NWC', 'WIO', 'NWC'),
                                 feature_group_count=c)
    if b is not None:
        y = y + b.astype(x.dtype)
    return y


def _pad_front(x):
    pad = [(0, 0)] * x.ndim
    pad[1] = (META_PAD, 0)
    return jnp.pad(x, pad)


def _swiglu(x, wg, wu, wd):
    return (jax.nn.silu(x @ wg) * (x @ wu)) @ wd


def _segsum(x):
    cs = jnp.cumsum(x, axis=-1)
    n = x.shape[-1]
    mask = jnp.tril(jnp.ones((n, n), dtype=bool))
    return jnp.where(mask, cs[..., :, None] - cs[..., None, :], -jnp.inf)


def _linear_combine(c1, c2):
    a1, b1 = c1
    a2, b2 = c2
    return a1 * a2, a2 * b1 + b2


def _rg_lru_group(u_raw, y_raw, conv_w, conv_b, w_a, b_a, w_x, b_x, lam):
    u = _causal_conv(u_raw, conv_w, conv_b).astype(F32)
    bsz, t, _ = u.shape
    uh = u.reshape(bsz, t, LRU_HEADS, LRU_BLOCK)
    r = jax.nn.sigmoid(jnp.einsum('bthi,hij->bthj', uh, w_a.astype(F32)).reshape(bsz, t, LRU_WIDTH) + b_a.astype(F32))
    i = jax.nn.sigmoid(jnp.einsum('bthi,hij->bthj', uh, w_x.astype(F32)).reshape(bsz, t, LRU_WIDTH) + b_x.astype(F32))
    log_a = -LRU_C * r * jax.nn.softplus(-lam.astype(F32))
    a = jnp.exp(log_a)
    b = jnp.sqrt(-jnp.expm1(2.0 * log_a)) * (i * u)
    _, h = lax.associative_scan(_linear_combine, (a, b), axis=1)
    return (h * jax.nn.gelu(y_raw.astype(F32))).astype(u_raw.dtype)


def _ssd_chunked(x, a, b, c):
    bsz, L, ng, ne, p = x.shape
    nc = L // CHUNK
    x = x.reshape(bsz, nc, CHUNK, ng, ne, p)
    b = b.reshape(bsz, nc, CHUNK, ng, -1)
    c = c.reshape(bsz, nc, CHUNK, ng, -1)
    a = jnp.transpose(a.reshape(bsz, nc, CHUNK, ng, ne), (0, 3, 4, 1, 2))
    a_cs = jnp.cumsum(a, axis=-1)
    l_mat = jnp.exp(_segsum(a))
    y_diag = jnp.einsum('bclgn,bcsgn,bgecls,bcsgep->bclgep', c, b, l_mat, x)
    decay_states = jnp.exp(a_cs[..., -1:] - a_cs)
    states = jnp.einsum('bclgn,bgecl,bclgep->bcgepn', b, decay_states, x)
    states = jnp.concatenate([jnp.zeros_like(states[:, :1]), states], axis=1)
    chunk_tot = jnp.pad(a_cs[..., -1], ((0, 0), (0, 0), (0, 0), (1, 0)))
    decay_chunk = jnp.exp(_segsum(chunk_tot))
    states = jnp.einsum('bgezc,bcgepn->bzgepn', decay_chunk, states)[:, :-1]
    y_off = jnp.einsum('bclgn,bcgepn,bgecl->bclgep', c, states, jnp.exp(a_cs))
    return (y_diag + y_off).reshape(bsz, L, ng, ne, p)


def _ssd_group(z, xbc_raw, dt_raw, conv_w, conv_b, dt_bias, a_log, d_skip, norm_w):
    xbc = jax.nn.silu(_causal_conv(xbc_raw, conv_w, conv_b)).astype(F32)
    bsz, t, _ = xbc.shape
    xs = xbc[..., :SSD_INNER].reshape(bsz, t, SSD_GROUPS, SSD_HPG, SSD_HEADDIM)
    bm = xbc[..., SSD_INNER:SSD_INNER + SSD_GROUPS * SSD_STATE].reshape(bsz, t, SSD_GROUPS, SSD_STATE)
    cm = xbc[..., SSD_INNER + SSD_GROUPS * SSD_STATE:].reshape(bsz, t, SSD_GROUPS, SSD_STATE)
    dt = jax.nn.softplus(dt_raw.astype(F32) + dt_bias.astype(F32)).reshape(bsz, t, SSD_GROUPS, SSD_HPG)
    a = -jnp.exp(a_log.astype(F32)).reshape(SSD_GROUPS, SSD_HPG)
    y = _ssd_chunked(_pad_front(xs * dt[..., None]), _pad_front(dt * a), _pad_front(bm), _pad_front(cm))[:, META_PAD:]
    y = y + d_skip.astype(F32).reshape(SSD_GROUPS, SSD_HPG)[:, :, None] * xs
    gw = SSD_HPG * SSD_HEADDIM
    y = y.reshape(bsz, t, SSD_GROUPS, gw) * jax.nn.silu(z.astype(F32)).reshape(bsz, t, SSD_GROUPS, gw)
    y = _rms_norm(y, norm_w.reshape(SSD_GROUPS, gw))
    return y.reshape(bsz, t, SSD_INNER).astype(z.dtype)


def _gated_delta_chunked(q, k, v, beta, g):
    bsz, nh, L, dk = q.shape
    dv = v.shape[-1]
    nc = L // CHUNK
    q = q.reshape(bsz, nh, nc, CHUNK, dk)
    k = k.reshape(bsz, nh, nc, CHUNK, dk)
    v = v.reshape(bsz, nh, nc, CHUNK, dv)
    beta = beta.reshape(bsz, nh, nc, CHUNK)
    g_cs = jnp.cumsum(g.reshape(bsz, nh, nc, CHUNK), axis=-1)
    incl = jnp.tril(jnp.ones((CHUNK, CHUNK), dtype=bool))
    strict = jnp.tril(jnp.ones((CHUNK, CHUNK), dtype=bool), k=-1)
    decay = jnp.exp(jnp.where(incl, g_cs[..., :, None] - g_cs[..., None, :], -jnp.inf))
    k_beta = k * beta[..., None]
    v_beta = v * beta[..., None]
    m = jnp.where(strict, jnp.einsum('bhnid,bhnjd->bhnij', k_beta, k) * decay, 0.0)
    eye = jnp.eye(CHUNK, dtype=m.dtype)
    rhs = jnp.concatenate([v_beta, k_beta * jnp.exp(g_cs)[..., None]], axis=-1)
    sol = lax.linalg.triangular_solve(eye + m, rhs, left_side=True, lower=True, unit_diagonal=True)
    u, w = sol[..., :dv], sol[..., dv:]
    attn = jnp.where(incl, jnp.einsum('bhnid,bhnjd->bhnij', q, k) * decay, 0.0)
    q_dec = q * jnp.exp(g_cs)[..., None]
    k_dec = k * jnp.exp(g_cs[..., -1:] - g_cs)[..., None]
    chunk_decay = jnp.exp(g_cs[..., -1])

    def step(state, inp):
        qd, kd, uu, ww, aa, cd = inp
        v_new = uu - jnp.einsum('bhcd,bhdv->bhcv', ww, state)
        out = jnp.einsum('bhcd,bhdv->bhcv', qd, state) + jnp.einsum('bhij,bhjv->bhiv', aa, v_new)
        state = state * cd[..., None, None] + jnp.einsum('bhcd,bhcv->bhdv', kd, v_new)
        return state, out

    xs = tuple(jnp.moveaxis(z, 2, 0) for z in (q_dec, k_dec, u, w, attn, chunk_decay))
    s0 = jnp.zeros((bsz, nh, dk, dv), q.dtype)
    _, o = lax.scan(step, s0, xs)
    return jnp.moveaxis(o, 0, 2).reshape(bsz, nh, L, dv)


def _deltanet_group(qkv_raw, gate_raw, beta_raw, alpha_raw, conv_w, a_log, dt_bias, norm_w):
    qkv = jax.nn.silu(_causal_conv(qkv_raw, conv_w)).astype(F32)
    bsz, t, _ = qkv.shape
    nq = DN_HEADS * DN_DK
    q = _l2norm(qkv[..., :nq].reshape(bsz, t, DN_HEADS, DN_DK)) * (DN_DK ** -0.5)
    k = _l2norm(qkv[..., nq:2 * nq].reshape(bsz, t, DN_HEADS, DN_DK))
    v = qkv[..., 2 * nq:].reshape(bsz, t, DN_HEADS, DN_DV)
    beta = jax.nn.sigmoid(beta_raw.astype(F32))
    g = -jnp.exp(a_log.astype(F32)) * jax.nn.softplus(alpha_raw.astype(F32) + dt_bias.astype(F32))
    to_h = lambda z: jnp.swapaxes(_pad_front(z), 1, 2)
    o = _gated_delta_chunked(to_h(q), to_h(k), to_h(v), to_h(beta), to_h(g))
    o = jnp.swapaxes(o, 1, 2)[:, META_PAD:]
    o = _rms_norm(o, norm_w) * jax.nn.silu(gate_raw.astype(F32).reshape(bsz, t, DN_HEADS, DN_DV))
    return o.reshape(bsz, t, DN_HEADS * DN_DV).astype(qkv_raw.dtype)


def _fwd_setup_inputs(seed: int = 0) -> dict:
    key = jax.random.key(seed)
    ks = jax.random.split(key, 26)
    nrm = lambda k, shape, s: jax.random.normal(k, shape, F32) * s
    unif = lambda k, shape, lo, hi: jax.random.uniform(k, shape, F32, lo, hi)

    def dt_bias_init(k, n):
        dt = jnp.exp(unif(k, (DEPTH, n), math.log(1e-3), math.log(1e-1)))
        return dt + jnp.log(-jnp.expm1(-dt))

    a_c = unif(ks[14], (DEPTH, LRU_WIDTH), 0.9, 0.999)
    s = a_c ** (1.0 / LRU_C)
    return {
        'x': nrm(ks[0], (BATCH, SEQ, D_MODEL), 1.0),
        'meta': nrm(ks[1], (N_META, D_MODEL), 1.0),
        'ln_g': 1.0 + nrm(ks[2], (DEPTH, 3, D_MODEL), 0.02),
        'ln_b': nrm(ks[3], (DEPTH, 3, D_MODEL), 0.02),
        'ffn_w_gate': nrm(ks[4], (DEPTH, 2, D_MODEL, D_FF), D_MODEL ** -0.5),
        'ffn_w_up': nrm(ks[5], (DEPTH, 2, D_MODEL, D_FF), D_MODEL ** -0.5),
        'ffn_w_down': nrm(ks[6], (DEPTH, 2, D_FF, D_MODEL), D_FF ** -0.5 * DEEPNORM_BETA),
        'w_in': nrm(ks[7], (DEPTH, D_MODEL, D_IN), D_MODEL ** -0.5),
        'lru_conv_w': nrm(ks[8], (DEPTH, CONV_K, LRU_WIDTH), CONV_K ** -0.5),
        'lru_conv_b': nrm(ks[9], (DEPTH, LRU_WIDTH), 0.01),
        'lru_w_a': nrm(ks[10], (DEPTH, LRU_HEADS, LRU_BLOCK, LRU_BLOCK), LRU_BLOCK ** -0.5),
        'lru_b_a': nrm(ks[11], (DEPTH, LRU_WIDTH), 0.01),
        'lru_w_x': nrm(ks[12], (DEPTH, LRU_HEADS, LRU_BLOCK, LRU_BLOCK), LRU_BLOCK ** -0.5),
        'lru_b_x': nrm(ks[13], (DEPTH, LRU_WIDTH), 0.01),
        'lru_lambda': jnp.log(s) - jnp.log1p(-s),
        'ssd_conv_w': nrm(ks[15], (DEPTH, CONV_K, SSD_CONV_DIM), CONV_K ** -0.5),
        'ssd_conv_b': nrm(ks[16], (DEPTH, SSD_CONV_DIM), 0.01),
        'ssd_dt_bias': dt_bias_init(ks[17], SSD_HEADS),
        'ssd_a_log': jnp.log(unif(ks[18], (DEPTH, SSD_HEADS), 1.0, 16.0)),
        'ssd_d': 1.0 + nrm(ks[19], (DEPTH, SSD_HEADS), 0.01),
        'ssd_norm_w': 1.0 + nrm(ks[20], (DEPTH, SSD_INNER), 0.01),
        'dn_conv_w': nrm(ks[21], (DEPTH, CONV_K, DN_CONV_DIM), CONV_K ** -0.5),
        'dn_a_log': jnp.log(unif(ks[22], (DEPTH, DN_HEADS), 1.0, 16.0)),
        'dn_dt_bias': dt_bias_init(ks[23], DN_HEADS),
        'dn_norm_w': 1.0 + nrm(ks[24], (DEPTH, DN_DV), 0.01),
        'w_out': nrm(ks[25], (DEPTH, MIX_WIDTH, D_MODEL), MIX_WIDTH ** -0.5 * DEEPNORM_BETA),
    }


def _fwd_reference(x, meta, ln_g, ln_b, ffn_w_gate, ffn_w_up, ffn_w_down, w_in,
              lru_conv_w, lru_conv_b, lru_w_a, lru_b_a, lru_w_x, lru_b_x, lru_lambda,
              ssd_conv_w, ssd_conv_b, ssd_dt_bias, ssd_a_log, ssd_d, ssd_norm_w,
              dn_conv_w, dn_a_log, dn_dt_bias, dn_norm_w, w_out):
    bsz = x.shape[0]
    h = jnp.concatenate([jnp.broadcast_to(meta.astype(x.dtype)[None], (bsz, N_META, D_MODEL)), x], axis=1)
    for l in range(DEPTH):
        h = _layer_norm(DEEPNORM_ALPHA * h + FFN_RES * _swiglu(h, ffn_w_gate[l, 0], ffn_w_up[l, 0], ffn_w_down[l, 0]),
                        ln_g[l, 0], ln_b[l, 0])
        (lru_u, lru_y, ssd_z, ssd_xbc, ssd_dt, dn_qkv, dn_gate, dn_beta, dn_alpha) = jnp.split(h @ w_in[l], IN_OFFSETS, axis=-1)
        o_lru = _rg_lru_group(lru_u, lru_y, lru_conv_w[l], lru_conv_b[l], lru_w_a[l], lru_b_a[l],
                              lru_w_x[l], lru_b_x[l], lru_lambda[l])
        o_ssd = _ssd_group(ssd_z, ssd_xbc, ssd_dt, ssd_conv_w[l], ssd_conv_b[l], ssd_dt_bias[l],
                           ssd_a_log[l], ssd_d[l], ssd_norm_w[l])
        o_dn = _deltanet_group(dn_qkv, dn_gate, dn_beta, dn_alpha, dn_conv_w[l], dn_a_log[l],
                               dn_dt_bias[l], dn_norm_w[l])
        mix = jnp.concatenate([o_lru, o_ssd, o_dn], axis=-1) @ w_out[l]
        h = _layer_norm(DEEPNORM_ALPHA * h + mix, ln_g[l, 1], ln_b[l, 1])
        h = _layer_norm(DEEPNORM_ALPHA * h + FFN_RES * _swiglu(h, ffn_w_gate[l, 1], ffn_w_up[l, 1], ffn_w_down[l, 1]),
                        ln_g[l, 2], ln_b[l, 2])
    return h[:, N_META:]


import jax as _jax
import jax.numpy as _jnp

TWIN_FORMAT = 'train_step'
FWD_PARAMS = ['x', 'meta', 'ln_g', 'ln_b', 'ffn_w_gate', 'ffn_w_up', 'ffn_w_down', 'w_in', 'lru_conv_w', 'lru_conv_b', 'lru_w_a', 'lru_b_a', 'lru_w_x', 'lru_b_x', 'lru_lambda', 'ssd_conv_w', 'ssd_conv_b', 'ssd_dt_bias', 'ssd_a_log', 'ssd_d', 'ssd_norm_w', 'dn_conv_w', 'dn_a_log', 'dn_dt_bias', 'dn_norm_w', 'w_out']
TWIN_WEIGHTS = ['meta', 'ln_g', 'ln_b', 'ffn_w_gate', 'ffn_w_up', 'ffn_w_down', 'w_in', 'lru_conv_w', 'lru_conv_b', 'lru_w_a', 'lru_b_a', 'lru_w_x', 'lru_b_x', 'lru_lambda', 'ssd_conv_w', 'ssd_conv_b', 'ssd_dt_bias', 'ssd_a_log', 'ssd_d', 'ssd_norm_w', 'dn_conv_w', 'dn_a_log', 'dn_dt_bias', 'dn_norm_w', 'w_out']
TWIN_DIFF_INPUT = 'x'
TWIN_INPUTS = ['x', 'meta', 'ln_g', 'ln_b', 'ffn_w_gate', 'ffn_w_up', 'ffn_w_down', 'w_in', 'lru_conv_w', 'lru_conv_b', 'lru_w_a', 'lru_b_a', 'lru_w_x', 'lru_b_x', 'lru_lambda', 'ssd_conv_w', 'ssd_conv_b', 'ssd_dt_bias', 'ssd_a_log', 'ssd_d', 'ssd_norm_w', 'dn_conv_w', 'dn_a_log', 'dn_dt_bias', 'dn_norm_w', 'w_out', 'loss_target', 'm_meta', 'm_ln_g', 'm_ln_b', 'm_ffn_w_gate', 'm_ffn_w_up', 'm_ffn_w_down', 'm_w_in', 'm_lru_conv_w', 'm_lru_conv_b', 'm_lru_w_a', 'm_lru_b_a', 'm_lru_w_x', 'm_lru_b_x', 'm_lru_lambda', 'm_ssd_conv_w', 'm_ssd_conv_b', 'm_ssd_dt_bias', 'm_ssd_a_log', 'm_ssd_d', 'm_ssd_norm_w', 'm_dn_conv_w', 'm_dn_a_log', 'm_dn_dt_bias', 'm_dn_norm_w', 'm_w_out', 'v_meta', 'v_ln_g', 'v_ln_b', 'v_ffn_w_gate', 'v_ffn_w_up', 'v_ffn_w_down', 'v_w_in', 'v_lru_conv_w', 'v_lru_conv_b', 'v_lru_w_a', 'v_lru_b_a', 'v_lru_w_x', 'v_lru_b_x', 'v_lru_lambda', 'v_ssd_conv_w', 'v_ssd_conv_b', 'v_ssd_dt_bias', 'v_ssd_a_log', 'v_ssd_d', 'v_ssd_norm_w', 'v_dn_conv_w', 'v_dn_a_log', 'v_dn_dt_bias', 'v_dn_norm_w', 'v_w_out']
TWIN_OUTPUTS = ['loss', 'grad_x', 'grad_meta', 'grad_ln_g', 'grad_ln_b', 'grad_ffn_w_gate', 'grad_ffn_w_up', 'grad_ffn_w_down', 'grad_w_in', 'grad_lru_conv_w', 'grad_lru_conv_b', 'grad_lru_w_a', 'grad_lru_b_a', 'grad_lru_w_x', 'grad_lru_b_x', 'grad_lru_lambda', 'grad_ssd_conv_w', 'grad_ssd_conv_b', 'grad_ssd_dt_bias', 'grad_ssd_a_log', 'grad_ssd_d', 'grad_ssd_norm_w', 'grad_dn_conv_w', 'grad_dn_a_log', 'grad_dn_dt_bias', 'grad_dn_norm_w', 'grad_w_out', 'delta_meta', 'delta_ln_g', 'delta_ln_b', 'delta_ffn_w_gate', 'delta_ffn_w_up', 'delta_ffn_w_down', 'delta_w_in', 'delta_lru_conv_w', 'delta_lru_conv_b', 'delta_lru_w_a', 'delta_lru_b_a', 'delta_lru_w_x', 'delta_lru_b_x', 'delta_lru_lambda', 'delta_ssd_conv_w', 'delta_ssd_conv_b', 'delta_ssd_dt_bias', 'delta_ssd_a_log', 'delta_ssd_d', 'delta_ssd_norm_w', 'delta_dn_conv_w', 'delta_dn_a_log', 'delta_dn_dt_bias', 'delta_dn_norm_w', 'delta_w_out', 'new_m_meta', 'new_m_ln_g', 'new_m_ln_b', 'new_m_ffn_w_gate', 'new_m_ffn_w_up', 'new_m_ffn_w_down', 'new_m_w_in', 'new_m_lru_conv_w', 'new_m_lru_conv_b', 'new_m_lru_w_a', 'new_m_lru_b_a', 'new_m_lru_w_x', 'new_m_lru_b_x', 'new_m_lru_lambda', 'new_m_ssd_conv_w', 'new_m_ssd_conv_b', 'new_m_ssd_dt_bias', 'new_m_ssd_a_log', 'new_m_ssd_d', 'new_m_ssd_norm_w', 'new_m_dn_conv_w', 'new_m_dn_a_log', 'new_m_dn_dt_bias', 'new_m_dn_norm_w', 'new_m_w_out', 'new_v_meta', 'new_v_ln_g', 'new_v_ln_b', 'new_v_ffn_w_gate', 'new_v_ffn_w_up', 'new_v_ffn_w_down', 'new_v_w_in', 'new_v_lru_conv_w', 'new_v_lru_conv_b', 'new_v_lru_w_a', 'new_v_lru_b_a', 'new_v_lru_w_x', 'new_v_lru_b_x', 'new_v_lru_lambda', 'new_v_ssd_conv_w', 'new_v_ssd_conv_b', 'new_v_ssd_dt_bias', 'new_v_ssd_a_log', 'new_v_ssd_d', 'new_v_ssd_norm_w', 'new_v_dn_conv_w', 'new_v_dn_a_log', 'new_v_dn_dt_bias', 'new_v_dn_norm_w', 'new_v_w_out']
TWIN_LEAF_KINDS = {'loss': 'loss', 'grad_x': 'grad_x', 'grad_meta': 'grad_w', 'grad_ln_g': 'grad_w', 'grad_ln_b': 'grad_w', 'grad_ffn_w_gate': 'grad_w', 'grad_ffn_w_up': 'grad_w', 'grad_ffn_w_down': 'grad_w', 'grad_w_in': 'grad_w', 'grad_lru_conv_w': 'grad_w', 'grad_lru_conv_b': 'grad_w', 'grad_lru_w_a': 'grad_w', 'grad_lru_b_a': 'grad_w', 'grad_lru_w_x': 'grad_w', 'grad_lru_b_x': 'grad_w', 'grad_lru_lambda': 'grad_w', 'grad_ssd_conv_w': 'grad_w', 'grad_ssd_conv_b': 'grad_w', 'grad_ssd_dt_bias': 'grad_w', 'grad_ssd_a_log': 'grad_w', 'grad_ssd_d': 'grad_w', 'grad_ssd_norm_w': 'grad_w', 'grad_dn_conv_w': 'grad_w', 'grad_dn_a_log': 'grad_w', 'grad_dn_dt_bias': 'grad_w', 'grad_dn_norm_w': 'grad_w', 'grad_w_out': 'grad_w', 'delta_meta': 'delta_w', 'delta_ln_g': 'delta_w', 'delta_ln_b': 'delta_w', 'delta_ffn_w_gate': 'delta_w', 'delta_ffn_w_up': 'delta_w', 'delta_ffn_w_down': 'delta_w', 'delta_w_in': 'delta_w', 'delta_lru_conv_w': 'delta_w', 'delta_lru_conv_b': 'delta_w', 'delta_lru_w_a': 'delta_w', 'delta_lru_b_a': 'delta_w', 'delta_lru_w_x': 'delta_w', 'delta_lru_b_x': 'delta_w', 'delta_lru_lambda': 'delta_w', 'delta_ssd_conv_w': 'delta_w', 'delta_ssd_conv_b': 'delta_w', 'delta_ssd_dt_bias': 'delta_w', 'delta_ssd_a_log': 'delta_w', 'delta_ssd_d': 'delta_w', 'delta_ssd_norm_w': 'delta_w', 'delta_dn_conv_w': 'delta_w', 'delta_dn_a_log': 'delta_w', 'delta_dn_dt_bias': 'delta_w', 'delta_dn_norm_w': 'delta_w', 'delta_w_out': 'delta_w', 'new_m_meta': 'new_m', 'new_m_ln_g': 'new_m', 'new_m_ln_b': 'new_m', 'new_m_ffn_w_gate': 'new_m', 'new_m_ffn_w_up': 'new_m', 'new_m_ffn_w_down': 'new_m', 'new_m_w_in': 'new_m', 'new_m_lru_conv_w': 'new_m', 'new_m_lru_conv_b': 'new_m', 'new_m_lru_w_a': 'new_m', 'new_m_lru_b_a': 'new_m', 'new_m_lru_w_x': 'new_m', 'new_m_lru_b_x': 'new_m', 'new_m_lru_lambda': 'new_m', 'new_m_ssd_conv_w': 'new_m', 'new_m_ssd_conv_b': 'new_m', 'new_m_ssd_dt_bias': 'new_m', 'new_m_ssd_a_log': 'new_m', 'new_m_ssd_d': 'new_m', 'new_m_ssd_norm_w': 'new_m', 'new_m_dn_conv_w': 'new_m', 'new_m_dn_a_log': 'new_m', 'new_m_dn_dt_bias': 'new_m', 'new_m_dn_norm_w': 'new_m', 'new_m_w_out': 'new_m', 'new_v_meta': 'new_v', 'new_v_ln_g': 'new_v', 'new_v_ln_b': 'new_v', 'new_v_ffn_w_gate': 'new_v', 'new_v_ffn_w_up': 'new_v', 'new_v_ffn_w_down': 'new_v', 'new_v_w_in': 'new_v', 'new_v_lru_conv_w': 'new_v', 'new_v_lru_conv_b': 'new_v', 'new_v_lru_w_a': 'new_v', 'new_v_lru_b_a': 'new_v', 'new_v_lru_w_x': 'new_v', 'new_v_lru_b_x': 'new_v', 'new_v_lru_lambda': 'new_v', 'new_v_ssd_conv_w': 'new_v', 'new_v_ssd_conv_b': 'new_v', 'new_v_ssd_dt_bias': 'new_v', 'new_v_ssd_a_log': 'new_v', 'new_v_ssd_d': 'new_v', 'new_v_ssd_norm_w': 'new_v', 'new_v_dn_conv_w': 'new_v', 'new_v_dn_a_log': 'new_v', 'new_v_dn_dt_bias': 'new_v', 'new_v_dn_norm_w': 'new_v', 'new_v_w_out': 'new_v'}


def _forward(args):
    return _fwd_reference(*[args[k] for k in FWD_PARAMS])


def _output_shape():
    out = _jax.eval_shape(lambda: _forward(_fwd_setup_inputs(0)))
    return out.shape, out.dtype

N_MICROBATCH = 1
ADAM_LR = 0.001
ADAM_B1 = 0.9
ADAM_B2 = 0.999
ADAM_EPS = 1e-08
ADAM_WD = 0.01
ADAM_STEP = 10
PER_EXAMPLE_BATCH_AXIS = {'x': 0, 'loss_target': 0}
SHARED_INPUTS = []
_WEIGHT_DTYPES = {'meta': _jnp.float32, 'ln_g': _jnp.float32, 'ln_b': _jnp.float32, 'ffn_w_gate': _jnp.float32, 'ffn_w_up': _jnp.float32, 'ffn_w_down': _jnp.float32, 'w_in': _jnp.float32, 'lru_conv_w': _jnp.float32, 'lru_conv_b': _jnp.float32, 'lru_w_a': _jnp.float32, 'lru_b_a': _jnp.float32, 'lru_w_x': _jnp.float32, 'lru_b_x': _jnp.float32, 'lru_lambda': _jnp.float32, 'ssd_conv_w': _jnp.float32, 'ssd_conv_b': _jnp.float32, 'ssd_dt_bias': _jnp.float32, 'ssd_a_log': _jnp.float32, 'ssd_d': _jnp.float32, 'ssd_norm_w': _jnp.float32, 'dn_conv_w': _jnp.float32, 'dn_a_log': _jnp.float32, 'dn_dt_bias': _jnp.float32, 'dn_norm_w': _jnp.float32, 'w_out': _jnp.float32}
MOMENT_SCALE = {'meta': 2.501709e-03, 'ln_g': 9.313925e+00, 'ln_b': 8.872502e-01, 'ffn_w_gate': 8.753634e-03, 'ffn_w_up': 8.491911e-03, 'ffn_w_down': 3.346975e-02, 'w_in': 3.520756e-02, 'lru_conv_w': 2.945061e-02, 'lru_conv_b': 3.614504e-01, 'lru_w_a': 1.159475e-02, 'lru_b_a': 8.109035e-03, 'lru_w_x': 2.073142e-02, 'lru_b_x': 1.058670e-02, 'lru_lambda': 1.579832e-02, 'ssd_conv_w': 4.023693e-02, 'ssd_conv_b': 5.213229e-02, 'ssd_dt_bias': 6.291687e-02, 'ssd_a_log': 9.053661e-02, 'ssd_d': 2.687858e-01, 'ssd_norm_w': 4.604425e-02, 'dn_conv_w': 2.605408e-02, 'dn_a_log': 2.400903e-01, 'dn_dt_bias': 2.352186e-01, 'dn_norm_w': 5.809500e-02, 'w_out': 9.365394e-02}


def _to_microbatches(a, axis):
    t = _jnp.moveaxis(a, axis, 0)
    t = t.reshape((N_MICROBATCH, t.shape[0] // N_MICROBATCH) + t.shape[1:])
    return _jnp.moveaxis(t, 1, axis + 1)


def setup_inputs(seed: int = 0) -> dict:
    inp = _fwd_setup_inputs(seed)
    key = _jax.random.fold_in(_jax.random.key(seed), 7919)
    shape, _ = _output_shape()
    out = dict(inp)
    out["loss_target"] = _jax.random.normal(_jax.random.fold_in(key, 0), shape, _jnp.float32)
    for i, name in enumerate(TWIN_WEIGHTS):
        w = inp[name].astype(_jnp.float32)
        if MOMENT_SCALE is None:
            s = _jnp.sqrt(_jnp.mean(_jnp.square(w)) + 1e-30)
        else:
            s = MOMENT_SCALE[name]
        km, kv = _jax.random.split(_jax.random.fold_in(key, i + 1))
        out[name] = w
        out["m_" + name] = s * _jax.random.normal(km, w.shape, _jnp.float32)
        out["v_" + name] = (s * s) * _jax.random.uniform(kv, w.shape, _jnp.float32, 0.5, 1.5)
    if N_MICROBATCH > 1:
        for name, axis in PER_EXAMPLE_BATCH_AXIS.items():
            out[name] = _to_microbatches(out[name], axis)
    return {'x': out['x'], 'meta': out['meta'], 'ln_g': out['ln_g'], 'ln_b': out['ln_b'], 'ffn_w_gate': out['ffn_w_gate'], 'ffn_w_up': out['ffn_w_up'], 'ffn_w_down': out['ffn_w_down'], 'w_in': out['w_in'], 'lru_conv_w': out['lru_conv_w'], 'lru_conv_b': out['lru_conv_b'], 'lru_w_a': out['lru_w_a'], 'lru_b_a': out['lru_b_a'], 'lru_w_x': out['lru_w_x'], 'lru_b_x': out['lru_b_x'], 'lru_lambda': out['lru_lambda'], 'ssd_conv_w': out['ssd_conv_w'], 'ssd_conv_b': out['ssd_conv_b'], 'ssd_dt_bias': out['ssd_dt_bias'], 'ssd_a_log': out['ssd_a_log'], 'ssd_d': out['ssd_d'], 'ssd_norm_w': out['ssd_norm_w'], 'dn_conv_w': out['dn_conv_w'], 'dn_a_log': out['dn_a_log'], 'dn_dt_bias': out['dn_dt_bias'], 'dn_norm_w': out['dn_norm_w'], 'w_out': out['w_out'], 'loss_target': out['loss_target'], 'm_meta': out['m_meta'], 'm_ln_g': out['m_ln_g'], 'm_ln_b': out['m_ln_b'], 'm_ffn_w_gate': out['m_ffn_w_gate'], 'm_ffn_w_up': out['m_ffn_w_up'], 'm_ffn_w_down': out['m_ffn_w_down'], 'm_w_in': out['m_w_in'], 'm_lru_conv_w': out['m_lru_conv_w'], 'm_lru_conv_b': out['m_lru_conv_b'], 'm_lru_w_a': out['m_lru_w_a'], 'm_lru_b_a': out['m_lru_b_a'], 'm_lru_w_x': out['m_lru_w_x'], 'm_lru_b_x': out['m_lru_b_x'], 'm_lru_lambda': out['m_lru_lambda'], 'm_ssd_conv_w': out['m_ssd_conv_w'], 'm_ssd_conv_b': out['m_ssd_conv_b'], 'm_ssd_dt_bias': out['m_ssd_dt_bias'], 'm_ssd_a_log': out['m_ssd_a_log'], 'm_ssd_d': out['m_ssd_d'], 'm_ssd_norm_w': out['m_ssd_norm_w'], 'm_dn_conv_w': out['m_dn_conv_w'], 'm_dn_a_log': out['m_dn_a_log'], 'm_dn_dt_bias': out['m_dn_dt_bias'], 'm_dn_norm_w': out['m_dn_norm_w'], 'm_w_out': out['m_w_out'], 'v_meta': out['v_meta'], 'v_ln_g': out['v_ln_g'], 'v_ln_b': out['v_ln_b'], 'v_ffn_w_gate': out['v_ffn_w_gate'], 'v_ffn_w_up': out['v_ffn_w_up'], 'v_ffn_w_down': out['v_ffn_w_down'], 'v_w_in': out['v_w_in'], 'v_lru_conv_w': out['v_lru_conv_w'], 'v_lru_conv_b': out['v_lru_conv_b'], 'v_lru_w_a': out['v_lru_w_a'], 'v_lru_b_a': out['v_lru_b_a'], 'v_lru_w_x': out['v_lru_w_x'], 'v_lru_b_x': out['v_lru_b_x'], 'v_lru_lambda': out['v_lru_lambda'], 'v_ssd_conv_w': out['v_ssd_conv_w'], 'v_ssd_conv_b': out['v_ssd_conv_b'], 'v_ssd_dt_bias': out['v_ssd_dt_bias'], 'v_ssd_a_log': out['v_ssd_a_log'], 'v_ssd_d': out['v_ssd_d'], 'v_ssd_norm_w': out['v_ssd_norm_w'], 'v_dn_conv_w': out['v_dn_conv_w'], 'v_dn_a_log': out['v_dn_a_log'], 'v_dn_dt_bias': out['v_dn_dt_bias'], 'v_dn_norm_w': out['v_dn_norm_w'], 'v_w_out': out['v_w_out']}


def _loss(weights, diff, rest, loss_target):
    with _jax.named_scope("forward"):
        args = {**rest, TWIN_DIFF_INPUT: diff, **{k: w.astype(_WEIGHT_DTYPES[k]) for k, w in weights.items()}}
        y = _forward(args)
    with _jax.named_scope("loss_head"):
        err = _jnp.square(y.astype(_jnp.float32) - loss_target)
        return 0.5 * _jnp.sum(_jnp.mean(err, axis=-1)) if err.ndim else 0.5 * err


def _adamw(w, g, m, v):
    m = ADAM_B1 * m + (1.0 - ADAM_B1) * g
    v = ADAM_B2 * v + (1.0 - ADAM_B2) * _jnp.square(g)
    m_hat = m / (1.0 - ADAM_B1 ** ADAM_STEP)
    v_hat = v / (1.0 - ADAM_B2 ** ADAM_STEP)
    delta = -ADAM_LR * (m_hat / (_jnp.sqrt(v_hat) + ADAM_EPS) + ADAM_WD * w)
    return delta, m, v


def reference(x, meta, ln_g, ln_b, ffn_w_gate, ffn_w_up, ffn_w_down, w_in, lru_conv_w, lru_conv_b, lru_w_a, lru_b_a, lru_w_x, lru_b_x, lru_lambda, ssd_conv_w, ssd_conv_b, ssd_dt_bias, ssd_a_log, ssd_d, ssd_norm_w, dn_conv_w, dn_a_log, dn_dt_bias, dn_norm_w, w_out, loss_target, m_meta, m_ln_g, m_ln_b, m_ffn_w_gate, m_ffn_w_up, m_ffn_w_down, m_w_in, m_lru_conv_w, m_lru_conv_b, m_lru_w_a, m_lru_b_a, m_lru_w_x, m_lru_b_x, m_lru_lambda, m_ssd_conv_w, m_ssd_conv_b, m_ssd_dt_bias, m_ssd_a_log, m_ssd_d, m_ssd_norm_w, m_dn_conv_w, m_dn_a_log, m_dn_dt_bias, m_dn_norm_w, m_w_out, v_meta, v_ln_g, v_ln_b, v_ffn_w_gate, v_ffn_w_up, v_ffn_w_down, v_w_in, v_lru_conv_w, v_lru_conv_b, v_lru_w_a, v_lru_b_a, v_lru_w_x, v_lru_b_x, v_lru_lambda, v_ssd_conv_w, v_ssd_conv_b, v_ssd_dt_bias, v_ssd_a_log, v_ssd_d, v_ssd_norm_w, v_dn_conv_w, v_dn_a_log, v_dn_dt_bias, v_dn_norm_w, v_w_out):
    given = dict(x=x, meta=meta, ln_g=ln_g, ln_b=ln_b, ffn_w_gate=ffn_w_gate, ffn_w_up=ffn_w_up, ffn_w_down=ffn_w_down, w_in=w_in, lru_conv_w=lru_conv_w, lru_conv_b=lru_conv_b, lru_w_a=lru_w_a, lru_b_a=lru_b_a, lru_w_x=lru_w_x, lru_b_x=lru_b_x, lru_lambda=lru_lambda, ssd_conv_w=ssd_conv_w, ssd_conv_b=ssd_conv_b, ssd_dt_bias=ssd_dt_bias, ssd_a_log=ssd_a_log, ssd_d=ssd_d, ssd_norm_w=ssd_norm_w, dn_conv_w=dn_conv_w, dn_a_log=dn_a_log, dn_dt_bias=dn_dt_bias, dn_norm_w=dn_norm_w, w_out=w_out, loss_target=loss_target, m_meta=m_meta, m_ln_g=m_ln_g, m_ln_b=m_ln_b, m_ffn_w_gate=m_ffn_w_gate, m_ffn_w_up=m_ffn_w_up, m_ffn_w_down=m_ffn_w_down, m_w_in=m_w_in, m_lru_conv_w=m_lru_conv_w, m_lru_conv_b=m_lru_conv_b, m_lru_w_a=m_lru_w_a, m_lru_b_a=m_lru_b_a, m_lru_w_x=m_lru_w_x, m_lru_b_x=m_lru_b_x, m_lru_lambda=m_lru_lambda, m_ssd_conv_w=m_ssd_conv_w, m_ssd_conv_b=m_ssd_conv_b, m_ssd_dt_bias=m_ssd_dt_bias, m_ssd_a_log=m_ssd_a_log, m_ssd_d=m_ssd_d, m_ssd_norm_w=m_ssd_norm_w, m_dn_conv_w=m_dn_conv_w, m_dn_a_log=m_dn_a_log, m_dn_dt_bias=m_dn_dt_bias, m_dn_norm_w=m_dn_norm_w, m_w_out=m_w_out, v_meta=v_meta, v_ln_g=v_ln_g, v_ln_b=v_ln_b, v_ffn_w_gate=v_ffn_w_gate, v_ffn_w_up=v_ffn_w_up, v_ffn_w_down=v_ffn_w_down, v_w_in=v_w_in, v_lru_conv_w=v_lru_conv_w, v_lru_conv_b=v_lru_conv_b, v_lru_w_a=v_lru_w_a, v_lru_b_a=v_lru_b_a, v_lru_w_x=v_lru_w_x, v_lru_b_x=v_lru_b_x, v_lru_lambda=v_lru_lambda, v_ssd_conv_w=v_ssd_conv_w, v_ssd_conv_b=v_ssd_conv_b, v_ssd_dt_bias=v_ssd_dt_bias, v_ssd_a_log=v_ssd_a_log, v_ssd_d=v_ssd_d, v_ssd_norm_w=v_ssd_norm_w, v_dn_conv_w=v_dn_conv_w, v_dn_a_log=v_dn_a_log, v_dn_dt_bias=v_dn_dt_bias, v_dn_norm_w=v_dn_norm_w, v_w_out=v_w_out)
    weights = {n: given[n] for n in TWIN_WEIGHTS}
    shared = {n: given[n] for n in SHARED_INPUTS}
    per_example = {n: given[n] for n in ['x']}
    grad_fn = _jax.value_and_grad(_loss, argnums=(0, 1))

    def one_microbatch(ex, loss_target):
        ex = dict(ex)
        diff = ex.pop(TWIN_DIFF_INPUT)
        return grad_fn(weights, diff, {**shared, **ex}, loss_target)

    if N_MICROBATCH == 1:
        loss, (grad_w, grad_x) = one_microbatch(per_example, given["loss_target"])
    else:
        def body(carry, xs):
            loss_sum, grad_sum = carry
            l_k, (gw_k, gx_k) = one_microbatch(xs[0], xs[1])
            with _jax.named_scope("update"):
                return (loss_sum + l_k, _jax.tree.map(_jnp.add, grad_sum, gw_k)), gx_k

        init = (_jnp.zeros((), _jnp.float32), _jax.tree.map(_jnp.zeros_like, weights))
        (loss, grad_w), grad_x = _jax.lax.scan(body, init, (per_example, given["loss_target"]))
    with _jax.named_scope("update"):
        delta_w, new_m, new_v = {}, {}, {}
        for n in TWIN_WEIGHTS:
            delta_w[n], new_m[n], new_v[n] = _adamw(weights[n], grad_w[n], given["m_" + n], given["v_" + n])
    return (loss, grad_x, *[grad_w[n] for n in TWIN_WEIGHTS], *[delta_w[n] for n in TWIN_WEIGHTS],
            *[new_m[n] for n in TWIN_WEIGHTS], *[new_v[n] for n in TWIN_WEIGHTS])
```

```python
import functools
import math

import jax
import jax.numpy as jnp
from jax import lax
from jax.experimental import pallas as pl
from jax.experimental.pallas import tpu as pltpu

F32 = jnp.float32
BF16 = jnp.bfloat16
HI = lax.Precision.HIGHEST
MESH_IDS = pl.DeviceIdType.MESH

N_DEV = 8
D_MODEL = 1024
DEPTH = 4
N_META = 16
CHUNK = 64
CONV_K = 4
D_FF = 2816
LRU_WIDTH = 256
LRU_HEADS = 4
LRU_C = 8.0
SSD_HEADS = 8
SSD_GROUPS = 2
SSD_HPG = 4
SSD_INNER = 512
DN_HEADS = 4
HEAD = 64
CONV_W = 1792
PROJ_W = 2944
OFF_LRU_Y, OFF_SSD_Z, OFF_DN_GATE, OFF_SMALL = 1792, 2048, 2560, 2816
D_IN = 2832
ALPHA = (2 * DEPTH) ** 0.25
FFN_RES = 0.5
LN_EPS = 1e-5
RMS_EPS = 1e-6
ADAM_LR, ADAM_B1, ADAM_B2, ADAM_EPS, ADAM_WD, ADAM_STEP = 0.001, 0.9, 0.999, 1e-08, 0.01, 10

VMEM_LIMIT = 56 * 1024 * 1024
ROW_TILE_CAP = 416
LANE = 128


def _tile(n, cap, mult=16):
    best = None
    for t in range(mult, min(n, cap) + 1, mult):
        if n % t == 0:
            best = t
    assert best is not None, (n, cap, mult)
    return best


def _params(sem=None):
    return pltpu.CompilerParams(dimension_semantics=sem, vmem_limit_bytes=VMEM_LIMIT)


def _resident(shape, index_map):
    return pl.BlockSpec(shape, index_map, pipeline_mode=pl.Buffered(1))


def _dot(a, b):
    return jnp.dot(a, b, preferred_element_type=F32)


def _dot_nt(a, b):
    return lax.dot_general(a, b, (((1,), (1,)), ((), ())), preferred_element_type=F32)


def _dot_tn(a, b):
    return lax.dot_general(a, b, (((0,), (0,)), ((), ())), preferred_element_type=F32)


def _sds(shape, dtype):
    return jax.ShapeDtypeStruct(shape, dtype)


def _ffn_up(xb, wgu, name):
    tp, d = xb.shape
    f = wgu.shape[1] // 2
    tm = _tile(tp, ROW_TILE_CAP)
    tn = _tile(f, 1408, LANE)
    nj = f // tn

    def body(x_ref, wg_ref, wu_ref, g_ref, u_ref, a_ref):
        x = x_ref[...]
        g = _dot(x, wg_ref[...])
        u = _dot(x, wu_ref[...])
        g_ref[...] = g.astype(BF16)
        u_ref[...] = u.astype(BF16)
        a_ref[...] = (g * jax.nn.sigmoid(g) * u).astype(BF16)

    out = _sds((tp, f), BF16)
    return pl.pallas_call(
        body, name=name, out_shape=(out, out, out), grid=(nj, tp // tm),
        in_specs=[pl.BlockSpec((tm, d), lambda j, i: (i, 0)),
                  pl.BlockSpec((d, tn), lambda j, i: (0, j)),
                  pl.BlockSpec((d, tn), lambda j, i: (0, j + nj))],
        out_specs=[pl.BlockSpec((tm, tn), lambda j, i: (i, j))] * 3,
        compiler_params=_params(("arbitrary", "arbitrary")),
    )(xb, wgu, wgu)


def _mm_resid_ln(a, w, h, gamma, beta, scale, name):
    tp, k = a.shape
    d = w.shape[1]
    tm = _tile(tp, ROW_TILE_CAP)

    def body(a_ref, w_ref, h_ref, g_ref, b_ref, y_ref, o_ref, ob_ref):
        y = ALPHA * h_ref[...] + scale * _dot(a_ref[...], w_ref[...])
        mu = jnp.mean(y, axis=-1, keepdims=True)
        yc = y - mu
        var = jnp.mean(yc * yc, axis=-1, keepdims=True)
        o = yc * lax.rsqrt(var + LN_EPS) * g_ref[...] + b_ref[...]
        y_ref[...] = y
        o_ref[...] = o
        ob_ref[...] = o.astype(BF16)

    row = lambda i: (i, 0)
    fix = lambda i: (0, 0)
    return pl.pallas_call(
        body, name=name, out_shape=(_sds((tp, d), F32), _sds((tp, d), F32), _sds((tp, d), BF16)),
        grid=(tp // tm,),
        in_specs=[pl.BlockSpec((tm, k), row), _resident((k, d), fix), pl.BlockSpec((tm, d), row),
                  pl.BlockSpec((1, d), fix), pl.BlockSpec((1, d), fix)],
        out_specs=[pl.BlockSpec((tm, d), row)] * 3,
        compiler_params=_params(("arbitrary",)),
    )(a, w, h, gamma, beta)


def _mm_nn(xb, w, name):
    tp, k = xb.shape
    n = w.shape[1]
    tm = _tile(tp, ROW_TILE_CAP)

    def body(x_ref, w_ref, o_ref):
        o_ref[...] = _dot(x_ref[...], w_ref[...])

    return pl.pallas_call(
        body, name=name, out_shape=_sds((tp, n), F32), grid=(tp // tm,),
        in_specs=[pl.BlockSpec((tm, k), lambda i: (i, 0)), _resident((k, n), lambda i: (0, 0))],
        out_specs=pl.BlockSpec((tm, n), lambda i: (i, 0)),
        compiler_params=_params(("arbitrary",)),
    )(xb, w)


def _mm_nt_add(a, w, resid, name):
    tp, k = a.shape
    n = w.shape[0]
    tm = _tile(tp, ROW_TILE_CAP)
    has_resid = resid is not None

    def body(*refs):
        if has_resid:
            a_ref, w_ref, r_ref, o_ref = refs
            o_ref[...] = r_ref[...] + _dot_nt(a_ref[...], w_ref[...])
        else:
            a_ref, w_ref, o_ref = refs
            o_ref[...] = _dot_nt(a_ref[...], w_ref[...])

    in_specs = [pl.BlockSpec((tm, k), lambda i: (i, 0)), _resident((n, k), lambda i: (0, 0))]
    args = [a, w]
    if has_resid:
        in_specs.append(pl.BlockSpec((tm, n), lambda i: (i, 0)))
        args.append(resid)
    return pl.pallas_call(
        body, name=name, out_shape=_sds((tp, n), F32), grid=(tp // tm,),
        in_specs=in_specs, out_specs=pl.BlockSpec((tm, n), lambda i: (i, 0)),
        compiler_params=_params(("arbitrary",)),
    )(*args)


def _ffn_dx(dg, du, wgu, resid, name):
    tp, f = dg.shape
    d = wgu.shape[0]
    tm = _tile(tp, ROW_TILE_CAP)

    def body(dg_ref, du_ref, w_ref, r_ref, o_ref):
        acc = r_ref[...] + _dot_nt(dg_ref[...], w_ref[:, 0:f])
        o_ref[...] = acc + _dot_nt(du_ref[...], w_ref[:, f:2 * f])

    row = lambda i: (i, 0)
    return pl.pallas_call(
        body, name=name, out_shape=_sds((tp, d), F32), grid=(tp // tm,),
        in_specs=[pl.BlockSpec((tm, f), row), pl.BlockSpec((tm, f), row), _resident((d, 2 * f), lambda i: (0, 0)),
                  pl.BlockSpec((tm, d), row)],
        out_specs=pl.BlockSpec((tm, d), row),
        compiler_params=_params(("arbitrary",)),
    )(dg, du, wgu, resid)


def _mm_tn(a, b, name):
    tp, ka = a.shape
    nb = b.shape[1]
    tt = _tile(tp, ROW_TILE_CAP)
    tk = _tile(ka, 512, LANE)
    tn = _tile(nb, 1536, LANE)
    nt = tp // tt

    def body(a_ref, b_ref, o_ref):
        @pl.when(pl.program_id(2) == 0)
        def _():
            o_ref[...] = jnp.zeros_like(o_ref)
        o_ref[...] += _dot_tn(a_ref[...], b_ref[...])

    return pl.pallas_call(
        body, name=name, out_shape=_sds((ka, nb), F32), grid=(ka // tk, nb // tn, nt),
        in_specs=[pl.BlockSpec((tt, tk), lambda i, j, t: (t, i)), pl.BlockSpec((tt, tn), lambda i, j, t: (t, j))],
        out_specs=pl.BlockSpec((tk, tn), lambda i, j, t: (i, j)),
        compiler_params=_params(("arbitrary", "arbitrary", "arbitrary")),
    )(a, b)


def _ln_bwd(y, dout, gamma, scale, name):
    tp, d = y.shape
    tm = _tile(tp, ROW_TILE_CAP)

    def body(y_ref, do_ref, g_ref, dres_ref, dyb_ref, dg_ref, db_ref):
        @pl.when(pl.program_id(0) == 0)
        def _():
            dg_ref[...] = jnp.zeros_like(dg_ref)
            db_ref[...] = jnp.zeros_like(db_ref)
        yv = y_ref[...]
        do = do_ref[...]
        mu = jnp.mean(yv, axis=-1, keepdims=True)
        yc = yv - mu
        var = jnp.mean(yc * yc, axis=-1, keepdims=True)
        rstd = lax.rsqrt(var + LN_EPS)
        xhat = yc * rstd
        dxh = do * g_ref[...]
        m1 = jnp.mean(dxh, axis=-1, keepdims=True)
        m2 = jnp.mean(dxh * xhat, axis=-1, keepdims=True)
        dy = rstd * (dxh - m1 - xhat * m2)
        dres_ref[...] = ALPHA * dy
        dyb_ref[...] = (scale * dy).astype(BF16)
        dg_ref[...] += jnp.sum(do * xhat, axis=0, keepdims=True)
        db_ref[...] += jnp.sum(do, axis=0, keepdims=True)

    row = lambda i: (i, 0)
    fix = lambda i: (0, 0)
    return pl.pallas_call(
        body, name=name,
        out_shape=(_sds((tp, d), F32), _sds((tp, d), BF16), _sds((1, d), F32), _sds((1, d), F32)),
        grid=(tp // tm,),
        in_specs=[pl.BlockSpec((tm, d), row), pl.BlockSpec((tm, d), row), pl.BlockSpec((1, d), fix)],
        out_specs=[pl.BlockSpec((tm, d), row), pl.BlockSpec((tm, d), row), pl.BlockSpec((1, d), fix),
                   pl.BlockSpec((1, d), fix)],
        compiler_params=_params(("arbitrary",)),
    )(y, dout, gamma)


def _ffn_dact(dyb, wd, g, u, name):
    tp, d = dyb.shape
    f = wd.shape[0]
    tm = _tile(tp, ROW_TILE_CAP)
    tn = _tile(f, 1408, LANE)

    def body(dy_ref, w_ref, g_ref, u_ref, dg_ref, du_ref):
        dact = _dot_nt(dy_ref[...], w_ref[...])
        gv = g_ref[...].astype(F32)
        uv = u_ref[...].astype(F32)
        sg = jax.nn.sigmoid(gv)
        dg_ref[...] = (dact * uv * (sg * (1.0 + gv * (1.0 - sg)))).astype(BF16)
        du_ref[...] = (dact * (gv * sg)).astype(BF16)

    out = _sds((tp, f), BF16)
    blk = pl.BlockSpec((tm, tn), lambda j, i: (i, j))
    return pl.pallas_call(
        body, name=name, out_shape=(out, out), grid=(f // tn, tp // tm),
        in_specs=[pl.BlockSpec((tm, d), lambda j, i: (i, 0)), pl.BlockSpec((tn, d), lambda j, i: (j, 0)), blk, blk],
        out_specs=[blk, blk],
        compiler_params=_params(("arbitrary", "arbitrary")),
    )(dyb, wd, g, u)


def _loss_grad(o, tgt, t_real, name):
    tp, d = o.shape
    tm = _tile(tp, ROW_TILE_CAP)

    def body(o_ref, t_ref, l_ref, d_ref):
        i = pl.program_id(0)

        @pl.when(i == 0)
        def _():
            l_ref[...] = jnp.zeros_like(l_ref)
        rows = i * tm + lax.broadcasted_iota(jnp.int32, (tm, 1), 0)
        real = jnp.logical_and(rows >= N_META, rows < t_real)
        err = jnp.where(real, o_ref[...] - t_ref[...], 0.0)
        d_ref[...] = err * (1.0 / d)
        l_ref[...] += jnp.sum(err * err, axis=0, keepdims=True) * (0.5 / d)

    row = lambda i: (i, 0)
    return pl.pallas_call(
        body, name=name, out_shape=(_sds((1, d), F32), _sds((tp, d), F32)), grid=(tp // tm,),
        in_specs=[pl.BlockSpec((tm, d), row), pl.BlockSpec((tm, d), row)],
        out_specs=[pl.BlockSpec((1, d), lambda i: (0, 0)), pl.BlockSpec((tm, d), row)],
        compiler_params=_params(("arbitrary",)),
    )(o, tgt)


CONV_TC = 256


def _silu_grad(y):
    s = jax.nn.sigmoid(y)
    return s * (1.0 + y * (1.0 - s))


def _conv_taps(x_ref, w, r0, rb):
    cur = x_ref[r0:r0 + rb, :]
    prev = x_ref[r0 - 8:r0, :] if r0 > 0 else jnp.zeros((8, cur.shape[1]), F32)
    xcat = jnp.concatenate([prev, cur], axis=0)
    taps = [xcat[5 + j:5 + j + rb] for j in range(CONV_K - 1)] + [cur]
    y = w[0:1] * taps[0]
    for j in range(1, CONV_K):
        y = y + w[j:j + 1] * taps[j]
    return y, taps


def _conv_fwd(proj, w, b, name):
    tp = proj.shape[0]
    rb = _tile(tp, ROW_TILE_CAP, 8)

    def body(x_ref, w_ref, b_ref, o_ref):
        gated = pl.program_id(0) > 0
        wv = w_ref[...]
        bv = b_ref[...]
        for r0 in range(0, tp, rb):
            y, _ = _conv_taps(x_ref, wv, r0, rb)
            y = y + bv
            o_ref[r0:r0 + rb, :] = jnp.where(gated, y * jax.nn.sigmoid(y), y)

    col = lambda j: (0, j)
    return pl.pallas_call(
        body, name=name, out_shape=_sds((tp, CONV_W), F32), grid=(CONV_W // CONV_TC,),
        in_specs=[pl.BlockSpec((tp, CONV_TC), col), pl.BlockSpec((CONV_K, CONV_TC), col), pl.BlockSpec((1, CONV_TC), col)],
        out_specs=pl.BlockSpec((tp, CONV_TC), col),
        compiler_params=_params(("arbitrary",)),
    )(proj, w, b)


def _conv_bwd(proj, w, b, dxc, name):
    tp = proj.shape[0]
    rb = _tile(tp, ROW_TILE_CAP, 8)

    def body(x_ref, w_ref, b_ref, d_ref, dx_ref, dw_ref, db_ref, dy_scr):
        gated = pl.program_id(0) > 0
        wv = w_ref[...]
        bv = b_ref[...]
        dw = [jnp.zeros((1, CONV_TC), F32) for _ in range(CONV_K)]
        db = jnp.zeros((1, CONV_TC), F32)
        for r0 in range(0, tp, rb):
            y, taps = _conv_taps(x_ref, wv, r0, rb)
            y = y + bv
            d = d_ref[r0:r0 + rb, :]
            dy = jnp.where(gated, d * _silu_grad(y), d)
            dy_scr[r0:r0 + rb, :] = dy
            for j in range(CONV_K):
                dw[j] = dw[j] + jnp.sum(dy * taps[j], axis=0, keepdims=True)
            db = db + jnp.sum(dy, axis=0, keepdims=True)
        for j in range(CONV_K):
            dw_ref[j:j + 1, :] = dw[j]
        db_ref[...] = db
        for r0 in range(0, tp, rb):
            cur = dy_scr[r0:r0 + rb, :]
            nxt = dy_scr[r0 + rb:r0 + rb + 8, :] if r0 + rb < tp else jnp.zeros((8, CONV_TC), F32)
            dcat = jnp.concatenate([cur, nxt], axis=0)
            dx = wv[3:4] * cur
            for s in range(1, CONV_K):
                dx = dx + wv[3 - s:4 - s] * dcat[s:s + rb]
            dx_ref[r0:r0 + rb, :] = dx

    col = lambda j: (0, j)
    return pl.pallas_call(
        body, name=name,
        out_shape=(_sds((tp, CONV_W), F32), _sds((CONV_K, CONV_W), F32), _sds((1, CONV_W), F32)),
        grid=(CONV_W // CONV_TC,),
        in_specs=[pl.BlockSpec((tp, CONV_TC), col), pl.BlockSpec((CONV_K, CONV_TC), col), pl.BlockSpec((1, CONV_TC), col),
                  pl.BlockSpec((tp, CONV_TC), col)],
        out_specs=[pl.BlockSpec((tp, CONV_TC), col), pl.BlockSpec((CONV_K, CONV_TC), col), pl.BlockSpec((1, CONV_TC), col)],
        scratch_shapes=[pltpu.VMEM((tp, CONV_TC), F32)],
        compiler_params=_params(("arbitrary",)),
    )(proj, w, b, dxc)


def _neg_expm1(x):
    series = -x * (1.0 + x * (0.5 + x * (1.0 / 6.0 + x * (1.0 / 24.0))))
    return jnp.where(jnp.abs(x) < 0.03, series, 1.0 - jnp.exp(x))


def _lru_gates(u, wa, ba, wx, bx, lam):
    r = jax.nn.sigmoid(jnp.dot(u, wa, precision=HI, preferred_element_type=F32) + ba)
    i = jax.nn.sigmoid(jnp.dot(u, wx, precision=HI, preferred_element_type=F32) + bx)
    log_a = -LRU_C * r * jax.nn.softplus(-lam)
    a = jnp.exp(log_a)
    b = jnp.sqrt(_neg_expm1(2.0 * log_a)) * (i * u)
    return a, b


def _lru_specs(tm, flip, n_tiles):
    idx = (lambda i: (n_tiles - 1 - i, 0)) if flip else (lambda i: (i, 0))
    return idx, lambda i: (0, 0)


def _lru_fwd(xc, proj, wa, ba, wx, bx, lam, name):
    tp = xc.shape[0]
    w = LRU_WIDTH
    tm = _tile(tp, ROW_TILE_CAP, 8)
    ycol = OFF_LRU_Y // w

    def body(u_ref, y_ref, wa_ref, ba_ref, wx_ref, bx_ref, lam_ref, o_ref, h_ref, a_scr, b_scr, carry):
        @pl.when(pl.program_id(0) == 0)
        def _():
            carry[...] = jnp.zeros_like(carry)
        a, b = _lru_gates(u_ref[...], wa_ref[...], ba_ref[...], wx_ref[...], bx_ref[...], lam_ref[...])
        a_scr[...] = a
        b_scr[...] = b

        def step(t, h):
            h = a_scr[pl.ds(t, 1), :] * h + b_scr[pl.ds(t, 1), :]
            h_ref[pl.ds(t, 1), :] = h
            return h

        carry[0:1, :] = lax.fori_loop(0, tm, step, carry[0:1, :])
        o_ref[...] = h_ref[...] * jax.nn.gelu(y_ref[...])

    fix = lambda i: (0, 0)
    return pl.pallas_call(
        body, name=name, out_shape=(_sds((tp, w), F32), _sds((tp, w), F32)), grid=(tp // tm,),
        in_specs=[pl.BlockSpec((tm, w), lambda i: (i, 0)), pl.BlockSpec((tm, w), lambda i: (i, ycol)),
                  pl.BlockSpec((w, w), fix), pl.BlockSpec((1, w), fix), pl.BlockSpec((w, w), fix),
                  pl.BlockSpec((1, w), fix), pl.BlockSpec((1, w), fix)],
        out_specs=[pl.BlockSpec((tm, w), lambda i: (i, 0))] * 2,
        scratch_shapes=[pltpu.VMEM((tm, w), F32), pltpu.VMEM((tm, w), F32), pltpu.VMEM((8, w), F32)],
        compiler_params=_params(("arbitrary",)),
    )(xc, proj, wa, ba, wx, bx, lam)


def _lru_bwd(xc, proj, h, dout, wa, ba, wx, bx, lam, name):
    tp = xc.shape[0]
    w = LRU_WIDTH
    tm = _tile(tp, ROW_TILE_CAP, 8)
    nt = tp // tm
    ycol = OFF_LRU_Y // w
    rev = lambda i: (nt - 1 - i, 0)
    prev8 = lambda i: (jnp.maximum((nt - 1 - i) * (tm // 8) - 1, 0), 0)

    def body(u_ref, y_ref, h_ref, hp_ref, do_ref, wa_ref, ba_ref, wx_ref, bx_ref, lam_ref,
             du_ref, dy_ref, dwa_ref, dba_ref, dwx_ref, dbx_ref, dlam_ref,
             a_scr, dh_scr, g_scr, da_scr, hext, carry):
        i = pl.program_id(0)

        @pl.when(i == 0)
        def _():
            carry[...] = jnp.zeros_like(carry)
            for r in (dwa_ref, dba_ref, dwx_ref, dbx_ref, dlam_ref):
                r[...] = jnp.zeros_like(r)
        params = (wa_ref[...], ba_ref[...], wx_ref[...], bx_ref[...], lam_ref[...])
        (a, _), gates_vjp = jax.vjp(_lru_gates, u_ref[...], *params)
        gel, gelu_vjp = jax.vjp(jax.nn.gelu, y_ref[...])
        do = do_ref[...]
        hv = h_ref[...]
        dy_ref[...] = gelu_vjp(do * hv)[0]
        a_scr[...] = a
        dh_scr[...] = do * gel
        hext[0:8, :] = jnp.where(i == nt - 1, 0.0, hp_ref[...])
        hext[8:8 + tm, :] = hv

        def step(s, c):
            t = tm - 1 - s
            g = dh_scr[pl.ds(t, 1), :] + c
            g_scr[pl.ds(t, 1), :] = g
            da_scr[pl.ds(t, 1), :] = g * hext[pl.ds(t + 7, 1), :]
            return a_scr[pl.ds(t, 1), :] * g

        carry[0:1, :] = lax.fori_loop(0, tm, step, carry[0:1, :])
        du, dwa, dba, dwx, dbx, dlam = gates_vjp((da_scr[...], g_scr[...]))
        du_ref[...] = du
        dwa_ref[...] += dwa
        dba_ref[...] += dba
        dwx_ref[...] += dwx
        dbx_ref[...] += dbx
        dlam_ref[...] += dlam

    fix = lambda i: (0, 0)
    tile = pl.BlockSpec((tm, w), rev)
    mat = pl.BlockSpec((w, w), fix)
    vec = pl.BlockSpec((1, w), fix)
    return pl.pallas_call(
        body, name=name,
        out_shape=(_sds((tp, w), F32), _sds((tp, w), F32), _sds((w, w), F32), _sds((1, w), F32), _sds((w, w), F32),
                   _sds((1, w), F32), _sds((1, w), F32)),
        grid=(nt,),
        in_specs=[tile, pl.BlockSpec((tm, w), lambda i: (nt - 1 - i, ycol)), tile, pl.BlockSpec((8, w), prev8), tile,
                  mat, vec, mat, vec, vec],
        out_specs=[tile, tile, mat, vec, mat, vec, vec],
        scratch_shapes=[pltpu.VMEM((tm, w), F32)] * 4 + [pltpu.VMEM((tm + 8, w), F32), pltpu.VMEM((8, w), F32)],
        compiler_params=_params(("arbitrary",)),
    )(xc, proj, h, h, dout, wa, ba, wx, bx, lam)


def _bmm(a, b):
    return lax.dot_general(a, b, (((2,), (1,)), ((0,), (0,))), precision=HI, preferred_element_type=F32)


def _bmm_nt(a, b):
    return lax.dot_general(a, b, (((2,), (2,)), ((0,), (0,))), precision=HI, preferred_element_type=F32)


def _bmm_tn(a, b):
    return lax.dot_general(a, b, (((1,), (1,)), ((0,), (0,))), precision=HI, preferred_element_type=F32)


def _chunk_masks(nh):
    r = lax.broadcasted_iota(jnp.int32, (CHUNK, CHUNK), 0)
    c = lax.broadcasted_iota(jnp.int32, (CHUNK, CHUNK), 1)
    full = lambda m: jnp.broadcast_to(m[None], (nh, CHUNK, CHUNK))
    return r, c, full


def _decay_terms(g, nh):
    r, c, full = _chunk_masks(nh)
    incl = r >= c
    ones = jnp.ones((nh, CHUNK, CHUNK), F32)
    cs = _bmm(full(incl.astype(F32)), g)
    cs_t = _bmm(ones, g * full((r <= c).astype(F32)))
    tot = _bmm(ones, g)
    m = full(incl)
    decay = jnp.where(m, jnp.exp(jnp.where(m, cs - cs_t, 0.0)), 0.0)
    return cs, decay, tot


def _rep_groups(x):
    return jnp.concatenate([jnp.broadcast_to(x[g:g + 1], (SSD_HPG,) + x.shape[1:]) for g in range(SSD_GROUPS)], axis=0)


def _ssd_chunk(xs, bm, cm, dtr, z, p_dtb, p_alog, p_d, p_nw, state):
    nh = SSD_HEADS
    dt = jax.nn.softplus(dtr + p_dtb)
    a = dt * (-jnp.exp(p_alog))
    x = xs * dt
    cs, decay, tot = _decay_terms(a, nh)
    b8 = _rep_groups(bm)
    c8 = _rep_groups(cm)
    y = _bmm(_rep_groups(_bmm_nt(cm, bm)) * decay, x)
    y = y + _bmm_nt(c8, state) * jnp.exp(cs)
    new_state = state * jnp.exp(tot) + _bmm_tn(x * jnp.exp(tot - cs), b8)
    y = y + p_d * xs
    y = y * (z * jax.nn.sigmoid(z))
    ss = jnp.sum(y * y, axis=-1, keepdims=True)
    ssg = jnp.concatenate(
        [jnp.broadcast_to(jnp.sum(ss[g * SSD_HPG:(g + 1) * SSD_HPG], axis=0, keepdims=True), (SSD_HPG, CHUNK, 1))
         for g in range(SSD_GROUPS)], axis=0)
    y = y * lax.rsqrt(ssg * (1.0 / (SSD_HPG * HEAD)) + RMS_EPS) * p_nw
    return y, new_state


def _unit_lower_inverse(m, nh):
    r, c, full = _chunk_masks(nh)
    eye = full((r == c).astype(F32))
    md = jnp.where(full((r // 16) == (c // 16)), m, 0.0)
    mo = m - md
    x = eye - md
    p = _bmm(md, md)
    x = x + _bmm(x, p)
    p = _bmm(p, p)
    x = x + _bmm(x, p)
    p = _bmm(p, p)
    x = x + _bmm(x, p)
    n = _bmm(x, mo)
    y = x - _bmm(n, x)
    return y + _bmm(_bmm(n, n), y)


def _dn_chunk(q, k, v, gate, braw, araw, p_alog, p_dtb, p_nw, state):
    nh = DN_HEADS
    r, c, full = _chunk_masks(nh)
    q = q * lax.rsqrt(jnp.sum(q * q, axis=-1, keepdims=True) + RMS_EPS) * (HEAD ** -0.5)
    k = k * lax.rsqrt(jnp.sum(k * k, axis=-1, keepdims=True) + RMS_EPS)
    beta = jax.nn.sigmoid(braw)
    g = -jnp.exp(p_alog) * jax.nn.softplus(araw + p_dtb)
    gcs, decay, tot = _decay_terms(g, nh)
    kb = k * beta
    vb = v * beta
    m = jnp.where(full(r > c), _bmm_nt(kb, k) * decay, 0.0)
    t = _unit_lower_inverse(m, nh)
    egcs = jnp.exp(gcs)
    u = _bmm(t, vb)
    w = _bmm(t, kb * egcs)
    attn = _bmm_nt(q, k) * decay
    v_new = u - _bmm(w, state)
    out = _bmm(q * egcs, state) + _bmm(attn, v_new)
    new_state = state * jnp.exp(tot) + _bmm_tn(k * jnp.exp(tot - gcs), v_new)
    out = out * lax.rsqrt(jnp.mean(out * out, axis=-1, keepdims=True) + RMS_EPS) * p_nw
    return out * (gate * jax.nn.sigmoid(gate)), new_state


def _chunk_scan_fwd(chunk_fn, seqs, params, nh, name):
    tp = seqs[0].shape[1]
    nc = tp // CHUNK
    ns, npar = len(seqs), len(params)

    def body(*refs):
        s_refs, p_refs = refs[:ns], refs[ns:ns + npar]
        y_ref, st_ref, state = refs[ns + npar:]

        @pl.when(pl.program_id(0) == 0)
        def _():
            state[...] = jnp.zeros_like(state)
        st = state[...]
        st_ref[0] = st
        y, new = chunk_fn(*[r[...] for r in s_refs], *[r[...] for r in p_refs], st)
        y_ref[...] = y
        state[...] = new

    seq_spec = lambda a: pl.BlockSpec((a.shape[0], CHUNK, HEAD), lambda c: (0, c, 0))
    par_spec = lambda a: pl.BlockSpec(a.shape, lambda c: (0, 0, 0))
    return pl.pallas_call(
        body, name=name,
        out_shape=(_sds((nh, tp, HEAD), F32), _sds((nc, nh, HEAD, HEAD), F32)),
        grid=(nc,),
        in_specs=[seq_spec(a) for a in seqs] + [par_spec(a) for a in params],
        out_specs=[pl.BlockSpec((nh, CHUNK, HEAD), lambda c: (0, c, 0)),
                   pl.BlockSpec((1, nh, HEAD, HEAD), lambda c: (c, 0, 0, 0))],
        scratch_shapes=[pltpu.VMEM((nh, HEAD, HEAD), F32)],
        compiler_params=_params(("arbitrary",)),
    )(*seqs, *params)


def _chunk_scan_bwd(chunk_fn, seqs, params, states, dy, nh, name):
    tp = seqs[0].shape[1]
    nc = tp // CHUNK
    ns, npar = len(seqs), len(params)

    def body(*refs):
        s_refs, p_refs = refs[:ns], refs[ns:ns + npar]
        st_ref, dy_ref = refs[ns + npar:ns + npar + 2]
        ds_refs = refs[ns + npar + 2:2 * ns + npar + 2]
        dp_refs = refs[2 * ns + npar + 2:2 * ns + 2 * npar + 2]
        dstate = refs[-1]

        @pl.when(pl.program_id(0) == 0)
        def _():
            dstate[...] = jnp.zeros_like(dstate)
            for r in dp_refs:
                r[...] = jnp.zeros_like(r)
        _, vjp = jax.vjp(chunk_fn, *[r[...] for r in s_refs], *[r[...] for r in p_refs], st_ref[0])
        grads = vjp((dy_ref[...], dstate[...]))
        for r, gr in zip(ds_refs, grads[:ns]):
            r[...] = gr
        for r, gr in zip(dp_refs, grads[ns:ns + npar]):
            r[...] += gr
        dstate[...] = grads[-1]

    seq_spec = lambda a: pl.BlockSpec((a.shape[0], CHUNK, HEAD), lambda c: (0, nc - 1 - c, 0))
    par_spec = lambda a: pl.BlockSpec(a.shape, lambda c: (0, 0, 0))
    return pl.pallas_call(
        body, name=name,
        out_shape=tuple(_sds(a.shape, F32) for a in seqs) + tuple(_sds(a.shape, F32) for a in params),
        grid=(nc,),
        in_specs=[seq_spec(a) for a in seqs] + [par_spec(a) for a in params]
        + [pl.BlockSpec((1, nh, HEAD, HEAD), lambda c: (nc - 1 - c, 0, 0, 0)), seq_spec(dy)],
        out_specs=[seq_spec(a) for a in seqs] + [par_spec(a) for a in params],
        scratch_shapes=[pltpu.VMEM((nh, HEAD, HEAD), F32)],
        compiler_params=_params(("arbitrary",)),
    )(*seqs, *params, states, dy)


def _hm(a, nh):
    return a.reshape(a.shape[0], nh, HEAD).transpose(1, 0, 2)


def _hm_inv(a):
    return a.transpose(1, 0, 2).reshape(a.shape[1], a.shape[0] * HEAD)


def _hm_scalar(a):
    return jnp.broadcast_to(a.T[:, :, None], (a.shape[1], a.shape[0], HEAD))


def _lane_param(p):
    return jnp.broadcast_to(p[:, None, None], (p.shape[0], 1, HEAD))


def _block_diag(w):
    out = jnp.zeros((LRU_WIDTH, LRU_WIDTH), F32)
    for h in range(LRU_HEADS):
        out = out.at[h * HEAD:(h + 1) * HEAD, h * HEAD:(h + 1) * HEAD].set(w[h])
    return out


def _block_diag_inv(w):
    return jnp.stack([w[h * HEAD:(h + 1) * HEAD, h * HEAD:(h + 1) * HEAD] for h in range(LRU_HEADS)])


def _ssd_inputs(xc, proj, lp):
    seqs = (_hm(xc[:, 256:768], SSD_HEADS), _hm(xc[:, 768:896], SSD_GROUPS), _hm(xc[:, 896:1024], SSD_GROUPS),
            _hm_scalar(proj[:, OFF_SMALL:OFF_SMALL + 8]), _hm(proj[:, OFF_SSD_Z:OFF_SSD_Z + SSD_INNER], SSD_HEADS))
    params = (_lane_param(lp["ssd_dt_bias"]), _lane_param(lp["ssd_a_log"]), _lane_param(lp["ssd_d"]),
              lp["ssd_norm_w"].reshape(SSD_HEADS, 1, HEAD))
    return seqs, params


def _dn_inputs(xc, proj, lp):
    seqs = (_hm(xc[:, 1024:1280], DN_HEADS), _hm(xc[:, 1280:1536], DN_HEADS), _hm(xc[:, 1536:1792], DN_HEADS),
            _hm(proj[:, OFF_DN_GATE:OFF_DN_GATE + 256], DN_HEADS),
            _hm_scalar(proj[:, OFF_SMALL + 8:OFF_SMALL + 12]), _hm_scalar(proj[:, OFF_SMALL + 12:OFF_SMALL + 16]))
    params = (_lane_param(lp["dn_a_log"]), _lane_param(lp["dn_dt_bias"]),
              jnp.broadcast_to(lp["dn_norm_w"][None, None, :], (DN_HEADS, 1, HEAD)))
    return seqs, params


def _lru_params(lp):
    return (_block_diag(lp["lru_w_a"]), lp["lru_b_a"][None], _block_diag(lp["lru_w_x"]), lp["lru_b_x"][None],
            lp["lru_lambda"][None])


def _conv_params(lp):
    w = jnp.concatenate([lp["lru_conv_w"], lp["ssd_conv_w"], lp["dn_conv_w"]], axis=1)
    b = jnp.concatenate([lp["lru_conv_b"], lp["ssd_conv_b"], jnp.zeros((768,), F32)])[None]
    return w, b


def _mixers_fwd(proj, lp, tag):
    cw, cb = _conv_params(lp)
    xc = _conv_fwd(proj, cw, cb, tag + "_conv")
    o_lru, h_lru = _lru_fwd(xc, proj, *_lru_params(lp), name=tag + "_lru")
    s_seqs, s_par = _ssd_inputs(xc, proj, lp)
    o_ssd, s_states = _chunk_scan_fwd(_ssd_chunk, s_seqs, s_par, SSD_HEADS, tag + "_ssd")
    d_seqs, d_par = _dn_inputs(xc, proj, lp)
    o_dn, d_states = _chunk_scan_fwd(_dn_chunk, d_seqs, d_par, DN_HEADS, tag + "_dn")
    cat = jnp.concatenate([o_lru, _hm_inv(o_ssd), _hm_inv(o_dn)], axis=1)
    return cat, (xc, h_lru, s_states, d_states)


def _mixers_bwd(proj, lp, saved, dcat, tag):
    xc, h_lru, s_states, d_states = saved
    tp = proj.shape[0]
    cw, cb = _conv_params(lp)
    g = {}
    du, dyraw, dwa, dba, dwx, dbx, dlam = _lru_bwd(xc, proj, h_lru, dcat[:, :LRU_WIDTH], *_lru_params(lp), name=tag + "_lru_bwd")
    g["lru_w_a"], g["lru_b_a"], g["lru_w_x"], g["lru_b_x"], g["lru_lambda"] = (
        _block_diag_inv(dwa), dba[0], _block_diag_inv(dwx), dbx[0], dlam[0])

    s_seqs, s_par = _ssd_inputs(xc, proj, lp)
    sg = _chunk_scan_bwd(_ssd_chunk, s_seqs, s_par, s_states, _hm(dcat[:, 256:768], SSD_HEADS), SSD_HEADS, tag + "_ssd_bwd")
    dxs, dbm, dcm, ddtr, dz, dp_dtb, dp_alog, dp_d, dp_nw = sg
    g["ssd_dt_bias"], g["ssd_a_log"], g["ssd_d"] = (jnp.sum(p, axis=(1, 2)) for p in (dp_dtb, dp_alog, dp_d))
    g["ssd_norm_w"] = dp_nw.reshape(SSD_INNER)

    d_seqs, d_par = _dn_inputs(xc, proj, lp)
    dg = _chunk_scan_bwd(_dn_chunk, d_seqs, d_par, d_states, _hm(dcat[:, 768:1024], DN_HEADS), DN_HEADS, tag + "_dn_bwd")
    dq, dk, dv, dgate, dbraw, daraw, dq_alog, dq_dtb, dq_nw = dg
    g["dn_a_log"], g["dn_dt_bias"] = (jnp.sum(p, axis=(1, 2)) for p in (dq_alog, dq_dtb))
    g["dn_norm_w"] = jnp.sum(dq_nw, axis=(0, 1))

    dxc = jnp.concatenate([du, _hm_inv(dxs), _hm_inv(dbm), _hm_inv(dcm), _hm_inv(dq), _hm_inv(dk), _hm_inv(dv)], axis=1)
    dconv, dcw, dcb = _conv_bwd(proj, cw, cb, dxc, tag + "_conv_bwd")
    g["lru_conv_w"], g["ssd_conv_w"], g["dn_conv_w"] = dcw[:, :256], dcw[:, 256:1024], dcw[:, 1024:]
    g["lru_conv_b"], g["ssd_conv_b"] = dcb[0, :256], dcb[0, 256:1024]
    scal = lambda a: jnp.sum(a, axis=2).T
    dsmall = jnp.concatenate([scal(ddtr), scal(dbraw), scal(daraw), jnp.zeros((tp, LANE - 16), F32)], axis=1)
    dproj = jnp.concatenate([dconv, dyraw, _hm_inv(dz), _hm_inv(dgate), dsmall], axis=1)
    return dproj, g


MIXER_PARAMS = ("lru_conv_w", "lru_conv_b", "lru_w_a", "lru_b_a", "lru_w_x", "lru_b_x", "lru_lambda",
                "ssd_conv_w", "ssd_conv_b", "ssd_dt_bias", "ssd_a_log", "ssd_d", "ssd_norm_w",
                "dn_conv_w", "dn_a_log", "dn_dt_bias", "dn_norm_w")


def _local_step(x, tgt, small, wgu, wd, win, wout):
    s = x.shape[0]
    t_real = N_META + s
    tp = -(-t_real // CHUNK) * CHUNK
    depth = wgu.shape[0]
    h = jnp.concatenate([small["meta"], x, jnp.zeros((tp - t_real, D_MODEL), F32)], axis=0)
    hb = h.astype(BF16)
    ln_g, ln_b = small["ln_g"], small["ln_b"]
    saved = []
    for l in range(depth):
        lp = {k: small[k][l] for k in MIXER_PARAMS}
        t = f"l{l}"
        g0, u0, a0 = _ffn_up(hb, wgu[l, 0], t + "_ffn0_up")
        y1, h1, h1b = _mm_resid_ln(a0, wd[l, 0], h, ln_g[l, 0][None], ln_b[l, 0][None], FFN_RES, t + "_ffn0_down")
        proj = _mm_nn(h1b, win[l], t + "_in_proj")
        cat, mix_saved = _mixers_fwd(proj, lp, t)
        catb = cat.astype(BF16)
        y2, h2, h2b = _mm_resid_ln(catb, wout[l], h1, ln_g[l, 1][None], ln_b[l, 1][None], 1.0, t + "_out_proj")
        g1, u1, a1 = _ffn_up(h2b, wgu[l, 1], t + "_ffn1_up")
        y3, h3, h3b = _mm_resid_ln(a1, wd[l, 1], h2, ln_g[l, 2][None], ln_b[l, 2][None], FFN_RES, t + "_ffn1_down")
        saved.append((hb, g0, u0, a0, y1, h1b, proj, mix_saved, catb, y2, h2b, g1, u1, a1, y3))
        h, hb = h3, h3b

    tgt_p = jnp.pad(tgt, ((N_META, tp - t_real), (0, 0)))
    lossv, dh = _loss_grad(h, tgt_p, t_real, "loss")
    loss = jnp.sum(lossv)

    gs = {k: [None] * depth for k in MIXER_PARAMS}
    d_ln_g = [[None] * 3 for _ in range(depth)]
    d_ln_b = [[None] * 3 for _ in range(depth)]
    d_wgu = [[None] * 2 for _ in range(depth)]
    d_wd = [[None] * 2 for _ in range(depth)]
    d_win = [None] * depth
    d_wout = [None] * depth

    def ffn_bwd(l, j, xb_in, g, u, a, y, dout):
        t = f"l{l}_ffn{j}"
        dres, dyb, dgam, dbet = _ln_bwd(y, dout, ln_g[l, 2 * j][None], FFN_RES, t + "_ln_bwd")
        d_ln_g[l][2 * j], d_ln_b[l][2 * j] = dgam[0], dbet[0]
        d_wd[l][j] = _mm_tn(a, dyb, t + "_dwd")
        dg, du = _ffn_dact(dyb, wd[l, j], g, u, t + "_dact")
        d_wgu[l][j] = jnp.concatenate([_mm_tn(xb_in, dg, t + "_dwg"), _mm_tn(xb_in, du, t + "_dwu")], axis=1)
        return _ffn_dx(dg, du, wgu[l, j], dres, t + "_dx")

    for l in reversed(range(depth)):
        hb_in, g0, u0, a0, y1, h1b, proj, mix_saved, catb, y2, h2b, g1, u1, a1, y3 = saved[l]
        lp = {k: small[k][l] for k in MIXER_PARAMS}
        t = f"l{l}"
        dh2 = ffn_bwd(l, 1, h2b, g1, u1, a1, y3, dh)
        dres, dyb, dgam, dbet = _ln_bwd(y2, dh2, ln_g[l, 1][None], 1.0, t + "_mix_ln_bwd")
        d_ln_g[l][1], d_ln_b[l][1] = dgam[0], dbet[0]
        d_wout[l] = _mm_tn(catb, dyb, t + "_dwout")
        dcat = _mm_nt_add(dyb, wout[l], None, t + "_dcat")
        dproj, mg = _mixers_bwd(proj, lp, mix_saved, dcat, t)
        for k in MIXER_PARAMS:
            gs[k][l] = mg[k]
        dprojb = dproj.astype(BF16)
        d_win[l] = _mm_tn(h1b, dprojb, t + "_dwin")
        dh1 = _mm_nt_add(dprojb, win[l], dres, t + "_dh1")
        dh = ffn_bwd(l, 0, hb_in, g0, u0, a0, y1, dh1)

    small_grads = {k: jnp.stack(v) for k, v in gs.items()}
    small_grads["ln_g"] = jnp.stack([jnp.stack(r) for r in d_ln_g])
    small_grads["ln_b"] = jnp.stack([jnp.stack(r) for r in d_ln_b])
    small_grads["meta"] = dh[:N_META]
    big = (jnp.stack([jnp.stack(r) for r in d_wgu]), jnp.stack([jnp.stack(r) for r in d_wd]),
           jnp.stack(d_win), jnp.stack(d_wout))
    return loss, dh[N_META:t_real], small_grads, big


def _mesh_pos():
    return lax.axis_index("x"), lax.axis_index("y"), lax.axis_index("c")


def _flip(pos, k):
    x, y, c = pos
    return (1 - x if k & 4 else x, 1 - y if k & 2 else y, 1 - c if k & 1 else c)


def _flat(pos):
    return 4 * pos[0] + 2 * pos[1] + pos[2]


def _exchange(arrs, slabs, name):
    n = len(arrs)
    npeer = N_DEV - 1

    def body(*refs):
        ins, outs = refs[:n], refs[n:2 * n]
        send, recv, loc = refs[2 * n:]
        pos = _mesh_pos()
        me = _flat(pos)
        local = []
        for a in range(n):
            src = ins[a].at[me] if slabs[a] else ins[a]
            cp = pltpu.make_async_copy(src, outs[a].at[me], loc.at[a])
            cp.start()
            local.append(cp)
        remote = []
        for k in range(1, N_DEV):
            peer = _flip(pos, k)
            for a in range(n):
                src = ins[a].at[_flat(peer)] if slabs[a] else ins[a]
                cp = pltpu.make_async_remote_copy(src_ref=src, dst_ref=outs[a].at[me], send_sem=send.at[a, k - 1],
                                                  recv_sem=recv.at[a, k - 1], device_id=peer, device_id_type=MESH_IDS)
                cp.start()
                remote.append((a, k, cp))
        for a, k, cp in remote:
            peer = _flip(pos, k)
            src = ins[a].at[me] if slabs[a] else ins[a]
            pltpu.make_async_remote_copy(src_ref=src, dst_ref=outs[a].at[_flat(peer)], send_sem=send.at[a, k - 1],
                                         recv_sem=recv.at[a, k - 1], device_id=peer, device_id_type=MESH_IDS).wait_recv()
        for a, k, cp in remote:
            cp.wait_send()
        for cp in local:
            cp.wait()

    hbm = pl.BlockSpec(memory_space=pl.ANY)
    out_shape = tuple(_sds(a.shape if s else (N_DEV,) + a.shape, a.dtype) for a, s in zip(arrs, slabs))
    return pl.pallas_call(
        body, name=name, out_shape=out_shape, in_specs=[hbm] * n, out_specs=[hbm] * n,
        scratch_shapes=[pltpu.SemaphoreType.DMA((n, npeer)), pltpu.SemaphoreType.DMA((n, npeer)),
                        pltpu.SemaphoreType.DMA((n,))],
    )(*arrs)


def _adam_math(w, g, m, v):
    m = ADAM_B1 * m + (1.0 - ADAM_B1) * g
    v = ADAM_B2 * v + (1.0 - ADAM_B2) * (g * g)
    m_hat = m / (1.0 - ADAM_B1 ** ADAM_STEP)
    v_hat = v / (1.0 - ADAM_B2 ** ADAM_STEP)
    delta = -ADAM_LR * (m_hat / (jnp.sqrt(v_hat) + ADAM_EPS) + ADAM_WD * w)
    return delta, m, v


def _adam(w, g, m, v, name):
    r, c = w.shape
    parts = g.ndim == 3
    tr = _tile(r, 512, 8)

    def body(w_ref, g_ref, m_ref, v_ref, go_ref, d_ref, mo_ref, vo_ref):
        if parts:
            gv = g_ref[0].astype(F32)
            for d in range(1, N_DEV):
                gv = gv + g_ref[d].astype(F32)
        else:
            gv = g_ref[...]
        delta, mn, vn = _adam_math(w_ref[...], gv, m_ref[...], v_ref[...])
        go_ref[...] = gv
        d_ref[...] = delta
        mo_ref[...] = mn
        vo_ref[...] = vn

    blk = pl.BlockSpec((tr, c), lambda i: (i, 0))
    gblk = pl.BlockSpec((N_DEV, tr, c), lambda i: (0, i, 0)) if parts else blk
    out = _sds((r, c), F32)
    return pl.pallas_call(
        body, name=name, out_shape=(out,) * 4, grid=(r // tr,),
        in_specs=[blk, gblk, blk, blk], out_specs=[blk] * 4,
        compiler_params=_params(("arbitrary",)),
    )(w, g, m, v)


def _sum_parts(parts, name):
    _, r, c = parts.shape

    def body(p_ref, o_ref):
        acc = p_ref[0]
        for d in range(1, N_DEV):
            acc = acc + p_ref[d]
        o_ref[...] = acc

    return pl.pallas_call(body, name=name, out_shape=_sds((r, c), F32), compiler_params=_params())(parts)


SMALL_SHARD_AXIS = {
    "meta": 1, "ln_g": 2, "ln_b": 2, "lru_conv_w": 2, "lru_conv_b": None, "lru_w_a": None, "lru_b_a": None,
    "lru_w_x": None, "lru_b_x": None, "lru_lambda": None, "ssd_conv_w": 2, "ssd_conv_b": None, "ssd_dt_bias": None,
    "ssd_a_log": None, "ssd_d": None, "ssd_norm_w": None, "dn_conv_w": 2, "dn_a_log": None, "dn_dt_bias": None,
    "dn_norm_w": None,
}
BIG = ("ffn_w_gate", "ffn_w_up", "ffn_w_down", "w_in", "w_out")
WEIGHT_ORDER = ("meta", "ln_g", "ln_b", "ffn_w_gate", "ffn_w_up", "ffn_w_down", "w_in", "lru_conv_w", "lru_conv_b",
                "lru_w_a", "lru_b_a", "lru_w_x", "lru_b_x", "lru_lambda", "ssd_conv_w", "ssd_conv_b", "ssd_dt_bias",
                "ssd_a_log", "ssd_d", "ssd_norm_w", "dn_conv_w", "dn_a_log", "dn_dt_bias", "dn_norm_w", "w_out")


def _pack(arrs):
    flat = jnp.concatenate([a.reshape(-1) for a in arrs])
    rows = -(-flat.shape[0] // (8 * LANE)) * 8
    return jnp.pad(flat, (0, rows * LANE - flat.shape[0])).reshape(rows, LANE)


def _unpack(buf, shapes, lead=()):
    flat = buf.reshape(lead + (-1,))
    out, off = [], 0
    for s in shapes:
        n = math.prod(s)
        out.append(flat[..., off:off + n].reshape(lead + tuple(s)))
        off += n
    return out


def _proj_cols(w):
    pad = jnp.zeros(w.shape[:-1] + (PROJ_W - D_IN,), w.dtype)
    return jnp.concatenate([w[..., 0:256], w[..., 1024:1792], w[..., 1800:2568], w[..., 256:512], w[..., 512:1024],
                            w[..., 2568:2824], w[..., 1792:1800], w[..., 2824:2832], pad], axis=-1)


def _proj_cols_inv(w):
    return jnp.concatenate([w[..., 0:256], w[..., 1792:2048], w[..., 2048:2560], w[..., 256:1024], w[..., 2816:2824],
                            w[..., 1024:1792], w[..., 2560:2816], w[..., 2824:2832]], axis=-1)


def kernel(x, meta, ln_g, ln_b, ffn_w_gate, ffn_w_up, ffn_w_down, w_in, lru_conv_w, lru_conv_b, lru_w_a, lru_b_a, lru_w_x, lru_b_x, lru_lambda, ssd_conv_w, ssd_conv_b, ssd_dt_bias, ssd_a_log, ssd_d, ssd_norm_w, dn_conv_w, dn_a_log, dn_dt_bias, dn_norm_w, w_out, loss_target, m_meta, m_ln_g, m_ln_b, m_ffn_w_gate, m_ffn_w_up, m_ffn_w_down, m_w_in, m_lru_conv_w, m_lru_conv_b, m_lru_w_a, m_lru_b_a, m_lru_w_x, m_lru_b_x, m_lru_lambda, m_ssd_conv_w, m_ssd_conv_b, m_ssd_dt_bias, m_ssd_a_log, m_ssd_d, m_ssd_norm_w, m_dn_conv_w, m_dn_a_log, m_dn_dt_bias, m_dn_norm_w, m_w_out, v_meta, v_ln_g, v_ln_b, v_ffn_w_gate, v_ffn_w_up, v_ffn_w_down, v_w_in, v_lru_conv_w, v_lru_conv_b, v_lru_w_a, v_lru_b_a, v_lru_w_x, v_lru_b_x, v_lru_lambda, v_ssd_conv_w, v_ssd_conv_b, v_ssd_dt_bias, v_ssd_a_log, v_ssd_d, v_ssd_norm_w, v_dn_conv_w, v_dn_a_log, v_dn_dt_bias, v_dn_norm_w, v_w_out):
    w = dict(meta=meta, ln_g=ln_g, ln_b=ln_b, ffn_w_gate=ffn_w_gate, ffn_w_up=ffn_w_up, ffn_w_down=ffn_w_down, w_in=w_in,
             lru_conv_w=lru_conv_w, lru_conv_b=lru_conv_b, lru_w_a=lru_w_a, lru_b_a=lru_b_a, lru_w_x=lru_w_x,
             lru_b_x=lru_b_x, lru_lambda=lru_lambda, ssd_conv_w=ssd_conv_w, ssd_conv_b=ssd_conv_b, ssd_dt_bias=ssd_dt_bias,
             ssd_a_log=ssd_a_log, ssd_d=ssd_d, ssd_norm_w=ssd_norm_w, dn_conv_w=dn_conv_w, dn_a_log=dn_a_log,
             dn_dt_bias=dn_dt_bias, dn_norm_w=dn_norm_w, w_out=w_out)
    m = dict(meta=m_meta, ln_g=m_ln_g, ln_b=m_ln_b, ffn_w_gate=m_ffn_w_gate, ffn_w_up=m_ffn_w_up, ffn_w_down=m_ffn_w_down,
             w_in=m_w_in, lru_conv_w=m_lru_conv_w, lru_conv_b=m_lru_conv_b, lru_w_a=m_lru_w_a, lru_b_a=m_lru_b_a,
             lru_w_x=m_lru_w_x, lru_b_x=m_lru_b_x, lru_lambda=m_lru_lambda, ssd_conv_w=m_ssd_conv_w, ssd_conv_b=m_ssd_conv_b,
             ssd_dt_bias=m_ssd_dt_bias, ssd_a_log=m_ssd_a_log, ssd_d=m_ssd_d, ssd_norm_w=m_ssd_norm_w, dn_conv_w=m_dn_conv_w,
             dn_a_log=m_dn_a_log, dn_dt_bias=m_dn_dt_bias, dn_norm_w=m_dn_norm_w, w_out=m_w_out)
    v = dict(meta=v_meta, ln_g=v_ln_g, ln_b=v_ln_b, ffn_w_gate=v_ffn_w_gate, ffn_w_up=v_ffn_w_up, ffn_w_down=v_ffn_w_down,
             w_in=v_w_in, lru_conv_w=v_lru_conv_w, lru_conv_b=v_lru_conv_b, lru_w_a=v_lru_w_a, lru_b_a=v_lru_b_a,
             lru_w_x=v_lru_w_x, lru_b_x=v_lru_b_x, lru_lambda=v_lru_lambda, ssd_conv_w=v_ssd_conv_w, ssd_conv_b=v_ssd_conv_b,
             ssd_dt_bias=v_ssd_dt_bias, ssd_a_log=v_ssd_a_log, ssd_d=v_ssd_d, ssd_norm_w=v_ssd_norm_w, dn_conv_w=v_dn_conv_w,
             dn_a_log=v_dn_a_log, dn_dt_bias=v_dn_dt_bias, dn_norm_w=v_dn_norm_w, w_out=v_w_out)
    depth = ln_g.shape[0]
    me = _flat(_mesh_pos())
    small_names = tuple(SMALL_SHARD_AXIS)
    sharded = tuple(k for k in small_names if SMALL_SHARD_AXIS[k] is not None)

    got = _exchange([w[k].astype(BF16) for k in BIG] + [_pack([w[k] for k in sharded])], [False] * 6, "gather_weights")
    g_gate, g_up, g_down, g_in, g_out, g_small = got
    f_loc = g_gate.shape[-1]
    gate_full = g_gate.transpose(1, 2, 3, 0, 4).reshape(depth, 2, D_MODEL, N_DEV * f_loc)
    up_full = g_up.transpose(1, 2, 3, 0, 4).reshape(depth, 2, D_MODEL, N_DEV * f_loc)
    wgu = jnp.concatenate([gate_full, up_full], axis=-1)
    wd = g_down.transpose(1, 2, 0, 3, 4).reshape(depth, 2, N_DEV * f_loc, D_MODEL)
    win = _proj_cols(g_in.transpose(1, 2, 0, 3).reshape(depth, D_MODEL, D_IN))
    wout = g_out.transpose(1, 0, 2, 3).reshape(depth, D_MODEL, D_MODEL)
    small = {k: w[k] for k in small_names if SMALL_SHARD_AXIS[k] is None}
    for k, piece in zip(sharded, _unpack(g_small, [w[k].shape for k in sharded], lead=(N_DEV,))):
        ax = SMALL_SHARD_AXIS[k]
        full = jnp.moveaxis(piece, 0, ax)
        small[k] = full.reshape(full.shape[:ax] + (N_DEV * w[k].shape[ax],) + full.shape[ax + 2:])

    loss, dx, small_grads, (d_wgu, d_wd, d_win, d_wout) = _local_step(x[0], loss_target[0], small, wgu, wd, win, wout)

    f_full = N_DEV * f_loc
    slab = lambda a: a.astype(BF16)
    p_gate = slab(d_wgu[..., :f_full].reshape(depth, 2, D_MODEL, N_DEV, f_loc).transpose(3, 0, 1, 2, 4))
    p_up = slab(d_wgu[..., f_full:].reshape(depth, 2, D_MODEL, N_DEV, f_loc).transpose(3, 0, 1, 2, 4))
    p_down = slab(d_wd.reshape(depth, 2, N_DEV, f_loc, D_MODEL).transpose(2, 0, 1, 3, 4))
    p_in = slab(_proj_cols_inv(d_win).reshape(depth, D_MODEL, N_DEV, D_IN // N_DEV).transpose(2, 0, 1, 3))
    p_out = slab(d_wout.reshape(depth, N_DEV, D_MODEL // N_DEV, D_MODEL).transpose(1, 0, 2, 3))
    p_small = _pack([small_grads[k] for k in small_names])
    r_gate, r_up, r_down, r_in, r_out, r_small = _exchange(
        [p_gate, p_up, p_down, p_in, p_out, p_small], [True] * 5 + [False], "exchange_grads")

    outs = {}
    for k, parts in zip(BIG, (r_gate, r_up, r_down, r_in, r_out)):
        shp = w[k].shape
        two = lambda a: a.reshape(-1, shp[-1])
        res = _adam(two(w[k]), parts.reshape(N_DEV, -1, shp[-1]), two(m[k]), two(v[k]), "adam_" + k)
        outs[k] = [a.reshape(shp) for a in res]
    g_full = _unpack(_sum_parts(r_small, "sum_small_grads"), [small[k].shape for k in small_names])
    g_loc = {}
    for k, gf in zip(small_names, g_full):
        ax = SMALL_SHARD_AXIS[k]
        g_loc[k] = gf if ax is None else lax.dynamic_slice_in_dim(gf, me * w[k].shape[ax], w[k].shape[ax], axis=ax)
    res = _adam(_pack([w[k] for k in small_names]), _pack([g_loc[k] for k in small_names]),
                _pack([m[k] for k in small_names]), _pack([v[k] for k in small_names]), "adam_small")
    shapes = [w[k].shape for k in small_names]
    for i, k in enumerate(small_names):
        outs[k] = [_unpack(r, shapes)[i] for r in res]

    loss = lax.psum(loss, ("x", "y", "c"))
    return (loss, dx[None], *[outs[k][0] for k in WEIGHT_ORDER], *[outs[k][1] for k in WEIGHT_ORDER],
            *[outs[k][2] for k in WEIGHT_ORDER], *[outs[k][3] for k in WEIGHT_ORDER])
```

```python
import functools
import math

import jax
import jax.numpy as jnp
from jax import lax
from jax.experimental import pallas as pl
from jax.experimental.pallas import tpu as pltpu

F32 = jnp.float32
BF16 = jnp.bfloat16
HI = lax.Precision.HIGHEST
MESH_IDS = pl.DeviceIdType.MESH

N_DEV = 8
D_MODEL = 1024
DEPTH = 4
N_META = 16
CHUNK = 64
CONV_K = 4
D_FF = 2816
LRU_WIDTH = 256
LRU_HEADS = 4
LRU_C = 8.0
SSD_HEADS = 8
SSD_GROUPS = 2
SSD_HPG = 4
SSD_INNER = 512
DN_HEADS = 4
HEAD = 64
CONV_W = 1792
PROJ_W = 2944
OFF_LRU_Y, OFF_SSD_Z, OFF_DN_GATE, OFF_SMALL = 1792, 2048, 2560, 2816
D_IN = 2832
ALPHA = (2 * DEPTH) ** 0.25
FFN_RES = 0.5
LN_EPS = 1e-5
RMS_EPS = 1e-6
ADAM_LR, ADAM_B1, ADAM_B2, ADAM_EPS, ADAM_WD, ADAM_STEP = 0.001, 0.9, 0.999, 1e-08, 0.01, 10

VMEM_LIMIT = 56 * 1024 * 1024
ROW_TILE_CAP = 416
LANE = 128


def _tile(n, cap, mult=16):
    best = None
    for t in range(mult, min(n, cap) + 1, mult):
        if n % t == 0:
            best = t
    assert best is not None, (n, cap, mult)
    return best


def _params(sem=None):
    return pltpu.CompilerParams(dimension_semantics=sem, vmem_limit_bytes=VMEM_LIMIT)


def _resident(shape, index_map):
    return pl.BlockSpec(shape, index_map, pipeline_mode=pl.Buffered(1))


def _dot(a, b):
    return jnp.dot(a, b, preferred_element_type=F32)


def _dot_nt(a, b):
    return lax.dot_general(a, b, (((1,), (1,)), ((), ())), preferred_element_type=F32)


def _dot_tn(a, b):
    return lax.dot_general(a, b, (((0,), (0,)), ((), ())), preferred_element_type=F32)


def _sds(shape, dtype):
    return jax.ShapeDtypeStruct(shape, dtype)


def _ffn_up(xb, wgu, name):
    tp, d = xb.shape
    f = wgu.shape[1] // 2
    tm = _tile(tp, ROW_TILE_CAP)
    tn = _tile(f, 1408, LANE)
    nj = f // tn

    def body(x_ref, wg_ref, wu_ref, g_ref, u_ref, a_ref):
        x = x_ref[...]
        g = _dot(x, wg_ref[...])
        u = _dot(x, wu_ref[...])
        g_ref[...] = g.astype(BF16)
        u_ref[...] = u.astype(BF16)
        a_ref[...] = (g * jax.nn.sigmoid(g) * u).astype(BF16)

    out = _sds((tp, f), BF16)
    return pl.pallas_call(
        body, name=name, out_shape=(out, out, out), grid=(nj, tp // tm),
        in_specs=[pl.BlockSpec((tm, d), lambda j, i: (i, 0)),
                  pl.BlockSpec((d, tn), lambda j, i: (0, j)),
                  pl.BlockSpec((d, tn), lambda j, i: (0, j + nj))],
        out_specs=[pl.BlockSpec((tm, tn), lambda j, i: (i, j))] * 3,
        compiler_params=_params(("arbitrary", "arbitrary")),
    )(xb, wgu, wgu)


def _mm_resid_ln(a, w, h, gamma, beta, scale, name):
    tp, k = a.shape
    d = w.shape[1]
    tm = _tile(tp, ROW_TILE_CAP)

    def body(a_ref, w_ref, h_ref, g_ref, b_ref, y_ref, o_ref, ob_ref):
        y = ALPHA * h_ref[...] + scale * _dot(a_ref[...], w_ref[...])
        mu = jnp.mean(y, axis=-1, keepdims=True)
        yc = y - mu
        var = jnp.mean(yc * yc, axis=-1, keepdims=True)
        o = yc * lax.rsqrt(var + LN_EPS) * g_ref[...] + b_ref[...]
        y_ref[...] = y
        o_ref[...] = o
        ob_ref[...] = o.astype(BF16)

    row = lambda i: (i, 0)
    fix = lambda i: (0, 0)
    return pl.pallas_call(
        body, name=name, out_shape=(_sds((tp, d), F32), _sds((tp, d), F32), _sds((tp, d), BF16)),
        grid=(tp // tm,),
        in_specs=[pl.BlockSpec((tm, k), row), _resident((k, d), fix), pl.BlockSpec((tm, d), row),
                  pl.BlockSpec((1, d), fix), pl.BlockSpec((1, d), fix)],
        out_specs=[pl.BlockSpec((tm, d), row)] * 3,
        compiler_params=_params(("arbitrary",)),
    )(a, w, h, gamma, beta)


def _mm_nn(xb, w, name):
    tp, k = xb.shape
    n = w.shape[1]
    tm = _tile(tp, ROW_TILE_CAP)

    def body(x_ref, w_ref, o_ref):
        o_ref[...] = _dot(x_ref[...], w_ref[...])

    return pl.pallas_call(
        body, name=name, out_shape=_sds((tp, n), F32), grid=(tp // tm,),
        in_specs=[pl.BlockSpec((tm, k), lambda i: (i, 0)), _resident((k, n), lambda i: (0, 0))],
        out_specs=pl.BlockSpec((tm, n), lambda i: (i, 0)),
        compiler_params=_params(("arbitrary",)),
    )(xb, w)


def _mm_nt_add(a, w, resid, name):
    tp, k = a.shape
    n = w.shape[0]
    tm = _tile(tp, ROW_TILE_CAP)
    has_resid = resid is not None

    def body(*refs):
        if has_resid:
            a_ref, w_ref, r_ref, o_ref = refs
            o_ref[...] = r_ref[...] + _dot_nt(a_ref[...], w_ref[...])
        else:
            a_ref, w_ref, o_ref = refs
            o_ref[...] = _dot_nt(a_ref[...], w_ref[...])

    in_specs = [pl.BlockSpec((tm, k), lambda i: (i, 0)), _resident((n, k), lambda i: (0, 0))]
    args = [a, w]
    if has_resid:
        in_specs.append(pl.BlockSpec((tm, n), lambda i: (i, 0)))
        args.append(resid)
    return pl.pallas_call(
        body, name=name, out_shape=_sds((tp, n), F32), grid=(tp // tm,),
        in_specs=in_specs, out_specs=pl.BlockSpec((tm, n), lambda i: (i, 0)),
        compiler_params=_params(("arbitrary",)),
    )(*args)


def _ffn_dx(dg, du, wgu, resid, name):
    tp, f = dg.shape
    d = wgu.shape[0]
    tm = _tile(tp, ROW_TILE_CAP)

    def body(dg_ref, du_ref, w_ref, r_ref, o_ref):
        acc = r_ref[...] + _dot_nt(dg_ref[...], w_ref[:, 0:f])
        o_ref[...] = acc + _dot_nt(du_ref[...], w_ref[:, f:2 * f])

    row = lambda i: (i, 0)
    return pl.pallas_call(
        body, name=name, out_shape=_sds((tp, d), F32), grid=(tp // tm,),
        in_specs=[pl.BlockSpec((tm, f), row), pl.BlockSpec((tm, f), row), _resident((d, 2 * f), lambda i: (0, 0)),
                  pl.BlockSpec((tm, d), row)],
        out_specs=pl.BlockSpec((tm, d), row),
        compiler_params=_params(("arbitrary",)),
    )(dg, du, wgu, resid)


def _mm_tn(a, b, name):
    tp, ka = a.shape
    nb = b.shape[1]
    tt = _tile(tp, 2 * ROW_TILE_CAP)
    tk = _tile(ka, 1408, LANE)
    tn = _tile(nb, 1536, LANE)
    if tn < 512:
        tn = nb
    nt = tp // tt

    def body(a_ref, b_ref, o_ref):
        @pl.when(pl.program_id(2) == 0)
        def _():
            o_ref[...] = jnp.zeros_like(o_ref)
        o_ref[...] += _dot_tn(a_ref[...], b_ref[...])

    return pl.pallas_call(
        body, name=name, out_shape=_sds((ka, nb), F32), grid=(ka // tk, nb // tn, nt),
        in_specs=[pl.BlockSpec((tt, tk), lambda i, j, t: (t, i)), pl.BlockSpec((tt, tn), lambda i, j, t: (t, j))],
        out_specs=pl.BlockSpec((tk, tn), lambda i, j, t: (i, j)),
        compiler_params=_params(("arbitrary", "arbitrary", "arbitrary")),
    )(a, b)


def _ln_bwd(y, dout, gamma, scale, name):
    tp, d = y.shape
    tm = _tile(tp, ROW_TILE_CAP)

    def body(y_ref, do_ref, g_ref, dres_ref, dyb_ref, dg_ref, db_ref):
        @pl.when(pl.program_id(0) == 0)
        def _():
            dg_ref[...] = jnp.zeros_like(dg_ref)
            db_ref[...] = jnp.zeros_like(db_ref)
        yv = y_ref[...]
        do = do_ref[...]
        mu = jnp.mean(yv, axis=-1, keepdims=True)
        yc = yv - mu
        var = jnp.mean(yc * yc, axis=-1, keepdims=True)
        rstd = lax.rsqrt(var + LN_EPS)
        xhat = yc * rstd
        dxh = do * g_ref[...]
        m1 = jnp.mean(dxh, axis=-1, keepdims=True)
        m2 = jnp.mean(dxh * xhat, axis=-1, keepdims=True)
        dy = rstd * (dxh - m1 - xhat * m2)
        dres_ref[...] = ALPHA * dy
        dyb_ref[...] = (scale * dy).astype(BF16)
        dg_ref[...] += jnp.sum(do * xhat, axis=0, keepdims=True)
        db_ref[...] += jnp.sum(do, axis=0, keepdims=True)

    row = lambda i: (i, 0)
    fix = lambda i: (0, 0)
    return pl.pallas_call(
        body, name=name,
        out_shape=(_sds((tp, d), F32), _sds((tp, d), BF16), _sds((1, d), F32), _sds((1, d), F32)),
        grid=(tp // tm,),
        in_specs=[pl.BlockSpec((tm, d), row), pl.BlockSpec((tm, d), row), pl.BlockSpec((1, d), fix)],
        out_specs=[pl.BlockSpec((tm, d), row), pl.BlockSpec((tm, d), row), pl.BlockSpec((1, d), fix),
                   pl.BlockSpec((1, d), fix)],
        compiler_params=_params(("arbitrary",)),
    )(y, dout, gamma)


def _ffn_dact(dyb, wd, g, u, name):
    tp, d = dyb.shape
    f = wd.shape[0]
    tm = _tile(tp, ROW_TILE_CAP)
    tn = _tile(f, 1408, LANE)

    def body(dy_ref, w_ref, g_ref, u_ref, dg_ref, du_ref):
        dact = _dot_nt(dy_ref[...], w_ref[...])
        gv = g_ref[...].astype(F32)
        uv = u_ref[...].astype(F32)
        sg = jax.nn.sigmoid(gv)
        dg_ref[...] = (dact * uv * (sg * (1.0 + gv * (1.0 - sg)))).astype(BF16)
        du_ref[...] = (dact * (gv * sg)).astype(BF16)

    out = _sds((tp, f), BF16)
    blk = pl.BlockSpec((tm, tn), lambda j, i: (i, j))
    return pl.pallas_call(
        body, name=name, out_shape=(out, out), grid=(f // tn, tp // tm),
        in_specs=[pl.BlockSpec((tm, d), lambda j, i: (i, 0)), pl.BlockSpec((tn, d), lambda j, i: (j, 0)), blk, blk],
        out_specs=[blk, blk],
        compiler_params=_params(("arbitrary", "arbitrary")),
    )(dyb, wd, g, u)


def _loss_grad(o, tgt, t_real, name):
    tp, d = o.shape
    tm = _tile(tp, ROW_TILE_CAP)

    def body(o_ref, t_ref, l_ref, d_ref):
        i = pl.program_id(0)

        @pl.when(i == 0)
        def _():
            l_ref[...] = jnp.zeros_like(l_ref)
        rows = i * tm + lax.broadcasted_iota(jnp.int32, (tm, 1), 0)
        real = jnp.logical_and(rows >= N_META, rows < t_real)
        err = jnp.where(real, o_ref[...] - t_ref[...], 0.0)
        d_ref[...] = err * (1.0 / d)
        l_ref[...] += jnp.sum(err * err, axis=0, keepdims=True) * (0.5 / d)

    row = lambda i: (i, 0)
    return pl.pallas_call(
        body, name=name, out_shape=(_sds((1, d), F32), _sds((tp, d), F32)), grid=(tp // tm,),
        in_specs=[pl.BlockSpec((tm, d), row), pl.BlockSpec((tm, d), row)],
        out_specs=[pl.BlockSpec((1, d), lambda i: (0, 0)), pl.BlockSpec((tm, d), row)],
        compiler_params=_params(("arbitrary",)),
    )(o, tgt)


CONV_TC = 256


def _silu_grad(y):
    s = jax.nn.sigmoid(y)
    return s * (1.0 + y * (1.0 - s))


def _conv_taps(x_ref, w, r0, rb):
    cur = x_ref[r0:r0 + rb, :]
    prev = x_ref[r0 - 8:r0, :] if r0 > 0 else jnp.zeros((8, cur.shape[1]), F32)
    xcat = jnp.concatenate([prev, cur], axis=0)
    taps = [xcat[5 + j:5 + j + rb] for j in range(CONV_K - 1)] + [cur]
    y = w[0:1] * taps[0]
    for j in range(1, CONV_K):
        y = y + w[j:j + 1] * taps[j]
    return y, taps


def _conv_fwd(proj, w, b, name):
    tp = proj.shape[0]
    rb = _tile(tp, ROW_TILE_CAP, 8)

    def body(x_ref, w_ref, b_ref, o_ref):
        gated = pl.program_id(0) > 0
        wv = w_ref[...]
        bv = b_ref[...]
        for r0 in range(0, tp, rb):
            y, _ = _conv_taps(x_ref, wv, r0, rb)
            y = y + bv
            o_ref[r0:r0 + rb, :] = jnp.where(gated, y * jax.nn.sigmoid(y), y)

    col = lambda j: (0, j)
    return pl.pallas_call(
        body, name=name, out_shape=_sds((tp, CONV_W), F32), grid=(CONV_W // CONV_TC,),
        in_specs=[pl.BlockSpec((tp, CONV_TC), col), pl.BlockSpec((CONV_K, CONV_TC), col), pl.BlockSpec((1, CONV_TC), col)],
        out_specs=pl.BlockSpec((tp, CONV_TC), col),
        compiler_params=_params(("arbitrary",)),
    )(proj, w, b)


def _conv_bwd(proj, w, b, dxc, name):
    tp = proj.shape[0]
    rb = _tile(tp, ROW_TILE_CAP, 8)

    def body(x_ref, w_ref, b_ref, d_ref, dx_ref, dw_ref, db_ref, dy_scr):
        gated = pl.program_id(0) > 0
        wv = w_ref[...]
        bv = b_ref[...]
        dw = [jnp.zeros((1, CONV_TC), F32) for _ in range(CONV_K)]
        db = jnp.zeros((1, CONV_TC), F32)
        for r0 in range(0, tp, rb):
            y, taps = _conv_taps(x_ref, wv, r0, rb)
            y = y + bv
            d = d_ref[r0:r0 + rb, :]
            dy = jnp.where(gated, d * _silu_grad(y), d)
            dy_scr[r0:r0 + rb, :] = dy
            for j in range(CONV_K):
                dw[j] = dw[j] + jnp.sum(dy * taps[j], axis=0, keepdims=True)
            db = db + jnp.sum(dy, axis=0, keepdims=True)
        for j in range(CONV_K):
            dw_ref[j:j + 1, :] = dw[j]
        db_ref[...] = db
        for r0 in range(0, tp, rb):
            cur = dy_scr[r0:r0 + rb, :]
            nxt = dy_scr[r0 + rb:r0 + rb + 8, :] if r0 + rb < tp else jnp.zeros((8, CONV_TC), F32)
            dcat = jnp.concatenate([cur, nxt], axis=0)
            dx = wv[3:4] * cur
            for s in range(1, CONV_K):
                dx = dx + wv[3 - s:4 - s] * dcat[s:s + rb]
            dx_ref[r0:r0 + rb, :] = dx

    col = lambda j: (0, j)
    return pl.pallas_call(
        body, name=name,
        out_shape=(_sds((tp, CONV_W), F32), _sds((CONV_K, CONV_W), F32), _sds((1, CONV_W), F32)),
        grid=(CONV_W // CONV_TC,),
        in_specs=[pl.BlockSpec((tp, CONV_TC), col), pl.BlockSpec((CONV_K, CONV_TC), col), pl.BlockSpec((1, CONV_TC), col),
                  pl.BlockSpec((tp, CONV_TC), col)],
        out_specs=[pl.BlockSpec((tp, CONV_TC), col), pl.BlockSpec((CONV_K, CONV_TC), col), pl.BlockSpec((1, CONV_TC), col)],
        scratch_shapes=[pltpu.VMEM((tp, CONV_TC), F32)],
        compiler_params=_params(("arbitrary",)),
    )(proj, w, b, dxc)


def _neg_expm1(x):
    series = -x * (1.0 + x * (0.5 + x * (1.0 / 6.0 + x * (1.0 / 24.0))))
    return jnp.where(jnp.abs(x) < 0.03, series, 1.0 - jnp.exp(x))


def _lru_gates(u, wa, ba, wx, bx, lam):
    r = jax.nn.sigmoid(jnp.dot(u, wa, precision=HI, preferred_element_type=F32) + ba)
    i = jax.nn.sigmoid(jnp.dot(u, wx, precision=HI, preferred_element_type=F32) + bx)
    log_a = -LRU_C * r * jax.nn.softplus(-lam)
    a = jnp.exp(log_a)
    b = jnp.sqrt(_neg_expm1(2.0 * log_a)) * (i * u)
    return a, b


def _lru_specs(tm, flip, n_tiles):
    idx = (lambda i: (n_tiles - 1 - i, 0)) if flip else (lambda i: (i, 0))
    return idx, lambda i: (0, 0)


def _lru_fwd(xc, proj, wa, ba, wx, bx, lam, name):
    tp = xc.shape[0]
    w = LRU_WIDTH
    tm = _tile(tp, ROW_TILE_CAP, 8)
    ycol = OFF_LRU_Y // w

    def body(u_ref, y_ref, wa_ref, ba_ref, wx_ref, bx_ref, lam_ref, o_ref, h_ref, a_scr, b_scr, carry):
        @pl.when(pl.program_id(0) == 0)
        def _():
            carry[...] = jnp.zeros_like(carry)
        a, b = _lru_gates(u_ref[...], wa_ref[...], ba_ref[...], wx_ref[...], bx_ref[...], lam_ref[...])
        a_scr[...] = a
        b_scr[...] = b

        def step(t, h):
            h = a_scr[pl.ds(t, 1), :] * h + b_scr[pl.ds(t, 1), :]
            h_ref[pl.ds(t, 1), :] = h
            return h

        carry[0:1, :] = lax.fori_loop(0, tm, step, carry[0:1, :])
        o_ref[...] = h_ref[...] * jax.nn.gelu(y_ref[...])

    fix = lambda i: (0, 0)
    return pl.pallas_call(
        body, name=name, out_shape=(_sds((tp, w), F32), _sds((tp, w), F32)), grid=(tp // tm,),
        in_specs=[pl.BlockSpec((tm, w), lambda i: (i, 0)), pl.BlockSpec((tm, w), lambda i: (i, ycol)),
                  pl.BlockSpec((w, w), fix), pl.BlockSpec((1, w), fix), pl.BlockSpec((w, w), fix),
                  pl.BlockSpec((1, w), fix), pl.BlockSpec((1, w), fix)],
        out_specs=[pl.BlockSpec((tm, w), lambda i: (i, 0))] * 2,
        scratch_shapes=[pltpu.VMEM((tm, w), F32), pltpu.VMEM((tm, w), F32), pltpu.VMEM((8, w), F32)],
        compiler_params=_params(("arbitrary",)),
    )(xc, proj, wa, ba, wx, bx, lam)


def _lru_bwd(xc, proj, h, dout, wa, ba, wx, bx, lam, name):
    tp = xc.shape[0]
    w = LRU_WIDTH
    tm = _tile(tp, ROW_TILE_CAP, 8)
    nt = tp // tm
    ycol = OFF_LRU_Y // w
    rev = lambda i: (nt - 1 - i, 0)
    prev8 = lambda i: (jnp.maximum((nt - 1 - i) * (tm // 8) - 1, 0), 0)

    def body(u_ref, y_ref, h_ref, hp_ref, do_ref, wa_ref, ba_ref, wx_ref, bx_ref, lam_ref,
             du_ref, dy_ref, dwa_ref, dba_ref, dwx_ref, dbx_ref, dlam_ref,
             a_scr, dh_scr, g_scr, da_scr, hext, carry):
        i = pl.program_id(0)

        @pl.when(i == 0)
        def _():
            carry[...] = jnp.zeros_like(carry)
            for r in (dwa_ref, dba_ref, dwx_ref, dbx_ref, dlam_ref):
                r[...] = jnp.zeros_like(r)
        params = (wa_ref[...], ba_ref[...], wx_ref[...], bx_ref[...], lam_ref[...])
        (a, _), gates_vjp = jax.vjp(_lru_gates, u_ref[...], *params)
        gel, gelu_vjp = jax.vjp(jax.nn.gelu, y_ref[...])
        do = do_ref[...]
        hv = h_ref[...]
        dy_ref[...] = gelu_vjp(do * hv)[0]
        a_scr[...] = a
        dh_scr[...] = do * gel
        hext[0:8, :] = jnp.where(i == nt - 1, 0.0, hp_ref[...])
        hext[8:8 + tm, :] = hv

        def step(s, c):
            t = tm - 1 - s
            g = dh_scr[pl.ds(t, 1), :] + c
            g_scr[pl.ds(t, 1), :] = g
            da_scr[pl.ds(t, 1), :] = g * hext[pl.ds(t + 7, 1), :]
            return a_scr[pl.ds(t, 1), :] * g

        carry[0:1, :] = lax.fori_loop(0, tm, step, carry[0:1, :])
        du, dwa, dba, dwx, dbx, dlam = gates_vjp((da_scr[...], g_scr[...]))
        du_ref[...] = du
        dwa_ref[...] += dwa
        dba_ref[...] += dba
        dwx_ref[...] += dwx
        dbx_ref[...] += dbx
        dlam_ref[...] += dlam

    fix = lambda i: (0, 0)
    tile = pl.BlockSpec((tm, w), rev)
    mat = pl.BlockSpec((w, w), fix)
    vec = pl.BlockSpec((1, w), fix)
    return pl.pallas_call(
        body, name=name,
        out_shape=(_sds((tp, w), F32), _sds((tp, w), F32), _sds((w, w), F32), _sds((1, w), F32), _sds((w, w), F32),
                   _sds((1, w), F32), _sds((1, w), F32)),
        grid=(nt,),
        in_specs=[tile, pl.BlockSpec((tm, w), lambda i: (nt - 1 - i, ycol)), tile, pl.BlockSpec((8, w), prev8), tile,
                  mat, vec, mat, vec, vec],
        out_specs=[tile, tile, mat, vec, mat, vec, vec],
        scratch_shapes=[pltpu.VMEM((tm, w), F32)] * 4 + [pltpu.VMEM((tm + 8, w), F32), pltpu.VMEM((8, w), F32)],
        compiler_params=_params(("arbitrary",)),
    )(xc, proj, h, h, dout, wa, ba, wx, bx, lam)


def _dot3(a, b, dims):
    ah = a.astype(BF16)
    al = (a - ah.astype(F32)).astype(BF16)
    bh = b.astype(BF16)
    bl = (b - bh.astype(F32)).astype(BF16)
    dot = lambda x, y: lax.dot_general(x, y, (dims, ((0,), (0,))), preferred_element_type=F32)
    return dot(ah, bh) + (dot(ah, bl) + dot(al, bh))


@jax.custom_vjp
def _bmm(a, b):
    return _dot3(a, b, ((2,), (1,)))


@jax.custom_vjp
def _bmm_nt(a, b):
    return _dot3(a, b, ((2,), (2,)))


@jax.custom_vjp
def _bmm_tn(a, b):
    return _dot3(a, b, ((1,), (1,)))


_bmm.defvjp(lambda a, b: (_bmm(a, b), (a, b)), lambda r, g: (_bmm_nt(g, r[1]), _bmm_tn(r[0], g)))
_bmm_nt.defvjp(lambda a, b: (_bmm_nt(a, b), (a, b)), lambda r, g: (_bmm(g, r[1]), _bmm_tn(g, r[0])))
_bmm_tn.defvjp(lambda a, b: (_bmm_tn(a, b), (a, b)), lambda r, g: (_bmm_nt(r[1], g), _bmm(r[0], g)))


def _chunk_masks(nh):
    r = lax.broadcasted_iota(jnp.int32, (CHUNK, CHUNK), 0)
    c = lax.broadcasted_iota(jnp.int32, (CHUNK, CHUNK), 1)
    full = lambda m: jnp.broadcast_to(m[None], (nh, CHUNK, CHUNK))
    return r, c, full


def _decay_terms(g, nh):
    r, c, full = _chunk_masks(nh)
    incl = r >= c
    ones = jnp.ones((nh, CHUNK, CHUNK), F32)
    cs = _bmm(full(incl.astype(F32)), g)
    cs_t = _bmm(ones, g * full((r <= c).astype(F32)))
    tot = _bmm(ones, g)
    m = full(incl)
    decay = jnp.where(m, jnp.exp(jnp.where(m, cs - cs_t, 0.0)), 0.0)
    return cs, decay, tot


def _rep_groups(x):
    return jnp.concatenate([jnp.broadcast_to(x[g:g + 1], (SSD_HPG,) + x.shape[1:]) for g in range(SSD_GROUPS)], axis=0)


def _ssd_chunk(xs, bm, cm, dtr, z, p_dtb, p_alog, p_d, p_nw, state):
    nh = SSD_HEADS
    dt = jax.nn.softplus(dtr + p_dtb)
    a = dt * (-jnp.exp(p_alog))
    x = xs * dt
    cs, decay, tot = _decay_terms(a, nh)
    b8 = _rep_groups(bm)
    c8 = _rep_groups(cm)
    y = _bmm(_rep_groups(_bmm_nt(cm, bm)) * decay, x)
    y = y + _bmm_nt(c8, state) * jnp.exp(cs)
    new_state = state * jnp.exp(tot) + _bmm_tn(x * jnp.exp(tot - cs), b8)
    y = y + p_d * xs
    y = y * (z * jax.nn.sigmoid(z))
    ss = jnp.sum(y * y, axis=-1, keepdims=True)
    ssg = jnp.concatenate(
        [jnp.broadcast_to(jnp.sum(ss[g * SSD_HPG:(g + 1) * SSD_HPG], axis=0, keepdims=True), (SSD_HPG, CHUNK, 1))
         for g in range(SSD_GROUPS)], axis=0)
    y = y * lax.rsqrt(ssg * (1.0 / (SSD_HPG * HEAD)) + RMS_EPS) * p_nw
    return y, new_state


def _unit_lower_inverse(m, nh):
    r, c, full = _chunk_masks(nh)
    eye = full((r == c).astype(F32))
    md = jnp.where(full((r // 16) == (c // 16)), m, 0.0)
    mo = m - md
    x = eye - md
    p = _bmm(md, md)
    x = x + _bmm(x, p)
    p = _bmm(p, p)
    x = x + _bmm(x, p)
    p = _bmm(p, p)
    x = x + _bmm(x, p)
    n = _bmm(x, mo)
    y = x - _bmm(n, x)
    return y + _bmm(_bmm(n, n), y)


def _dn_chunk(q, k, v, gate, braw, araw, p_alog, p_dtb, p_nw, state):
    nh = DN_HEADS
    r, c, full = _chunk_masks(nh)
    q = q * lax.rsqrt(jnp.sum(q * q, axis=-1, keepdims=True) + RMS_EPS) * (HEAD ** -0.5)
    k = k * lax.rsqrt(jnp.sum(k * k, axis=-1, keepdims=True) + RMS_EPS)
    beta = jax.nn.sigmoid(braw)
    g = -jnp.exp(p_alog) * jax.nn.softplus(araw + p_dtb)
    gcs, decay, tot = _decay_terms(g, nh)
    kb = k * beta
    vb = v * beta
    m = jnp.where(full(r > c), _bmm_nt(kb, k) * decay, 0.0)
    t = _unit_lower_inverse(m, nh)
    egcs = jnp.exp(gcs)
    u = _bmm(t, vb)
    w = _bmm(t, kb * egcs)
    attn = _bmm_nt(q, k) * decay
    v_new = u - _bmm(w, state)
    out = _bmm(q * egcs, state) + _bmm(attn, v_new)
    new_state = state * jnp.exp(tot) + _bmm_tn(k * jnp.exp(tot - gcs), v_new)
    out = out * lax.rsqrt(jnp.mean(out * out, axis=-1, keepdims=True) + RMS_EPS) * p_nw
    return out * (gate * jax.nn.sigmoid(gate)), new_state


def _chunk_scan_fwd(chunk_fn, seqs, params, nh, name, comm=None):
    tp = seqs[0].shape[1]
    nc = tp // CHUNK
    ns, npar = len(seqs), len(params)
    ncm = len(comm[0]) if comm else 0

    def body(*refs):
        s_refs, p_refs = refs[:ns], refs[ns:ns + npar]
        c_in = refs[ns + npar:ns + npar + ncm]
        y_ref, st_ref = refs[ns + npar + ncm:ns + npar + ncm + 2]
        c_out = refs[ns + npar + ncm + 2:ns + npar + 2 * ncm + 2]
        state = refs[ns + npar + 2 * ncm + 2]
        sems = refs[ns + npar + 2 * ncm + 3:]

        @pl.when(pl.program_id(0) == 0)
        def _():
            state[...] = jnp.zeros_like(state)
            if comm:
                _exchange_start(c_in, c_out, comm[1], *sems)
        st = state[...]
        st_ref[0] = st
        y, new = chunk_fn(*[r[...] for r in s_refs], *[r[...] for r in p_refs], st)
        y_ref[...] = y
        state[...] = new
        if comm:
            @pl.when(pl.program_id(0) == nc - 1)
            def _():
                _exchange_wait(c_in, c_out, comm[1], *sems)

    seq_spec = lambda a: pl.BlockSpec((a.shape[0], CHUNK, HEAD), lambda c: (0, c, 0))
    par_spec = lambda a: pl.BlockSpec(a.shape, lambda c: (0, 0, 0))
    hbm = pl.BlockSpec(memory_space=pl.ANY)
    res = pl.pallas_call(
        body, name=name,
        out_shape=(_sds((nh, tp, HEAD), F32), _sds((nc, nh, HEAD, HEAD), F32)) + (_exchange_shapes(*comm) if comm else ()),
        grid=(nc,),
        in_specs=[seq_spec(a) for a in seqs] + [par_spec(a) for a in params] + [hbm] * ncm,
        out_specs=[pl.BlockSpec((nh, CHUNK, HEAD), lambda c: (0, c, 0)),
                   pl.BlockSpec((1, nh, HEAD, HEAD), lambda c: (c, 0, 0, 0))] + [hbm] * ncm,
        scratch_shapes=[pltpu.VMEM((nh, HEAD, HEAD), F32)] + (_exchange_sems(ncm) if comm else []),
        compiler_params=_params(("arbitrary",)),
    )(*seqs, *params, *(comm[0] if comm else ()))
    return res[0], res[1], tuple(res[2:])


def _chunk_scan_bwd(chunk_fn, seqs, params, states, dy, nh, name, comm=None):
    tp = seqs[0].shape[1]
    nc = tp // CHUNK
    ns, npar = len(seqs), len(params)
    ncm = len(comm[0]) if comm else 0
    nin = ns + npar + 2

    def body(*refs):
        s_refs, p_refs = refs[:ns], refs[ns:ns + npar]
        st_ref, dy_ref = refs[ns + npar:nin]
        c_in = refs[nin:nin + ncm]
        ds_refs = refs[nin + ncm:nin + ncm + ns]
        dp_refs = refs[nin + ncm + ns:nin + ncm + ns + npar]
        c_out = refs[nin + ncm + ns + npar:nin + 2 * ncm + ns + npar]
        dstate = refs[nin + 2 * ncm + ns + npar]
        sems = refs[nin + 2 * ncm + ns + npar + 1:]

        @pl.when(pl.program_id(0) == 0)
        def _():
            dstate[...] = jnp.zeros_like(dstate)
            for r in dp_refs:
                r[...] = jnp.zeros_like(r)
            if comm:
                _exchange_start(c_in, c_out, comm[1], *sems)
        _, vjp = jax.vjp(chunk_fn, *[r[...] for r in s_refs], *[r[...] for r in p_refs], st_ref[0])
        grads = vjp((dy_ref[...], dstate[...]))
        for r, gr in zip(ds_refs, grads[:ns]):
            r[...] = gr
        for r, gr in zip(dp_refs, grads[ns:ns + npar]):
            r[...] += gr
        dstate[...] = grads[-1]
        if comm:
            @pl.when(pl.program_id(0) == nc - 1)
            def _():
                _exchange_wait(c_in, c_out, comm[1], *sems)

    seq_spec = lambda a: pl.BlockSpec((a.shape[0], CHUNK, HEAD), lambda c: (0, nc - 1 - c, 0))
    par_spec = lambda a: pl.BlockSpec(a.shape, lambda c: (0, 0, 0))
    hbm = pl.BlockSpec(memory_space=pl.ANY)
    res = pl.pallas_call(
        body, name=name,
        out_shape=tuple(_sds(a.shape, F32) for a in seqs) + tuple(_sds(a.shape, F32) for a in params)
        + (_exchange_shapes(*comm) if comm else ()),
        grid=(nc,),
        in_specs=[seq_spec(a) for a in seqs] + [par_spec(a) for a in params]
        + [pl.BlockSpec((1, nh, HEAD, HEAD), lambda c: (nc - 1 - c, 0, 0, 0)), seq_spec(dy)] + [hbm] * ncm,
        out_specs=[seq_spec(a) for a in seqs] + [par_spec(a) for a in params] + [hbm] * ncm,
        scratch_shapes=[pltpu.VMEM((nh, HEAD, HEAD), F32)] + (_exchange_sems(ncm) if comm else []),
        compiler_params=_params(("arbitrary",)),
    )(*seqs, *params, states, dy, *(comm[0] if comm else ()))
    return tuple(res[:ns + npar]), tuple(res[ns + npar:])


def _hm(a, nh):
    return a.reshape(a.shape[0], nh, HEAD).transpose(1, 0, 2)


def _hm_inv(a):
    return a.transpose(1, 0, 2).reshape(a.shape[1], a.shape[0] * HEAD)


def _hm_scalar(a):
    return jnp.broadcast_to(a.T[:, :, None], (a.shape[1], a.shape[0], HEAD))


def _lane_param(p):
    return jnp.broadcast_to(p[:, None, None], (p.shape[0], 1, HEAD))


def _block_diag(w):
    out = jnp.zeros((LRU_WIDTH, LRU_WIDTH), F32)
    for h in range(LRU_HEADS):
        out = out.at[h * HEAD:(h + 1) * HEAD, h * HEAD:(h + 1) * HEAD].set(w[h])
    return out


def _block_diag_inv(w):
    return jnp.stack([w[h * HEAD:(h + 1) * HEAD, h * HEAD:(h + 1) * HEAD] for h in range(LRU_HEADS)])


def _ssd_inputs(xc, proj, lp):
    seqs = (_hm(xc[:, 256:768], SSD_HEADS), _hm(xc[:, 768:896], SSD_GROUPS), _hm(xc[:, 896:1024], SSD_GROUPS),
            _hm_scalar(proj[:, OFF_SMALL:OFF_SMALL + 8]), _hm(proj[:, OFF_SSD_Z:OFF_SSD_Z + SSD_INNER], SSD_HEADS))
    params = (_lane_param(lp["ssd_dt_bias"]), _lane_param(lp["ssd_a_log"]), _lane_param(lp["ssd_d"]),
              lp["ssd_norm_w"].reshape(SSD_HEADS, 1, HEAD))
    return seqs, params


def _dn_inputs(xc, proj, lp):
    seqs = (_hm(xc[:, 1024:1280], DN_HEADS), _hm(xc[:, 1280:1536], DN_HEADS), _hm(xc[:, 1536:1792], DN_HEADS),
            _hm(proj[:, OFF_DN_GATE:OFF_DN_GATE + 256], DN_HEADS),
            _hm_scalar(proj[:, OFF_SMALL + 8:OFF_SMALL + 12]), _hm_scalar(proj[:, OFF_SMALL + 12:OFF_SMALL + 16]))
    params = (_lane_param(lp["dn_a_log"]), _lane_param(lp["dn_dt_bias"]),
              jnp.broadcast_to(lp["dn_norm_w"][None, None, :], (DN_HEADS, 1, HEAD)))
    return seqs, params


def _lru_params(lp):
    return (_block_diag(lp["lru_w_a"]), lp["lru_b_a"][None], _block_diag(lp["lru_w_x"]), lp["lru_b_x"][None],
            lp["lru_lambda"][None])


def _conv_params(lp):
    w = jnp.concatenate([lp["lru_conv_w"], lp["ssd_conv_w"], lp["dn_conv_w"]], axis=1)
    b = jnp.concatenate([lp["lru_conv_b"], lp["ssd_conv_b"], jnp.zeros((768,), F32)])[None]
    return w, b


def _mixers_fwd(proj, lp, tag, comm=None):
    cw, cb = _conv_params(lp)
    xc = _conv_fwd(proj, cw, cb, tag + "_conv")
    o_lru, h_lru = _lru_fwd(xc, proj, *_lru_params(lp), name=tag + "_lru")
    s_seqs, s_par = _ssd_inputs(xc, proj, lp)
    o_ssd, s_states, _ = _chunk_scan_fwd(_ssd_chunk, s_seqs, s_par, SSD_HEADS, tag + "_ssd")
    d_seqs, d_par = _dn_inputs(xc, proj, lp)
    o_dn, d_states, got = _chunk_scan_fwd(_dn_chunk, d_seqs, d_par, DN_HEADS, tag + "_dn", comm=comm)
    cat = jnp.concatenate([o_lru, _hm_inv(o_ssd), _hm_inv(o_dn)], axis=1)
    return cat, (xc, h_lru, s_states, d_states), got


def _mixers_bwd(proj, lp, saved, dcat, tag, comm=None):
    xc, h_lru, s_states, d_states = saved
    tp = proj.shape[0]
    cw, cb = _conv_params(lp)
    g = {}
    du, dyraw, dwa, dba, dwx, dbx, dlam = _lru_bwd(xc, proj, h_lru, dcat[:, :LRU_WIDTH], *_lru_params(lp), name=tag + "_lru_bwd")
    g["lru_w_a"], g["lru_b_a"], g["lru_w_x"], g["lru_b_x"], g["lru_lambda"] = (
        _block_diag_inv(dwa), dba[0], _block_diag_inv(dwx), dbx[0], dlam[0])

    s_seqs, s_par = _ssd_inputs(xc, proj, lp)
    sg, _ = _chunk_scan_bwd(_ssd_chunk, s_seqs, s_par, s_states, _hm(dcat[:, 256:768], SSD_HEADS), SSD_HEADS,
                            tag + "_ssd_bwd")
    dxs, dbm, dcm, ddtr, dz, dp_dtb, dp_alog, dp_d, dp_nw = sg
    g["ssd_dt_bias"], g["ssd_a_log"], g["ssd_d"] = (jnp.sum(p, axis=(1, 2)) for p in (dp_dtb, dp_alog, dp_d))
    g["ssd_norm_w"] = dp_nw.reshape(SSD_INNER)

    d_seqs, d_par = _dn_inputs(xc, proj, lp)
    dg, got = _chunk_scan_bwd(_dn_chunk, d_seqs, d_par, d_states, _hm(dcat[:, 768:1024], DN_HEADS), DN_HEADS,
                              tag + "_dn_bwd", comm=comm)
    dq, dk, dv, dgate, dbraw, daraw, dq_alog, dq_dtb, dq_nw = dg
    g["dn_a_log"], g["dn_dt_bias"] = (jnp.sum(p, axis=(1, 2)) for p in (dq_alog, dq_dtb))
    g["dn_norm_w"] = jnp.sum(dq_nw, axis=(0, 1))

    dxc = jnp.concatenate([du, _hm_inv(dxs), _hm_inv(dbm), _hm_inv(dcm), _hm_inv(dq), _hm_inv(dk), _hm_inv(dv)], axis=1)
    dconv, dcw, dcb = _conv_bwd(proj, cw, cb, dxc, tag + "_conv_bwd")
    g["lru_conv_w"], g["ssd_conv_w"], g["dn_conv_w"] = dcw[:, :256], dcw[:, 256:1024], dcw[:, 1024:]
    g["lru_conv_b"], g["ssd_conv_b"] = dcb[0, :256], dcb[0, 256:1024]
    scal = lambda a: jnp.sum(a, axis=2).T
    dsmall = jnp.concatenate([scal(ddtr), scal(dbraw), scal(daraw), jnp.zeros((tp, LANE - 16), F32)], axis=1)
    dproj = jnp.concatenate([dconv, dyraw, _hm_inv(dz), _hm_inv(dgate), dsmall], axis=1)
    return dproj, g, got


MIXER_PARAMS = ("lru_conv_w", "lru_conv_b", "lru_w_a", "lru_b_a", "lru_w_x", "lru_b_x", "lru_lambda",
                "ssd_conv_w", "ssd_conv_b", "ssd_dt_bias", "ssd_a_log", "ssd_d", "ssd_norm_w",
                "dn_conv_w", "dn_a_log", "dn_dt_bias", "dn_norm_w")


def _layer_weights(gathered):
    g_gate, g_up, g_down, g_in, g_out = gathered
    f_full = N_DEV * g_gate.shape[-1]
    full = lambda a: a.transpose(1, 2, 0, 3).reshape(2, D_MODEL, f_full)
    wgu = jnp.concatenate([full(g_gate), full(g_up)], axis=-1)
    wd = g_down.transpose(1, 0, 2, 3).reshape(2, f_full, D_MODEL)
    win = _proj_cols(g_in.transpose(1, 0, 2).reshape(D_MODEL, D_IN))
    wout = g_out.reshape(D_MODEL, D_MODEL)
    return wgu, wd, win, wout


def _grad_slabs(d_wgu, d_wd, d_win, d_wout):
    f_full = d_wd.shape[1]
    f_loc = f_full // N_DEV
    cols = lambda a: a.reshape(2, D_MODEL, N_DEV, f_loc).transpose(2, 0, 1, 3).astype(BF16)
    return (cols(d_wgu[..., :f_full]), cols(d_wgu[..., f_full:]),
            d_wd.reshape(2, N_DEV, f_loc, D_MODEL).transpose(1, 0, 2, 3).astype(BF16),
            _proj_cols_inv(d_win).reshape(D_MODEL, N_DEV, D_IN // N_DEV).transpose(1, 0, 2).astype(BF16),
            d_wout.reshape(N_DEV, D_MODEL // N_DEV, D_MODEL).astype(BF16))


def _local_step(x, tgt, small, gathered, shards):
    s = x.shape[0]
    t_real = N_META + s
    tp = -(-t_real // CHUNK) * CHUNK
    depth = len(gathered)
    gathered = list(gathered)
    h = jnp.concatenate([small["meta"], x, jnp.zeros((tp - t_real, D_MODEL), F32)], axis=0)
    hb = h.astype(BF16)
    ln_g, ln_b = small["ln_g"], small["ln_b"]
    saved, weights = [], []
    for l in range(depth):
        lp = {k: small[k][l] for k in MIXER_PARAMS}
        t = f"l{l}"
        wgu, wd, win, wout = _layer_weights(gathered[l])
        weights.append((wgu, wd, win, wout))
        g0, u0, a0 = _ffn_up(hb, wgu[0], t + "_ffn0_up")
        y1, h1, h1b = _mm_resid_ln(a0, wd[0], h, ln_g[l, 0][None], ln_b[l, 0][None], FFN_RES, t + "_ffn0_down")
        proj = _mm_nn(h1b, win, t + "_in_proj")
        fetch = l + 1 < depth and gathered[l + 1] is None
        cat, mix_saved, got = _mixers_fwd(proj, lp, t, comm=(list(shards[l + 1]), [False] * 5) if fetch else None)
        if fetch:
            gathered[l + 1] = got
        catb = cat.astype(BF16)
        y2, h2, h2b = _mm_resid_ln(catb, wout, h1, ln_g[l, 1][None], ln_b[l, 1][None], 1.0, t + "_out_proj")
        g1, u1, a1 = _ffn_up(h2b, wgu[1], t + "_ffn1_up")
        y3, h3, h3b = _mm_resid_ln(a1, wd[1], h2, ln_g[l, 2][None], ln_b[l, 2][None], FFN_RES, t + "_ffn1_down")
        saved.append((hb, g0, u0, a0, y1, h1b, proj, mix_saved, catb, y2, h2b, g1, u1, a1, y3))
        h, hb = h3, h3b

    tgt_p = jnp.pad(tgt, ((N_META, tp - t_real), (0, 0)))
    lossv, dh = _loss_grad(h, tgt_p, t_real, "loss")
    loss = jnp.sum(lossv)

    gs = {k: [None] * depth for k in MIXER_PARAMS}
    d_ln_g = [[None] * 3 for _ in range(depth)]
    d_ln_b = [[None] * 3 for _ in range(depth)]
    slabs = [None] * depth

    def ffn_bwd(l, j, xb_in, g, u, a, y, dout):
        t = f"l{l}_ffn{j}"
        wgu, wd = weights[l][0][j], weights[l][1][j]
        dres, dyb, dgam, dbet = _ln_bwd(y, dout, ln_g[l, 2 * j][None], FFN_RES, t + "_ln_bwd")
        d_ln_g[l][2 * j], d_ln_b[l][2 * j] = dgam[0], dbet[0]
        d_wd = _mm_tn(a, dyb, t + "_dwd")
        dg, du = _ffn_dact(dyb, wd, g, u, t + "_dact")
        d_wgu = jnp.concatenate([_mm_tn(xb_in, dg, t + "_dwg"), _mm_tn(xb_in, du, t + "_dwu")], axis=1)
        return _ffn_dx(dg, du, wgu, dres, t + "_dx"), d_wgu, d_wd

    for l in reversed(range(depth)):
        hb_in, g0, u0, a0, y1, h1b, proj, mix_saved, catb, y2, h2b, g1, u1, a1, y3 = saved[l]
        lp = {k: small[k][l] for k in MIXER_PARAMS}
        t = f"l{l}"
        _, _, win, wout = weights[l]
        dh2, d_wgu1, d_wd1 = ffn_bwd(l, 1, h2b, g1, u1, a1, y3, dh)
        dres, dyb, dgam, dbet = _ln_bwd(y2, dh2, ln_g[l, 1][None], 1.0, t + "_mix_ln_bwd")
        d_ln_g[l][1], d_ln_b[l][1] = dgam[0], dbet[0]
        d_wout = _mm_tn(catb, dyb, t + "_dwout")
        dcat = _mm_nt_add(dyb, wout, None, t + "_dcat")
        send = shards is not None and l + 1 < depth
        dproj, mg, got = _mixers_bwd(proj, lp, mix_saved, dcat, t, comm=(list(slabs[l + 1]), [True] * 5) if send else None)
        if send:
            slabs[l + 1] = got
        for k in MIXER_PARAMS:
            gs[k][l] = mg[k]
        dprojb = dproj.astype(BF16)
        d_win = _mm_tn(h1b, dprojb, t + "_dwin")
        dh1 = _mm_nt_add(dprojb, win, dres, t + "_dh1")
        dh, d_wgu0, d_wd0 = ffn_bwd(l, 0, hb_in, g0, u0, a0, y1, dh1)
        slabs[l] = _grad_slabs(jnp.stack([d_wgu0, d_wgu1]), jnp.stack([d_wd0, d_wd1]), d_win, d_wout)

    small_grads = {k: jnp.stack(v) for k, v in gs.items()}
    small_grads["ln_g"] = jnp.stack([jnp.stack(r) for r in d_ln_g])
    small_grads["ln_b"] = jnp.stack([jnp.stack(r) for r in d_ln_b])
    small_grads["meta"] = dh[:N_META]
    return loss, dh[N_META:t_real], small_grads, slabs


def _mesh_pos():
    return lax.axis_index("x"), lax.axis_index("y"), lax.axis_index("c")


def _flip(pos, k):
    x, y, c = pos
    return (1 - x if k & 4 else x, 1 - y if k & 2 else y, 1 - c if k & 1 else c)


def _flat(pos):
    return 4 * pos[0] + 2 * pos[1] + pos[2]


def _exchange_copies(ins, outs, slabs, send, recv, loc):
    pos = _mesh_pos()
    me = _flat(pos)
    local, sends, receives = [], [], []
    for a in range(len(ins)):
        local.append(pltpu.make_async_copy(ins[a].at[me] if slabs[a] else ins[a], outs[a].at[me], loc.at[a]))
    for k in range(1, N_DEV):
        peer = _flip(pos, k)
        for a in range(len(ins)):
            sem = dict(send_sem=send.at[a, k - 1], recv_sem=recv.at[a, k - 1], device_id=peer, device_id_type=MESH_IDS)
            sends.append(pltpu.make_async_remote_copy(
                src_ref=ins[a].at[_flat(peer)] if slabs[a] else ins[a], dst_ref=outs[a].at[me], **sem))
            receives.append(pltpu.make_async_remote_copy(
                src_ref=ins[a].at[me] if slabs[a] else ins[a], dst_ref=outs[a].at[_flat(peer)], **sem))
    return local, sends, receives


def _exchange_start(ins, outs, slabs, send, recv, loc):
    local, sends, _ = _exchange_copies(ins, outs, slabs, send, recv, loc)
    for cp in local + sends:
        cp.start()


def _exchange_wait(ins, outs, slabs, send, recv, loc):
    local, sends, receives = _exchange_copies(ins, outs, slabs, send, recv, loc)
    for cp in receives:
        cp.wait_recv()
    for cp in sends:
        cp.wait_send()
    for cp in local:
        cp.wait()


def _exchange_shapes(arrs, slabs):
    return tuple(_sds(a.shape if s else (N_DEV,) + a.shape, a.dtype) for a, s in zip(arrs, slabs))


def _exchange_sems(n):
    return [pltpu.SemaphoreType.DMA((n, N_DEV - 1)), pltpu.SemaphoreType.DMA((n, N_DEV - 1)),
            pltpu.SemaphoreType.DMA((n,))]


def _exchange(arrs, slabs, name):
    n = len(arrs)

    def body(*refs):
        ins, outs, sems = refs[:n], refs[n:2 * n], refs[2 * n:]
        _exchange_start(ins, outs, slabs, *sems)
        _exchange_wait(ins, outs, slabs, *sems)

    hbm = pl.BlockSpec(memory_space=pl.ANY)
    return pl.pallas_call(
        body, name=name, out_shape=_exchange_shapes(arrs, slabs), in_specs=[hbm] * n, out_specs=[hbm] * n,
        scratch_shapes=_exchange_sems(n),
    )(*arrs)


def _adam_math(w, g, m, v):
    m = ADAM_B1 * m + (1.0 - ADAM_B1) * g
    v = ADAM_B2 * v + (1.0 - ADAM_B2) * (g * g)
    m_hat = m / (1.0 - ADAM_B1 ** ADAM_STEP)
    v_hat = v / (1.0 - ADAM_B2 ** ADAM_STEP)
    delta = -ADAM_LR * (m_hat / (jnp.sqrt(v_hat) + ADAM_EPS) + ADAM_WD * w)
    return delta, m, v


def _adam(w, g, m, v, name):
    r, c = w.shape
    parts = g.ndim == 3
    tr = _tile(r, 512, 8)

    def body(w_ref, g_ref, m_ref, v_ref, go_ref, d_ref, mo_ref, vo_ref):
        if parts:
            gv = g_ref[0].astype(F32)
            for d in range(1, N_DEV):
                gv = gv + g_ref[d].astype(F32)
        else:
            gv = g_ref[...]
        delta, mn, vn = _adam_math(w_ref[...], gv, m_ref[...], v_ref[...])
        go_ref[...] = gv
        d_ref[...] = delta
        mo_ref[...] = mn
        vo_ref[...] = vn

    blk = pl.BlockSpec((tr, c), lambda i: (i, 0))
    gblk = pl.BlockSpec((N_DEV, tr, c), lambda i: (0, i, 0)) if parts else blk
    out = _sds((r, c), F32)
    return pl.pallas_call(
        body, name=name, out_shape=(out,) * 4, grid=(r // tr,),
        in_specs=[blk, gblk, blk, blk], out_specs=[blk] * 4,
        compiler_params=_params(("arbitrary",)),
    )(w, g, m, v)


def _sum_parts(parts, name):
    _, r, c = parts.shape

    def body(p_ref, o_ref):
        acc = p_ref[0]
        for d in range(1, N_DEV):
            acc = acc + p_ref[d]
        o_ref[...] = acc

    return pl.pallas_call(body, name=name, out_shape=_sds((r, c), F32), compiler_params=_params())(parts)


SMALL_SHARD_AXIS = {
    "meta": 1, "ln_g": 2, "ln_b": 2, "lru_conv_w": 2, "lru_conv_b": None, "lru_w_a": None, "lru_b_a": None,
    "lru_w_x": None, "lru_b_x": None, "lru_lambda": None, "ssd_conv_w": 2, "ssd_conv_b": None, "ssd_dt_bias": None,
    "ssd_a_log": None, "ssd_d": None, "ssd_norm_w": None, "dn_conv_w": 2, "dn_a_log": None, "dn_dt_bias": None,
    "dn_norm_w": None,
}
BIG = ("ffn_w_gate", "ffn_w_up", "ffn_w_down", "w_in", "w_out")
WEIGHT_ORDER = ("meta", "ln_g", "ln_b", "ffn_w_gate", "ffn_w_up", "ffn_w_down", "w_in", "lru_conv_w", "lru_conv_b",
                "lru_w_a", "lru_b_a", "lru_w_x", "lru_b_x", "lru_lambda", "ssd_conv_w", "ssd_conv_b", "ssd_dt_bias",
                "ssd_a_log", "ssd_d", "ssd_norm_w", "dn_conv_w", "dn_a_log", "dn_dt_bias", "dn_norm_w", "w_out")


def _pack(arrs):
    flat = jnp.concatenate([a.reshape(-1) for a in arrs])
    rows = -(-flat.shape[0] // (8 * LANE)) * 8
    return jnp.pad(flat, (0, rows * LANE - flat.shape[0])).reshape(rows, LANE)


def _unpack(buf, shapes, lead=()):
    flat = buf.reshape(lead + (-1,))
    out, off = [], 0
    for s in shapes:
        n = math.prod(s)
        out.append(flat[..., off:off + n].reshape(lead + tuple(s)))
        off += n
    return out


def _proj_cols(w):
    pad = jnp.zeros(w.shape[:-1] + (PROJ_W - D_IN,), w.dtype)
    return jnp.concatenate([w[..., 0:256], w[..., 1024:1792], w[..., 1800:2568], w[..., 256:512], w[..., 512:1024],
                            w[..., 2568:2824], w[..., 1792:1800], w[..., 2824:2832], pad], axis=-1)


def _proj_cols_inv(w):
    return jnp.concatenate([w[..., 0:256], w[..., 1792:2048], w[..., 2048:2560], w[..., 256:1024], w[..., 2816:2824],
                            w[..., 1024:1792], w[..., 2560:2816], w[..., 2824:2832]], axis=-1)


def kernel(x, meta, ln_g, ln_b, ffn_w_gate, ffn_w_up, ffn_w_down, w_in, lru_conv_w, lru_conv_b, lru_w_a, lru_b_a, lru_w_x, lru_b_x, lru_lambda, ssd_conv_w, ssd_conv_b, ssd_dt_bias, ssd_a_log, ssd_d, ssd_norm_w, dn_conv_w, dn_a_log, dn_dt_bias, dn_norm_w, w_out, loss_target, m_meta, m_ln_g, m_ln_b, m_ffn_w_gate, m_ffn_w_up, m_ffn_w_down, m_w_in, m_lru_conv_w, m_lru_conv_b, m_lru_w_a, m_lru_b_a, m_lru_w_x, m_lru_b_x, m_lru_lambda, m_ssd_conv_w, m_ssd_conv_b, m_ssd_dt_bias, m_ssd_a_log, m_ssd_d, m_ssd_norm_w, m_dn_conv_w, m_dn_a_log, m_dn_dt_bias, m_dn_norm_w, m_w_out, v_meta, v_ln_g, v_ln_b, v_ffn_w_gate, v_ffn_w_up, v_ffn_w_down, v_w_in, v_lru_conv_w, v_lru_conv_b, v_lru_w_a, v_lru_b_a, v_lru_w_x, v_lru_b_x, v_lru_lambda, v_ssd_conv_w, v_ssd_conv_b, v_ssd_dt_bias, v_ssd_a_log, v_ssd_d, v_ssd_norm_w, v_dn_conv_w, v_dn_a_log, v_dn_dt_bias, v_dn_norm_w, v_w_out):
    w = dict(meta=meta, ln_g=ln_g, ln_b=ln_b, ffn_w_gate=ffn_w_gate, ffn_w_up=ffn_w_up, ffn_w_down=ffn_w_down, w_in=w_in,
             lru_conv_w=lru_conv_w, lru_conv_b=lru_conv_b, lru_w_a=lru_w_a, lru_b_a=lru_b_a, lru_w_x=lru_w_x,
             lru_b_x=lru_b_x, lru_lambda=lru_lambda, ssd_conv_w=ssd_conv_w, ssd_conv_b=ssd_conv_b, ssd_dt_bias=ssd_dt_bias,
             ssd_a_log=ssd_a_log, ssd_d=ssd_d, ssd_norm_w=ssd_norm_w, dn_conv_w=dn_conv_w, dn_a_log=dn_a_log,
             dn_dt_bias=dn_dt_bias, dn_norm_w=dn_norm_w, w_out=w_out)
    m = dict(meta=m_meta, ln_g=m_ln_g, ln_b=m_ln_b, ffn_w_gate=m_ffn_w_gate, ffn_w_up=m_ffn_w_up, ffn_w_down=m_ffn_w_down,
             w_in=m_w_in, lru_conv_w=m_lru_conv_w, lru_conv_b=m_lru_conv_b, lru_w_a=m_lru_w_a, lru_b_a=m_lru_b_a,
             lru_w_x=m_lru_w_x, lru_b_x=m_lru_b_x, lru_lambda=m_lru_lambda, ssd_conv_w=m_ssd_conv_w, ssd_conv_b=m_ssd_conv_b,
             ssd_dt_bias=m_ssd_dt_bias, ssd_a_log=m_ssd_a_log, ssd_d=m_ssd_d, ssd_norm_w=m_ssd_norm_w, dn_conv_w=m_dn_conv_w,
             dn_a_log=m_dn_a_log, dn_dt_bias=m_dn_dt_bias, dn_norm_w=m_dn_norm_w, w_out=m_w_out)
    v = dict(meta=v_meta, ln_g=v_ln_g, ln_b=v_ln_b, ffn_w_gate=v_ffn_w_gate, ffn_w_up=v_ffn_w_up, ffn_w_down=v_ffn_w_down,
             w_in=v_w_in, lru_conv_w=v_lru_conv_w, lru_conv_b=v_lru_conv_b, lru_w_a=v_lru_w_a, lru_b_a=v_lru_b_a,
             lru_w_x=v_lru_w_x, lru_b_x=v_lru_b_x, lru_lambda=v_lru_lambda, ssd_conv_w=v_ssd_conv_w, ssd_conv_b=v_ssd_conv_b,
             ssd_dt_bias=v_ssd_dt_bias, ssd_a_log=v_ssd_a_log, ssd_d=v_ssd_d, ssd_norm_w=v_ssd_norm_w, dn_conv_w=v_dn_conv_w,
             dn_a_log=v_dn_a_log, dn_dt_bias=v_dn_dt_bias, dn_norm_w=v_dn_norm_w, w_out=v_w_out)
    depth = ln_g.shape[0]
    me = _flat(_mesh_pos())
    small_names = tuple(SMALL_SHARD_AXIS)
    sharded = tuple(k for k in small_names if SMALL_SHARD_AXIS[k] is not None)

    shards = [[w[k][l].astype(BF16) for k in BIG] for l in range(depth)]
    got = _exchange(shards[0] + [_pack([w[k] for k in sharded])], [False] * 6, "gather_layer0")
    small = {k: w[k] for k in small_names if SMALL_SHARD_AXIS[k] is None}
    for k, piece in zip(sharded, _unpack(got[5], [w[k].shape for k in sharded], lead=(N_DEV,))):
        ax = SMALL_SHARD_AXIS[k]
        full = jnp.moveaxis(piece, 0, ax)
        small[k] = full.reshape(full.shape[:ax] + (N_DEV * w[k].shape[ax],) + full.shape[ax + 2:])

    loss, dx, small_grads, slabs = _local_step(x[0], loss_target[0], small, [got[:5]] + [None] * (depth - 1), shards)

    p_small = _pack([small_grads[k] for k in small_names])
    last = _exchange(list(slabs[0]) + [p_small], [True] * 5 + [False], "exchange_layer0")
    slabs[0], r_small = last[:5], last[5]

    outs = {}
    for i, k in enumerate(BIG):
        shp = w[k].shape
        two = lambda a: a.reshape(-1, shp[-1])
        parts = jnp.stack([slabs[l][i] for l in range(depth)], axis=1)
        res = _adam(two(w[k]), parts.reshape(N_DEV, -1, shp[-1]), two(m[k]), two(v[k]), "adam_" + k)
        outs[k] = [a.reshape(shp) for a in res]
    g_full = _unpack(_sum_parts(r_small, "sum_small_grads"), [small[k].shape for k in small_names])
    g_loc = {}
    for k, gf in zip(small_names, g_full):
        ax = SMALL_SHARD_AXIS[k]
        g_loc[k] = gf if ax is None else lax.dynamic_slice_in_dim(gf, me * w[k].shape[ax], w[k].shape[ax], axis=ax)
    res = _adam(_pack([w[k] for k in small_names]), _pack([g_loc[k] for k in small_names]),
                _pack([m[k] for k in small_names]), _pack([v[k] for k in small_names]), "adam_small")
    shapes = [w[k].shape for k in small_names]
    for i, k in enumerate(small_names):
        outs[k] = [_unpack(r, shapes)[i] for r in res]

    loss = lax.psum(loss, ("x", "y", "c"))
    return (loss, dx[None], *[outs[k][0] for k in WEIGHT_ORDER], *[outs[k][1] for k in WEIGHT_ORDER],
            *[outs[k][2] for k in WEIGHT_ORDER], *[outs[k][3] for k in WEIGHT_ORDER])
```

```python
import functools
import math

import jax
import jax.numpy as jnp
from jax import lax
from jax.experimental import pallas as pl
from jax.experimental.pallas import tpu as pltpu

F32 = jnp.float32
BF16 = jnp.bfloat16
HI = lax.Precision.HIGHEST
MESH_IDS = pl.DeviceIdType.MESH

N_DEV = 8
D_MODEL = 1024
DEPTH = 4
N_META = 16
CHUNK = 64
CONV_K = 4
D_FF = 2816
LRU_WIDTH = 256
LRU_HEADS = 4
LRU_C = 8.0
SSD_HEADS = 8
SSD_GROUPS = 2
SSD_HPG = 4
SSD_INNER = 512
DN_HEADS = 4
HEAD = 64
CONV_W = 1792
PROJ_W = 2944
OFF_LRU_Y, OFF_SSD_Z, OFF_DN_GATE, OFF_SMALL = 1792, 2048, 2560, 2816
D_IN = 2832
ALPHA = (2 * DEPTH) ** 0.25
FFN_RES = 0.5
LN_EPS = 1e-5
RMS_EPS = 1e-6
ADAM_LR, ADAM_B1, ADAM_B2, ADAM_EPS, ADAM_WD, ADAM_STEP = 0.001, 0.9, 0.999, 1e-08, 0.01, 10

VMEM_LIMIT = 56 * 1024 * 1024
ROW_TILE_CAP = 416
LANE = 128


def _tile(n, cap, mult=16):
    best = None
    for t in range(mult, min(n, cap) + 1, mult):
        if n % t == 0:
            best = t
    assert best is not None, (n, cap, mult)
    return best


def _params(sem=None):
    return pltpu.CompilerParams(dimension_semantics=sem, vmem_limit_bytes=VMEM_LIMIT)


def _resident(shape, index_map):
    return pl.BlockSpec(shape, index_map, pipeline_mode=pl.Buffered(1))


def _dot(a, b):
    return jnp.dot(a, b, preferred_element_type=F32)


def _dot_nt(a, b):
    return lax.dot_general(a, b, (((1,), (1,)), ((), ())), preferred_element_type=F32)


def _dot_tn(a, b):
    return lax.dot_general(a, b, (((0,), (0,)), ((), ())), preferred_element_type=F32)


def _sds(shape, dtype):
    return jax.ShapeDtypeStruct(shape, dtype)


US_PER_MB = 92.5


class _Unit:
    def __init__(self, arr, slab):
        self.arr, self.slab, self.out = arr, slab, None
        per_peer = arr.size * arr.dtype.itemsize / (N_DEV if slab else 1)
        self.us = per_peer / 1e6 * US_PER_MB


_QUEUE = []
_STANDALONE = [0]


def _enqueue(arr, slab):
    unit = _Unit(arr, slab)
    _QUEUE.append(unit)
    return unit


def _take_units(host_us):
    units = []
    while _QUEUE and host_us >= 0.6 * _QUEUE[0].us:
        host_us -= _QUEUE[0].us
        units.append(_QUEUE.pop(0))
    return units


def _collect(unit):
    if unit.out is None:
        n = _QUEUE.index(unit) + 1
        units = [_QUEUE.pop(0) for _ in range(n)]
        _STANDALONE[0] += 1
        res = _exchange([u.arr for u in units], [u.slab for u in units], f"exchange_{_STANDALONE[0]}")
        for u, r in zip(units, res):
            u.out = r
    return unit.out


def _call(body, host_us, *, name, out_shape, in_specs, out_specs, grid=(), scratch_shapes=(), compiler_params=None):
    units = _take_units(host_us)
    kw = dict(name=name, grid=grid, compiler_params=compiler_params)
    if not units:
        return pl.pallas_call(body, out_shape=out_shape, in_specs=in_specs, out_specs=out_specs,
                              scratch_shapes=list(scratch_shapes), **kw)
    single = not isinstance(out_shape, (tuple, list))
    outs = (out_shape,) if single else tuple(out_shape)
    ospecs = [out_specs] if single else list(out_specs)
    nin, nout, nscr, ncm = len(in_specs), len(outs), len(scratch_shapes), len(units)
    slabs = [u.slab for u in units]

    def hosted(*refs):
        ins, c_in = refs[:nin], refs[nin:nin + ncm]
        o, c_out = refs[nin + ncm:nin + ncm + nout], refs[nin + ncm + nout:nin + 2 * ncm + nout]
        scr, sems = refs[nin + 2 * ncm + nout:nin + 2 * ncm + nout + nscr], refs[nin + 2 * ncm + nout + nscr:]
        ids = [pl.program_id(d) for d in range(len(grid))]
        first = functools.reduce(jnp.logical_and, [i == 0 for i in ids])
        last = functools.reduce(jnp.logical_and, [i == g - 1 for i, g in zip(ids, grid)])

        @pl.when(first)
        def _():
            _exchange_start(c_in, c_out, slabs, *sems)
        body(*ins, *o, *scr)

        @pl.when(last)
        def _():
            _exchange_wait(c_in, c_out, slabs, *sems)

    hbm = pl.BlockSpec(memory_space=pl.ANY)
    fn = pl.pallas_call(
        hosted, out_shape=outs + _exchange_shapes([u.arr for u in units], slabs), in_specs=list(in_specs) + [hbm] * ncm,
        out_specs=ospecs + [hbm] * ncm, scratch_shapes=list(scratch_shapes) + _exchange_sems(ncm), **kw)

    def run(*args):
        res = fn(*args, *[u.arr for u in units])
        for u, r in zip(units, res[nout:]):
            u.out = r
        return res[0] if single else tuple(res[:nout])

    return run


def _ffn_up(xb, wgu, name):
    tp, d = xb.shape
    f = wgu.shape[1] // 2
    tm = _tile(tp, ROW_TILE_CAP)
    tn = _tile(f, 1408, LANE)
    nj = f // tn

    def body(x_ref, wg_ref, wu_ref, g_ref, u_ref, a_ref):
        x = x_ref[...]
        g = _dot(x, wg_ref[...])
        u = _dot(x, wu_ref[...])
        g_ref[...] = g.astype(BF16)
        u_ref[...] = u.astype(BF16)
        a_ref[...] = (g * jax.nn.sigmoid(g) * u).astype(BF16)

    out = _sds((tp, f), BF16)
    return _call(
        body, 60, name=name, out_shape=(out, out, out), grid=(nj, tp // tm),
        in_specs=[pl.BlockSpec((tm, d), lambda j, i: (i, 0)),
                  pl.BlockSpec((d, tn), lambda j, i: (0, j)),
                  pl.BlockSpec((d, tn), lambda j, i: (0, j + nj))],
        out_specs=[pl.BlockSpec((tm, tn), lambda j, i: (i, j))] * 3,
        compiler_params=_params(("arbitrary", "arbitrary")),
    )(xb, wgu, wgu)


def _mm_resid_ln(a, w, h, gamma, beta, scale, name):
    tp, k = a.shape
    d = w.shape[1]
    tm = _tile(tp, ROW_TILE_CAP)

    def body(a_ref, w_ref, h_ref, g_ref, b_ref, y_ref, o_ref, ob_ref):
        y = ALPHA * h_ref[...] + scale * _dot(a_ref[...], w_ref[...])
        mu = jnp.mean(y, axis=-1, keepdims=True)
        yc = y - mu
        var = jnp.mean(yc * yc, axis=-1, keepdims=True)
        o = yc * lax.rsqrt(var + LN_EPS) * g_ref[...] + b_ref[...]
        y_ref[...] = y
        o_ref[...] = o
        ob_ref[...] = o.astype(BF16)

    row = lambda i: (i, 0)
    fix = lambda i: (0, 0)
    return _call(
        body, 30, name=name, out_shape=(_sds((tp, d), F32), _sds((tp, d), F32), _sds((tp, d), BF16)),
        grid=(tp // tm,),
        in_specs=[pl.BlockSpec((tm, k), row), _resident((k, d), fix), pl.BlockSpec((tm, d), row),
                  pl.BlockSpec((1, d), fix), pl.BlockSpec((1, d), fix)],
        out_specs=[pl.BlockSpec((tm, d), row)] * 3,
        compiler_params=_params(("arbitrary",)),
    )(a, w, h, gamma, beta)


def _mm_nn(xb, w, name):
    tp, k = xb.shape
    n = w.shape[1]
    tm = _tile(tp, ROW_TILE_CAP)

    def body(x_ref, w_ref, o_ref):
        o_ref[...] = _dot(x_ref[...], w_ref[...])

    return _call(
        body, 30, name=name, out_shape=_sds((tp, n), F32), grid=(tp // tm,),
        in_specs=[pl.BlockSpec((tm, k), lambda i: (i, 0)), _resident((k, n), lambda i: (0, 0))],
        out_specs=pl.BlockSpec((tm, n), lambda i: (i, 0)),
        compiler_params=_params(("arbitrary",)),
    )(xb, w)


def _mm_nt_add(a, w, resid, name):
    tp, k = a.shape
    n = w.shape[0]
    tm = _tile(tp, ROW_TILE_CAP)
    has_resid = resid is not None

    def body(*refs):
        if has_resid:
            a_ref, w_ref, r_ref, o_ref = refs
            o_ref[...] = r_ref[...] + _dot_nt(a_ref[...], w_ref[...])
        else:
            a_ref, w_ref, o_ref = refs
            o_ref[...] = _dot_nt(a_ref[...], w_ref[...])

    in_specs = [pl.BlockSpec((tm, k), lambda i: (i, 0)), _resident((n, k), lambda i: (0, 0))]
    args = [a, w]
    if has_resid:
        in_specs.append(pl.BlockSpec((tm, n), lambda i: (i, 0)))
        args.append(resid)
    return _call(
        body, 25, name=name, out_shape=_sds((tp, n), F32), grid=(tp // tm,),
        in_specs=in_specs, out_specs=pl.BlockSpec((tm, n), lambda i: (i, 0)),
        compiler_params=_params(("arbitrary",)),
    )(*args)


def _ffn_dx(dg, du, wgu, resid, name):
    tp, f = dg.shape
    d = wgu.shape[0]
    tm = _tile(tp, ROW_TILE_CAP)

    def body(dg_ref, du_ref, w_ref, r_ref, o_ref):
        acc = r_ref[...] + _dot_nt(dg_ref[...], w_ref[:, 0:f])
        o_ref[...] = acc + _dot_nt(du_ref[...], w_ref[:, f:2 * f])

    row = lambda i: (i, 0)
    return _call(
        body, 55, name=name, out_shape=_sds((tp, d), F32), grid=(tp // tm,),
        in_specs=[pl.BlockSpec((tm, f), row), pl.BlockSpec((tm, f), row), _resident((d, 2 * f), lambda i: (0, 0)),
                  pl.BlockSpec((tm, d), row)],
        out_specs=pl.BlockSpec((tm, d), row),
        compiler_params=_params(("arbitrary",)),
    )(dg, du, wgu, resid)


def _mm_tn(a, b, name):
    tp, ka = a.shape
    nb = b.shape[1]
    tt = _tile(tp, 2 * ROW_TILE_CAP)
    tk = _tile(ka, 1408, LANE)
    tn = _tile(nb, 1536, LANE)
    if tn < 512:
        tn = nb
    nt = tp // tt

    def body(a_ref, b_ref, o_ref):
        @pl.when(pl.program_id(2) == 0)
        def _():
            o_ref[...] = jnp.zeros_like(o_ref)
        o_ref[...] += _dot_tn(a_ref[...], b_ref[...])

    return _call(
        body, 40, name=name, out_shape=_sds((ka, nb), F32), grid=(ka // tk, nb // tn, nt),
        in_specs=[pl.BlockSpec((tt, tk), lambda i, j, t: (t, i)), pl.BlockSpec((tt, tn), lambda i, j, t: (t, j))],
        out_specs=pl.BlockSpec((tk, tn), lambda i, j, t: (i, j)),
        compiler_params=_params(("arbitrary", "arbitrary", "arbitrary")),
    )(a, b)


def _ln_bwd(y, dout, gamma, scale, name):
    tp, d = y.shape
    tm = _tile(tp, ROW_TILE_CAP)

    def body(y_ref, do_ref, g_ref, dres_ref, dyb_ref, dg_ref, db_ref):
        @pl.when(pl.program_id(0) == 0)
        def _():
            dg_ref[...] = jnp.zeros_like(dg_ref)
            db_ref[...] = jnp.zeros_like(db_ref)
        yv = y_ref[...]
        do = do_ref[...]
        mu = jnp.mean(yv, axis=-1, keepdims=True)
        yc = yv - mu
        var = jnp.mean(yc * yc, axis=-1, keepdims=True)
        rstd = lax.rsqrt(var + LN_EPS)
        xhat = yc * rstd
        dxh = do * g_ref[...]
        m1 = jnp.mean(dxh, axis=-1, keepdims=True)
        m2 = jnp.mean(dxh * xhat, axis=-1, keepdims=True)
        dy = rstd * (dxh - m1 - xhat * m2)
        dres_ref[...] = ALPHA * dy
        dyb_ref[...] = (scale * dy).astype(BF16)
        dg_ref[...] += jnp.sum(do * xhat, axis=0, keepdims=True)
        db_ref[...] += jnp.sum(do, axis=0, keepdims=True)

    row = lambda i: (i, 0)
    fix = lambda i: (0, 0)
    return _call(
        body, 22, name=name,
        out_shape=(_sds((tp, d), F32), _sds((tp, d), BF16), _sds((1, d), F32), _sds((1, d), F32)),
        grid=(tp // tm,),
        in_specs=[pl.BlockSpec((tm, d), row), pl.BlockSpec((tm, d), row), pl.BlockSpec((1, d), fix)],
        out_specs=[pl.BlockSpec((tm, d), row), pl.BlockSpec((tm, d), row), pl.BlockSpec((1, d), fix),
                   pl.BlockSpec((1, d), fix)],
        compiler_params=_params(("arbitrary",)),
    )(y, dout, gamma)


def _ffn_dact(dyb, wd, g, u, name):
    tp, d = dyb.shape
    f = wd.shape[0]
    tm = _tile(tp, ROW_TILE_CAP)
    tn = _tile(f, 1408, LANE)

    def body(dy_ref, w_ref, g_ref, u_ref, dg_ref, du_ref):
        dact = _dot_nt(dy_ref[...], w_ref[...])
        gv = g_ref[...].astype(F32)
        uv = u_ref[...].astype(F32)
        sg = jax.nn.sigmoid(gv)
        dg_ref[...] = (dact * uv * (sg * (1.0 + gv * (1.0 - sg)))).astype(BF16)
        du_ref[...] = (dact * (gv * sg)).astype(BF16)

    out = _sds((tp, f), BF16)
    blk = pl.BlockSpec((tm, tn), lambda j, i: (i, j))
    return _call(
        body, 55, name=name, out_shape=(out, out), grid=(f // tn, tp // tm),
        in_specs=[pl.BlockSpec((tm, d), lambda j, i: (i, 0)), pl.BlockSpec((tn, d), lambda j, i: (j, 0)), blk, blk],
        out_specs=[blk, blk],
        compiler_params=_params(("arbitrary", "arbitrary")),
    )(dyb, wd, g, u)


def _loss_grad(o, tgt, t_real, name):
    tp, d = o.shape
    tm = _tile(tp, ROW_TILE_CAP)

    def body(o_ref, t_ref, l_ref, d_ref):
        i = pl.program_id(0)

        @pl.when(i == 0)
        def _():
            l_ref[...] = jnp.zeros_like(l_ref)
        rows = i * tm + lax.broadcasted_iota(jnp.int32, (tm, 1), 0)
        real = jnp.logical_and(rows >= N_META, rows < t_real)
        err = jnp.where(real, o_ref[...] - t_ref[...], 0.0)
        d_ref[...] = err * (1.0 / d)
        l_ref[...] += jnp.sum(err * err, axis=0, keepdims=True) * (0.5 / d)

    row = lambda i: (i, 0)
    return pl.pallas_call(
        body, name=name, out_shape=(_sds((1, d), F32), _sds((tp, d), F32)), grid=(tp // tm,),
        in_specs=[pl.BlockSpec((tm, d), row), pl.BlockSpec((tm, d), row)],
        out_specs=[pl.BlockSpec((1, d), lambda i: (0, 0)), pl.BlockSpec((tm, d), row)],
        compiler_params=_params(("arbitrary",)),
    )(o, tgt)


CONV_TC = 256


def _silu_grad(y):
    s = jax.nn.sigmoid(y)
    return s * (1.0 + y * (1.0 - s))


def _conv_taps(x_ref, w, r0, rb):
    cur = x_ref[r0:r0 + rb, :]
    prev = x_ref[r0 - 8:r0, :] if r0 > 0 else jnp.zeros((8, cur.shape[1]), F32)
    xcat = jnp.concatenate([prev, cur], axis=0)
    taps = [xcat[5 + j:5 + j + rb] for j in range(CONV_K - 1)] + [cur]
    y = w[0:1] * taps[0]
    for j in range(1, CONV_K):
        y = y + w[j:j + 1] * taps[j]
    return y, taps


def _conv_fwd(proj, w, b, name):
    tp = proj.shape[0]
    rb = _tile(tp, ROW_TILE_CAP, 8)

    def body(x_ref, w_ref, b_ref, o_ref):
        gated = pl.program_id(0) > 0
        wv = w_ref[...]
        bv = b_ref[...]
        for r0 in range(0, tp, rb):
            y, _ = _conv_taps(x_ref, wv, r0, rb)
            y = y + bv
            o_ref[r0:r0 + rb, :] = jnp.where(gated, y * jax.nn.sigmoid(y), y)

    col = lambda j: (0, j)
    return _call(
        body, 25, name=name, out_shape=_sds((tp, CONV_W), F32), grid=(CONV_W // CONV_TC,),
        in_specs=[pl.BlockSpec((tp, CONV_TC), col), pl.BlockSpec((CONV_K, CONV_TC), col), pl.BlockSpec((1, CONV_TC), col)],
        out_specs=pl.BlockSpec((tp, CONV_TC), col),
        compiler_params=_params(("arbitrary",)),
    )(proj, w, b)


def _conv_bwd(proj, w, b, dxc, name):
    tp = proj.shape[0]
    rb = _tile(tp, ROW_TILE_CAP, 8)

    def body(x_ref, w_ref, b_ref, d_ref, dx_ref, dw_ref, db_ref, dy_scr):
        gated = pl.program_id(0) > 0
        wv = w_ref[...]
        bv = b_ref[...]
        dw = [jnp.zeros((1, CONV_TC), F32) for _ in range(CONV_K)]
        db = jnp.zeros((1, CONV_TC), F32)
        for r0 in range(0, tp, rb):
            y, taps = _conv_taps(x_ref, wv, r0, rb)
            y = y + bv
            d = d_ref[r0:r0 + rb, :]
            dy = jnp.where(gated, d * _silu_grad(y), d)
            dy_scr[r0:r0 + rb, :] = dy
            for j in range(CONV_K):
                dw[j] = dw[j] + jnp.sum(dy * taps[j], axis=0, keepdims=True)
            db = db + jnp.sum(dy, axis=0, keepdims=True)
        for j in range(CONV_K):
            dw_ref[j:j + 1, :] = dw[j]
        db_ref[...] = db
        for r0 in range(0, tp, rb):
            cur = dy_scr[r0:r0 + rb, :]
            nxt = dy_scr[r0 + rb:r0 + rb + 8, :] if r0 + rb < tp else jnp.zeros((8, CONV_TC), F32)
            dcat = jnp.concatenate([cur, nxt], axis=0)
            dx = wv[3:4] * cur
            for s in range(1, CONV_K):
                dx = dx + wv[3 - s:4 - s] * dcat[s:s + rb]
            dx_ref[r0:r0 + rb, :] = dx

    col = lambda j: (0, j)
    return _call(
        body, 70, name=name,
        out_shape=(_sds((tp, CONV_W), F32), _sds((CONV_K, CONV_W), F32), _sds((1, CONV_W), F32)),
        grid=(CONV_W // CONV_TC,),
        in_specs=[pl.BlockSpec((tp, CONV_TC), col), pl.BlockSpec((CONV_K, CONV_TC), col), pl.BlockSpec((1, CONV_TC), col),
                  pl.BlockSpec((tp, CONV_TC), col)],
        out_specs=[pl.BlockSpec((tp, CONV_TC), col), pl.BlockSpec((CONV_K, CONV_TC), col), pl.BlockSpec((1, CONV_TC), col)],
        scratch_shapes=[pltpu.VMEM((tp, CONV_TC), F32)],
        compiler_params=_params(("arbitrary",)),
    )(proj, w, b, dxc)


def _neg_expm1(x):
    series = -x * (1.0 + x * (0.5 + x * (1.0 / 6.0 + x * (1.0 / 24.0))))
    return jnp.where(jnp.abs(x) < 0.03, series, 1.0 - jnp.exp(x))


def _lru_gates(u, wa, ba, wx, bx, lam):
    r = jax.nn.sigmoid(jnp.dot(u, wa, precision=HI, preferred_element_type=F32) + ba)
    i = jax.nn.sigmoid(jnp.dot(u, wx, precision=HI, preferred_element_type=F32) + bx)
    log_a = -LRU_C * r * jax.nn.softplus(-lam)
    a = jnp.exp(log_a)
    b = jnp.sqrt(_neg_expm1(2.0 * log_a)) * (i * u)
    return a, b


def _lru_specs(tm, flip, n_tiles):
    idx = (lambda i: (n_tiles - 1 - i, 0)) if flip else (lambda i: (i, 0))
    return idx, lambda i: (0, 0)


def _lru_fwd(xc, proj, wa, ba, wx, bx, lam, name):
    tp = xc.shape[0]
    w = LRU_WIDTH
    tm = _tile(tp, ROW_TILE_CAP, 8)
    ycol = OFF_LRU_Y // w

    def body(u_ref, y_ref, wa_ref, ba_ref, wx_ref, bx_ref, lam_ref, o_ref, h_ref, a_scr, b_scr, carry):
        @pl.when(pl.program_id(0) == 0)
        def _():
            carry[...] = jnp.zeros_like(carry)
        a, b = _lru_gates(u_ref[...], wa_ref[...], ba_ref[...], wx_ref[...], bx_ref[...], lam_ref[...])
        a_scr[...] = a
        b_scr[...] = b

        def step(t, h):
            h = a_scr[pl.ds(t, 1), :] * h + b_scr[pl.ds(t, 1), :]
            h_ref[pl.ds(t, 1), :] = h
            return h

        carry[0:1, :] = lax.fori_loop(0, tm, step, carry[0:1, :])
        o_ref[...] = h_ref[...] * jax.nn.gelu(y_ref[...])

    fix = lambda i: (0, 0)
    return _call(
        body, 40, name=name, out_shape=(_sds((tp, w), F32), _sds((tp, w), F32)), grid=(tp // tm,),
        in_specs=[pl.BlockSpec((tm, w), lambda i: (i, 0)), pl.BlockSpec((tm, w), lambda i: (i, ycol)),
                  pl.BlockSpec((w, w), fix), pl.BlockSpec((1, w), fix), pl.BlockSpec((w, w), fix),
                  pl.BlockSpec((1, w), fix), pl.BlockSpec((1, w), fix)],
        out_specs=[pl.BlockSpec((tm, w), lambda i: (i, 0))] * 2,
        scratch_shapes=[pltpu.VMEM((tm, w), F32), pltpu.VMEM((tm, w), F32), pltpu.VMEM((8, w), F32)],
        compiler_params=_params(("arbitrary",)),
    )(xc, proj, wa, ba, wx, bx, lam)


def _lru_bwd(xc, proj, h, dout, wa, ba, wx, bx, lam, name):
    tp = xc.shape[0]
    w = LRU_WIDTH
    tm = _tile(tp, ROW_TILE_CAP, 8)
    nt = tp // tm
    ycol = OFF_LRU_Y // w
    rev = lambda i: (nt - 1 - i, 0)
    prev8 = lambda i: (jnp.maximum((nt - 1 - i) * (tm // 8) - 1, 0), 0)

    def body(u_ref, y_ref, h_ref, hp_ref, do_ref, wa_ref, ba_ref, wx_ref, bx_ref, lam_ref,
             du_ref, dy_ref, dwa_ref, dba_ref, dwx_ref, dbx_ref, dlam_ref,
             a_scr, dh_scr, g_scr, da_scr, hext, carry):
        i = pl.program_id(0)

        @pl.when(i == 0)
        def _():
            carry[...] = jnp.zeros_like(carry)
            for r in (dwa_ref, dba_ref, dwx_ref, dbx_ref, dlam_ref):
                r[...] = jnp.zeros_like(r)
        params = (wa_ref[...], ba_ref[...], wx_ref[...], bx_ref[...], lam_ref[...])
        (a, _), gates_vjp = jax.vjp(_lru_gates, u_ref[...], *params)
        gel, gelu_vjp = jax.vjp(jax.nn.gelu, y_ref[...])
        do = do_ref[...]
        hv = h_ref[...]
        dy_ref[...] = gelu_vjp(do * hv)[0]
        a_scr[...] = a
        dh_scr[...] = do * gel
        hext[0:8, :] = jnp.where(i == nt - 1, 0.0, hp_ref[...])
        hext[8:8 + tm, :] = hv

        def step(s, c):
            t = tm - 1 - s
            g = dh_scr[pl.ds(t, 1), :] + c
            g_scr[pl.ds(t, 1), :] = g
            da_scr[pl.ds(t, 1), :] = g * hext[pl.ds(t + 7, 1), :]
            return a_scr[pl.ds(t, 1), :] * g

        carry[0:1, :] = lax.fori_loop(0, tm, step, carry[0:1, :])
        du, dwa, dba, dwx, dbx, dlam = gates_vjp((da_scr[...], g_scr[...]))
        du_ref[...] = du
        dwa_ref[...] += dwa
        dba_ref[...] += dba
        dwx_ref[...] += dwx
        dbx_ref[...] += dbx
        dlam_ref[...] += dlam

    fix = lambda i: (0, 0)
    tile = pl.BlockSpec((tm, w), rev)
    mat = pl.BlockSpec((w, w), fix)
    vec = pl.BlockSpec((1, w), fix)
    return _call(
        body, 90, name=name,
        out_shape=(_sds((tp, w), F32), _sds((tp, w), F32), _sds((w, w), F32), _sds((1, w), F32), _sds((w, w), F32),
                   _sds((1, w), F32), _sds((1, w), F32)),
        grid=(nt,),
        in_specs=[tile, pl.BlockSpec((tm, w), lambda i: (nt - 1 - i, ycol)), tile, pl.BlockSpec((8, w), prev8), tile,
                  mat, vec, mat, vec, vec],
        out_specs=[tile, tile, mat, vec, mat, vec, vec],
        scratch_shapes=[pltpu.VMEM((tm, w), F32)] * 4 + [pltpu.VMEM((tm + 8, w), F32), pltpu.VMEM((8, w), F32)],
        compiler_params=_params(("arbitrary",)),
    )(xc, proj, h, h, dout, wa, ba, wx, bx, lam)


def _dot3(a, b, dims):
    ah = a.astype(BF16)
    al = (a - ah.astype(F32)).astype(BF16)
    bh = b.astype(BF16)
    bl = (b - bh.astype(F32)).astype(BF16)
    dot = lambda x, y: lax.dot_general(x, y, (dims, ((0,), (0,))), preferred_element_type=F32)
    return dot(ah, bh) + (dot(ah, bl) + dot(al, bh))


@jax.custom_vjp
def _bmm(a, b):
    return _dot3(a, b, ((2,), (1,)))


@jax.custom_vjp
def _bmm_nt(a, b):
    return _dot3(a, b, ((2,), (2,)))


@jax.custom_vjp
def _bmm_tn(a, b):
    return _dot3(a, b, ((1,), (1,)))


_bmm.defvjp(lambda a, b: (_bmm(a, b), (a, b)), lambda r, g: (_bmm_nt(g, r[1]), _bmm_tn(r[0], g)))
_bmm_nt.defvjp(lambda a, b: (_bmm_nt(a, b), (a, b)), lambda r, g: (_bmm(g, r[1]), _bmm_tn(g, r[0])))
_bmm_tn.defvjp(lambda a, b: (_bmm_tn(a, b), (a, b)), lambda r, g: (_bmm_nt(r[1], g), _bmm(r[0], g)))


def _chunk_masks(nh):
    r = lax.broadcasted_iota(jnp.int32, (CHUNK, CHUNK), 0)
    c = lax.broadcasted_iota(jnp.int32, (CHUNK, CHUNK), 1)
    full = lambda m: jnp.broadcast_to(m[None], (nh, CHUNK, CHUNK))
    return r, c, full


def _decay_terms(g, nh):
    r, c, full = _chunk_masks(nh)
    incl = r >= c
    ones = jnp.ones((nh, CHUNK, CHUNK), F32)
    cs = _bmm(full(incl.astype(F32)), g)
    cs_t = _bmm(ones, g * full((r <= c).astype(F32)))
    tot = _bmm(ones, g)
    m = full(incl)
    decay = jnp.where(m, jnp.exp(jnp.where(m, cs - cs_t, 0.0)), 0.0)
    return cs, decay, tot


def _rep_groups(x):
    return jnp.concatenate([jnp.broadcast_to(x[g:g + 1], (SSD_HPG,) + x.shape[1:]) for g in range(SSD_GROUPS)], axis=0)


def _ssd_chunk(xs, bm, cm, dtr, z, p_dtb, p_alog, p_d, p_nw, state):
    nh = SSD_HEADS
    dt = jax.nn.softplus(dtr + p_dtb)
    a = dt * (-jnp.exp(p_alog))
    x = xs * dt
    cs, decay, tot = _decay_terms(a, nh)
    b8 = _rep_groups(bm)
    c8 = _rep_groups(cm)
    y = _bmm(_rep_groups(_bmm_nt(cm, bm)) * decay, x)
    y = y + _bmm_nt(c8, state) * jnp.exp(cs)
    new_state = state * jnp.exp(tot) + _bmm_tn(x * jnp.exp(tot - cs), b8)
    y = y + p_d * xs
    y = y * (z * jax.nn.sigmoid(z))
    ss = jnp.sum(y * y, axis=-1, keepdims=True)
    ssg = jnp.concatenate(
        [jnp.broadcast_to(jnp.sum(ss[g * SSD_HPG:(g + 1) * SSD_HPG], axis=0, keepdims=True), (SSD_HPG, CHUNK, 1))
         for g in range(SSD_GROUPS)], axis=0)
    y = y * lax.rsqrt(ssg * (1.0 / (SSD_HPG * HEAD)) + RMS_EPS) * p_nw
    return y, new_state


def _unit_lower_inverse(m, nh):
    r, c, full = _chunk_masks(nh)
    eye = full((r == c).astype(F32))
    md = jnp.where(full((r // 16) == (c // 16)), m, 0.0)
    mo = m - md
    x = eye - md
    p = _bmm(md, md)
    x = x + _bmm(x, p)
    p = _bmm(p, p)
    x = x + _bmm(x, p)
    p = _bmm(p, p)
    x = x + _bmm(x, p)
    n = _bmm(x, mo)
    y = x - _bmm(n, x)
    return y + _bmm(_bmm(n, n), y)


def _dn_chunk(q, k, v, gate, braw, araw, p_alog, p_dtb, p_nw, state):
    nh = DN_HEADS
    r, c, full = _chunk_masks(nh)
    q = q * lax.rsqrt(jnp.sum(q * q, axis=-1, keepdims=True) + RMS_EPS) * (HEAD ** -0.5)
    k = k * lax.rsqrt(jnp.sum(k * k, axis=-1, keepdims=True) + RMS_EPS)
    beta = jax.nn.sigmoid(braw)
    g = -jnp.exp(p_alog) * jax.nn.softplus(araw + p_dtb)
    gcs, decay, tot = _decay_terms(g, nh)
    kb = k * beta
    vb = v * beta
    m = jnp.where(full(r > c), _bmm_nt(kb, k) * decay, 0.0)
    t = _unit_lower_inverse(m, nh)
    egcs = jnp.exp(gcs)
    u = _bmm(t, vb)
    w = _bmm(t, kb * egcs)
    attn = _bmm_nt(q, k) * decay
    v_new = u - _bmm(w, state)
    out = _bmm(q * egcs, state) + _bmm(attn, v_new)
    new_state = state * jnp.exp(tot) + _bmm_tn(k * jnp.exp(tot - gcs), v_new)
    out = out * lax.rsqrt(jnp.mean(out * out, axis=-1, keepdims=True) + RMS_EPS) * p_nw
    return out * (gate * jax.nn.sigmoid(gate)), new_state


def _chunks_per_step(nc):
    return max(n for n in range(1, 6) if nc % n == 0)


def _chunk_scan_fwd(chunk_fn, seqs, params, nh, host_us, name):
    tp = seqs[0].shape[1]
    nc = tp // CHUNK
    nb = _chunks_per_step(nc)
    rows = nb * CHUNK
    ns, npar = len(seqs), len(params)

    def body(*refs):
        s_refs, p_refs = refs[:ns], refs[ns:ns + npar]
        y_ref, st_ref, state = refs[ns + npar:]

        @pl.when(pl.program_id(0) == 0)
        def _():
            state[...] = jnp.zeros_like(state)
        par = [r[...] for r in p_refs]
        st = state[...]
        for k in range(nb):
            sl = slice(k * CHUNK, (k + 1) * CHUNK)
            st_ref[k] = st
            y, st = chunk_fn(*[r[:, sl, :] for r in s_refs], *par, st)
            y_ref[:, sl, :] = y
        state[...] = st

    seq_spec = lambda a: pl.BlockSpec((a.shape[0], rows, HEAD), lambda c: (0, c, 0))
    par_spec = lambda a: pl.BlockSpec(a.shape, lambda c: (0, 0, 0))
    return _call(
        body, host_us, name=name,
        out_shape=(_sds((nh, tp, HEAD), F32), _sds((nc, nh, HEAD, HEAD), F32)),
        grid=(nc // nb,),
        in_specs=[seq_spec(a) for a in seqs] + [par_spec(a) for a in params],
        out_specs=[pl.BlockSpec((nh, rows, HEAD), lambda c: (0, c, 0)),
                   pl.BlockSpec((nb, nh, HEAD, HEAD), lambda c: (c, 0, 0, 0))],
        scratch_shapes=[pltpu.VMEM((nh, HEAD, HEAD), F32)],
        compiler_params=_params(("arbitrary",)),
    )(*seqs, *params)


def _chunk_scan_bwd(chunk_fn, seqs, params, states, dy, nh, host_us, name):
    tp = seqs[0].shape[1]
    nc = tp // CHUNK
    nb = _chunks_per_step(nc)
    rows = nb * CHUNK
    steps = nc // nb
    ns, npar = len(seqs), len(params)

    def body(*refs):
        s_refs, p_refs = refs[:ns], refs[ns:ns + npar]
        st_ref, dy_ref = refs[ns + npar:ns + npar + 2]
        ds_refs = refs[ns + npar + 2:2 * ns + npar + 2]
        dp_refs = refs[2 * ns + npar + 2:2 * ns + 2 * npar + 2]
        dstate = refs[-1]

        @pl.when(pl.program_id(0) == 0)
        def _():
            dstate[...] = jnp.zeros_like(dstate)
            for r in dp_refs:
                r[...] = jnp.zeros_like(r)
        par = [r[...] for r in p_refs]
        dst = dstate[...]
        dpar = None
        for k in reversed(range(nb)):
            sl = slice(k * CHUNK, (k + 1) * CHUNK)
            _, vjp = jax.vjp(chunk_fn, *[r[:, sl, :] for r in s_refs], *par, st_ref[k])
            grads = vjp((dy_ref[:, sl, :], dst))
            for r, gr in zip(ds_refs, grads[:ns]):
                r[:, sl, :] = gr
            gp = grads[ns:ns + npar]
            dpar = gp if dpar is None else [a + b for a, b in zip(dpar, gp)]
            dst = grads[-1]
        for r, gr in zip(dp_refs, dpar):
            r[...] += gr
        dstate[...] = dst

    seq_spec = lambda a: pl.BlockSpec((a.shape[0], rows, HEAD), lambda c: (0, steps - 1 - c, 0))
    par_spec = lambda a: pl.BlockSpec(a.shape, lambda c: (0, 0, 0))
    return tuple(_call(
        body, host_us, name=name,
        out_shape=tuple(_sds(a.shape, F32) for a in seqs) + tuple(_sds(a.shape, F32) for a in params),
        grid=(steps,),
        in_specs=[seq_spec(a) for a in seqs] + [par_spec(a) for a in params]
        + [pl.BlockSpec((nb, nh, HEAD, HEAD), lambda c: (steps - 1 - c, 0, 0, 0)), seq_spec(dy)],
        out_specs=[seq_spec(a) for a in seqs] + [par_spec(a) for a in params],
        scratch_shapes=[pltpu.VMEM((nh, HEAD, HEAD), F32)],
        compiler_params=_params(("arbitrary",)),
    )(*seqs, *params, states, dy))


def _hm(a, nh):
    return a.reshape(a.shape[0], nh, HEAD).transpose(1, 0, 2)


def _hm_inv(a):
    return a.transpose(1, 0, 2).reshape(a.shape[1], a.shape[0] * HEAD)


def _hm_scalar(a):
    return jnp.broadcast_to(a.T[:, :, None], (a.shape[1], a.shape[0], HEAD))


def _lane_param(p):
    return jnp.broadcast_to(p[:, None, None], (p.shape[0], 1, HEAD))


def _block_diag(w):
    out = jnp.zeros((LRU_WIDTH, LRU_WIDTH), F32)
    for h in range(LRU_HEADS):
        out = out.at[h * HEAD:(h + 1) * HEAD, h * HEAD:(h + 1) * HEAD].set(w[h])
    return out


def _block_diag_inv(w):
    return jnp.stack([w[h * HEAD:(h + 1) * HEAD, h * HEAD:(h + 1) * HEAD] for h in range(LRU_HEADS)])


def _ssd_inputs(xc, proj, lp):
    seqs = (_hm(xc[:, 256:768], SSD_HEADS), _hm(xc[:, 768:896], SSD_GROUPS), _hm(xc[:, 896:1024], SSD_GROUPS),
            _hm_scalar(proj[:, OFF_SMALL:OFF_SMALL + 8]), _hm(proj[:, OFF_SSD_Z:OFF_SSD_Z + SSD_INNER], SSD_HEADS))
    params = (_lane_param(lp["ssd_dt_bias"]), _lane_param(lp["ssd_a_log"]), _lane_param(lp["ssd_d"]),
              lp["ssd_norm_w"].reshape(SSD_HEADS, 1, HEAD))
    return seqs, params


def _dn_inputs(xc, proj, lp):
    seqs = (_hm(xc[:, 1024:1280], DN_HEADS), _hm(xc[:, 1280:1536], DN_HEADS), _hm(xc[:, 1536:1792], DN_HEADS),
            _hm(proj[:, OFF_DN_GATE:OFF_DN_GATE + 256], DN_HEADS),
            _hm_scalar(proj[:, OFF_SMALL + 8:OFF_SMALL + 12]), _hm_scalar(proj[:, OFF_SMALL + 12:OFF_SMALL + 16]))
    params = (_lane_param(lp["dn_a_log"]), _lane_param(lp["dn_dt_bias"]),
              jnp.broadcast_to(lp["dn_norm_w"][None, None, :], (DN_HEADS, 1, HEAD)))
    return seqs, params


def _lru_params(lp):
    return (_block_diag(lp["lru_w_a"]), lp["lru_b_a"][None], _block_diag(lp["lru_w_x"]), lp["lru_b_x"][None],
            lp["lru_lambda"][None])


def _conv_params(lp):
    w = jnp.concatenate([lp["lru_conv_w"], lp["ssd_conv_w"], lp["dn_conv_w"]], axis=1)
    b = jnp.concatenate([lp["lru_conv_b"], lp["ssd_conv_b"], jnp.zeros((768,), F32)])[None]
    return w, b


def _mixers_fwd(proj, lp, tag):
    cw, cb = _conv_params(lp)
    xc = _conv_fwd(proj, cw, cb, tag + "_conv")
    o_lru, h_lru = _lru_fwd(xc, proj, *_lru_params(lp), name=tag + "_lru")
    s_seqs, s_par = _ssd_inputs(xc, proj, lp)
    o_ssd, s_states = _chunk_scan_fwd(_ssd_chunk, s_seqs, s_par, SSD_HEADS, 60, tag + "_ssd")
    d_seqs, d_par = _dn_inputs(xc, proj, lp)
    o_dn, d_states = _chunk_scan_fwd(_dn_chunk, d_seqs, d_par, DN_HEADS, 100, tag + "_dn")
    cat = jnp.concatenate([o_lru, _hm_inv(o_ssd), _hm_inv(o_dn)], axis=1)
    return cat, (xc, h_lru, s_states, d_states)


def _mixers_bwd(proj, lp, saved, dcat, tag):
    xc, h_lru, s_states, d_states = saved
    tp = proj.shape[0]
    cw, cb = _conv_params(lp)
    g = {}
    du, dyraw, dwa, dba, dwx, dbx, dlam = _lru_bwd(xc, proj, h_lru, dcat[:, :LRU_WIDTH], *_lru_params(lp), name=tag + "_lru_bwd")
    g["lru_w_a"], g["lru_b_a"], g["lru_w_x"], g["lru_b_x"], g["lru_lambda"] = (
        _block_diag_inv(dwa), dba[0], _block_diag_inv(dwx), dbx[0], dlam[0])

    s_seqs, s_par = _ssd_inputs(xc, proj, lp)
    sg = _chunk_scan_bwd(_ssd_chunk, s_seqs, s_par, s_states, _hm(dcat[:, 256:768], SSD_HEADS), SSD_HEADS, 120,
                         tag + "_ssd_bwd")
    dxs, dbm, dcm, ddtr, dz, dp_dtb, dp_alog, dp_d, dp_nw = sg
    g["ssd_dt_bias"], g["ssd_a_log"], g["ssd_d"] = (jnp.sum(p, axis=(1, 2)) for p in (dp_dtb, dp_alog, dp_d))
    g["ssd_norm_w"] = dp_nw.reshape(SSD_INNER)

    d_seqs, d_par = _dn_inputs(xc, proj, lp)
    dg = _chunk_scan_bwd(_dn_chunk, d_seqs, d_par, d_states, _hm(dcat[:, 768:1024], DN_HEADS), DN_HEADS, 250,
                         tag + "_dn_bwd")
    dq, dk, dv, dgate, dbraw, daraw, dq_alog, dq_dtb, dq_nw = dg
    g["dn_a_log"], g["dn_dt_bias"] = (jnp.sum(p, axis=(1, 2)) for p in (dq_alog, dq_dtb))
    g["dn_norm_w"] = jnp.sum(dq_nw, axis=(0, 1))

    dxc = jnp.concatenate([du, _hm_inv(dxs), _hm_inv(dbm), _hm_inv(dcm), _hm_inv(dq), _hm_inv(dk), _hm_inv(dv)], axis=1)
    dconv, dcw, dcb = _conv_bwd(proj, cw, cb, dxc, tag + "_conv_bwd")
    g["lru_conv_w"], g["ssd_conv_w"], g["dn_conv_w"] = dcw[:, :256], dcw[:, 256:1024], dcw[:, 1024:]
    g["lru_conv_b"], g["ssd_conv_b"] = dcb[0, :256], dcb[0, 256:1024]
    scal = lambda a: jnp.sum(a, axis=2).T
    dsmall = jnp.concatenate([scal(ddtr), scal(dbraw), scal(daraw), jnp.zeros((tp, LANE - 16), F32)], axis=1)
    dproj = jnp.concatenate([dconv, dyraw, _hm_inv(dz), _hm_inv(dgate), dsmall], axis=1)
    return dproj, g


MIXER_PARAMS = ("lru_conv_w", "lru_conv_b", "lru_w_a", "lru_b_a", "lru_w_x", "lru_b_x", "lru_lambda",
                "ssd_conv_w", "ssd_conv_b", "ssd_dt_bias", "ssd_a_log", "ssd_d", "ssd_norm_w",
                "dn_conv_w", "dn_a_log", "dn_dt_bias", "dn_norm_w")


UNITS = ("gate0", "gate1", "up0", "up1", "down0", "down1", "win", "wout")


def _layer_shards(w, l):
    out = {"win": w["w_in"][l], "wout": w["w_out"][l]}
    for j in range(2):
        out[f"gate{j}"], out[f"up{j}"], out[f"down{j}"] = w["ffn_w_gate"][l, j], w["ffn_w_up"][l, j], w["ffn_w_down"][l, j]
    return {k: a.astype(BF16) for k, a in out.items()}


def _layer_weights(g):
    cols = lambda a: a.transpose(1, 0, 2).reshape(a.shape[1], -1)
    wgu = [jnp.concatenate([cols(g[f"gate{j}"]), cols(g[f"up{j}"])], axis=-1) for j in range(2)]
    wd = [g[f"down{j}"].reshape(-1, D_MODEL) for j in range(2)]
    return wgu, wd, _proj_cols(cols(g["win"])), g["wout"].reshape(D_MODEL, D_MODEL)


def _col_slabs(a):
    return a.reshape(a.shape[0], N_DEV, -1).transpose(1, 0, 2).astype(BF16)


def _row_slabs(a):
    return a.reshape(N_DEV, -1, a.shape[1]).astype(BF16)


def _local_step(x, tgt, small, first, shards):
    s = x.shape[0]
    t_real = N_META + s
    tp = -(-t_real // CHUNK) * CHUNK
    depth = len(first) if shards is None else len(shards)
    h = jnp.concatenate([small["meta"], x, jnp.zeros((tp - t_real, D_MODEL), F32)], axis=0)
    hb = h.astype(BF16)
    ln_g, ln_b = small["ln_g"], small["ln_b"]
    saved, weights = [], []
    pending = None
    for l in range(depth):
        lp = {k: small[k][l] for k in MIXER_PARAMS}
        t = f"l{l}"
        if shards is None:
            gathered = first[l]
        else:
            gathered = first if l == 0 else {k: _collect(u) for k, u in pending.items()}
            if l + 1 < depth:
                pending = {k: _enqueue(shards[l + 1][k], False) for k in UNITS}
        wgu, wd, win, wout = _layer_weights(gathered)
        weights.append((wgu, wd, win, wout))
        g0, u0, a0 = _ffn_up(hb, wgu[0], t + "_ffn0_up")
        y1, h1, h1b = _mm_resid_ln(a0, wd[0], h, ln_g[l, 0][None], ln_b[l, 0][None], FFN_RES, t + "_ffn0_down")
        proj = _mm_nn(h1b, win, t + "_in_proj")
        cat, mix_saved = _mixers_fwd(proj, lp, t)
        catb = cat.astype(BF16)
        y2, h2, h2b = _mm_resid_ln(catb, wout, h1, ln_g[l, 1][None], ln_b[l, 1][None], 1.0, t + "_out_proj")
        g1, u1, a1 = _ffn_up(h2b, wgu[1], t + "_ffn1_up")
        y3, h3, h3b = _mm_resid_ln(a1, wd[1], h2, ln_g[l, 2][None], ln_b[l, 2][None], FFN_RES, t + "_ffn1_down")
        saved.append((hb, g0, u0, a0, y1, h1b, proj, mix_saved, catb, y2, h2b, g1, u1, a1, y3))
        h, hb = h3, h3b

    tgt_p = jnp.pad(tgt, ((N_META, tp - t_real), (0, 0)))
    lossv, dh = _loss_grad(h, tgt_p, t_real, "loss")
    loss = jnp.sum(lossv)

    gs = {k: [None] * depth for k in MIXER_PARAMS}
    d_ln_g = [[None] * 3 for _ in range(depth)]
    d_ln_b = [[None] * 3 for _ in range(depth)]
    slabs = [{} for _ in range(depth)]
    send = (lambda a: a) if shards is None else (lambda a: _enqueue(a, True))

    def ffn_bwd(l, j, xb_in, g, u, a, y, dout):
        t = f"l{l}_ffn{j}"
        wgu, wd = weights[l][0][j], weights[l][1][j]
        dres, dyb, dgam, dbet = _ln_bwd(y, dout, ln_g[l, 2 * j][None], FFN_RES, t + "_ln_bwd")
        d_ln_g[l][2 * j], d_ln_b[l][2 * j] = dgam[0], dbet[0]
        slabs[l][f"down{j}"] = send(_row_slabs(_mm_tn(a, dyb, t + "_dwd")))
        dg, du = _ffn_dact(dyb, wd, g, u, t + "_dact")
        slabs[l][f"gate{j}"] = send(_col_slabs(_mm_tn(xb_in, dg, t + "_dwg")))
        slabs[l][f"up{j}"] = send(_col_slabs(_mm_tn(xb_in, du, t + "_dwu")))
        return _ffn_dx(dg, du, wgu, dres, t + "_dx")

    for l in reversed(range(depth)):
        hb_in, g0, u0, a0, y1, h1b, proj, mix_saved, catb, y2, h2b, g1, u1, a1, y3 = saved[l]
        lp = {k: small[k][l] for k in MIXER_PARAMS}
        t = f"l{l}"
        _, _, win, wout = weights[l]
        dh2 = ffn_bwd(l, 1, h2b, g1, u1, a1, y3, dh)
        dres, dyb, dgam, dbet = _ln_bwd(y2, dh2, ln_g[l, 1][None], 1.0, t + "_mix_ln_bwd")
        d_ln_g[l][1], d_ln_b[l][1] = dgam[0], dbet[0]
        slabs[l]["wout"] = send(_row_slabs(_mm_tn(catb, dyb, t + "_dwout")))
        dcat = _mm_nt_add(dyb, wout, None, t + "_dcat")
        dproj, mg = _mixers_bwd(proj, lp, mix_saved, dcat, t)
        for k in MIXER_PARAMS:
            gs[k][l] = mg[k]
        dprojb = dproj.astype(BF16)
        slabs[l]["win"] = send(_col_slabs(_proj_cols_inv(_mm_tn(h1b, dprojb, t + "_dwin"))))
        dh1 = _mm_nt_add(dprojb, win, dres, t + "_dh1")
        dh = ffn_bwd(l, 0, hb_in, g0, u0, a0, y1, dh1)

    small_grads = {k: jnp.stack(v) for k, v in gs.items()}
    small_grads["ln_g"] = jnp.stack([jnp.stack(r) for r in d_ln_g])
    small_grads["ln_b"] = jnp.stack([jnp.stack(r) for r in d_ln_b])
    small_grads["meta"] = dh[:N_META]
    return loss, dh[N_META:t_real], small_grads, slabs


def _mesh_pos():
    return lax.axis_index("x"), lax.axis_index("y"), lax.axis_index("c")


def _flip(pos, k):
    x, y, c = pos
    return (1 - x if k & 4 else x, 1 - y if k & 2 else y, 1 - c if k & 1 else c)


def _flat(pos):
    return 4 * pos[0] + 2 * pos[1] + pos[2]


def _exchange_copies(ins, outs, slabs, send, recv, loc):
    pos = _mesh_pos()
    me = _flat(pos)
    local, sends, receives = [], [], []
    for a in range(len(ins)):
        local.append(pltpu.make_async_copy(ins[a].at[me] if slabs[a] else ins[a], outs[a].at[me], loc.at[a]))
    for k in range(1, N_DEV):
        peer = _flip(pos, k)
        for a in range(len(ins)):
            sem = dict(send_sem=send.at[a, k - 1], recv_sem=recv.at[a, k - 1], device_id=peer, device_id_type=MESH_IDS)
            sends.append(pltpu.make_async_remote_copy(
                src_ref=ins[a].at[_flat(peer)] if slabs[a] else ins[a], dst_ref=outs[a].at[me], **sem))
            receives.append(pltpu.make_async_remote_copy(
                src_ref=ins[a].at[me] if slabs[a] else ins[a], dst_ref=outs[a].at[_flat(peer)], **sem))
    return local, sends, receives


def _exchange_start(ins, outs, slabs, send, recv, loc):
    local, sends, _ = _exchange_copies(ins, outs, slabs, send, recv, loc)
    for cp in local + sends:
        cp.start()


def _exchange_wait(ins, outs, slabs, send, recv, loc):
    local, sends, receives = _exchange_copies(ins, outs, slabs, send, recv, loc)
    for cp in receives:
        cp.wait_recv()
    for cp in sends:
        cp.wait_send()
    for cp in local:
        cp.wait()


def _exchange_shapes(arrs, slabs):
    return tuple(_sds(a.shape if s else (N_DEV,) + a.shape, a.dtype) for a, s in zip(arrs, slabs))


def _exchange_sems(n):
    return [pltpu.SemaphoreType.DMA((n, N_DEV - 1)), pltpu.SemaphoreType.DMA((n, N_DEV - 1)),
            pltpu.SemaphoreType.DMA((n,))]


def _exchange(arrs, slabs, name):
    n = len(arrs)

    def body(*refs):
        ins, outs, sems = refs[:n], refs[n:2 * n], refs[2 * n:]
        _exchange_start(ins, outs, slabs, *sems)
        _exchange_wait(ins, outs, slabs, *sems)

    hbm = pl.BlockSpec(memory_space=pl.ANY)
    return pl.pallas_call(
        body, name=name, out_shape=_exchange_shapes(arrs, slabs), in_specs=[hbm] * n, out_specs=[hbm] * n,
        scratch_shapes=_exchange_sems(n),
    )(*arrs)


def _adam_math(w, g, m, v):
    m = ADAM_B1 * m + (1.0 - ADAM_B1) * g
    v = ADAM_B2 * v + (1.0 - ADAM_B2) * (g * g)
    m_hat = m / (1.0 - ADAM_B1 ** ADAM_STEP)
    v_hat = v / (1.0 - ADAM_B2 ** ADAM_STEP)
    delta = -ADAM_LR * (m_hat / (jnp.sqrt(v_hat) + ADAM_EPS) + ADAM_WD * w)
    return delta, m, v


def _adam(w, g, m, v, name):
    r, c = w.shape
    parts = g.ndim == 3
    tr = _tile(r, 512, 8)

    def body(w_ref, g_ref, m_ref, v_ref, go_ref, d_ref, mo_ref, vo_ref):
        if parts:
            gv = g_ref[0].astype(F32)
            for d in range(1, N_DEV):
                gv = gv + g_ref[d].astype(F32)
        else:
            gv = g_ref[...]
        delta, mn, vn = _adam_math(w_ref[...], gv, m_ref[...], v_ref[...])
        go_ref[...] = gv
        d_ref[...] = delta
        mo_ref[...] = mn
        vo_ref[...] = vn

    blk = pl.BlockSpec((tr, c), lambda i: (i, 0))
    gblk = pl.BlockSpec((N_DEV, tr, c), lambda i: (0, i, 0)) if parts else blk
    out = _sds((r, c), F32)
    return pl.pallas_call(
        body, name=name, out_shape=(out,) * 4, grid=(r // tr,),
        in_specs=[blk, gblk, blk, blk], out_specs=[blk] * 4,
        compiler_params=_params(("arbitrary",)),
    )(w, g, m, v)


def _sum_parts(parts, name):
    _, r, c = parts.shape

    def body(p_ref, o_ref):
        acc = p_ref[0]
        for d in range(1, N_DEV):
            acc = acc + p_ref[d]
        o_ref[...] = acc

    return pl.pallas_call(body, name=name, out_shape=_sds((r, c), F32), compiler_params=_params())(parts)


SMALL_SHARD_AXIS = {
    "meta": 1, "ln_g": 2, "ln_b": 2, "lru_conv_w": 2, "lru_conv_b": None, "lru_w_a": None, "lru_b_a": None,
    "lru_w_x": None, "lru_b_x": None, "lru_lambda": None, "ssd_conv_w": 2, "ssd_conv_b": None, "ssd_dt_bias": None,
    "ssd_a_log": None, "ssd_d": None, "ssd_norm_w": None, "dn_conv_w": 2, "dn_a_log": None, "dn_dt_bias": None,
    "dn_norm_w": None,
}
BIG = ("ffn_w_gate", "ffn_w_up", "ffn_w_down", "w_in", "w_out")
WEIGHT_ORDER = ("meta", "ln_g", "ln_b", "ffn_w_gate", "ffn_w_up", "ffn_w_down", "w_in", "lru_conv_w", "lru_conv_b",
                "lru_w_a", "lru_b_a", "lru_w_x", "lru_b_x", "lru_lambda", "ssd_conv_w", "ssd_conv_b", "ssd_dt_bias",
                "ssd_a_log", "ssd_d", "ssd_norm_w", "dn_conv_w", "dn_a_log", "dn_dt_bias", "dn_norm_w", "w_out")


def _pack(arrs):
    flat = jnp.concatenate([a.reshape(-1) for a in arrs])
    rows = -(-flat.shape[0] // (8 * LANE)) * 8
    return jnp.pad(flat, (0, rows * LANE - flat.shape[0])).reshape(rows, LANE)


def _unpack(buf, shapes, lead=()):
    flat = buf.reshape(lead + (-1,))
    out, off = [], 0
    for s in shapes:
        n = math.prod(s)
        out.append(flat[..., off:off + n].reshape(lead + tuple(s)))
        off += n
    return out


def _proj_cols(w):
    pad = jnp.zeros(w.shape[:-1] + (PROJ_W - D_IN,), w.dtype)
    return jnp.concatenate([w[..., 0:256], w[..., 1024:1792], w[..., 1800:2568], w[..., 256:512], w[..., 512:1024],
                            w[..., 2568:2824], w[..., 1792:1800], w[..., 2824:2832], pad], axis=-1)


def _proj_cols_inv(w):
    return jnp.concatenate([w[..., 0:256], w[..., 1792:2048], w[..., 2048:2560], w[..., 256:1024], w[..., 2816:2824],
                            w[..., 1024:1792], w[..., 2560:2816], w[..., 2824:2832]], axis=-1)


def kernel(x, meta, ln_g, ln_b, ffn_w_gate, ffn_w_up, ffn_w_down, w_in, lru_conv_w, lru_conv_b, lru_w_a, lru_b_a, lru_w_x, lru_b_x, lru_lambda, ssd_conv_w, ssd_conv_b, ssd_dt_bias, ssd_a_log, ssd_d, ssd_norm_w, dn_conv_w, dn_a_log, dn_dt_bias, dn_norm_w, w_out, loss_target, m_meta, m_ln_g, m_ln_b, m_ffn_w_gate, m_ffn_w_up, m_ffn_w_down, m_w_in, m_lru_conv_w, m_lru_conv_b, m_lru_w_a, m_lru_b_a, m_lru_w_x, m_lru_b_x, m_lru_lambda, m_ssd_conv_w, m_ssd_conv_b, m_ssd_dt_bias, m_ssd_a_log, m_ssd_d, m_ssd_norm_w, m_dn_conv_w, m_dn_a_log, m_dn_dt_bias, m_dn_norm_w, m_w_out, v_meta, v_ln_g, v_ln_b, v_ffn_w_gate, v_ffn_w_up, v_ffn_w_down, v_w_in, v_lru_conv_w, v_lru_conv_b, v_lru_w_a, v_lru_b_a, v_lru_w_x, v_lru_b_x, v_lru_lambda, v_ssd_conv_w, v_ssd_conv_b, v_ssd_dt_bias, v_ssd_a_log, v_ssd_d, v_ssd_norm_w, v_dn_conv_w, v_dn_a_log, v_dn_dt_bias, v_dn_norm_w, v_w_out):
    w = dict(meta=meta, ln_g=ln_g, ln_b=ln_b, ffn_w_gate=ffn_w_gate, ffn_w_up=ffn_w_up, ffn_w_down=ffn_w_down, w_in=w_in,
             lru_conv_w=lru_conv_w, lru_conv_b=lru_conv_b, lru_w_a=lru_w_a, lru_b_a=lru_b_a, lru_w_x=lru_w_x,
             lru_b_x=lru_b_x, lru_lambda=lru_lambda, ssd_conv_w=ssd_conv_w, ssd_conv_b=ssd_conv_b, ssd_dt_bias=ssd_dt_bias,
             ssd_a_log=ssd_a_log, ssd_d=ssd_d, ssd_norm_w=ssd_norm_w, dn_conv_w=dn_conv_w, dn_a_log=dn_a_log,
             dn_dt_bias=dn_dt_bias, dn_norm_w=dn_norm_w, w_out=w_out)
    m = dict(meta=m_meta, ln_g=m_ln_g, ln_b=m_ln_b, ffn_w_gate=m_ffn_w_gate, ffn_w_up=m_ffn_w_up, ffn_w_down=m_ffn_w_down,
             w_in=m_w_in, lru_conv_w=m_lru_conv_w, lru_conv_b=m_lru_conv_b, lru_w_a=m_lru_w_a, lru_b_a=m_lru_b_a,
             lru_w_x=m_lru_w_x, lru_b_x=m_lru_b_x, lru_lambda=m_lru_lambda, ssd_conv_w=m_ssd_conv_w, ssd_conv_b=m_ssd_conv_b,
             ssd_dt_bias=m_ssd_dt_bias, ssd_a_log=m_ssd_a_log, ssd_d=m_ssd_d, ssd_norm_w=m_ssd_norm_w, dn_conv_w=m_dn_conv_w,
             dn_a_log=m_dn_a_log, dn_dt_bias=m_dn_dt_bias, dn_norm_w=m_dn_norm_w, w_out=m_w_out)
    v = dict(meta=v_meta, ln_g=v_ln_g, ln_b=v_ln_b, ffn_w_gate=v_ffn_w_gate, ffn_w_up=v_ffn_w_up, ffn_w_down=v_ffn_w_down,
             w_in=v_w_in, lru_conv_w=v_lru_conv_w, lru_conv_b=v_lru_conv_b, lru_w_a=v_lru_w_a, lru_b_a=v_lru_b_a,
             lru_w_x=v_lru_w_x, lru_b_x=v_lru_b_x, lru_lambda=v_lru_lambda, ssd_conv_w=v_ssd_conv_w, ssd_conv_b=v_ssd_conv_b,
             ssd_dt_bias=v_ssd_dt_bias, ssd_a_log=v_ssd_a_log, ssd_d=v_ssd_d, ssd_norm_w=v_ssd_norm_w, dn_conv_w=v_dn_conv_w,
             dn_a_log=v_dn_a_log, dn_dt_bias=v_dn_dt_bias, dn_norm_w=v_dn_norm_w, w_out=v_w_out)
    depth = ln_g.shape[0]
    me = _flat(_mesh_pos())
    small_names = tuple(SMALL_SHARD_AXIS)
    sharded = tuple(k for k in small_names if SMALL_SHARD_AXIS[k] is not None)

    del _QUEUE[:]
    _STANDALONE[0] = 0
    shards = [_layer_shards(w, l) for l in range(depth)]
    first = {k: _enqueue(shards[0][k], False) for k in UNITS}
    g_small = _collect(_enqueue(_pack([w[k] for k in sharded]), False))
    first = {k: _collect(u) for k, u in first.items()}
    small = {k: w[k] for k in small_names if SMALL_SHARD_AXIS[k] is None}
    for k, piece in zip(sharded, _unpack(g_small, [w[k].shape for k in sharded], lead=(N_DEV,))):
        ax = SMALL_SHARD_AXIS[k]
        full = jnp.moveaxis(piece, 0, ax)
        small[k] = full.reshape(full.shape[:ax] + (N_DEV * w[k].shape[ax],) + full.shape[ax + 2:])

    loss, dx, small_grads, slabs = _local_step(x[0], loss_target[0], small, first, shards)

    r_small = _collect(_enqueue(_pack([small_grads[k] for k in small_names]), False))
    got = [{k: _collect(u) for k, u in layer.items()} for layer in slabs]

    outs = {}
    ffn = lambda name: jnp.stack([jnp.stack([got[l][f"{name}{j}"] for j in range(2)], axis=1) for l in range(depth)], axis=1)
    one = lambda name: jnp.stack([got[l][name] for l in range(depth)], axis=1)
    parts_of = {"ffn_w_gate": ffn("gate"), "ffn_w_up": ffn("up"), "ffn_w_down": ffn("down"), "w_in": one("win"),
                "w_out": one("wout")}
    for k in BIG:
        shp = w[k].shape
        two = lambda a: a.reshape(-1, shp[-1])
        res = _adam(two(w[k]), parts_of[k].reshape(N_DEV, -1, shp[-1]), two(m[k]), two(v[k]), "adam_" + k)
        outs[k] = [a.reshape(shp) for a in res]
    g_full = _unpack(_sum_parts(r_small, "sum_small_grads"), [small[k].shape for k in small_names])
    g_loc = {}
    for k, gf in zip(small_names, g_full):
        ax = SMALL_SHARD_AXIS[k]
        g_loc[k] = gf if ax is None else lax.dynamic_slice_in_dim(gf, me * w[k].shape[ax], w[k].shape[ax], axis=ax)
    res = _adam(_pack([w[k] for k in small_names]), _pack([g_loc[k] for k in small_names]),
                _pack([m[k] for k in small_names]), _pack([v[k] for k in small_names]), "adam_small")
    shapes = [w[k].shape for k in small_names]
    for i, k in enumerate(small_names):
        outs[k] = [_unpack(r, shapes)[i] for r in res]

    loss = lax.psum(loss, ("x", "y", "c"))
    return (loss, dx[None], *[outs[k][0] for k in WEIGHT_ORDER], *[outs[k][1] for k in WEIGHT_ORDER],
            *[outs[k][2] for k in WEIGHT_ORDER], *[outs[k][3] for k in WEIGHT_ORDER])
```

```python
import functools
import math

import jax
import jax.numpy as jnp
from jax import lax
from jax.experimental import pallas as pl
from jax.experimental.pallas import tpu as pltpu

F32 = jnp.float32
BF16 = jnp.bfloat16
HI = lax.Precision.HIGHEST
MESH_IDS = pl.DeviceIdType.MESH

N_DEV = 8
D_MODEL = 1024
DEPTH = 4
N_META = 16
CHUNK = 64
CONV_K = 4
D_FF = 2816
LRU_WIDTH = 256
LRU_HEADS = 4
LRU_C = 8.0
SSD_HEADS = 8
SSD_GROUPS = 2
SSD_HPG = 4
SSD_INNER = 512
DN_HEADS = 4
HEAD = 64
CONV_W = 1792
PROJ_W = 2944
OFF_LRU_Y, OFF_SSD_Z, OFF_DN_GATE, OFF_SMALL = 1792, 2048, 2560, 2816
D_IN = 2832
ALPHA = (2 * DEPTH) ** 0.25
FFN_RES = 0.5
LN_EPS = 1e-5
RMS_EPS = 1e-6
ADAM_LR, ADAM_B1, ADAM_B2, ADAM_EPS, ADAM_WD, ADAM_STEP = 0.001, 0.9, 0.999, 1e-08, 0.01, 10

VMEM_LIMIT = 56 * 1024 * 1024
ROW_TILE_CAP = 416
LANE = 128


def _tile(n, cap, mult=16):
    best = None
    for t in range(mult, min(n, cap) + 1, mult):
        if n % t == 0:
            best = t
    assert best is not None, (n, cap, mult)
    return best


def _params(sem=None):
    return pltpu.CompilerParams(dimension_semantics=sem, vmem_limit_bytes=VMEM_LIMIT)


def _resident(shape, index_map):
    return pl.BlockSpec(shape, index_map, pipeline_mode=pl.Buffered(1))


def _dot(a, b):
    return jnp.dot(a, b, preferred_element_type=F32)


def _dot_nt(a, b):
    return lax.dot_general(a, b, (((1,), (1,)), ((), ())), preferred_element_type=F32)


def _dot_tn(a, b):
    return lax.dot_general(a, b, (((0,), (0,)), ((), ())), preferred_element_type=F32)


def _sds(shape, dtype):
    return jax.ShapeDtypeStruct(shape, dtype)


US_PER_MB = 92.5


class _Unit:
    def __init__(self, arr, slab):
        self.arr, self.slab, self.out = arr, slab, None
        per_peer = arr.size * arr.dtype.itemsize / (N_DEV if slab else 1)
        self.us = per_peer / 1e6 * US_PER_MB


_QUEUE = []
_STANDALONE = [0]


def _enqueue(arr, slab):
    unit = _Unit(arr, slab)
    _QUEUE.append(unit)
    return unit


def _enqueue_halves(arr, slab, axis):
    half = arr.shape[axis] // 2
    parts = (lax.slice_in_dim(arr, 0, half, axis=axis), lax.slice_in_dim(arr, half, arr.shape[axis], axis=axis))
    return [_enqueue(p, slab) for p in parts], axis + (0 if slab else 1)


def _collect_halves(group):
    units, axis = group
    return jnp.concatenate([_collect(u) for u in units], axis=axis)


def _take_units(host_us):
    units = []
    while _QUEUE and host_us >= 0.5 * _QUEUE[0].us:
        host_us -= _QUEUE[0].us
        units.append(_QUEUE.pop(0))
    return units


def _collect(unit):
    if unit.out is None:
        n = _QUEUE.index(unit) + 1
        units = [_QUEUE.pop(0) for _ in range(n)]
        _STANDALONE[0] += 1
        res = _exchange([u.arr for u in units], [u.slab for u in units], f"exchange_{_STANDALONE[0]}")
        for u, r in zip(units, res):
            u.out = r
    return unit.out


def _call(body, host_us, *, name, out_shape, in_specs, out_specs, grid=(), scratch_shapes=(), compiler_params=None):
    units = _take_units(host_us)
    kw = dict(name=name, grid=grid, compiler_params=compiler_params)
    if not units:
        return pl.pallas_call(body, out_shape=out_shape, in_specs=in_specs, out_specs=out_specs,
                              scratch_shapes=list(scratch_shapes), **kw)
    single = not isinstance(out_shape, (tuple, list))
    outs = (out_shape,) if single else tuple(out_shape)
    ospecs = [out_specs] if single else list(out_specs)
    nin, nout, nscr, ncm = len(in_specs), len(outs), len(scratch_shapes), len(units)
    slabs = [u.slab for u in units]

    def hosted(*refs):
        ins, c_in = refs[:nin], refs[nin:nin + ncm]
        o, c_out = refs[nin + ncm:nin + ncm + nout], refs[nin + ncm + nout:nin + 2 * ncm + nout]
        scr, sems = refs[nin + 2 * ncm + nout:nin + 2 * ncm + nout + nscr], refs[nin + 2 * ncm + nout + nscr:]
        ids = [pl.program_id(d) for d in range(len(grid))]
        first = functools.reduce(jnp.logical_and, [i == 0 for i in ids])
        last = functools.reduce(jnp.logical_and, [i == g - 1 for i, g in zip(ids, grid)])

        @pl.when(first)
        def _():
            _exchange_start(c_in, c_out, slabs, *sems)
        body(*ins, *o, *scr)

        @pl.when(last)
        def _():
            _exchange_wait(c_in, c_out, slabs, *sems)

    hbm = pl.BlockSpec(memory_space=pl.ANY)
    fn = pl.pallas_call(
        hosted, out_shape=outs + _exchange_shapes([u.arr for u in units], slabs), in_specs=list(in_specs) + [hbm] * ncm,
        out_specs=ospecs + [hbm] * ncm, scratch_shapes=list(scratch_shapes) + _exchange_sems(ncm), **kw)

    def run(*args):
        res = fn(*args, *[u.arr for u in units])
        for u, r in zip(units, res[nout:]):
            u.out = r
        return res[0] if single else tuple(res[:nout])

    return run


def _ffn_up(xb, wgu, name):
    tp, d = xb.shape
    f = wgu.shape[1] // 2
    tm = _tile(tp, ROW_TILE_CAP)
    tn = _tile(f, 1408, LANE)
    nj = f // tn

    def body(x_ref, wg_ref, wu_ref, g_ref, u_ref, a_ref):
        x = x_ref[...]
        g = _dot(x, wg_ref[...])
        u = _dot(x, wu_ref[...])
        g_ref[...] = g.astype(BF16)
        u_ref[...] = u.astype(BF16)
        a_ref[...] = (g * jax.nn.sigmoid(g) * u).astype(BF16)

    out = _sds((tp, f), BF16)
    return _call(
        body, 60, name=name, out_shape=(out, out, out), grid=(nj, tp // tm),
        in_specs=[pl.BlockSpec((tm, d), lambda j, i: (i, 0)),
                  pl.BlockSpec((d, tn), lambda j, i: (0, j)),
                  pl.BlockSpec((d, tn), lambda j, i: (0, j + nj))],
        out_specs=[pl.BlockSpec((tm, tn), lambda j, i: (i, j))] * 3,
        compiler_params=_params(("arbitrary", "arbitrary")),
    )(xb, wgu, wgu)


def _mm_resid_ln(a, w, h, gamma, beta, scale, name):
    tp, k = a.shape
    d = w.shape[1]
    tm = _tile(tp, ROW_TILE_CAP)

    def body(a_ref, w_ref, h_ref, g_ref, b_ref, y_ref, o_ref, ob_ref):
        y = ALPHA * h_ref[...] + scale * _dot(a_ref[...], w_ref[...])
        mu = jnp.mean(y, axis=-1, keepdims=True)
        yc = y - mu
        var = jnp.mean(yc * yc, axis=-1, keepdims=True)
        o = yc * lax.rsqrt(var + LN_EPS) * g_ref[...] + b_ref[...]
        y_ref[...] = y
        o_ref[...] = o
        ob_ref[...] = o.astype(BF16)

    row = lambda i: (i, 0)
    fix = lambda i: (0, 0)
    return _call(
        body, 30, name=name, out_shape=(_sds((tp, d), F32), _sds((tp, d), F32), _sds((tp, d), BF16)),
        grid=(tp // tm,),
        in_specs=[pl.BlockSpec((tm, k), row), _resident((k, d), fix), pl.BlockSpec((tm, d), row),
                  pl.BlockSpec((1, d), fix), pl.BlockSpec((1, d), fix)],
        out_specs=[pl.BlockSpec((tm, d), row)] * 3,
        compiler_params=_params(("arbitrary",)),
    )(a, w, h, gamma, beta)


def _mm_nn(xb, w, name):
    tp, k = xb.shape
    n = w.shape[1]
    tm = _tile(tp, ROW_TILE_CAP)

    def body(x_ref, w_ref, o_ref):
        o_ref[...] = _dot(x_ref[...], w_ref[...])

    return _call(
        body, 30, name=name, out_shape=_sds((tp, n), F32), grid=(tp // tm,),
        in_specs=[pl.BlockSpec((tm, k), lambda i: (i, 0)), _resident((k, n), lambda i: (0, 0))],
        out_specs=pl.BlockSpec((tm, n), lambda i: (i, 0)),
        compiler_params=_params(("arbitrary",)),
    )(xb, w)


def _mm_nt_add(a, w, resid, name):
    tp, k = a.shape
    n = w.shape[0]
    tm = _tile(tp, ROW_TILE_CAP)
    has_resid = resid is not None

    def body(*refs):
        if has_resid:
            a_ref, w_ref, r_ref, o_ref = refs
            o_ref[...] = r_ref[...] + _dot_nt(a_ref[...], w_ref[...])
        else:
            a_ref, w_ref, o_ref = refs
            o_ref[...] = _dot_nt(a_ref[...], w_ref[...])

    in_specs = [pl.BlockSpec((tm, k), lambda i: (i, 0)), _resident((n, k), lambda i: (0, 0))]
    args = [a, w]
    if has_resid:
        in_specs.append(pl.BlockSpec((tm, n), lambda i: (i, 0)))
        args.append(resid)
    return _call(
        body, 25, name=name, out_shape=_sds((tp, n), F32), grid=(tp // tm,),
        in_specs=in_specs, out_specs=pl.BlockSpec((tm, n), lambda i: (i, 0)),
        compiler_params=_params(("arbitrary",)),
    )(*args)


def _ffn_dx(dg, du, wgu, resid, name):
    tp, f = dg.shape
    d = wgu.shape[0]
    tm = _tile(tp, ROW_TILE_CAP)

    def body(dg_ref, du_ref, w_ref, r_ref, o_ref):
        acc = r_ref[...] + _dot_nt(dg_ref[...], w_ref[:, 0:f])
        o_ref[...] = acc + _dot_nt(du_ref[...], w_ref[:, f:2 * f])

    row = lambda i: (i, 0)
    return _call(
        body, 55, name=name, out_shape=_sds((tp, d), F32), grid=(tp // tm,),
        in_specs=[pl.BlockSpec((tm, f), row), pl.BlockSpec((tm, f), row), _resident((d, 2 * f), lambda i: (0, 0)),
                  pl.BlockSpec((tm, d), row)],
        out_specs=pl.BlockSpec((tm, d), row),
        compiler_params=_params(("arbitrary",)),
    )(dg, du, wgu, resid)


def _mm_tn(a, b, name):
    tp, ka = a.shape
    nb = b.shape[1]
    tt = _tile(tp, 2 * ROW_TILE_CAP)
    tk = _tile(ka, 1408, LANE)
    tn = _tile(nb, 1536, LANE)
    if tn < 512:
        tn = nb
    nt = tp // tt

    def body(a_ref, b_ref, o_ref):
        @pl.when(pl.program_id(2) == 0)
        def _():
            o_ref[...] = jnp.zeros_like(o_ref)
        o_ref[...] += _dot_tn(a_ref[...], b_ref[...])

    return _call(
        body, 40, name=name, out_shape=_sds((ka, nb), F32), grid=(ka // tk, nb // tn, nt),
        in_specs=[pl.BlockSpec((tt, tk), lambda i, j, t: (t, i)), pl.BlockSpec((tt, tn), lambda i, j, t: (t, j))],
        out_specs=pl.BlockSpec((tk, tn), lambda i, j, t: (i, j)),
        compiler_params=_params(("arbitrary", "arbitrary", "arbitrary")),
    )(a, b)


def _ln_bwd(y, dout, gamma, scale, name):
    tp, d = y.shape
    tm = _tile(tp, ROW_TILE_CAP)

    def body(y_ref, do_ref, g_ref, dres_ref, dyb_ref, dg_ref, db_ref):
        @pl.when(pl.program_id(0) == 0)
        def _():
            dg_ref[...] = jnp.zeros_like(dg_ref)
            db_ref[...] = jnp.zeros_like(db_ref)
        yv = y_ref[...]
        do = do_ref[...]
        mu = jnp.mean(yv, axis=-1, keepdims=True)
        yc = yv - mu
        var = jnp.mean(yc * yc, axis=-1, keepdims=True)
        rstd = lax.rsqrt(var + LN_EPS)
        xhat = yc * rstd
        dxh = do * g_ref[...]
        m1 = jnp.mean(dxh, axis=-1, keepdims=True)
        m2 = jnp.mean(dxh * xhat, axis=-1, keepdims=True)
        dy = rstd * (dxh - m1 - xhat * m2)
        dres_ref[...] = ALPHA * dy
        dyb_ref[...] = (scale * dy).astype(BF16)
        dg_ref[...] += jnp.sum(do * xhat, axis=0, keepdims=True)
        db_ref[...] += jnp.sum(do, axis=0, keepdims=True)

    row = lambda i: (i, 0)
    fix = lambda i: (0, 0)
    return _call(
        body, 22, name=name,
        out_shape=(_sds((tp, d), F32), _sds((tp, d), BF16), _sds((1, d), F32), _sds((1, d), F32)),
        grid=(tp // tm,),
        in_specs=[pl.BlockSpec((tm, d), row), pl.BlockSpec((tm, d), row), pl.BlockSpec((1, d), fix)],
        out_specs=[pl.BlockSpec((tm, d), row), pl.BlockSpec((tm, d), row), pl.BlockSpec((1, d), fix),
                   pl.BlockSpec((1, d), fix)],
        compiler_params=_params(("arbitrary",)),
    )(y, dout, gamma)


def _ffn_dact(dyb, wd, g, u, name):
    tp, d = dyb.shape
    f = wd.shape[0]
    tm = _tile(tp, ROW_TILE_CAP)
    tn = _tile(f, 1408, LANE)

    def body(dy_ref, w_ref, g_ref, u_ref, dg_ref, du_ref):
        dact = _dot_nt(dy_ref[...], w_ref[...])
        gv = g_ref[...].astype(F32)
        uv = u_ref[...].astype(F32)
        sg = jax.nn.sigmoid(gv)
        dg_ref[...] = (dact * uv * (sg * (1.0 + gv * (1.0 - sg)))).astype(BF16)
        du_ref[...] = (dact * (gv * sg)).astype(BF16)

    out = _sds((tp, f), BF16)
    blk = pl.BlockSpec((tm, tn), lambda j, i: (i, j))
    return _call(
        body, 55, name=name, out_shape=(out, out), grid=(f // tn, tp // tm),
        in_specs=[pl.BlockSpec((tm, d), lambda j, i: (i, 0)), pl.BlockSpec((tn, d), lambda j, i: (j, 0)), blk, blk],
        out_specs=[blk, blk],
        compiler_params=_params(("arbitrary", "arbitrary")),
    )(dyb, wd, g, u)


def _loss_grad(o, tgt, t_real, name):
    tp, d = o.shape
    tm = _tile(tp, ROW_TILE_CAP)

    def body(o_ref, t_ref, l_ref, d_ref):
        i = pl.program_id(0)

        @pl.when(i == 0)
        def _():
            l_ref[...] = jnp.zeros_like(l_ref)
        rows = i * tm + lax.broadcasted_iota(jnp.int32, (tm, 1), 0)
        real = jnp.logical_and(rows >= N_META, rows < t_real)
        err = jnp.where(real, o_ref[...] - t_ref[...], 0.0)
        d_ref[...] = err * (1.0 / d)
        l_ref[...] += jnp.sum(err * err, axis=0, keepdims=True) * (0.5 / d)

    row = lambda i: (i, 0)
    return pl.pallas_call(
        body, name=name, out_shape=(_sds((1, d), F32), _sds((tp, d), F32)), grid=(tp // tm,),
        in_specs=[pl.BlockSpec((tm, d), row), pl.BlockSpec((tm, d), row)],
        out_specs=[pl.BlockSpec((1, d), lambda i: (0, 0)), pl.BlockSpec((tm, d), row)],
        compiler_params=_params(("arbitrary",)),
    )(o, tgt)


CONV_TC = 256


def _silu_grad(y):
    s = jax.nn.sigmoid(y)
    return s * (1.0 + y * (1.0 - s))


def _conv_taps(x_ref, w, r0, rb):
    cur = x_ref[r0:r0 + rb, :]
    prev = x_ref[r0 - 8:r0, :] if r0 > 0 else jnp.zeros((8, cur.shape[1]), F32)
    xcat = jnp.concatenate([prev, cur], axis=0)
    taps = [xcat[5 + j:5 + j + rb] for j in range(CONV_K - 1)] + [cur]
    y = w[0:1] * taps[0]
    for j in range(1, CONV_K):
        y = y + w[j:j + 1] * taps[j]
    return y, taps


def _conv_fwd(proj, w, b, name):
    tp = proj.shape[0]
    rb = _tile(tp, ROW_TILE_CAP, 8)

    def body(x_ref, w_ref, b_ref, o_ref):
        gated = pl.program_id(0) > 0
        wv = w_ref[...]
        bv = b_ref[...]
        for r0 in range(0, tp, rb):
            y, _ = _conv_taps(x_ref, wv, r0, rb)
            y = y + bv
            o_ref[r0:r0 + rb, :] = jnp.where(gated, y * jax.nn.sigmoid(y), y)

    col = lambda j: (0, j)
    return _call(
        body, 25, name=name, out_shape=_sds((tp, CONV_W), F32), grid=(CONV_W // CONV_TC,),
        in_specs=[pl.BlockSpec((tp, CONV_TC), col), pl.BlockSpec((CONV_K, CONV_TC), col), pl.BlockSpec((1, CONV_TC), col)],
        out_specs=pl.BlockSpec((tp, CONV_TC), col),
        compiler_params=_params(("arbitrary",)),
    )(proj, w, b)


def _conv_bwd(proj, w, b, dxc, name):
    tp = proj.shape[0]
    rb = _tile(tp, ROW_TILE_CAP, 8)

    def body(x_ref, w_ref, b_ref, d_ref, dx_ref, dw_ref, db_ref, dy_scr):
        gated = pl.program_id(0) > 0
        wv = w_ref[...]
        bv = b_ref[...]
        dw = [jnp.zeros((1, CONV_TC), F32) for _ in range(CONV_K)]
        db = jnp.zeros((1, CONV_TC), F32)
        for r0 in range(0, tp, rb):
            y, taps = _conv_taps(x_ref, wv, r0, rb)
            y = y + bv
            d = d_ref[r0:r0 + rb, :]
            dy = jnp.where(gated, d * _silu_grad(y), d)
            dy_scr[r0:r0 + rb, :] = dy
            for j in range(CONV_K):
                dw[j] = dw[j] + jnp.sum(dy * taps[j], axis=0, keepdims=True)
            db = db + jnp.sum(dy, axis=0, keepdims=True)
        for j in range(CONV_K):
            dw_ref[j:j + 1, :] = dw[j]
        db_ref[...] = db
        for r0 in range(0, tp, rb):
            cur = dy_scr[r0:r0 + rb, :]
            nxt = dy_scr[r0 + rb:r0 + rb + 8, :] if r0 + rb < tp else jnp.zeros((8, CONV_TC), F32)
            dcat = jnp.concatenate([cur, nxt], axis=0)
            dx = wv[3:4] * cur
            for s in range(1, CONV_K):
                dx = dx + wv[3 - s:4 - s] * dcat[s:s + rb]
            dx_ref[r0:r0 + rb, :] = dx

    col = lambda j: (0, j)
    return _call(
        body, 70, name=name,
        out_shape=(_sds((tp, CONV_W), F32), _sds((CONV_K, CONV_W), F32), _sds((1, CONV_W), F32)),
        grid=(CONV_W // CONV_TC,),
        in_specs=[pl.BlockSpec((tp, CONV_TC), col), pl.BlockSpec((CONV_K, CONV_TC), col), pl.BlockSpec((1, CONV_TC), col),
                  pl.BlockSpec((tp, CONV_TC), col)],
        out_specs=[pl.BlockSpec((tp, CONV_TC), col), pl.BlockSpec((CONV_K, CONV_TC), col), pl.BlockSpec((1, CONV_TC), col)],
        scratch_shapes=[pltpu.VMEM((tp, CONV_TC), F32)],
        compiler_params=_params(("arbitrary",)),
    )(proj, w, b, dxc)


def _neg_expm1(x):
    series = -x * (1.0 + x * (0.5 + x * (1.0 / 6.0 + x * (1.0 / 24.0))))
    return jnp.where(jnp.abs(x) < 0.03, series, 1.0 - jnp.exp(x))


def _lru_gates(u, wa, ba, wx, bx, lam):
    r = jax.nn.sigmoid(jnp.dot(u, wa, precision=HI, preferred_element_type=F32) + ba)
    i = jax.nn.sigmoid(jnp.dot(u, wx, precision=HI, preferred_element_type=F32) + bx)
    log_a = -LRU_C * r * jax.nn.softplus(-lam)
    a = jnp.exp(log_a)
    b = jnp.sqrt(_neg_expm1(2.0 * log_a)) * (i * u)
    return a, b


def _lru_specs(tm, flip, n_tiles):
    idx = (lambda i: (n_tiles - 1 - i, 0)) if flip else (lambda i: (i, 0))
    return idx, lambda i: (0, 0)


def _lru_fwd(xc, proj, wa, ba, wx, bx, lam, name):
    tp = xc.shape[0]
    w = LRU_WIDTH
    tm = _tile(tp, ROW_TILE_CAP, 8)
    ycol = OFF_LRU_Y // w

    def body(u_ref, y_ref, wa_ref, ba_ref, wx_ref, bx_ref, lam_ref, o_ref, h_ref, a_scr, b_scr, carry):
        @pl.when(pl.program_id(0) == 0)
        def _():
            carry[...] = jnp.zeros_like(carry)
        a, b = _lru_gates(u_ref[...], wa_ref[...], ba_ref[...], wx_ref[...], bx_ref[...], lam_ref[...])
        a_scr[...] = a
        b_scr[...] = b

        def step(t, h):
            h = a_scr[pl.ds(t, 1), :] * h + b_scr[pl.ds(t, 1), :]
            h_ref[pl.ds(t, 1), :] = h
            return h

        carry[0:1, :] = lax.fori_loop(0, tm, step, carry[0:1, :])
        o_ref[...] = (h_ref[...] * jax.nn.gelu(y_ref[...])).astype(BF16)

    fix = lambda i: (0, 0)
    return _call(
        body, 40, name=name, out_shape=(_sds((tp, w), BF16), _sds((tp, w), F32)), grid=(tp // tm,),
        in_specs=[pl.BlockSpec((tm, w), lambda i: (i, 0)), pl.BlockSpec((tm, w), lambda i: (i, ycol)),
                  pl.BlockSpec((w, w), fix), pl.BlockSpec((1, w), fix), pl.BlockSpec((w, w), fix),
                  pl.BlockSpec((1, w), fix), pl.BlockSpec((1, w), fix)],
        out_specs=[pl.BlockSpec((tm, w), lambda i: (i, 0))] * 2,
        scratch_shapes=[pltpu.VMEM((tm, w), F32), pltpu.VMEM((tm, w), F32), pltpu.VMEM((8, w), F32)],
        compiler_params=_params(("arbitrary",)),
    )(xc, proj, wa, ba, wx, bx, lam)


def _lru_bwd(xc, proj, h, dout, wa, ba, wx, bx, lam, name):
    tp = xc.shape[0]
    w = LRU_WIDTH
    tm = _tile(tp, ROW_TILE_CAP, 8)
    nt = tp // tm
    ycol = OFF_LRU_Y // w
    rev = lambda i: (nt - 1 - i, 0)
    prev8 = lambda i: (jnp.maximum((nt - 1 - i) * (tm // 8) - 1, 0), 0)

    def body(u_ref, y_ref, h_ref, hp_ref, do_ref, wa_ref, ba_ref, wx_ref, bx_ref, lam_ref,
             du_ref, dy_ref, dwa_ref, dba_ref, dwx_ref, dbx_ref, dlam_ref,
             a_scr, dh_scr, g_scr, da_scr, hext, carry):
        i = pl.program_id(0)

        @pl.when(i == 0)
        def _():
            carry[...] = jnp.zeros_like(carry)
            for r in (dwa_ref, dba_ref, dwx_ref, dbx_ref, dlam_ref):
                r[...] = jnp.zeros_like(r)
        params = (wa_ref[...], ba_ref[...], wx_ref[...], bx_ref[...], lam_ref[...])
        (a, _), gates_vjp = jax.vjp(_lru_gates, u_ref[...], *params)
        gel, gelu_vjp = jax.vjp(jax.nn.gelu, y_ref[...])
        do = do_ref[...]
        hv = h_ref[...]
        dy_ref[...] = gelu_vjp(do * hv)[0]
        a_scr[...] = a
        dh_scr[...] = do * gel
        hext[0:8, :] = jnp.where(i == nt - 1, 0.0, hp_ref[...])
        hext[8:8 + tm, :] = hv

        def step(s, c):
            t = tm - 1 - s
            g = dh_scr[pl.ds(t, 1), :] + c
            g_scr[pl.ds(t, 1), :] = g
            da_scr[pl.ds(t, 1), :] = g * hext[pl.ds(t + 7, 1), :]
            return a_scr[pl.ds(t, 1), :] * g

        carry[0:1, :] = lax.fori_loop(0, tm, step, carry[0:1, :])
        du, dwa, dba, dwx, dbx, dlam = gates_vjp((da_scr[...], g_scr[...]))
        du_ref[...] = du
        dwa_ref[...] += dwa
        dba_ref[...] += dba
        dwx_ref[...] += dwx
        dbx_ref[...] += dbx
        dlam_ref[...] += dlam

    fix = lambda i: (0, 0)
    tile = pl.BlockSpec((tm, w), rev)
    mat = pl.BlockSpec((w, w), fix)
    vec = pl.BlockSpec((1, w), fix)
    return _call(
        body, 90, name=name,
        out_shape=(_sds((tp, w), F32), _sds((tp, w), F32), _sds((w, w), F32), _sds((1, w), F32), _sds((w, w), F32),
                   _sds((1, w), F32), _sds((1, w), F32)),
        grid=(nt,),
        in_specs=[tile, pl.BlockSpec((tm, w), lambda i: (nt - 1 - i, ycol)), tile, pl.BlockSpec((8, w), prev8), tile,
                  mat, vec, mat, vec, vec],
        out_specs=[tile, tile, mat, vec, mat, vec, vec],
        scratch_shapes=[pltpu.VMEM((tm, w), F32)] * 4 + [pltpu.VMEM((tm + 8, w), F32), pltpu.VMEM((8, w), F32)],
        compiler_params=_params(("arbitrary",)),
    )(xc, proj, h, h, dout, wa, ba, wx, bx, lam)


def _dot3(a, b, dims):
    ah = a.astype(BF16)
    al = (a - ah.astype(F32)).astype(BF16)
    bh = b.astype(BF16)
    bl = (b - bh.astype(F32)).astype(BF16)
    dot = lambda x, y: lax.dot_general(x, y, (dims, ((0,), (0,))), preferred_element_type=F32)
    return dot(ah, bh) + (dot(ah, bl) + dot(al, bh))


@jax.custom_vjp
def _bmm(a, b):
    return _dot3(a, b, ((2,), (1,)))


@jax.custom_vjp
def _bmm_nt(a, b):
    return _dot3(a, b, ((2,), (2,)))


@jax.custom_vjp
def _bmm_tn(a, b):
    return _dot3(a, b, ((1,), (1,)))


_bmm.defvjp(lambda a, b: (_bmm(a, b), (a, b)), lambda r, g: (_bmm_nt(g, r[1]), _bmm_tn(r[0], g)))
_bmm_nt.defvjp(lambda a, b: (_bmm_nt(a, b), (a, b)), lambda r, g: (_bmm(g, r[1]), _bmm_tn(g, r[0])))
_bmm_tn.defvjp(lambda a, b: (_bmm_tn(a, b), (a, b)), lambda r, g: (_bmm_nt(r[1], g), _bmm(r[0], g)))


def _chunk_masks(nh):
    r = lax.broadcasted_iota(jnp.int32, (CHUNK, CHUNK), 0)
    c = lax.broadcasted_iota(jnp.int32, (CHUNK, CHUNK), 1)
    full = lambda m: jnp.broadcast_to(m[None], (nh, CHUNK, CHUNK))
    return r, c, full


def _decay_terms(g, nh):
    r, c, full = _chunk_masks(nh)
    incl = r >= c
    ones = jnp.ones((nh, CHUNK, CHUNK), F32)
    cs = _bmm(full(incl.astype(F32)), g)
    cs_t = _bmm(ones, g * full((r <= c).astype(F32)))
    tot = _bmm(ones, g)
    m = full(incl)
    decay = jnp.where(m, jnp.exp(jnp.where(m, cs - cs_t, 0.0)), 0.0)
    return cs, decay, tot


def _rep_groups(x):
    return jnp.concatenate([jnp.broadcast_to(x[g:g + 1], (SSD_HPG,) + x.shape[1:]) for g in range(SSD_GROUPS)], axis=0)


def _ssd_chunk(xs, bm, cm, dtr, z, p_dtb, p_alog, p_d, p_nw, state):
    nh = SSD_HEADS
    dt = jax.nn.softplus(dtr + p_dtb)
    a = dt * (-jnp.exp(p_alog))
    x = xs * dt
    cs, decay, tot = _decay_terms(a, nh)
    b8 = _rep_groups(bm)
    c8 = _rep_groups(cm)
    y = _bmm(_rep_groups(_bmm_nt(cm, bm)) * decay, x)
    y = y + _bmm_nt(c8, state) * jnp.exp(cs)
    new_state = state * jnp.exp(tot) + _bmm_tn(x * jnp.exp(tot - cs), b8)
    y = y + p_d * xs
    y = y * (z * jax.nn.sigmoid(z))
    ss = jnp.sum(y * y, axis=-1, keepdims=True)
    ssg = jnp.concatenate(
        [jnp.broadcast_to(jnp.sum(ss[g * SSD_HPG:(g + 1) * SSD_HPG], axis=0, keepdims=True), (SSD_HPG, CHUNK, 1))
         for g in range(SSD_GROUPS)], axis=0)
    y = y * lax.rsqrt(ssg * (1.0 / (SSD_HPG * HEAD)) + RMS_EPS) * p_nw
    return y, new_state


def _unit_lower_inverse(m, nh):
    r, c, full = _chunk_masks(nh)
    eye = full((r == c).astype(F32))
    md = jnp.where(full((r // 16) == (c // 16)), m, 0.0)
    mo = m - md
    x = eye - md
    p = _bmm(md, md)
    x = x + _bmm(x, p)
    p = _bmm(p, p)
    x = x + _bmm(x, p)
    p = _bmm(p, p)
    x = x + _bmm(x, p)
    n = _bmm(x, mo)
    y = x - _bmm(n, x)
    return y + _bmm(_bmm(n, n), y)


def _dn_chunk(q, k, v, gate, braw, araw, p_alog, p_dtb, p_nw, state):
    nh = DN_HEADS
    r, c, full = _chunk_masks(nh)
    q = q * lax.rsqrt(jnp.sum(q * q, axis=-1, keepdims=True) + RMS_EPS) * (HEAD ** -0.5)
    k = k * lax.rsqrt(jnp.sum(k * k, axis=-1, keepdims=True) + RMS_EPS)
    beta = jax.nn.sigmoid(braw)
    g = -jnp.exp(p_alog) * jax.nn.softplus(araw + p_dtb)
    gcs, decay, tot = _decay_terms(g, nh)
    kb = k * beta
    vb = v * beta
    m = jnp.where(full(r > c), _bmm_nt(kb, k) * decay, 0.0)
    t = _unit_lower_inverse(m, nh)
    egcs = jnp.exp(gcs)
    u = _bmm(t, vb)
    w = _bmm(t, kb * egcs)
    attn = _bmm_nt(q, k) * decay
    v_new = u - _bmm(w, state)
    out = _bmm(q * egcs, state) + _bmm(attn, v_new)
    new_state = state * jnp.exp(tot) + _bmm_tn(k * jnp.exp(tot - gcs), v_new)
    out = out * lax.rsqrt(jnp.mean(out * out, axis=-1, keepdims=True) + RMS_EPS) * p_nw
    return out * (gate * jax.nn.sigmoid(gate)), new_state


def _chunks_per_step(nc):
    return max(n for n in range(1, 6) if nc % n == 0)


def _heads(t, n):
    return jnp.stack([t[:, HEAD * h:HEAD * (h + 1)] for h in range(n)])


def _unheads(a):
    return jnp.concatenate([a[h] for h in range(a.shape[0])], axis=-1)


def _head_scalars(t, off, n):
    return jnp.stack([jnp.broadcast_to(t[:, off + h:off + h + 1], (CHUNK, HEAD)) for h in range(n)])


def _unhead_scalars(d, off):
    red = jnp.sum(d, axis=-1, keepdims=True)
    lane = lax.broadcasted_iota(jnp.int32, (CHUNK, LANE), 1)
    out = jnp.zeros((CHUNK, LANE), F32)
    for h in range(d.shape[0]):
        out = out + jnp.where(lane == off + h, red[h], 0.0)
    return out


def _chunk_scan_fwd(chunk_fn, srcs, read, params, nh, host_us, name):
    tp = srcs[0][0].shape[0]
    nc = tp // CHUNK
    nb = _chunks_per_step(nc)
    rows = nb * CHUNK
    ns, npar = len(srcs), len(params)

    def body(*refs):
        s_refs, p_refs = refs[:ns], refs[ns:ns + npar]
        y_ref, st_ref, state = refs[ns + npar:]

        @pl.when(pl.program_id(0) == 0)
        def _():
            state[...] = jnp.zeros_like(state)
        par = [r[...] for r in p_refs]
        st = state[...]
        for k in range(nb):
            sl = slice(k * CHUNK, (k + 1) * CHUNK)
            st_ref[k] = st
            y, st = chunk_fn(*read(s_refs, sl), *par, st)
            y_ref[sl, :] = _unheads(y).astype(BF16)
        state[...] = st

    src_spec = lambda s: pl.BlockSpec((rows, s[1]), lambda c: (c, s[2]))
    par_spec = lambda a: pl.BlockSpec(a.shape, lambda c: (0, 0, 0))
    return _call(
        body, host_us, name=name,
        out_shape=(_sds((tp, nh * HEAD), BF16), _sds((nc, nh, HEAD, HEAD), F32)),
        grid=(nc // nb,),
        in_specs=[src_spec(s) for s in srcs] + [par_spec(a) for a in params],
        out_specs=[pl.BlockSpec((rows, nh * HEAD), lambda c: (c, 0)),
                   pl.BlockSpec((nb, nh, HEAD, HEAD), lambda c: (c, 0, 0, 0))],
        scratch_shapes=[pltpu.VMEM((nh, HEAD, HEAD), F32)],
        compiler_params=_params(("arbitrary",)),
    )(*[s[0] for s in srcs], *params)


def _chunk_scan_bwd(chunk_fn, srcs, read, params, states, dsrc, write, out_widths, nh, host_us, name):
    tp = srcs[0][0].shape[0]
    nc = tp // CHUNK
    nb = _chunks_per_step(nc)
    rows = nb * CHUNK
    steps = nc // nb
    ns, npar, nout = len(srcs), len(params), len(out_widths)
    d_arr, d_off = dsrc

    def body(*refs):
        s_refs, p_refs = refs[:ns], refs[ns:ns + npar]
        st_ref, dy_ref = refs[ns + npar:ns + npar + 2]
        o_refs = refs[ns + npar + 2:ns + npar + 2 + nout]
        dp_refs = refs[ns + npar + 2 + nout:ns + 2 * npar + 2 + nout]
        dstate = refs[-1]

        @pl.when(pl.program_id(0) == 0)
        def _():
            dstate[...] = jnp.zeros_like(dstate)
            for r in dp_refs:
                r[...] = jnp.zeros_like(r)
        par = [r[...] for r in p_refs]
        dst = dstate[...]
        dpar = None
        for k in reversed(range(nb)):
            sl = slice(k * CHUNK, (k + 1) * CHUNK)
            seqs = read(s_refs, sl)
            _, vjp = jax.vjp(chunk_fn, *seqs, *par, st_ref[k])
            grads = vjp((_heads(dy_ref[sl, d_off:d_off + nh * HEAD], nh), dst))
            for r, tile in zip(o_refs, write(*grads[:len(seqs)])):
                r[sl, :] = tile
            gp = grads[len(seqs):len(seqs) + npar]
            dpar = gp if dpar is None else [a + b for a, b in zip(dpar, gp)]
            dst = grads[-1]
        for r, gr in zip(dp_refs, dpar):
            r[...] += gr
        dstate[...] = dst

    rev = lambda c: steps - 1 - c
    src_spec = lambda s: pl.BlockSpec((rows, s[1]), lambda c: (rev(c), s[2]))
    par_spec = lambda a: pl.BlockSpec(a.shape, lambda c: (0, 0, 0))
    out_spec = lambda w: pl.BlockSpec((rows, w), lambda c: (rev(c), 0))
    return tuple(_call(
        body, host_us, name=name,
        out_shape=tuple(_sds((tp, w), F32) for w in out_widths) + tuple(_sds(a.shape, F32) for a in params),
        grid=(steps,),
        in_specs=[src_spec(s) for s in srcs] + [par_spec(a) for a in params]
        + [pl.BlockSpec((nb, nh, HEAD, HEAD), lambda c: (rev(c), 0, 0, 0)), out_spec(d_arr.shape[1])],
        out_specs=[out_spec(w) for w in out_widths] + [par_spec(a) for a in params],
        scratch_shapes=[pltpu.VMEM((nh, HEAD, HEAD), F32)],
        compiler_params=_params(("arbitrary",)),
    )(*[s[0] for s in srcs], *params, states, d_arr))


def _lane_param(p):
    return jnp.broadcast_to(p[:, None, None], (p.shape[0], 1, HEAD))


def _block_diag(w):
    out = jnp.zeros((LRU_WIDTH, LRU_WIDTH), F32)
    for h in range(LRU_HEADS):
        out = out.at[h * HEAD:(h + 1) * HEAD, h * HEAD:(h + 1) * HEAD].set(w[h])
    return out


def _block_diag_inv(w):
    return jnp.stack([w[h * HEAD:(h + 1) * HEAD, h * HEAD:(h + 1) * HEAD] for h in range(LRU_HEADS)])


def _ssd_inputs(xc, proj, lp):
    srcs = ((xc, CONV_W, 0), (proj, SSD_INNER, OFF_SSD_Z // SSD_INNER), (proj, LANE, OFF_SMALL // LANE))
    params = (_lane_param(lp["ssd_dt_bias"]), _lane_param(lp["ssd_a_log"]), _lane_param(lp["ssd_d"]),
              lp["ssd_norm_w"].reshape(SSD_HEADS, 1, HEAD))
    return srcs, params


def _ssd_read(refs, rows):
    xc, z, sm = refs
    return (_heads(xc[rows, 256:768], SSD_HEADS), _heads(xc[rows, 768:896], SSD_GROUPS),
            _heads(xc[rows, 896:1024], SSD_GROUPS), _head_scalars(sm[rows, :], 0, SSD_HEADS), _heads(z[rows, :], SSD_HEADS))


def _ssd_write(dxs, dbm, dcm, ddtr, dz):
    return (jnp.concatenate([_unheads(dxs), _unheads(dbm), _unheads(dcm)], axis=-1), _unheads(dz), _unhead_scalars(ddtr, 0))


def _dn_inputs(xc, proj, lp):
    srcs = ((xc, CONV_W, 0), (proj, 256, OFF_DN_GATE // 256), (proj, LANE, OFF_SMALL // LANE))
    params = (_lane_param(lp["dn_a_log"]), _lane_param(lp["dn_dt_bias"]),
              jnp.broadcast_to(lp["dn_norm_w"][None, None, :], (DN_HEADS, 1, HEAD)))
    return srcs, params


def _dn_read(refs, rows):
    xc, gate, sm = refs
    return (_heads(xc[rows, 1024:1280], DN_HEADS), _heads(xc[rows, 1280:1536], DN_HEADS),
            _heads(xc[rows, 1536:1792], DN_HEADS), _heads(gate[rows, :], DN_HEADS),
            _head_scalars(sm[rows, :], 8, DN_HEADS), _head_scalars(sm[rows, :], 12, DN_HEADS))


def _dn_write(dq, dk, dv, dgate, dbraw, daraw):
    return (jnp.concatenate([_unheads(dq), _unheads(dk), _unheads(dv)], axis=-1), _unheads(dgate),
            _unhead_scalars(dbraw, 8) + _unhead_scalars(daraw, 12))


def _lru_params(lp):
    return (_block_diag(lp["lru_w_a"]), lp["lru_b_a"][None], _block_diag(lp["lru_w_x"]), lp["lru_b_x"][None],
            lp["lru_lambda"][None])


def _conv_params(lp):
    w = jnp.concatenate([lp["lru_conv_w"], lp["ssd_conv_w"], lp["dn_conv_w"]], axis=1)
    b = jnp.concatenate([lp["lru_conv_b"], lp["ssd_conv_b"], jnp.zeros((768,), F32)])[None]
    return w, b


def _mixers_fwd(proj, lp, tag):
    cw, cb = _conv_params(lp)
    xc = _conv_fwd(proj, cw, cb, tag + "_conv")
    o_lru, h_lru = _lru_fwd(xc, proj, *_lru_params(lp), name=tag + "_lru")
    s_srcs, s_par = _ssd_inputs(xc, proj, lp)
    o_ssd, s_states = _chunk_scan_fwd(_ssd_chunk, s_srcs, _ssd_read, s_par, SSD_HEADS, 88, tag + "_ssd")
    d_srcs, d_par = _dn_inputs(xc, proj, lp)
    o_dn, d_states = _chunk_scan_fwd(_dn_chunk, d_srcs, _dn_read, d_par, DN_HEADS, 170, tag + "_dn")
    cat = jnp.concatenate([o_lru, o_ssd, o_dn], axis=1)
    return cat, (xc, h_lru, s_states, d_states)


def _mixers_bwd(proj, lp, saved, dcat, tag):
    xc, h_lru, s_states, d_states = saved
    tp = proj.shape[0]
    cw, cb = _conv_params(lp)
    g = {}
    du, dyraw, dwa, dba, dwx, dbx, dlam = _lru_bwd(xc, proj, h_lru, dcat, *_lru_params(lp), name=tag + "_lru_bwd")
    g["lru_w_a"], g["lru_b_a"], g["lru_w_x"], g["lru_b_x"], g["lru_lambda"] = (
        _block_diag_inv(dwa), dba[0], _block_diag_inv(dwx), dbx[0], dlam[0])

    s_srcs, s_par = _ssd_inputs(xc, proj, lp)
    dxbc, dz, dsm_ssd, dp_dtb, dp_alog, dp_d, dp_nw = _chunk_scan_bwd(
        _ssd_chunk, s_srcs, _ssd_read, s_par, s_states, (dcat, 256), _ssd_write, (768, SSD_INNER, LANE), SSD_HEADS, 198,
        tag + "_ssd_bwd")
    g["ssd_dt_bias"], g["ssd_a_log"], g["ssd_d"] = (jnp.sum(p, axis=(1, 2)) for p in (dp_dtb, dp_alog, dp_d))
    g["ssd_norm_w"] = dp_nw.reshape(SSD_INNER)

    d_srcs, d_par = _dn_inputs(xc, proj, lp)
    dqkv, dgate, dsm_dn, dq_alog, dq_dtb, dq_nw = _chunk_scan_bwd(
        _dn_chunk, d_srcs, _dn_read, d_par, d_states, (dcat, 768), _dn_write, (768, 256, LANE), DN_HEADS, 425,
        tag + "_dn_bwd")
    g["dn_a_log"], g["dn_dt_bias"] = (jnp.sum(p, axis=(1, 2)) for p in (dq_alog, dq_dtb))
    g["dn_norm_w"] = jnp.sum(dq_nw, axis=(0, 1))

    dconv, dcw, dcb = _conv_bwd(proj, cw, cb, jnp.concatenate([du, dxbc, dqkv], axis=1), tag + "_conv_bwd")
    g["lru_conv_w"], g["ssd_conv_w"], g["dn_conv_w"] = dcw[:, :256], dcw[:, 256:1024], dcw[:, 1024:]
    g["lru_conv_b"], g["ssd_conv_b"] = dcb[0, :256], dcb[0, 256:1024]
    dproj = jnp.concatenate([dconv, dyraw, dz, dgate, dsm_ssd + dsm_dn], axis=1)
    return dproj, g


MIXER_PARAMS = ("lru_conv_w", "lru_conv_b", "lru_w_a", "lru_b_a", "lru_w_x", "lru_b_x", "lru_lambda",
                "ssd_conv_w", "ssd_conv_b", "ssd_dt_bias", "ssd_a_log", "ssd_d", "ssd_norm_w",
                "dn_conv_w", "dn_a_log", "dn_dt_bias", "dn_norm_w")


UNITS = ("gate0", "up0", "down0", "win", "wout", "gate1", "up1", "down1")


def _layer_shards(w, l):
    out = {"win": w["w_in"][l], "wout": w["w_out"][l]}
    for j in range(2):
        out[f"gate{j}"], out[f"up{j}"], out[f"down{j}"] = w["ffn_w_gate"][l, j], w["ffn_w_up"][l, j], w["ffn_w_down"][l, j]
    return {k: a.astype(BF16) for k, a in out.items()}


def _enqueue_matrix(name, arr, slab):
    if name == "wout":
        return [_enqueue(arr, slab)], 1
    return _enqueue_halves(arr, slab, 1 if slab else 0)


def _enqueue_layer(shards):
    return {k: _enqueue_matrix(k, shards[k], False) for k in UNITS}


def _gathered_cols(a):
    return a.transpose(1, 0, 2).reshape(a.shape[1], -1)


def _col_slabs(a):
    return a.reshape(a.shape[0], N_DEV, -1).transpose(1, 0, 2).astype(BF16)


def _row_slabs(a):
    return a.reshape(N_DEV, -1, a.shape[1]).astype(BF16)


def _local_step(x, tgt, small, first, shards):
    s = x.shape[0]
    t_real = N_META + s
    tp = -(-t_real // CHUNK) * CHUNK
    depth = len(first) if shards is None else len(shards)
    h = jnp.concatenate([small["meta"], x, jnp.zeros((tp - t_real, D_MODEL), F32)], axis=0)
    hb = h.astype(BF16)
    ln_g, ln_b = small["ln_g"], small["ln_b"]
    saved, weights = [], []
    queued = first
    for l in range(depth):
        lp = {k: small[k][l] for k in MIXER_PARAMS}
        t = f"l{l}"
        if shards is None:
            get = lambda k, g=first[l]: g[k]
        else:
            get = lambda k, g=queued: _collect_halves(g[k])
            if l + 1 < depth:
                queued = _enqueue_layer(shards[l + 1])

        def ffn_weights(j):
            up = _gathered_cols(get(f"up{j}"))
            return jnp.concatenate([_gathered_cols(get(f"gate{j}")), up], axis=-1)

        wgu0 = ffn_weights(0)
        g0, u0, a0 = _ffn_up(hb, wgu0, t + "_ffn0_up")
        wd0 = get("down0").reshape(-1, D_MODEL)
        y1, h1, h1b = _mm_resid_ln(a0, wd0, h, ln_g[l, 0][None], ln_b[l, 0][None], FFN_RES, t + "_ffn0_down")
        win = _proj_cols(_gathered_cols(get("win")))
        proj = _mm_nn(h1b, win, t + "_in_proj")
        cat, mix_saved = _mixers_fwd(proj, lp, t)
        catb = cat.astype(BF16)
        wout = get("wout").reshape(D_MODEL, D_MODEL)
        y2, h2, h2b = _mm_resid_ln(catb, wout, h1, ln_g[l, 1][None], ln_b[l, 1][None], 1.0, t + "_out_proj")
        wgu1 = ffn_weights(1)
        g1, u1, a1 = _ffn_up(h2b, wgu1, t + "_ffn1_up")
        wd1 = get("down1").reshape(-1, D_MODEL)
        y3, h3, h3b = _mm_resid_ln(a1, wd1, h2, ln_g[l, 2][None], ln_b[l, 2][None], FFN_RES, t + "_ffn1_down")
        weights.append(([wgu0, wgu1], [wd0, wd1], win, wout))
        saved.append((hb, g0, u0, a0, y1, h1b, proj, mix_saved, catb, y2, h2b, g1, u1, a1, y3))
        h, hb = h3, h3b

    tgt_p = jnp.pad(tgt, ((N_META, tp - t_real), (0, 0)))
    lossv, dh = _loss_grad(h, tgt_p, t_real, "loss")
    loss = jnp.sum(lossv)

    gs = {k: [None] * depth for k in MIXER_PARAMS}
    d_ln_g = [[None] * 3 for _ in range(depth)]
    d_ln_b = [[None] * 3 for _ in range(depth)]
    slabs = [{} for _ in range(depth)]
    send = (lambda k, a: a) if shards is None else (lambda k, a: _enqueue_matrix(k, a, True))

    def ffn_bwd(l, j, xb_in, g, u, a, y, dout):
        t = f"l{l}_ffn{j}"
        wgu, wd = weights[l][0][j], weights[l][1][j]
        dres, dyb, dgam, dbet = _ln_bwd(y, dout, ln_g[l, 2 * j][None], FFN_RES, t + "_ln_bwd")
        d_ln_g[l][2 * j], d_ln_b[l][2 * j] = dgam[0], dbet[0]
        slabs[l][f"down{j}"] = send("down", _row_slabs(_mm_tn(a, dyb, t + "_dwd")))
        dg, du = _ffn_dact(dyb, wd, g, u, t + "_dact")
        slabs[l][f"gate{j}"] = send("gate", _col_slabs(_mm_tn(xb_in, dg, t + "_dwg")))
        slabs[l][f"up{j}"] = send("up", _col_slabs(_mm_tn(xb_in, du, t + "_dwu")))
        return _ffn_dx(dg, du, wgu, dres, t + "_dx")

    for l in reversed(range(depth)):
        hb_in, g0, u0, a0, y1, h1b, proj, mix_saved, catb, y2, h2b, g1, u1, a1, y3 = saved[l]
        lp = {k: small[k][l] for k in MIXER_PARAMS}
        t = f"l{l}"
        _, _, win, wout = weights[l]
        dh2 = ffn_bwd(l, 1, h2b, g1, u1, a1, y3, dh)
        dres, dyb, dgam, dbet = _ln_bwd(y2, dh2, ln_g[l, 1][None], 1.0, t + "_mix_ln_bwd")
        d_ln_g[l][1], d_ln_b[l][1] = dgam[0], dbet[0]
        slabs[l]["wout"] = send("wout", _row_slabs(_mm_tn(catb, dyb, t + "_dwout")))
        dcat = _mm_nt_add(dyb, wout, None, t + "_dcat")
        dproj, mg = _mixers_bwd(proj, lp, mix_saved, dcat, t)
        for k in MIXER_PARAMS:
            gs[k][l] = mg[k]
        dprojb = dproj.astype(BF16)
        slabs[l]["win"] = send("win", _col_slabs(_proj_cols_inv(_mm_tn(h1b, dprojb, t + "_dwin"))))
        dh1 = _mm_nt_add(dprojb, win, dres, t + "_dh1")
        dh = ffn_bwd(l, 0, hb_in, g0, u0, a0, y1, dh1)

    small_grads = {k: jnp.stack(v) for k, v in gs.items()}
    small_grads["ln_g"] = jnp.stack([jnp.stack(r) for r in d_ln_g])
    small_grads["ln_b"] = jnp.stack([jnp.stack(r) for r in d_ln_b])
    small_grads["meta"] = dh[:N_META]
    return loss, dh[N_META:t_real], small_grads, slabs


def _mesh_pos():
    return lax.axis_index("x"), lax.axis_index("y"), lax.axis_index("c")


def _flip(pos, k):
    x, y, c = pos
    return (1 - x if k & 4 else x, 1 - y if k & 2 else y, 1 - c if k & 1 else c)


def _flat(pos):
    return 4 * pos[0] + 2 * pos[1] + pos[2]


def _exchange_copies(ins, outs, slabs, send, recv, loc):
    pos = _mesh_pos()
    me = _flat(pos)
    local, sends, receives = [], [], []
    for a in range(len(ins)):
        local.append(pltpu.make_async_copy(ins[a].at[me] if slabs[a] else ins[a], outs[a].at[me], loc.at[a]))
    for k in range(1, N_DEV):
        peer = _flip(pos, k)
        for a in range(len(ins)):
            sem = dict(send_sem=send.at[a, k - 1], recv_sem=recv.at[a, k - 1], device_id=peer, device_id_type=MESH_IDS)
            sends.append(pltpu.make_async_remote_copy(
                src_ref=ins[a].at[_flat(peer)] if slabs[a] else ins[a], dst_ref=outs[a].at[me], **sem))
            receives.append(pltpu.make_async_remote_copy(
                src_ref=ins[a].at[me] if slabs[a] else ins[a], dst_ref=outs[a].at[_flat(peer)], **sem))
    return local, sends, receives


def _exchange_start(ins, outs, slabs, send, recv, loc):
    local, sends, _ = _exchange_copies(ins, outs, slabs, send, recv, loc)
    for cp in local + sends:
        cp.start()


def _exchange_wait(ins, outs, slabs, send, recv, loc):
    local, sends, receives = _exchange_copies(ins, outs, slabs, send, recv, loc)
    for cp in receives:
        cp.wait_recv()
    for cp in sends:
        cp.wait_send()
    for cp in local:
        cp.wait()


def _exchange_shapes(arrs, slabs):
    return tuple(_sds(a.shape if s else (N_DEV,) + a.shape, a.dtype) for a, s in zip(arrs, slabs))


def _exchange_sems(n):
    return [pltpu.SemaphoreType.DMA((n, N_DEV - 1)), pltpu.SemaphoreType.DMA((n, N_DEV - 1)),
            pltpu.SemaphoreType.DMA((n,))]


def _exchange(arrs, slabs, name):
    n = len(arrs)

    def body(*refs):
        ins, outs, sems = refs[:n], refs[n:2 * n], refs[2 * n:]
        _exchange_start(ins, outs, slabs, *sems)
        _exchange_wait(ins, outs, slabs, *sems)

    hbm = pl.BlockSpec(memory_space=pl.ANY)
    return pl.pallas_call(
        body, name=name, out_shape=_exchange_shapes(arrs, slabs), in_specs=[hbm] * n, out_specs=[hbm] * n,
        scratch_shapes=_exchange_sems(n),
    )(*arrs)


def _adam_math(w, g, m, v):
    m = ADAM_B1 * m + (1.0 - ADAM_B1) * g
    v = ADAM_B2 * v + (1.0 - ADAM_B2) * (g * g)
    m_hat = m / (1.0 - ADAM_B1 ** ADAM_STEP)
    v_hat = v / (1.0 - ADAM_B2 ** ADAM_STEP)
    delta = -ADAM_LR * (m_hat / (jnp.sqrt(v_hat) + ADAM_EPS) + ADAM_WD * w)
    return delta, m, v


def _adam(w, g, m, v, name):
    r, c = w.shape
    parts = g.ndim == 3
    tr = _tile(r, 512, 8)

    def body(w_ref, g_ref, m_ref, v_ref, go_ref, d_ref, mo_ref, vo_ref):
        if parts:
            gv = g_ref[0].astype(F32)
            for d in range(1, N_DEV):
                gv = gv + g_ref[d].astype(F32)
        else:
            gv = g_ref[...]
        delta, mn, vn = _adam_math(w_ref[...], gv, m_ref[...], v_ref[...])
        go_ref[...] = gv
        d_ref[...] = delta
        mo_ref[...] = mn
        vo_ref[...] = vn

    blk = pl.BlockSpec((tr, c), lambda i: (i, 0))
    gblk = pl.BlockSpec((N_DEV, tr, c), lambda i: (0, i, 0)) if parts else blk
    out = _sds((r, c), F32)
    return pl.pallas_call(
        body, name=name, out_shape=(out,) * 4, grid=(r // tr,),
        in_specs=[blk, gblk, blk, blk], out_specs=[blk] * 4,
        compiler_params=_params(("arbitrary",)),
    )(w, g, m, v)


def _sum_parts(parts, name):
    _, r, c = parts.shape

    def body(p_ref, o_ref):
        acc = p_ref[0]
        for d in range(1, N_DEV):
            acc = acc + p_ref[d]
        o_ref[...] = acc

    return pl.pallas_call(body, name=name, out_shape=_sds((r, c), F32), compiler_params=_params())(parts)


SMALL_SHARD_AXIS = {
    "meta": 1, "ln_g": 2, "ln_b": 2, "lru_conv_w": 2, "lru_conv_b": None, "lru_w_a": None, "lru_b_a": None,
    "lru_w_x": None, "lru_b_x": None, "lru_lambda": None, "ssd_conv_w": 2, "ssd_conv_b": None, "ssd_dt_bias": None,
    "ssd_a_log": None, "ssd_d": None, "ssd_norm_w": None, "dn_conv_w": 2, "dn_a_log": None, "dn_dt_bias": None,
    "dn_norm_w": None,
}
BIG = ("ffn_w_gate", "ffn_w_up", "ffn_w_down", "w_in", "w_out")
WEIGHT_ORDER = ("meta", "ln_g", "ln_b", "ffn_w_gate", "ffn_w_up", "ffn_w_down", "w_in", "lru_conv_w", "lru_conv_b",
                "lru_w_a", "lru_b_a", "lru_w_x", "lru_b_x", "lru_lambda", "ssd_conv_w", "ssd_conv_b", "ssd_dt_bias",
                "ssd_a_log", "ssd_d", "ssd_norm_w", "dn_conv_w", "dn_a_log", "dn_dt_bias", "dn_norm_w", "w_out")


def _pack(arrs):
    flat = jnp.concatenate([a.reshape(-1) for a in arrs])
    rows = -(-flat.shape[0] // (8 * LANE)) * 8
    return jnp.pad(flat, (0, rows * LANE - flat.shape[0])).reshape(rows, LANE)


def _unpack(buf, shapes, lead=()):
    flat = buf.reshape(lead + (-1,))
    out, off = [], 0
    for s in shapes:
        n = math.prod(s)
        out.append(flat[..., off:off + n].reshape(lead + tuple(s)))
        off += n
    return out


def _proj_cols(w):
    pad = jnp.zeros(w.shape[:-1] + (PROJ_W - D_IN,), w.dtype)
    return jnp.concatenate([w[..., 0:256], w[..., 1024:1792], w[..., 1800:2568], w[..., 256:512], w[..., 512:1024],
                            w[..., 2568:2824], w[..., 1792:1800], w[..., 2824:2832], pad], axis=-1)


def _proj_cols_inv(w):
    return jnp.concatenate([w[..., 0:256], w[..., 1792:2048], w[..., 2048:2560], w[..., 256:1024], w[..., 2816:2824],
                            w[..., 1024:1792], w[..., 2560:2816], w[..., 2824:2832]], axis=-1)


def kernel(x, meta, ln_g, ln_b, ffn_w_gate, ffn_w_up, ffn_w_down, w_in, lru_conv_w, lru_conv_b, lru_w_a, lru_b_a, lru_w_x, lru_b_x, lru_lambda, ssd_conv_w, ssd_conv_b, ssd_dt_bias, ssd_a_log, ssd_d, ssd_norm_w, dn_conv_w, dn_a_log, dn_dt_bias, dn_norm_w, w_out, loss_target, m_meta, m_ln_g, m_ln_b, m_ffn_w_gate, m_ffn_w_up, m_ffn_w_down, m_w_in, m_lru_conv_w, m_lru_conv_b, m_lru_w_a, m_lru_b_a, m_lru_w_x, m_lru_b_x, m_lru_lambda, m_ssd_conv_w, m_ssd_conv_b, m_ssd_dt_bias, m_ssd_a_log, m_ssd_d, m_ssd_norm_w, m_dn_conv_w, m_dn_a_log, m_dn_dt_bias, m_dn_norm_w, m_w_out, v_meta, v_ln_g, v_ln_b, v_ffn_w_gate, v_ffn_w_up, v_ffn_w_down, v_w_in, v_lru_conv_w, v_lru_conv_b, v_lru_w_a, v_lru_b_a, v_lru_w_x, v_lru_b_x, v_lru_lambda, v_ssd_conv_w, v_ssd_conv_b, v_ssd_dt_bias, v_ssd_a_log, v_ssd_d, v_ssd_norm_w, v_dn_conv_w, v_dn_a_log, v_dn_dt_bias, v_dn_norm_w, v_w_out):
    w = dict(meta=meta, ln_g=ln_g, ln_b=ln_b, ffn_w_gate=ffn_w_gate, ffn_w_up=ffn_w_up, ffn_w_down=ffn_w_down, w_in=w_in,
             lru_conv_w=lru_conv_w, lru_conv_b=lru_conv_b, lru_w_a=lru_w_a, lru_b_a=lru_b_a, lru_w_x=lru_w_x,
             lru_b_x=lru_b_x, lru_lambda=lru_lambda, ssd_conv_w=ssd_conv_w, ssd_conv_b=ssd_conv_b, ssd_dt_bias=ssd_dt_bias,
             ssd_a_log=ssd_a_log, ssd_d=ssd_d, ssd_norm_w=ssd_norm_w, dn_conv_w=dn_conv_w, dn_a_log=dn_a_log,
             dn_dt_bias=dn_dt_bias, dn_norm_w=dn_norm_w, w_out=w_out)
    m = dict(meta=m_meta, ln_g=m_ln_g, ln_b=m_ln_b, ffn_w_gate=m_ffn_w_gate, ffn_w_up=m_ffn_w_up, ffn_w_down=m_ffn_w_down,
             w_in=m_w_in, lru_conv_w=m_lru_conv_w, lru_conv_b=m_lru_conv_b, lru_w_a=m_lru_w_a, lru_b_a=m_lru_b_a,
             lru_w_x=m_lru_w_x, lru_b_x=m_lru_b_x, lru_lambda=m_lru_lambda, ssd_conv_w=m_ssd_conv_w, ssd_conv_b=m_ssd_conv_b,
             ssd_dt_bias=m_ssd_dt_bias, ssd_a_log=m_ssd_a_log, ssd_d=m_ssd_d, ssd_norm_w=m_ssd_norm_w, dn_conv_w=m_dn_conv_w,
             dn_a_log=m_dn_a_log, dn_dt_bias=m_dn_dt_bias, dn_norm_w=m_dn_norm_w, w_out=m_w_out)
    v = dict(meta=v_meta, ln_g=v_ln_g, ln_b=v_ln_b, ffn_w_gate=v_ffn_w_gate, ffn_w_up=v_ffn_w_up, ffn_w_down=v_ffn_w_down,
             w_in=v_w_in, lru_conv_w=v_lru_conv_w, lru_conv_b=v_lru_conv_b, lru_w_a=v_lru_w_a, lru_b_a=v_lru_b_a,
             lru_w_x=v_lru_w_x, lru_b_x=v_lru_b_x, lru_lambda=v_lru_lambda, ssd_conv_w=v_ssd_conv_w, ssd_conv_b=v_ssd_conv_b,
             ssd_dt_bias=v_ssd_dt_bias, ssd_a_log=v_ssd_a_log, ssd_d=v_ssd_d, ssd_norm_w=v_ssd_norm_w, dn_conv_w=v_dn_conv_w,
             dn_a_log=v_dn_a_log, dn_dt_bias=v_dn_dt_bias, dn_norm_w=v_dn_norm_w, w_out=v_w_out)
    depth = ln_g.shape[0]
    me = _flat(_mesh_pos())
    small_names = tuple(SMALL_SHARD_AXIS)
    sharded = tuple(k for k in small_names if SMALL_SHARD_AXIS[k] is not None)

    del _QUEUE[:]
    _STANDALONE[0] = 0
    shards = [_layer_shards(w, l) for l in range(depth)]
    small_unit = _enqueue(_pack([w[k] for k in sharded]), False)
    first = _enqueue_layer(shards[0])
    g_small = _collect(small_unit)
    small = {k: w[k] for k in small_names if SMALL_SHARD_AXIS[k] is None}
    for k, piece in zip(sharded, _unpack(g_small, [w[k].shape for k in sharded], lead=(N_DEV,))):
        ax = SMALL_SHARD_AXIS[k]
        full = jnp.moveaxis(piece, 0, ax)
        small[k] = full.reshape(full.shape[:ax] + (N_DEV * w[k].shape[ax],) + full.shape[ax + 2:])

    loss, dx, small_grads, slabs = _local_step(x[0], loss_target[0], small, first, shards)

    r_small = _collect(_enqueue(_pack([small_grads[k] for k in small_names]), False))
    got = [{k: _collect_halves(g) for k, g in layer.items()} for layer in slabs]

    outs = {}
    ffn = lambda name: jnp.stack([jnp.stack([got[l][f"{name}{j}"] for j in range(2)], axis=1) for l in range(depth)], axis=1)
    one = lambda name: jnp.stack([got[l][name] for l in range(depth)], axis=1)
    parts_of = {"ffn_w_gate": ffn("gate"), "ffn_w_up": ffn("up"), "ffn_w_down": ffn("down"), "w_in": one("win"),
                "w_out": one("wout")}
    for k in BIG:
        shp = w[k].shape
        two = lambda a: a.reshape(-1, shp[-1])
        res = _adam(two(w[k]), parts_of[k].reshape(N_DEV, -1, shp[-1]), two(m[k]), two(v[k]), "adam_" + k)
        outs[k] = [a.reshape(shp) for a in res]
    g_full = _unpack(_sum_parts(r_small, "sum_small_grads"), [small[k].shape for k in small_names])
    g_loc = {}
    for k, gf in zip(small_names, g_full):
        ax = SMALL_SHARD_AXIS[k]
        g_loc[k] = gf if ax is None else lax.dynamic_slice_in_dim(gf, me * w[k].shape[ax], w[k].shape[ax], axis=ax)
    res = _adam(_pack([w[k] for k in small_names]), _pack([g_loc[k] for k in small_names]),
                _pack([m[k] for k in small_names]), _pack([v[k] for k in small_names]), "adam_small")
    shapes = [w[k].shape for k in small_names]
    for i, k in enumerate(small_names):
        outs[k] = [_unpack(r, shapes)[i] for r in res]

    loss = lax.psum(loss, ("x", "y", "c"))
    return (loss, dx[None], *[outs[k][0] for k in WEIGHT_ORDER], *[outs[k][1] for k in WEIGHT_ORDER],
            *[outs[k][2] for k in WEIGHT_ORDER], *[outs[k][3] for k in WEIGHT_ORDER])
```

```python
import functools
import math

import jax
import jax.numpy as jnp
from jax import lax
from jax.experimental import pallas as pl
from jax.experimental.pallas import tpu as pltpu

F32 = jnp.float32
BF16 = jnp.bfloat16
HI = lax.Precision.HIGHEST
MESH_IDS = pl.DeviceIdType.MESH

N_DEV = 8
D_MODEL = 1024
DEPTH = 4
N_META = 16
CHUNK = 64
CONV_K = 4
D_FF = 2816
LRU_WIDTH = 256
LRU_HEADS = 4
LRU_C = 8.0
SSD_HEADS = 8
SSD_GROUPS = 2
SSD_HPG = 4
SSD_INNER = 512
DN_HEADS = 4
HEAD = 64
CONV_W = 1792
PROJ_W = 2944
OFF_LRU_Y, OFF_SSD_Z, OFF_DN_GATE, OFF_SMALL = 1792, 2048, 2560, 2816
D_IN = 2832
ALPHA = (2 * DEPTH) ** 0.25
FFN_RES = 0.5
LN_EPS = 1e-5
RMS_EPS = 1e-6
ADAM_LR, ADAM_B1, ADAM_B2, ADAM_EPS, ADAM_WD, ADAM_STEP = 0.001, 0.9, 0.999, 1e-08, 0.01, 10

VMEM_LIMIT = 56 * 1024 * 1024
ROW_TILE_CAP = 832
LANE = 128


def _tile(n, cap, mult=16):
    best = None
    for t in range(mult, min(n, cap) + 1, mult):
        if n % t == 0:
            best = t
    assert best is not None, (n, cap, mult)
    return best


def _params(sem=None):
    return pltpu.CompilerParams(dimension_semantics=sem, vmem_limit_bytes=VMEM_LIMIT)


def _resident(shape, index_map):
    return pl.BlockSpec(shape, index_map, pipeline_mode=pl.Buffered(1))


def _dot(a, b):
    return jnp.dot(a, b, preferred_element_type=F32)


def _dot_nt(a, b):
    return lax.dot_general(a, b, (((1,), (1,)), ((), ())), preferred_element_type=F32)


def _dot_tn(a, b):
    return lax.dot_general(a, b, (((0,), (0,)), ((), ())), preferred_element_type=F32)


def _sds(shape, dtype):
    return jax.ShapeDtypeStruct(shape, dtype)


US_PER_MB = 92.5


class _Unit:
    def __init__(self, arr, slab):
        self.arr, self.slab, self.out = arr, slab, None
        per_peer = arr.size * arr.dtype.itemsize / (N_DEV if slab else 1)
        self.us = per_peer / 1e6 * US_PER_MB


_QUEUE = []
_STANDALONE = [0]


def _enqueue(arr, slab):
    unit = _Unit(arr, slab)
    _QUEUE.append(unit)
    return unit


def _enqueue_halves(arr, slab, axis):
    half = arr.shape[axis] // 2
    parts = (lax.slice_in_dim(arr, 0, half, axis=axis), lax.slice_in_dim(arr, half, arr.shape[axis], axis=axis))
    return [_enqueue(p, slab) for p in parts], axis + (0 if slab else 1)


def _collect_halves(group):
    units, axis = group
    return jnp.concatenate([_collect(u) for u in units], axis=axis)


def _take_units(host_us):
    units = []
    while _QUEUE and host_us >= 0.5 * _QUEUE[0].us:
        host_us -= _QUEUE[0].us
        units.append(_QUEUE.pop(0))
    return units


def _collect(unit):
    if unit.out is None:
        n = _QUEUE.index(unit) + 1
        units = [_QUEUE.pop(0) for _ in range(n)]
        _STANDALONE[0] += 1
        res = _exchange([u.arr for u in units], [u.slab for u in units], f"exchange_{_STANDALONE[0]}")
        for u, r in zip(units, res):
            u.out = r
    return unit.out


def _call(body, host_us, *, name, out_shape, in_specs, out_specs, grid=(), scratch_shapes=(), compiler_params=None):
    units = _take_units(host_us)
    kw = dict(name=name, grid=grid, compiler_params=compiler_params)
    if not units:
        return pl.pallas_call(body, out_shape=out_shape, in_specs=in_specs, out_specs=out_specs,
                              scratch_shapes=list(scratch_shapes), **kw)
    single = not isinstance(out_shape, (tuple, list))
    outs = (out_shape,) if single else tuple(out_shape)
    ospecs = [out_specs] if single else list(out_specs)
    nin, nout, nscr, ncm = len(in_specs), len(outs), len(scratch_shapes), len(units)
    slabs = [u.slab for u in units]

    def hosted(*refs):
        ins, c_in = refs[:nin], refs[nin:nin + ncm]
        o, c_out = refs[nin + ncm:nin + ncm + nout], refs[nin + ncm + nout:nin + 2 * ncm + nout]
        scr, sems = refs[nin + 2 * ncm + nout:nin + 2 * ncm + nout + nscr], refs[nin + 2 * ncm + nout + nscr:]
        ids = [pl.program_id(d) for d in range(len(grid))]
        first = functools.reduce(jnp.logical_and, [i == 0 for i in ids])
        last = functools.reduce(jnp.logical_and, [i == g - 1 for i, g in zip(ids, grid)])

        @pl.when(first)
        def _():
            _exchange_start(c_in, c_out, slabs, *sems)
        body(*ins, *o, *scr)

        @pl.when(last)
        def _():
            _exchange_wait(c_in, c_out, slabs, *sems)

    hbm = pl.BlockSpec(memory_space=pl.ANY)
    fn = pl.pallas_call(
        hosted, out_shape=outs + _exchange_shapes([u.arr for u in units], slabs), in_specs=list(in_specs) + [hbm] * ncm,
        out_specs=ospecs + [hbm] * ncm, scratch_shapes=list(scratch_shapes) + _exchange_sems(ncm), **kw)

    def run(*args):
        res = fn(*args, *[u.arr for u in units])
        for u, r in zip(units, res[nout:]):
            u.out = r
        return res[0] if single else tuple(res[:nout])

    return run


def _ffn_up(xb, wgu, name):
    tp, d = xb.shape
    f = wgu.shape[1] // 2
    tm = _tile(tp, ROW_TILE_CAP)
    tn = _tile(f, 1408, LANE)
    nj = f // tn

    def body(x_ref, wg_ref, wu_ref, g_ref, u_ref, a_ref):
        x = x_ref[...]
        g = _dot(x, wg_ref[...])
        u = _dot(x, wu_ref[...])
        g_ref[...] = g.astype(BF16)
        u_ref[...] = u.astype(BF16)
        a_ref[...] = (g * jax.nn.sigmoid(g) * u).astype(BF16)

    out = _sds((tp, f), BF16)
    return _call(
        body, 60, name=name, out_shape=(out, out, out), grid=(nj, tp // tm),
        in_specs=[pl.BlockSpec((tm, d), lambda j, i: (i, 0)),
                  pl.BlockSpec((d, tn), lambda j, i: (0, j)),
                  pl.BlockSpec((d, tn), lambda j, i: (0, j + nj))],
        out_specs=[pl.BlockSpec((tm, tn), lambda j, i: (i, j))] * 3,
        compiler_params=_params(("arbitrary", "arbitrary")),
    )(xb, wgu, wgu)


def _mm_resid_ln(a, w, h, gamma, beta, scale, name):
    tp, k = a.shape
    d = w.shape[1]
    tm = _tile(tp, ROW_TILE_CAP)

    def body(a_ref, w_ref, h_ref, g_ref, b_ref, y_ref, o_ref, ob_ref):
        y = ALPHA * h_ref[...] + scale * _dot(a_ref[...], w_ref[...])
        mu = jnp.mean(y, axis=-1, keepdims=True)
        yc = y - mu
        var = jnp.mean(yc * yc, axis=-1, keepdims=True)
        o = yc * lax.rsqrt(var + LN_EPS) * g_ref[...] + b_ref[...]
        y_ref[...] = y
        o_ref[...] = o
        ob_ref[...] = o.astype(BF16)

    row = lambda i: (i, 0)
    fix = lambda i: (0, 0)
    return _call(
        body, 30, name=name, out_shape=(_sds((tp, d), F32), _sds((tp, d), F32), _sds((tp, d), BF16)),
        grid=(tp // tm,),
        in_specs=[pl.BlockSpec((tm, k), row), _resident((k, d), fix), pl.BlockSpec((tm, d), row),
                  pl.BlockSpec((1, d), fix), pl.BlockSpec((1, d), fix)],
        out_specs=[pl.BlockSpec((tm, d), row)] * 3,
        compiler_params=_params(("arbitrary",)),
    )(a, w, h, gamma, beta)


def _mm_nn(xb, w, name):
    tp, k = xb.shape
    n = w.shape[1]
    tm = _tile(tp, ROW_TILE_CAP)

    def body(x_ref, w_ref, o_ref):
        o_ref[...] = _dot(x_ref[...], w_ref[...])

    return _call(
        body, 30, name=name, out_shape=_sds((tp, n), F32), grid=(tp // tm,),
        in_specs=[pl.BlockSpec((tm, k), lambda i: (i, 0)), _resident((k, n), lambda i: (0, 0))],
        out_specs=pl.BlockSpec((tm, n), lambda i: (i, 0)),
        compiler_params=_params(("arbitrary",)),
    )(xb, w)


def _mm_nt_add(a, w, resid, name):
    tp, k = a.shape
    n = w.shape[0]
    tm = _tile(tp, ROW_TILE_CAP)
    has_resid = resid is not None

    def body(*refs):
        if has_resid:
            a_ref, w_ref, r_ref, o_ref = refs
            o_ref[...] = r_ref[...] + _dot_nt(a_ref[...], w_ref[...])
        else:
            a_ref, w_ref, o_ref = refs
            o_ref[...] = _dot_nt(a_ref[...], w_ref[...])

    in_specs = [pl.BlockSpec((tm, k), lambda i: (i, 0)), _resident((n, k), lambda i: (0, 0))]
    args = [a, w]
    if has_resid:
        in_specs.append(pl.BlockSpec((tm, n), lambda i: (i, 0)))
        args.append(resid)
    return _call(
        body, 25, name=name, out_shape=_sds((tp, n), F32), grid=(tp // tm,),
        in_specs=in_specs, out_specs=pl.BlockSpec((tm, n), lambda i: (i, 0)),
        compiler_params=_params(("arbitrary",)),
    )(*args)


def _ffn_dx(dg, du, wgu, resid, name):
    tp, f = dg.shape
    d = wgu.shape[0]
    tm = _tile(tp, ROW_TILE_CAP)

    def body(dg_ref, du_ref, w_ref, r_ref, o_ref):
        acc = r_ref[...] + _dot_nt(dg_ref[...], w_ref[:, 0:f])
        o_ref[...] = acc + _dot_nt(du_ref[...], w_ref[:, f:2 * f])

    row = lambda i: (i, 0)
    return _call(
        body, 55, name=name, out_shape=_sds((tp, d), F32), grid=(tp // tm,),
        in_specs=[pl.BlockSpec((tm, f), row), pl.BlockSpec((tm, f), row), _resident((d, 2 * f), lambda i: (0, 0)),
                  pl.BlockSpec((tm, d), row)],
        out_specs=pl.BlockSpec((tm, d), row),
        compiler_params=_params(("arbitrary",)),
    )(dg, du, wgu, resid)


def _mm_tn(a, b, name):
    tp, ka = a.shape
    nb = b.shape[1]
    tt = _tile(tp, ROW_TILE_CAP)
    tk = _tile(ka, 1408, LANE)
    tn = _tile(nb, 1536, LANE)
    if tn < 512:
        tn = nb
    nt = tp // tt

    def body(a_ref, b_ref, o_ref):
        @pl.when(pl.program_id(2) == 0)
        def _():
            o_ref[...] = jnp.zeros_like(o_ref)
        o_ref[...] += _dot_tn(a_ref[...], b_ref[...])

    return _call(
        body, 40, name=name, out_shape=_sds((ka, nb), F32), grid=(ka // tk, nb // tn, nt),
        in_specs=[pl.BlockSpec((tt, tk), lambda i, j, t: (t, i)), pl.BlockSpec((tt, tn), lambda i, j, t: (t, j))],
        out_specs=pl.BlockSpec((tk, tn), lambda i, j, t: (i, j)),
        compiler_params=_params(("arbitrary", "arbitrary", "arbitrary")),
    )(a, b)


def _ln_bwd(y, dout, gamma, scale, name):
    tp, d = y.shape
    tm = _tile(tp, ROW_TILE_CAP)

    def body(y_ref, do_ref, g_ref, dres_ref, dyb_ref, dg_ref, db_ref):
        @pl.when(pl.program_id(0) == 0)
        def _():
            dg_ref[...] = jnp.zeros_like(dg_ref)
            db_ref[...] = jnp.zeros_like(db_ref)
        yv = y_ref[...]
        do = do_ref[...]
        mu = jnp.mean(yv, axis=-1, keepdims=True)
        yc = yv - mu
        var = jnp.mean(yc * yc, axis=-1, keepdims=True)
        rstd = lax.rsqrt(var + LN_EPS)
        xhat = yc * rstd
        dxh = do * g_ref[...]
        m1 = jnp.mean(dxh, axis=-1, keepdims=True)
        m2 = jnp.mean(dxh * xhat, axis=-1, keepdims=True)
        dy = rstd * (dxh - m1 - xhat * m2)
        dres_ref[...] = ALPHA * dy
        dyb_ref[...] = (scale * dy).astype(BF16)
        dg_ref[...] += jnp.sum(do * xhat, axis=0, keepdims=True)
        db_ref[...] += jnp.sum(do, axis=0, keepdims=True)

    row = lambda i: (i, 0)
    fix = lambda i: (0, 0)
    return _call(
        body, 22, name=name,
        out_shape=(_sds((tp, d), F32), _sds((tp, d), BF16), _sds((1, d), F32), _sds((1, d), F32)),
        grid=(tp // tm,),
        in_specs=[pl.BlockSpec((tm, d), row), pl.BlockSpec((tm, d), row), pl.BlockSpec((1, d), fix)],
        out_specs=[pl.BlockSpec((tm, d), row), pl.BlockSpec((tm, d), row), pl.BlockSpec((1, d), fix),
                   pl.BlockSpec((1, d), fix)],
        compiler_params=_params(("arbitrary",)),
    )(y, dout, gamma)


def _ffn_dact(dyb, wd, g, u, name):
    tp, d = dyb.shape
    f = wd.shape[0]
    tm = _tile(tp, ROW_TILE_CAP)
    tn = _tile(f, 1408, LANE)

    def body(dy_ref, w_ref, g_ref, u_ref, dg_ref, du_ref):
        dact = _dot_nt(dy_ref[...], w_ref[...])
        gv = g_ref[...].astype(F32)
        uv = u_ref[...].astype(F32)
        sg = jax.nn.sigmoid(gv)
        dg_ref[...] = (dact * uv * (sg * (1.0 + gv * (1.0 - sg)))).astype(BF16)
        du_ref[...] = (dact * (gv * sg)).astype(BF16)

    out = _sds((tp, f), BF16)
    blk = pl.BlockSpec((tm, tn), lambda j, i: (i, j))
    return _call(
        body, 55, name=name, out_shape=(out, out), grid=(f // tn, tp // tm),
        in_specs=[pl.BlockSpec((tm, d), lambda j, i: (i, 0)), pl.BlockSpec((tn, d), lambda j, i: (j, 0)), blk, blk],
        out_specs=[blk, blk],
        compiler_params=_params(("arbitrary", "arbitrary")),
    )(dyb, wd, g, u)


def _loss_grad(o, tgt, t_real, name):
    tp, d = o.shape
    tm = _tile(tp, ROW_TILE_CAP)

    def body(o_ref, t_ref, l_ref, d_ref):
        i = pl.program_id(0)

        @pl.when(i == 0)
        def _():
            l_ref[...] = jnp.zeros_like(l_ref)
        rows = i * tm + lax.broadcasted_iota(jnp.int32, (tm, 1), 0)
        real = jnp.logical_and(rows >= N_META, rows < t_real)
        err = jnp.where(real, o_ref[...] - t_ref[...], 0.0)
        d_ref[...] = err * (1.0 / d)
        l_ref[...] += jnp.sum(err * err, axis=0, keepdims=True) * (0.5 / d)

    row = lambda i: (i, 0)
    return pl.pallas_call(
        body, name=name, out_shape=(_sds((1, d), F32), _sds((tp, d), F32)), grid=(tp // tm,),
        in_specs=[pl.BlockSpec((tm, d), row), pl.BlockSpec((tm, d), row)],
        out_specs=[pl.BlockSpec((1, d), lambda i: (0, 0)), pl.BlockSpec((tm, d), row)],
        compiler_params=_params(("arbitrary",)),
    )(o, tgt)


CONV_TC = 256


def _silu_grad(y):
    s = jax.nn.sigmoid(y)
    return s * (1.0 + y * (1.0 - s))


def _conv_taps(x_ref, w, r0, rb):
    cur = x_ref[r0:r0 + rb, :]
    prev = x_ref[r0 - 8:r0, :] if r0 > 0 else jnp.zeros((8, cur.shape[1]), F32)
    xcat = jnp.concatenate([prev, cur], axis=0)
    taps = [xcat[5 + j:5 + j + rb] for j in range(CONV_K - 1)] + [cur]
    y = w[0:1] * taps[0]
    for j in range(1, CONV_K):
        y = y + w[j:j + 1] * taps[j]
    return y, taps


def _conv_fwd(proj, w, b, name):
    tp = proj.shape[0]
    rb = _tile(tp, ROW_TILE_CAP, 8)

    def body(x_ref, w_ref, b_ref, o_ref):
        gated = pl.program_id(0) > 0
        wv = w_ref[...]
        bv = b_ref[...]
        for r0 in range(0, tp, rb):
            y, _ = _conv_taps(x_ref, wv, r0, rb)
            y = y + bv
            o_ref[r0:r0 + rb, :] = jnp.where(gated, y * jax.nn.sigmoid(y), y)

    col = lambda j: (0, j)
    return _call(
        body, 25, name=name, out_shape=_sds((tp, CONV_W), F32), grid=(CONV_W // CONV_TC,),
        in_specs=[pl.BlockSpec((tp, CONV_TC), col), pl.BlockSpec((CONV_K, CONV_TC), col), pl.BlockSpec((1, CONV_TC), col)],
        out_specs=pl.BlockSpec((tp, CONV_TC), col),
        compiler_params=_params(("arbitrary",)),
    )(proj, w, b)


def _conv_bwd(proj, w, b, dxc, name):
    tp = proj.shape[0]
    rb = _tile(tp, ROW_TILE_CAP, 8)

    def body(x_ref, w_ref, b_ref, d_ref, dx_ref, dw_ref, db_ref, dy_scr):
        gated = pl.program_id(0) > 0
        wv = w_ref[...]
        bv = b_ref[...]
        dw = [jnp.zeros((1, CONV_TC), F32) for _ in range(CONV_K)]
        db = jnp.zeros((1, CONV_TC), F32)
        for r0 in range(0, tp, rb):
            y, taps = _conv_taps(x_ref, wv, r0, rb)
            y = y + bv
            d = d_ref[r0:r0 + rb, :]
            dy = jnp.where(gated, d * _silu_grad(y), d)
            dy_scr[r0:r0 + rb, :] = dy
            for j in range(CONV_K):
                dw[j] = dw[j] + jnp.sum(dy * taps[j], axis=0, keepdims=True)
            db = db + jnp.sum(dy, axis=0, keepdims=True)
        for j in range(CONV_K):
            dw_ref[j:j + 1, :] = dw[j]
        db_ref[...] = db
        for r0 in range(0, tp, rb):
            cur = dy_scr[r0:r0 + rb, :]
            nxt = dy_scr[r0 + rb:r0 + rb + 8, :] if r0 + rb < tp else jnp.zeros((8, CONV_TC), F32)
            dcat = jnp.concatenate([cur, nxt], axis=0)
            dx = wv[3:4] * cur
            for s in range(1, CONV_K):
                dx = dx + wv[3 - s:4 - s] * dcat[s:s + rb]
            dx_ref[r0:r0 + rb, :] = dx

    col = lambda j: (0, j)
    return _call(
        body, 70, name=name,
        out_shape=(_sds((tp, CONV_W), F32), _sds((CONV_K, CONV_W), F32), _sds((1, CONV_W), F32)),
        grid=(CONV_W // CONV_TC,),
        in_specs=[pl.BlockSpec((tp, CONV_TC), col), pl.BlockSpec((CONV_K, CONV_TC), col), pl.BlockSpec((1, CONV_TC), col),
                  pl.BlockSpec((tp, CONV_TC), col)],
        out_specs=[pl.BlockSpec((tp, CONV_TC), col), pl.BlockSpec((CONV_K, CONV_TC), col), pl.BlockSpec((1, CONV_TC), col)],
        scratch_shapes=[pltpu.VMEM((tp, CONV_TC), F32)],
        compiler_params=_params(("arbitrary",)),
    )(proj, w, b, dxc)


def _neg_expm1(x):
    series = -x * (1.0 + x * (0.5 + x * (1.0 / 6.0 + x * (1.0 / 24.0))))
    return jnp.where(jnp.abs(x) < 0.03, series, 1.0 - jnp.exp(x))


def _lru_gates(u, wa, ba, wx, bx, lam):
    r = jax.nn.sigmoid(jnp.dot(u, wa, precision=HI, preferred_element_type=F32) + ba)
    i = jax.nn.sigmoid(jnp.dot(u, wx, precision=HI, preferred_element_type=F32) + bx)
    log_a = -LRU_C * r * jax.nn.softplus(-lam)
    a = jnp.exp(log_a)
    b = jnp.sqrt(_neg_expm1(2.0 * log_a)) * (i * u)
    return a, b


def _lru_specs(tm, flip, n_tiles):
    idx = (lambda i: (n_tiles - 1 - i, 0)) if flip else (lambda i: (i, 0))
    return idx, lambda i: (0, 0)


def _lru_fwd(xc, proj, wa, ba, wx, bx, lam, name):
    tp = xc.shape[0]
    w = LRU_WIDTH
    tm = _tile(tp, ROW_TILE_CAP, 8)
    ycol = OFF_LRU_Y // w

    def body(u_ref, y_ref, wa_ref, ba_ref, wx_ref, bx_ref, lam_ref, o_ref, h_ref, a_scr, b_scr, carry):
        @pl.when(pl.program_id(0) == 0)
        def _():
            carry[...] = jnp.zeros_like(carry)
        a, b = _lru_gates(u_ref[...], wa_ref[...], ba_ref[...], wx_ref[...], bx_ref[...], lam_ref[...])
        a_scr[...] = a
        b_scr[...] = b

        def step(t, h):
            h = a_scr[pl.ds(t, 1), :] * h + b_scr[pl.ds(t, 1), :]
            h_ref[pl.ds(t, 1), :] = h
            return h

        carry[0:1, :] = lax.fori_loop(0, tm, step, carry[0:1, :])
        o_ref[...] = (h_ref[...] * jax.nn.gelu(y_ref[...])).astype(BF16)

    fix = lambda i: (0, 0)
    return _call(
        body, 40, name=name, out_shape=(_sds((tp, w), BF16), _sds((tp, w), F32)), grid=(tp // tm,),
        in_specs=[pl.BlockSpec((tm, w), lambda i: (i, 0)), pl.BlockSpec((tm, w), lambda i: (i, ycol)),
                  pl.BlockSpec((w, w), fix), pl.BlockSpec((1, w), fix), pl.BlockSpec((w, w), fix),
                  pl.BlockSpec((1, w), fix), pl.BlockSpec((1, w), fix)],
        out_specs=[pl.BlockSpec((tm, w), lambda i: (i, 0))] * 2,
        scratch_shapes=[pltpu.VMEM((tm, w), F32), pltpu.VMEM((tm, w), F32), pltpu.VMEM((8, w), F32)],
        compiler_params=_params(("arbitrary",)),
    )(xc, proj, wa, ba, wx, bx, lam)


def _lru_bwd(xc, proj, h, dout, wa, ba, wx, bx, lam, name):
    tp = xc.shape[0]
    w = LRU_WIDTH
    tm = _tile(tp, ROW_TILE_CAP, 8)
    nt = tp // tm
    ycol = OFF_LRU_Y // w
    rev = lambda i: (nt - 1 - i, 0)
    prev8 = lambda i: (jnp.maximum((nt - 1 - i) * (tm // 8) - 1, 0), 0)

    def body(u_ref, y_ref, h_ref, hp_ref, do_ref, wa_ref, ba_ref, wx_ref, bx_ref, lam_ref,
             du_ref, dy_ref, dwa_ref, dba_ref, dwx_ref, dbx_ref, dlam_ref,
             a_scr, dh_scr, g_scr, da_scr, hext, carry):
        i = pl.program_id(0)

        @pl.when(i == 0)
        def _():
            carry[...] = jnp.zeros_like(carry)
            for r in (dwa_ref, dba_ref, dwx_ref, dbx_ref, dlam_ref):
                r[...] = jnp.zeros_like(r)
        params = (wa_ref[...], ba_ref[...], wx_ref[...], bx_ref[...], lam_ref[...])
        (a, _), gates_vjp = jax.vjp(_lru_gates, u_ref[...], *params)
        gel, gelu_vjp = jax.vjp(jax.nn.gelu, y_ref[...])
        do = do_ref[...]
        hv = h_ref[...]
        dy_ref[...] = gelu_vjp(do * hv)[0]
        a_scr[...] = a
        dh_scr[...] = do * gel
        hext[0:8, :] = jnp.where(i == nt - 1, 0.0, hp_ref[...])
        hext[8:8 + tm, :] = hv

        def step(s, c):
            t = tm - 1 - s
            g = dh_scr[pl.ds(t, 1), :] + c
            g_scr[pl.ds(t, 1), :] = g
            da_scr[pl.ds(t, 1), :] = g * hext[pl.ds(t + 7, 1), :]
            return a_scr[pl.ds(t, 1), :] * g

        carry[0:1, :] = lax.fori_loop(0, tm, step, carry[0:1, :])
        du, dwa, dba, dwx, dbx, dlam = gates_vjp((da_scr[...], g_scr[...]))
        du_ref[...] = du
        dwa_ref[...] += dwa
        dba_ref[...] += dba
        dwx_ref[...] += dwx
        dbx_ref[...] += dbx
        dlam_ref[...] += dlam

    fix = lambda i: (0, 0)
    tile = pl.BlockSpec((tm, w), rev)
    mat = pl.BlockSpec((w, w), fix)
    vec = pl.BlockSpec((1, w), fix)
    return _call(
        body, 90, name=name,
        out_shape=(_sds((tp, w), F32), _sds((tp, w), F32), _sds((w, w), F32), _sds((1, w), F32), _sds((w, w), F32),
                   _sds((1, w), F32), _sds((1, w), F32)),
        grid=(nt,),
        in_specs=[tile, pl.BlockSpec((tm, w), lambda i: (nt - 1 - i, ycol)), tile, pl.BlockSpec((8, w), prev8), tile,
                  mat, vec, mat, vec, vec],
        out_specs=[tile, tile, mat, vec, mat, vec, vec],
        scratch_shapes=[pltpu.VMEM((tm, w), F32)] * 4 + [pltpu.VMEM((tm + 8, w), F32), pltpu.VMEM((8, w), F32)],
        compiler_params=_params(("arbitrary",)),
    )(xc, proj, h, h, dout, wa, ba, wx, bx, lam)


def _dot3(a, b, dims):
    ah = a.astype(BF16)
    al = (a - ah.astype(F32)).astype(BF16)
    bh = b.astype(BF16)
    bl = (b - bh.astype(F32)).astype(BF16)
    dot = lambda x, y: lax.dot_general(x, y, (dims, ((0,), (0,))), preferred_element_type=F32)
    return dot(ah, bh) + (dot(ah, bl) + dot(al, bh))


@jax.custom_vjp
def _bmm(a, b):
    return _dot3(a, b, ((2,), (1,)))


@jax.custom_vjp
def _bmm_nt(a, b):
    return _dot3(a, b, ((2,), (2,)))


@jax.custom_vjp
def _bmm_tn(a, b):
    return _dot3(a, b, ((1,), (1,)))


_bmm.defvjp(lambda a, b: (_bmm(a, b), (a, b)), lambda r, g: (_bmm_nt(g, r[1]), _bmm_tn(r[0], g)))
_bmm_nt.defvjp(lambda a, b: (_bmm_nt(a, b), (a, b)), lambda r, g: (_bmm(g, r[1]), _bmm_tn(g, r[0])))
_bmm_tn.defvjp(lambda a, b: (_bmm_tn(a, b), (a, b)), lambda r, g: (_bmm_nt(r[1], g), _bmm(r[0], g)))


def _dot1(a, b, dims):
    return lax.dot_general(a.astype(BF16), b.astype(BF16), (dims, ((0,), (0,))), preferred_element_type=F32)


@jax.custom_vjp
def _bmm1(a, b):
    return _dot1(a, b, ((2,), (1,)))


@jax.custom_vjp
def _bmm1_nt(a, b):
    return _dot1(a, b, ((2,), (2,)))


@jax.custom_vjp
def _bmm1_tn(a, b):
    return _dot1(a, b, ((1,), (1,)))


_bmm1.defvjp(lambda a, b: (_bmm1(a, b), (a, b)), lambda r, g: (_bmm1_nt(g, r[1]), _bmm1_tn(r[0], g)))
_bmm1_nt.defvjp(lambda a, b: (_bmm1_nt(a, b), (a, b)), lambda r, g: (_bmm1(g, r[1]), _bmm1_tn(g, r[0])))
_bmm1_tn.defvjp(lambda a, b: (_bmm1_tn(a, b), (a, b)), lambda r, g: (_bmm1_nt(r[1], g), _bmm1(r[0], g)))


def _chunk_masks(nh):
    r = lax.broadcasted_iota(jnp.int32, (CHUNK, CHUNK), 0)
    c = lax.broadcasted_iota(jnp.int32, (CHUNK, CHUNK), 1)
    full = lambda m: jnp.broadcast_to(m[None], (nh, CHUNK, CHUNK))
    return r, c, full


def _decay_terms(g, nh):
    r, c, full = _chunk_masks(nh)
    incl = r >= c
    cs = _bmm(full(incl.astype(F32)), g)
    cs_t = jnp.swapaxes(cs, 1, 2)
    tot = jnp.broadcast_to(jnp.sum(g, axis=1, keepdims=True), g.shape)
    m = full(incl)
    decay = jnp.where(m, jnp.exp(jnp.where(m, cs - cs_t, 0.0)), 0.0)
    return cs, decay, tot


def _rep_groups(x):
    return jnp.concatenate([jnp.broadcast_to(x[g:g + 1], (SSD_HPG,) + x.shape[1:]) for g in range(SSD_GROUPS)], axis=0)


def _ssd_chunk(xs, bm, cm, dtr, z, p_dtb, p_alog, p_d, p_nw, state):
    nh = SSD_HEADS
    dt = jax.nn.softplus(dtr + p_dtb)
    a = dt * (-jnp.exp(p_alog))
    x = xs * dt
    cs, decay, tot = _decay_terms(a, nh)
    b8 = _rep_groups(bm)
    c8 = _rep_groups(cm)
    y = _bmm1(_rep_groups(_bmm1_nt(cm, bm)) * decay, x)
    y = y + _bmm1_nt(c8, state) * jnp.exp(cs)
    new_state = state * jnp.exp(tot) + _bmm1_tn(x * jnp.exp(tot - cs), b8)
    y = y + p_d * xs
    y = y * (z * jax.nn.sigmoid(z))
    ss = jnp.sum(y * y, axis=-1, keepdims=True)
    ssg = jnp.concatenate(
        [jnp.broadcast_to(jnp.sum(ss[g * SSD_HPG:(g + 1) * SSD_HPG], axis=0, keepdims=True), (SSD_HPG, CHUNK, 1))
         for g in range(SSD_GROUPS)], axis=0)
    y = y * lax.rsqrt(ssg * (1.0 / (SSD_HPG * HEAD)) + RMS_EPS) * p_nw
    return y, new_state


def _unit_lower_inverse(m, nh):
    r, c, full = _chunk_masks(nh)
    eye = full((r == c).astype(F32))
    md = jnp.where(full((r // 16) == (c // 16)), m, 0.0)
    mo = m - md
    x = eye - md
    p = _bmm(md, md)
    x = x + _bmm(x, p)
    p = _bmm(p, p)
    x = x + _bmm(x, p)
    p = _bmm(p, p)
    x = x + _bmm(x, p)
    n = _bmm(x, mo)
    y = x - _bmm(n, x)
    return y + _bmm(_bmm(n, n), y)


def _dn_chunk(q, k, v, gate, braw, araw, p_alog, p_dtb, p_nw, state):
    nh = DN_HEADS
    r, c, full = _chunk_masks(nh)
    q = q * lax.rsqrt(jnp.sum(q * q, axis=-1, keepdims=True) + RMS_EPS) * (HEAD ** -0.5)
    k = k * lax.rsqrt(jnp.sum(k * k, axis=-1, keepdims=True) + RMS_EPS)
    beta = jax.nn.sigmoid(braw)
    g = -jnp.exp(p_alog) * jax.nn.softplus(araw + p_dtb)
    gcs, decay, tot = _decay_terms(g, nh)
    kb = k * beta
    vb = v * beta
    m = jnp.where(full(r > c), _bmm1_nt(kb, k) * decay, 0.0)
    t = _unit_lower_inverse(m, nh)
    egcs = jnp.exp(gcs)
    u = _bmm(t, vb)
    w = _bmm(t, kb * egcs)
    attn = _bmm1_nt(q, k) * decay
    v_new = u - _bmm1(w, state)
    out = _bmm1(q * egcs, state) + _bmm1(attn, v_new)
    new_state = state * jnp.exp(tot) + _bmm1_tn(k * jnp.exp(tot - gcs), v_new)
    out = out * lax.rsqrt(jnp.mean(out * out, axis=-1, keepdims=True) + RMS_EPS) * p_nw
    return out * (gate * jax.nn.sigmoid(gate)), new_state


def _chunks_per_step(nc):
    return max(n for n in range(1, 6) if nc % n == 0)


def _heads(t, n):
    return jnp.stack([t[:, HEAD * h:HEAD * (h + 1)] for h in range(n)])


def _unheads(a):
    return jnp.concatenate([a[h] for h in range(a.shape[0])], axis=-1)


def _head_scalars(t, off, n):
    return jnp.stack([jnp.broadcast_to(t[:, off + h:off + h + 1], (CHUNK, HEAD)) for h in range(n)])


def _unhead_scalars(d, off):
    red = jnp.sum(d, axis=-1, keepdims=True)
    lane = lax.broadcasted_iota(jnp.int32, (CHUNK, LANE), 1)
    out = jnp.zeros((CHUNK, LANE), F32)
    for h in range(d.shape[0]):
        out = out + jnp.where(lane == off + h, red[h], 0.0)
    return out


def _chunk_scan_fwd(chunk_fn, srcs, read, params, nh, host_us, name):
    tp = srcs[0][0].shape[0]
    nc = tp // CHUNK
    nb = _chunks_per_step(nc)
    rows = nb * CHUNK
    ns, npar = len(srcs), len(params)

    def body(*refs):
        s_refs, p_refs = refs[:ns], refs[ns:ns + npar]
        y_ref, st_ref, state = refs[ns + npar:]

        @pl.when(pl.program_id(0) == 0)
        def _():
            state[...] = jnp.zeros_like(state)
        par = [r[...] for r in p_refs]
        st = state[...]
        for k in range(nb):
            sl = slice(k * CHUNK, (k + 1) * CHUNK)
            st_ref[k] = st
            y, st = chunk_fn(*read(s_refs, sl), *par, st)
            y_ref[sl, :] = _unheads(y).astype(BF16)
        state[...] = st

    src_spec = lambda s: pl.BlockSpec((rows, s[1]), lambda c: (c, s[2]))
    par_spec = lambda a: pl.BlockSpec(a.shape, lambda c: (0, 0, 0))
    return _call(
        body, host_us, name=name,
        out_shape=(_sds((tp, nh * HEAD), BF16), _sds((nc, nh, HEAD, HEAD), F32)),
        grid=(nc // nb,),
        in_specs=[src_spec(s) for s in srcs] + [par_spec(a) for a in params],
        out_specs=[pl.BlockSpec((rows, nh * HEAD), lambda c: (c, 0)),
                   pl.BlockSpec((nb, nh, HEAD, HEAD), lambda c: (c, 0, 0, 0))],
        scratch_shapes=[pltpu.VMEM((nh, HEAD, HEAD), F32)],
        compiler_params=_params(("arbitrary",)),
    )(*[s[0] for s in srcs], *params)


def _chunk_scan_bwd(chunk_fn, srcs, read, params, states, dsrc, write, out_widths, nh, host_us, name):
    tp = srcs[0][0].shape[0]
    nc = tp // CHUNK
    nb = _chunks_per_step(nc)
    rows = nb * CHUNK
    steps = nc // nb
    ns, npar, nout = len(srcs), len(params), len(out_widths)
    d_arr, d_off = dsrc

    def body(*refs):
        s_refs, p_refs = refs[:ns], refs[ns:ns + npar]
        st_ref, dy_ref = refs[ns + npar:ns + npar + 2]
        o_refs = refs[ns + npar + 2:ns + npar + 2 + nout]
        dp_refs = refs[ns + npar + 2 + nout:ns + 2 * npar + 2 + nout]
        dstate = refs[-1]

        @pl.when(pl.program_id(0) == 0)
        def _():
            dstate[...] = jnp.zeros_like(dstate)
            for r in dp_refs:
                r[...] = jnp.zeros_like(r)
        par = [r[...] for r in p_refs]
        dst = dstate[...]
        dpar = None
        for k in reversed(range(nb)):
            sl = slice(k * CHUNK, (k + 1) * CHUNK)
            seqs = read(s_refs, sl)
            _, vjp = jax.vjp(chunk_fn, *seqs, *par, st_ref[k])
            grads = vjp((_heads(dy_ref[sl, d_off:d_off + nh * HEAD], nh), dst))
            for r, tile in zip(o_refs, write(*grads[:len(seqs)])):
                r[sl, :] = tile
            gp = grads[len(seqs):len(seqs) + npar]
            dpar = gp if dpar is None else [a + b for a, b in zip(dpar, gp)]
            dst = grads[-1]
        for r, gr in zip(dp_refs, dpar):
            r[...] += gr
        dstate[...] = dst

    rev = lambda c: steps - 1 - c
    src_spec = lambda s: pl.BlockSpec((rows, s[1]), lambda c: (rev(c), s[2]))
    par_spec = lambda a: pl.BlockSpec(a.shape, lambda c: (0, 0, 0))
    out_spec = lambda w: pl.BlockSpec((rows, w), lambda c: (rev(c), 0))
    return tuple(_call(
        body, host_us, name=name,
        out_shape=tuple(_sds((tp, w), F32) for w in out_widths) + tuple(_sds(a.shape, F32) for a in params),
        grid=(steps,),
        in_specs=[src_spec(s) for s in srcs] + [par_spec(a) for a in params]
        + [pl.BlockSpec((nb, nh, HEAD, HEAD), lambda c: (rev(c), 0, 0, 0)), out_spec(d_arr.shape[1])],
        out_specs=[out_spec(w) for w in out_widths] + [par_spec(a) for a in params],
        scratch_shapes=[pltpu.VMEM((nh, HEAD, HEAD), F32)],
        compiler_params=_params(("arbitrary",)),
    )(*[s[0] for s in srcs], *params, states, d_arr))


def _lane_param(p):
    return jnp.broadcast_to(p[:, None, None], (p.shape[0], 1, HEAD))


def _block_diag(w):
    out = jnp.zeros((LRU_WIDTH, LRU_WIDTH), F32)
    for h in range(LRU_HEADS):
        out = out.at[h * HEAD:(h + 1) * HEAD, h * HEAD:(h + 1) * HEAD].set(w[h])
    return out


def _block_diag_inv(w):
    return jnp.stack([w[h * HEAD:(h + 1) * HEAD, h * HEAD:(h + 1) * HEAD] for h in range(LRU_HEADS)])


def _ssd_inputs(xc, proj, lp):
    srcs = ((xc, CONV_W, 0), (proj, SSD_INNER, OFF_SSD_Z // SSD_INNER), (proj, LANE, OFF_SMALL // LANE))
    params = (_lane_param(lp["ssd_dt_bias"]), _lane_param(lp["ssd_a_log"]), _lane_param(lp["ssd_d"]),
              lp["ssd_norm_w"].reshape(SSD_HEADS, 1, HEAD))
    return srcs, params


def _ssd_read(refs, rows):
    xc, z, sm = refs
    return (_heads(xc[rows, 256:768], SSD_HEADS), _heads(xc[rows, 768:896], SSD_GROUPS),
            _heads(xc[rows, 896:1024], SSD_GROUPS), _head_scalars(sm[rows, :], 0, SSD_HEADS), _heads(z[rows, :], SSD_HEADS))


def _ssd_write(dxs, dbm, dcm, ddtr, dz):
    return (jnp.concatenate([_unheads(dxs), _unheads(dbm), _unheads(dcm)], axis=-1), _unheads(dz), _unhead_scalars(ddtr, 0))


def _dn_inputs(xc, proj, lp):
    srcs = ((xc, CONV_W, 0), (proj, 256, OFF_DN_GATE // 256), (proj, LANE, OFF_SMALL // LANE))
    params = (_lane_param(lp["dn_a_log"]), _lane_param(lp["dn_dt_bias"]),
              jnp.broadcast_to(lp["dn_norm_w"][None, None, :], (DN_HEADS, 1, HEAD)))
    return srcs, params


def _dn_read(refs, rows):
    xc, gate, sm = refs
    return (_heads(xc[rows, 1024:1280], DN_HEADS), _heads(xc[rows, 1280:1536], DN_HEADS),
            _heads(xc[rows, 1536:1792], DN_HEADS), _heads(gate[rows, :], DN_HEADS),
            _head_scalars(sm[rows, :], 8, DN_HEADS), _head_scalars(sm[rows, :], 12, DN_HEADS))


def _dn_write(dq, dk, dv, dgate, dbraw, daraw):
    return (jnp.concatenate([_unheads(dq), _unheads(dk), _unheads(dv)], axis=-1), _unheads(dgate),
            _unhead_scalars(dbraw, 8) + _unhead_scalars(daraw, 12))


def _lru_params(lp):
    return (_block_diag(lp["lru_w_a"]), lp["lru_b_a"][None], _block_diag(lp["lru_w_x"]), lp["lru_b_x"][None],
            lp["lru_lambda"][None])


def _conv_params(lp):
    w = jnp.concatenate([lp["lru_conv_w"], lp["ssd_conv_w"], lp["dn_conv_w"]], axis=1)
    b = jnp.concatenate([lp["lru_conv_b"], lp["ssd_conv_b"], jnp.zeros((768,), F32)])[None]
    return w, b


def _mixers_fwd(proj, lp, tag):
    cw, cb = _conv_params(lp)
    xc = _conv_fwd(proj, cw, cb, tag + "_conv")
    o_lru, h_lru = _lru_fwd(xc, proj, *_lru_params(lp), name=tag + "_lru")
    s_srcs, s_par = _ssd_inputs(xc, proj, lp)
    o_ssd, s_states = _chunk_scan_fwd(_ssd_chunk, s_srcs, _ssd_read, s_par, SSD_HEADS, 88, tag + "_ssd")
    d_srcs, d_par = _dn_inputs(xc, proj, lp)
    o_dn, d_states = _chunk_scan_fwd(_dn_chunk, d_srcs, _dn_read, d_par, DN_HEADS, 170, tag + "_dn")
    cat = jnp.concatenate([o_lru, o_ssd, o_dn], axis=1)
    return cat, (xc, h_lru, s_states, d_states)


def _mixers_bwd(proj, lp, saved, dcat, tag):
    xc, h_lru, s_states, d_states = saved
    tp = proj.shape[0]
    cw, cb = _conv_params(lp)
    g = {}
    du, dyraw, dwa, dba, dwx, dbx, dlam = _lru_bwd(xc, proj, h_lru, dcat, *_lru_params(lp), name=tag + "_lru_bwd")
    g["lru_w_a"], g["lru_b_a"], g["lru_w_x"], g["lru_b_x"], g["lru_lambda"] = (
        _block_diag_inv(dwa), dba[0], _block_diag_inv(dwx), dbx[0], dlam[0])

    s_srcs, s_par = _ssd_inputs(xc, proj, lp)
    dxbc, dz, dsm_ssd, dp_dtb, dp_alog, dp_d, dp_nw = _chunk_scan_bwd(
        _ssd_chunk, s_srcs, _ssd_read, s_par, s_states, (dcat, 256), _ssd_write, (768, SSD_INNER, LANE), SSD_HEADS, 198,
        tag + "_ssd_bwd")
    g["ssd_dt_bias"], g["ssd_a_log"], g["ssd_d"] = (jnp.sum(p, axis=(1, 2)) for p in (dp_dtb, dp_alog, dp_d))
    g["ssd_norm_w"] = dp_nw.reshape(SSD_INNER)

    d_srcs, d_par = _dn_inputs(xc, proj, lp)
    dqkv, dgate, dsm_dn, dq_alog, dq_dtb, dq_nw = _chunk_scan_bwd(
        _dn_chunk, d_srcs, _dn_read, d_par, d_states, (dcat, 768), _dn_write, (768, 256, LANE), DN_HEADS, 425,
        tag + "_dn_bwd")
    g["dn_a_log"], g["dn_dt_bias"] = (jnp.sum(p, axis=(1, 2)) for p in (dq_alog, dq_dtb))
    g["dn_norm_w"] = jnp.sum(dq_nw, axis=(0, 1))

    dconv, dcw, dcb = _conv_bwd(proj, cw, cb, jnp.concatenate([du, dxbc, dqkv], axis=1), tag + "_conv_bwd")
    g["lru_conv_w"], g["ssd_conv_w"], g["dn_conv_w"] = dcw[:, :256], dcw[:, 256:1024], dcw[:, 1024:]
    g["lru_conv_b"], g["ssd_conv_b"] = dcb[0, :256], dcb[0, 256:1024]
    dproj = jnp.concatenate([dconv, dyraw, dz, dgate, dsm_ssd + dsm_dn], axis=1)
    return dproj, g


MIXER_PARAMS = ("lru_conv_w", "lru_conv_b", "lru_w_a", "lru_b_a", "lru_w_x", "lru_b_x", "lru_lambda",
                "ssd_conv_w", "ssd_conv_b", "ssd_dt_bias", "ssd_a_log", "ssd_d", "ssd_norm_w",
                "dn_conv_w", "dn_a_log", "dn_dt_bias", "dn_norm_w")


UNITS = ("gate0", "up0", "down0", "win", "wout", "gate1", "up1", "down1")


def _layer_shards(w, l):
    out = {"win": w["w_in"][l], "wout": w["w_out"][l]}
    for j in range(2):
        out[f"gate{j}"], out[f"up{j}"], out[f"down{j}"] = w["ffn_w_gate"][l, j], w["ffn_w_up"][l, j], w["ffn_w_down"][l, j]
    return {k: a.astype(BF16) for k, a in out.items()}


def _enqueue_matrix(name, arr, slab):
    if name == "wout":
        return [_enqueue(arr, slab)], 1
    return _enqueue_halves(arr, slab, 1 if slab else 0)


def _enqueue_layer(shards):
    return {k: _enqueue_matrix(k, shards[k], False) for k in UNITS}


def _gathered_cols(a):
    return a.transpose(1, 0, 2).reshape(a.shape[1], -1)


def _col_slabs(a):
    return a.reshape(a.shape[0], N_DEV, -1).transpose(1, 0, 2).astype(BF16)


def _row_slabs(a):
    return a.reshape(N_DEV, -1, a.shape[1]).astype(BF16)


def _local_step(x, tgt, small, first, shards):
    s = x.shape[0]
    t_real = N_META + s
    tp = -(-t_real // CHUNK) * CHUNK
    depth = len(first) if shards is None else len(shards)
    h = jnp.concatenate([small["meta"], x, jnp.zeros((tp - t_real, D_MODEL), F32)], axis=0)
    hb = h.astype(BF16)
    ln_g, ln_b = small["ln_g"], small["ln_b"]
    saved, weights = [], []
    queued = first
    for l in range(depth):
        lp = {k: small[k][l] for k in MIXER_PARAMS}
        t = f"l{l}"
        if shards is None:
            get = lambda k, g=first[l]: g[k]
        else:
            get = lambda k, g=queued: _collect_halves(g[k])
            if l + 1 < depth:
                queued = _enqueue_layer(shards[l + 1])

        def ffn_weights(j):
            up = _gathered_cols(get(f"up{j}"))
            return jnp.concatenate([_gathered_cols(get(f"gate{j}")), up], axis=-1)

        wgu0 = ffn_weights(0)
        g0, u0, a0 = _ffn_up(hb, wgu0, t + "_ffn0_up")
        wd0 = get("down0").reshape(-1, D_MODEL)
        y1, h1, h1b = _mm_resid_ln(a0, wd0, h, ln_g[l, 0][None], ln_b[l, 0][None], FFN_RES, t + "_ffn0_down")
        win = _proj_cols(_gathered_cols(get("win")))
        proj = _mm_nn(h1b, win, t + "_in_proj")
        cat, mix_saved = _mixers_fwd(proj, lp, t)
        catb = cat.astype(BF16)
        wout = get("wout").reshape(D_MODEL, D_MODEL)
        y2, h2, h2b = _mm_resid_ln(catb, wout, h1, ln_g[l, 1][None], ln_b[l, 1][None], 1.0, t + "_out_proj")
        wgu1 = ffn_weights(1)
        g1, u1, a1 = _ffn_up(h2b, wgu1, t + "_ffn1_up")
        wd1 = get("down1").reshape(-1, D_MODEL)
        y3, h3, h3b = _mm_resid_ln(a1, wd1, h2, ln_g[l, 2][None], ln_b[l, 2][None], FFN_RES, t + "_ffn1_down")
        weights.append(([wgu0, wgu1], [wd0, wd1], win, wout))
        saved.append((hb, g0, u0, a0, y1, h1b, proj, mix_saved, catb, y2, h2b, g1, u1, a1, y3))
        h, hb = h3, h3b

    tgt_p = jnp.pad(tgt, ((N_META, tp - t_real), (0, 0)))
    lossv, dh = _loss_grad(h, tgt_p, t_real, "loss")
    loss = jnp.sum(lossv)

    gs = {k: [None] * depth for k in MIXER_PARAMS}
    d_ln_g = [[None] * 3 for _ in range(depth)]
    d_ln_b = [[None] * 3 for _ in range(depth)]
    slabs = [{} for _ in range(depth)]
    send = (lambda k, a: a) if shards is None else (lambda k, a: _enqueue_matrix(k, a, True))

    def ffn_bwd(l, j, xb_in, g, u, a, y, dout):
        t = f"l{l}_ffn{j}"
        wgu, wd = weights[l][0][j], weights[l][1][j]
        dres, dyb, dgam, dbet = _ln_bwd(y, dout, ln_g[l, 2 * j][None], FFN_RES, t + "_ln_bwd")
        d_ln_g[l][2 * j], d_ln_b[l][2 * j] = dgam[0], dbet[0]
        slabs[l][f"down{j}"] = send("down", _row_slabs(_mm_tn(a, dyb, t + "_dwd")))
        dg, du = _ffn_dact(dyb, wd, g, u, t + "_dact")
        slabs[l][f"gate{j}"] = send("gate", _col_slabs(_mm_tn(xb_in, dg, t + "_dwg")))
        slabs[l][f"up{j}"] = send("up", _col_slabs(_mm_tn(xb_in, du, t + "_dwu")))
        return _ffn_dx(dg, du, wgu, dres, t + "_dx")

    for l in reversed(range(depth)):
        hb_in, g0, u0, a0, y1, h1b, proj, mix_saved, catb, y2, h2b, g1, u1, a1, y3 = saved[l]
        lp = {k: small[k][l] for k in MIXER_PARAMS}
        t = f"l{l}"
        _, _, win, wout = weights[l]
        dh2 = ffn_bwd(l, 1, h2b, g1, u1, a1, y3, dh)
        dres, dyb, dgam, dbet = _ln_bwd(y2, dh2, ln_g[l, 1][None], 1.0, t + "_mix_ln_bwd")
        d_ln_g[l][1], d_ln_b[l][1] = dgam[0], dbet[0]
        slabs[l]["wout"] = send("wout", _row_slabs(_mm_tn(catb, dyb, t + "_dwout")))
        dcat = _mm_nt_add(dyb, wout, None, t + "_dcat")
        dproj, mg = _mixers_bwd(proj, lp, mix_saved, dcat, t)
        for k in MIXER_PARAMS:
            gs[k][l] = mg[k]
        dprojb = dproj.astype(BF16)
        slabs[l]["win"] = send("win", _col_slabs(_proj_cols_inv(_mm_tn(h1b, dprojb, t + "_dwin"))))
        dh1 = _mm_nt_add(dprojb, win, dres, t + "_dh1")
        dh = ffn_bwd(l, 0, hb_in, g0, u0, a0, y1, dh1)

    small_grads = {k: jnp.stack(v) for k, v in gs.items()}
    small_grads["ln_g"] = jnp.stack([jnp.stack(r) for r in d_ln_g])
    small_grads["ln_b"] = jnp.stack([jnp.stack(r) for r in d_ln_b])
    small_grads["meta"] = dh[:N_META]
    return loss, dh[N_META:t_real], small_grads, slabs


def _mesh_pos():
    return lax.axis_index("x"), lax.axis_index("y"), lax.axis_index("c")


def _flip(pos, k):
    x, y, c = pos
    return (1 - x if k & 4 else x, 1 - y if k & 2 else y, 1 - c if k & 1 else c)


def _flat(pos):
    return 4 * pos[0] + 2 * pos[1] + pos[2]


def _exchange_copies(ins, outs, slabs, send, recv, loc):
    pos = _mesh_pos()
    me = _flat(pos)
    local, sends, receives = [], [], []
    for a in range(len(ins)):
        local.append(pltpu.make_async_copy(ins[a].at[me] if slabs[a] else ins[a], outs[a].at[me], loc.at[a]))
    for k in range(1, N_DEV):
        peer = _flip(pos, k)
        for a in range(len(ins)):
            sem = dict(send_sem=send.at[a, k - 1], recv_sem=recv.at[a, k - 1], device_id=peer, device_id_type=MESH_IDS)
            sends.append(pltpu.make_async_remote_copy(
                src_ref=ins[a].at[_flat(peer)] if slabs[a] else ins[a], dst_ref=outs[a].at[me], **sem))
            receives.append(pltpu.make_async_remote_copy(
                src_ref=ins[a].at[me] if slabs[a] else ins[a], dst_ref=outs[a].at[_flat(peer)], **sem))
    return local, sends, receives


def _exchange_start(ins, outs, slabs, send, recv, loc):
    local, sends, _ = _exchange_copies(ins, outs, slabs, send, recv, loc)
    for cp in local + sends:
        cp.start()


def _exchange_wait(ins, outs, slabs, send, recv, loc):
    local, sends, receives = _exchange_copies(ins, outs, slabs, send, recv, loc)
    for cp in receives:
        cp.wait_recv()
    for cp in sends:
        cp.wait_send()
    for cp in local:
        cp.wait()


def _exchange_shapes(arrs, slabs):
    return tuple(_sds(a.shape if s else (N_DEV,) + a.shape, a.dtype) for a, s in zip(arrs, slabs))


def _exchange_sems(n):
    return [pltpu.SemaphoreType.DMA((n, N_DEV - 1)), pltpu.SemaphoreType.DMA((n, N_DEV - 1)),
            pltpu.SemaphoreType.DMA((n,))]


def _exchange(arrs, slabs, name):
    n = len(arrs)

    def body(*refs):
        ins, outs, sems = refs[:n], refs[n:2 * n], refs[2 * n:]
        _exchange_start(ins, outs, slabs, *sems)
        _exchange_wait(ins, outs, slabs, *sems)

    hbm = pl.BlockSpec(memory_space=pl.ANY)
    return pl.pallas_call(
        body, name=name, out_shape=_exchange_shapes(arrs, slabs), in_specs=[hbm] * n, out_specs=[hbm] * n,
        scratch_shapes=_exchange_sems(n),
    )(*arrs)


def _adam_math(w, g, m, v):
    m = ADAM_B1 * m + (1.0 - ADAM_B1) * g
    v = ADAM_B2 * v + (1.0 - ADAM_B2) * (g * g)
    m_hat = m / (1.0 - ADAM_B1 ** ADAM_STEP)
    v_hat = v / (1.0 - ADAM_B2 ** ADAM_STEP)
    delta = -ADAM_LR * (m_hat / (jnp.sqrt(v_hat) + ADAM_EPS) + ADAM_WD * w)
    return delta, m, v


def _adam(w, g, m, v, name):
    r, c = w.shape
    parts = g.ndim == 3
    tr = _tile(r, 512, 8)

    def body(w_ref, g_ref, m_ref, v_ref, go_ref, d_ref, mo_ref, vo_ref):
        if parts:
            gv = g_ref[0].astype(F32)
            for d in range(1, N_DEV):
                gv = gv + g_ref[d].astype(F32)
        else:
            gv = g_ref[...]
        delta, mn, vn = _adam_math(w_ref[...], gv, m_ref[...], v_ref[...])
        go_ref[...] = gv
        d_ref[...] = delta
        mo_ref[...] = mn
        vo_ref[...] = vn

    blk = pl.BlockSpec((tr, c), lambda i: (i, 0))
    gblk = pl.BlockSpec((N_DEV, tr, c), lambda i: (0, i, 0)) if parts else blk
    out = _sds((r, c), F32)
    return pl.pallas_call(
        body, name=name, out_shape=(out,) * 4, grid=(r // tr,),
        in_specs=[blk, gblk, blk, blk], out_specs=[blk] * 4,
        compiler_params=_params(("arbitrary",)),
    )(w, g, m, v)


def _sum_parts(parts, name):
    _, r, c = parts.shape

    def body(p_ref, o_ref):
        acc = p_ref[0]
        for d in range(1, N_DEV):
            acc = acc + p_ref[d]
        o_ref[...] = acc

    return pl.pallas_call(body, name=name, out_shape=_sds((r, c), F32), compiler_params=_params())(parts)


SMALL_SHARD_AXIS = {
    "meta": 1, "ln_g": 2, "ln_b": 2, "lru_conv_w": 2, "lru_conv_b": None, "lru_w_a": None, "lru_b_a": None,
    "lru_w_x": None, "lru_b_x": None, "lru_lambda": None, "ssd_conv_w": 2, "ssd_conv_b": None, "ssd_dt_bias": None,
    "ssd_a_log": None, "ssd_d": None, "ssd_norm_w": None, "dn_conv_w": 2, "dn_a_log": None, "dn_dt_bias": None,
    "dn_norm_w": None,
}
BIG = ("ffn_w_gate", "ffn_w_up", "ffn_w_down", "w_in", "w_out")
WEIGHT_ORDER = ("meta", "ln_g", "ln_b", "ffn_w_gate", "ffn_w_up", "ffn_w_down", "w_in", "lru_conv_w", "lru_conv_b",
                "lru_w_a", "lru_b_a", "lru_w_x", "lru_b_x", "lru_lambda", "ssd_conv_w", "ssd_conv_b", "ssd_dt_bias",
                "ssd_a_log", "ssd_d", "ssd_norm_w", "dn_conv_w", "dn_a_log", "dn_dt_bias", "dn_norm_w", "w_out")


def _pack(arrs):
    flat = jnp.concatenate([a.reshape(-1) for a in arrs])
    rows = -(-flat.shape[0] // (8 * LANE)) * 8
    return jnp.pad(flat, (0, rows * LANE - flat.shape[0])).reshape(rows, LANE)


def _unpack(buf, shapes, lead=()):
    flat = buf.reshape(lead + (-1,))
    out, off = [], 0
    for s in shapes:
        n = math.prod(s)
        out.append(flat[..., off:off + n].reshape(lead + tuple(s)))
        off += n
    return out


def _proj_cols(w):
    pad = jnp.zeros(w.shape[:-1] + (PROJ_W - D_IN,), w.dtype)
    return jnp.concatenate([w[..., 0:256], w[..., 1024:1792], w[..., 1800:2568], w[..., 256:512], w[..., 512:1024],
                            w[..., 2568:2824], w[..., 1792:1800], w[..., 2824:2832], pad], axis=-1)


def _proj_cols_inv(w):
    return jnp.concatenate([w[..., 0:256], w[..., 1792:2048], w[..., 2048:2560], w[..., 256:1024], w[..., 2816:2824],
                            w[..., 1024:1792], w[..., 2560:2816], w[..., 2824:2832]], axis=-1)


def kernel(x, meta, ln_g, ln_b, ffn_w_gate, ffn_w_up, ffn_w_down, w_in, lru_conv_w, lru_conv_b, lru_w_a, lru_b_a, lru_w_x, lru_b_x, lru_lambda, ssd_conv_w, ssd_conv_b, ssd_dt_bias, ssd_a_log, ssd_d, ssd_norm_w, dn_conv_w, dn_a_log, dn_dt_bias, dn_norm_w, w_out, loss_target, m_meta, m_ln_g, m_ln_b, m_ffn_w_gate, m_ffn_w_up, m_ffn_w_down, m_w_in, m_lru_conv_w, m_lru_conv_b, m_lru_w_a, m_lru_b_a, m_lru_w_x, m_lru_b_x, m_lru_lambda, m_ssd_conv_w, m_ssd_conv_b, m_ssd_dt_bias, m_ssd_a_log, m_ssd_d, m_ssd_norm_w, m_dn_conv_w, m_dn_a_log, m_dn_dt_bias, m_dn_norm_w, m_w_out, v_meta, v_ln_g, v_ln_b, v_ffn_w_gate, v_ffn_w_up, v_ffn_w_down, v_w_in, v_lru_conv_w, v_lru_conv_b, v_lru_w_a, v_lru_b_a, v_lru_w_x, v_lru_b_x, v_lru_lambda, v_ssd_conv_w, v_ssd_conv_b, v_ssd_dt_bias, v_ssd_a_log, v_ssd_d, v_ssd_norm_w, v_dn_conv_w, v_dn_a_log, v_dn_dt_bias, v_dn_norm_w, v_w_out):
    w = dict(meta=meta, ln_g=ln_g, ln_b=ln_b, ffn_w_gate=ffn_w_gate, ffn_w_up=ffn_w_up, ffn_w_down=ffn_w_down, w_in=w_in,
             lru_conv_w=lru_conv_w, lru_conv_b=lru_conv_b, lru_w_a=lru_w_a, lru_b_a=lru_b_a, lru_w_x=lru_w_x,
             lru_b_x=lru_b_x, lru_lambda=lru_lambda, ssd_conv_w=ssd_conv_w, ssd_conv_b=ssd_conv_b, ssd_dt_bias=ssd_dt_bias,
             ssd_a_log=ssd_a_log, ssd_d=ssd_d, ssd_norm_w=ssd_norm_w, dn_conv_w=dn_conv_w, dn_a_log=dn_a_log,
             dn_dt_bias=dn_dt_bias, dn_norm_w=dn_norm_w, w_out=w_out)
    m = dict(meta=m_meta, ln_g=m_ln_g, ln_b=m_ln_b, ffn_w_gate=m_ffn_w_gate, ffn_w_up=m_ffn_w_up, ffn_w_down=m_ffn_w_down,
             w_in=m_w_in, lru_conv_w=m_lru_conv_w, lru_conv_b=m_lru_conv_b, lru_w_a=m_lru_w_a, lru_b_a=m_lru_b_a,
             lru_w_x=m_lru_w_x, lru_b_x=m_lru_b_x, lru_lambda=m_lru_lambda, ssd_conv_w=m_ssd_conv_w, ssd_conv_b=m_ssd_conv_b,
             ssd_dt_bias=m_ssd_dt_bias, ssd_a_log=m_ssd_a_log, ssd_d=m_ssd_d, ssd_norm_w=m_ssd_norm_w, dn_conv_w=m_dn_conv_w,
             dn_a_log=m_dn_a_log, dn_dt_bias=m_dn_dt_bias, dn_norm_w=m_dn_norm_w, w_out=m_w_out)
    v = dict(meta=v_meta, ln_g=v_ln_g, ln_b=v_ln_b, ffn_w_gate=v_ffn_w_gate, ffn_w_up=v_ffn_w_up, ffn_w_down=v_ffn_w_down,
             w_in=v_w_in, lru_conv_w=v_lru_conv_w, lru_conv_b=v_lru_conv_b, lru_w_a=v_lru_w_a, lru_b_a=v_lru_b_a,
             lru_w_x=v_lru_w_x, lru_b_x=v_lru_b_x, lru_lambda=v_lru_lambda, ssd_conv_w=v_ssd_conv_w, ssd_conv_b=v_ssd_conv_b,
             ssd_dt_bias=v_ssd_dt_bias, ssd_a_log=v_ssd_a_log, ssd_d=v_ssd_d, ssd_norm_w=v_ssd_norm_w, dn_conv_w=v_dn_conv_w,
             dn_a_log=v_dn_a_log, dn_dt_bias=v_dn_dt_bias, dn_norm_w=v_dn_norm_w, w_out=v_w_out)
    depth = ln_g.shape[0]
    me = _flat(_mesh_pos())
    small_names = tuple(SMALL_SHARD_AXIS)
    sharded = tuple(k for k in small_names if SMALL_SHARD_AXIS[k] is not None)

    del _QUEUE[:]
    _STANDALONE[0] = 0
    shards = [_layer_shards(w, l) for l in range(depth)]
    small_unit = _enqueue(_pack([w[k] for k in sharded]), False)
    first = _enqueue_layer(shards[0])
    g_small = _collect(small_unit)
    small = {k: w[k] for k in small_names if SMALL_SHARD_AXIS[k] is None}
    for k, piece in zip(sharded, _unpack(g_small, [w[k].shape for k in sharded], lead=(N_DEV,))):
        ax = SMALL_SHARD_AXIS[k]
        full = jnp.moveaxis(piece, 0, ax)
        small[k] = full.reshape(full.shape[:ax] + (N_DEV * w[k].shape[ax],) + full.shape[ax + 2:])

    loss, dx, small_grads, slabs = _local_step(x[0], loss_target[0], small, first, shards)

    r_small = _collect(_enqueue(_pack([small_grads[k] for k in small_names]), False))
    got = [{k: _collect_halves(g) for k, g in layer.items()} for layer in slabs]

    outs = {}
    ffn = lambda name: jnp.stack([jnp.stack([got[l][f"{name}{j}"] for j in range(2)], axis=1) for l in range(depth)], axis=1)
    one = lambda name: jnp.stack([got[l][name] for l in range(depth)], axis=1)
    parts_of = {"ffn_w_gate": ffn("gate"), "ffn_w_up": ffn("up"), "ffn_w_down": ffn("down"), "w_in": one("win"),
                "w_out": one("wout")}
    for k in BIG:
        shp = w[k].shape
        two = lambda a: a.reshape(-1, shp[-1])
        res = _adam(two(w[k]), parts_of[k].reshape(N_DEV, -1, shp[-1]), two(m[k]), two(v[k]), "adam_" + k)
        outs[k] = [a.reshape(shp) for a in res]
    g_full = _unpack(_sum_parts(r_small, "sum_small_grads"), [small[k].shape for k in small_names])
    g_loc = {}
    for k, gf in zip(small_names, g_full):
        ax = SMALL_SHARD_AXIS[k]
        g_loc[k] = gf if ax is None else lax.dynamic_slice_in_dim(gf, me * w[k].shape[ax], w[k].shape[ax], axis=ax)
    res = _adam(_pack([w[k] for k in small_names]), _pack([g_loc[k] for k in small_names]),
                _pack([m[k] for k in small_names]), _pack([v[k] for k in small_names]), "adam_small")
    shapes = [w[k].shape for k in small_names]
    for i, k in enumerate(small_names):
        outs[k] = [_unpack(r, shapes)[i] for r in res]

    loss = lax.psum(loss, ("x", "y", "c"))
    return (loss, dx[None], *[outs[k][0] for k in WEIGHT_ORDER], *[outs[k][1] for k in WEIGHT_ORDER],
            *[outs[k][2] for k in WEIGHT_ORDER], *[outs[k][3] for k in WEIGHT_ORDER])
```

```python
import functools
import math

import jax
import jax.numpy as jnp
from jax import lax
from jax.experimental import pallas as pl
from jax.experimental.pallas import tpu as pltpu

F32 = jnp.float32
BF16 = jnp.bfloat16
HI = lax.Precision.HIGHEST
MESH_IDS = pl.DeviceIdType.MESH

N_DEV = 8
D_MODEL = 1024
DEPTH = 4
N_META = 16
CHUNK = 64
CONV_K = 4
D_FF = 2816
LRU_WIDTH = 256
LRU_HEADS = 4
LRU_C = 8.0
SSD_HEADS = 8
SSD_GROUPS = 2
SSD_HPG = 4
SSD_INNER = 512
DN_HEADS = 4
HEAD = 64
CONV_W = 1792
PROJ_W = 2944
OFF_LRU_Y, OFF_SSD_Z, OFF_DN_GATE, OFF_SMALL = 1792, 2048, 2560, 2816
D_IN = 2832
ALPHA = (2 * DEPTH) ** 0.25
FFN_RES = 0.5
LN_EPS = 1e-5
RMS_EPS = 1e-6
ADAM_LR, ADAM_B1, ADAM_B2, ADAM_EPS, ADAM_WD, ADAM_STEP = 0.001, 0.9, 0.999, 1e-08, 0.01, 10

VMEM_LIMIT = 56 * 1024 * 1024
ROW_TILE_CAP = 832
LANE = 128


def _tile(n, cap, mult=16):
    best = None
    for t in range(mult, min(n, cap) + 1, mult):
        if n % t == 0:
            best = t
    assert best is not None, (n, cap, mult)
    return best


def _params(sem=None):
    return pltpu.CompilerParams(dimension_semantics=sem, vmem_limit_bytes=VMEM_LIMIT)


def _resident(shape, index_map):
    return pl.BlockSpec(shape, index_map, pipeline_mode=pl.Buffered(1))


def _dot(a, b):
    return jnp.dot(a, b, preferred_element_type=F32)


def _dot_nt(a, b):
    return lax.dot_general(a, b, (((1,), (1,)), ((), ())), preferred_element_type=F32)


def _dot_tn(a, b):
    return lax.dot_general(a, b, (((0,), (0,)), ((), ())), preferred_element_type=F32)


def _sds(shape, dtype):
    return jax.ShapeDtypeStruct(shape, dtype)


US_PER_MB = 92.5


class _Unit:
    def __init__(self, arr, slab):
        self.arr, self.slab, self.out = arr, slab, None
        per_peer = arr.size * arr.dtype.itemsize / (N_DEV if slab else 1)
        self.us = per_peer / 1e6 * US_PER_MB * (1.0 if slab else 0.5)


_QUEUE = []
_STANDALONE = [0]


def _enqueue(arr, slab):
    unit = _Unit(arr, slab)
    _QUEUE.append(unit)
    return unit


def _enqueue_halves(arr, slab, axis):
    half = arr.shape[axis] // 2
    parts = (lax.slice_in_dim(arr, 0, half, axis=axis), lax.slice_in_dim(arr, half, arr.shape[axis], axis=axis))
    return [_enqueue(p, slab) for p in parts], axis + (0 if slab else 1)


def _collect_halves(group):
    units, axis = group
    return jnp.concatenate([_collect(u) for u in units], axis=axis)


def _take_units(host_us):
    units = []
    while _QUEUE and host_us >= 0.5 * _QUEUE[0].us:
        host_us -= _QUEUE[0].us
        units.append(_QUEUE.pop(0))
    return units


def _collect(unit):
    if unit.out is None:
        n = _QUEUE.index(unit) + 1
        units = [_QUEUE.pop(0) for _ in range(n)]
        _STANDALONE[0] += 1
        res = _exchange([u.arr for u in units], [u.slab for u in units], f"exchange_{_STANDALONE[0]}")
        for u, r in zip(units, res):
            u.out = r
    return unit.out


def _call(body, host_us, *, name, out_shape, in_specs, out_specs, grid=(), scratch_shapes=(), compiler_params=None):
    units = _take_units(host_us)
    kw = dict(name=name, grid=grid, compiler_params=compiler_params)
    if not units:
        return pl.pallas_call(body, out_shape=out_shape, in_specs=in_specs, out_specs=out_specs,
                              scratch_shapes=list(scratch_shapes), **kw)
    single = not isinstance(out_shape, (tuple, list))
    outs = (out_shape,) if single else tuple(out_shape)
    ospecs = [out_specs] if single else list(out_specs)
    nin, nout, nscr, ncm = len(in_specs), len(outs), len(scratch_shapes), len(units)
    slabs = [u.slab for u in units]

    def hosted(*refs):
        ins, c_in = refs[:nin], refs[nin:nin + ncm]
        o, c_out = refs[nin + ncm:nin + ncm + nout], refs[nin + ncm + nout:nin + 2 * ncm + nout]
        scr, sems = refs[nin + 2 * ncm + nout:nin + 2 * ncm + nout + nscr], refs[nin + 2 * ncm + nout + nscr:]
        ids = [pl.program_id(d) for d in range(len(grid))]
        first = functools.reduce(jnp.logical_and, [i == 0 for i in ids])
        last = functools.reduce(jnp.logical_and, [i == g - 1 for i, g in zip(ids, grid)])

        @pl.when(first)
        def _():
            _exchange_start(c_in, c_out, slabs, *sems)
        body(*ins, *o, *scr)

        @pl.when(last)
        def _():
            _exchange_wait(c_in, c_out, slabs, *sems)

    hbm = pl.BlockSpec(memory_space=pl.ANY)
    fn = pl.pallas_call(
        hosted, out_shape=outs + _exchange_shapes([u.arr for u in units], slabs), in_specs=list(in_specs) + [hbm] * ncm,
        out_specs=ospecs + [hbm] * ncm, scratch_shapes=list(scratch_shapes) + _exchange_sems(ncm), **kw)

    def run(*args):
        res = fn(*args, *[u.arr for u in units])
        for u, r in zip(units, res[nout:]):
            u.out = r
        return res[0] if single else tuple(res[:nout])

    return run


def _ffn_up(xb, wgu, name):
    tp, d = xb.shape
    f = wgu.shape[1] // 2
    tm = _tile(tp, ROW_TILE_CAP)
    tn = _tile(f, 1408, LANE)
    nj = f // tn

    def body(x_ref, wg_ref, wu_ref, g_ref, u_ref, a_ref):
        x = x_ref[...]
        g = _dot(x, wg_ref[...])
        u = _dot(x, wu_ref[...])
        g_ref[...] = g.astype(BF16)
        u_ref[...] = u.astype(BF16)
        a_ref[...] = (g * jax.nn.sigmoid(g) * u).astype(BF16)

    out = _sds((tp, f), BF16)
    return _call(
        body, 60, name=name, out_shape=(out, out, out), grid=(nj, tp // tm),
        in_specs=[pl.BlockSpec((tm, d), lambda j, i: (i, 0)),
                  pl.BlockSpec((d, tn), lambda j, i: (0, j)),
                  pl.BlockSpec((d, tn), lambda j, i: (0, j + nj))],
        out_specs=[pl.BlockSpec((tm, tn), lambda j, i: (i, j))] * 3,
        compiler_params=_params(("arbitrary", "arbitrary")),
    )(xb, wgu, wgu)


def _mm_resid_ln(a, w, h, gamma, beta, scale, name):
    tp, k = a.shape
    d = w.shape[1]
    tm = _tile(tp, ROW_TILE_CAP)

    def body(a_ref, w_ref, h_ref, g_ref, b_ref, y_ref, o_ref, ob_ref):
        y = ALPHA * h_ref[...] + scale * _dot(a_ref[...], w_ref[...])
        mu = jnp.mean(y, axis=-1, keepdims=True)
        yc = y - mu
        var = jnp.mean(yc * yc, axis=-1, keepdims=True)
        o = yc * lax.rsqrt(var + LN_EPS) * g_ref[...] + b_ref[...]
        y_ref[...] = y
        o_ref[...] = o
        ob_ref[...] = o.astype(BF16)

    row = lambda i: (i, 0)
    fix = lambda i: (0, 0)
    return _call(
        body, 30, name=name, out_shape=(_sds((tp, d), F32), _sds((tp, d), F32), _sds((tp, d), BF16)),
        grid=(tp // tm,),
        in_specs=[pl.BlockSpec((tm, k), row), _resident((k, d), fix), pl.BlockSpec((tm, d), row),
                  pl.BlockSpec((1, d), fix), pl.BlockSpec((1, d), fix)],
        out_specs=[pl.BlockSpec((tm, d), row)] * 3,
        compiler_params=_params(("arbitrary",)),
    )(a, w, h, gamma, beta)


def _mm_nn(xb, w, name):
    tp, k = xb.shape
    n = w.shape[1]
    tm = _tile(tp, ROW_TILE_CAP)

    def body(x_ref, w_ref, o_ref):
        o_ref[...] = _dot(x_ref[...], w_ref[...])

    return _call(
        body, 30, name=name, out_shape=_sds((tp, n), F32), grid=(tp // tm,),
        in_specs=[pl.BlockSpec((tm, k), lambda i: (i, 0)), _resident((k, n), lambda i: (0, 0))],
        out_specs=pl.BlockSpec((tm, n), lambda i: (i, 0)),
        compiler_params=_params(("arbitrary",)),
    )(xb, w)


def _mm_nt_add(a, w, resid, name):
    tp, k = a.shape
    n = w.shape[0]
    tm = _tile(tp, ROW_TILE_CAP)
    has_resid = resid is not None

    def body(*refs):
        if has_resid:
            a_ref, w_ref, r_ref, o_ref = refs
            o_ref[...] = r_ref[...] + _dot_nt(a_ref[...], w_ref[...])
        else:
            a_ref, w_ref, o_ref = refs
            o_ref[...] = _dot_nt(a_ref[...], w_ref[...])

    in_specs = [pl.BlockSpec((tm, k), lambda i: (i, 0)), _resident((n, k), lambda i: (0, 0))]
    args = [a, w]
    if has_resid:
        in_specs.append(pl.BlockSpec((tm, n), lambda i: (i, 0)))
        args.append(resid)
    return _call(
        body, 25, name=name, out_shape=_sds((tp, n), F32), grid=(tp // tm,),
        in_specs=in_specs, out_specs=pl.BlockSpec((tm, n), lambda i: (i, 0)),
        compiler_params=_params(("arbitrary",)),
    )(*args)


def _ffn_dx(dg, du, wgu, resid, name):
    tp, f = dg.shape
    d = wgu.shape[0]
    tm = _tile(tp, ROW_TILE_CAP)

    def body(dg_ref, du_ref, w_ref, r_ref, o_ref):
        acc = r_ref[...] + _dot_nt(dg_ref[...], w_ref[:, 0:f])
        o_ref[...] = acc + _dot_nt(du_ref[...], w_ref[:, f:2 * f])

    row = lambda i: (i, 0)
    return _call(
        body, 55, name=name, out_shape=_sds((tp, d), F32), grid=(tp // tm,),
        in_specs=[pl.BlockSpec((tm, f), row), pl.BlockSpec((tm, f), row), _resident((d, 2 * f), lambda i: (0, 0)),
                  pl.BlockSpec((tm, d), row)],
        out_specs=pl.BlockSpec((tm, d), row),
        compiler_params=_params(("arbitrary",)),
    )(dg, du, wgu, resid)


def _mm_tn(a, b, name):
    tp, ka = a.shape
    nb = b.shape[1]
    tt = _tile(tp, ROW_TILE_CAP)
    tk = _tile(ka, 1408, LANE)
    tn = _tile(nb, 1536, LANE)
    if tn < 512:
        tn = nb
    nt = tp // tt

    def body(a_ref, b_ref, o_ref):
        @pl.when(pl.program_id(2) == 0)
        def _():
            o_ref[...] = jnp.zeros_like(o_ref)
        o_ref[...] += _dot_tn(a_ref[...], b_ref[...])

    return _call(
        body, 40, name=name, out_shape=_sds((ka, nb), F32), grid=(ka // tk, nb // tn, nt),
        in_specs=[pl.BlockSpec((tt, tk), lambda i, j, t: (t, i)), pl.BlockSpec((tt, tn), lambda i, j, t: (t, j))],
        out_specs=pl.BlockSpec((tk, tn), lambda i, j, t: (i, j)),
        compiler_params=_params(("arbitrary", "arbitrary", "arbitrary")),
    )(a, b)


def _ln_bwd(y, dout, gamma, scale, name):
    tp, d = y.shape
    tm = _tile(tp, ROW_TILE_CAP)

    def body(y_ref, do_ref, g_ref, dres_ref, dyb_ref, dg_ref, db_ref):
        @pl.when(pl.program_id(0) == 0)
        def _():
            dg_ref[...] = jnp.zeros_like(dg_ref)
            db_ref[...] = jnp.zeros_like(db_ref)
        yv = y_ref[...]
        do = do_ref[...]
        mu = jnp.mean(yv, axis=-1, keepdims=True)
        yc = yv - mu
        var = jnp.mean(yc * yc, axis=-1, keepdims=True)
        rstd = lax.rsqrt(var + LN_EPS)
        xhat = yc * rstd
        dxh = do * g_ref[...]
        m1 = jnp.mean(dxh, axis=-1, keepdims=True)
        m2 = jnp.mean(dxh * xhat, axis=-1, keepdims=True)
        dy = rstd * (dxh - m1 - xhat * m2)
        dres_ref[...] = ALPHA * dy
        dyb_ref[...] = (scale * dy).astype(BF16)
        dg_ref[...] += jnp.sum(do * xhat, axis=0, keepdims=True)
        db_ref[...] += jnp.sum(do, axis=0, keepdims=True)

    row = lambda i: (i, 0)
    fix = lambda i: (0, 0)
    return _call(
        body, 22, name=name,
        out_shape=(_sds((tp, d), F32), _sds((tp, d), BF16), _sds((1, d), F32), _sds((1, d), F32)),
        grid=(tp // tm,),
        in_specs=[pl.BlockSpec((tm, d), row), pl.BlockSpec((tm, d), row), pl.BlockSpec((1, d), fix)],
        out_specs=[pl.BlockSpec((tm, d), row), pl.BlockSpec((tm, d), row), pl.BlockSpec((1, d), fix),
                   pl.BlockSpec((1, d), fix)],
        compiler_params=_params(("arbitrary",)),
    )(y, dout, gamma)


def _ffn_dact(dyb, wd, g, u, name):
    tp, d = dyb.shape
    f = wd.shape[0]
    tm = _tile(tp, ROW_TILE_CAP)
    tn = _tile(f, 1408, LANE)

    def body(dy_ref, w_ref, g_ref, u_ref, dg_ref, du_ref):
        dact = _dot_nt(dy_ref[...], w_ref[...])
        gv = g_ref[...].astype(F32)
        uv = u_ref[...].astype(F32)
        sg = jax.nn.sigmoid(gv)
        dg_ref[...] = (dact * uv * (sg * (1.0 + gv * (1.0 - sg)))).astype(BF16)
        du_ref[...] = (dact * (gv * sg)).astype(BF16)

    out = _sds((tp, f), BF16)
    blk = pl.BlockSpec((tm, tn), lambda j, i: (i, j))
    return _call(
        body, 55, name=name, out_shape=(out, out), grid=(f // tn, tp // tm),
        in_specs=[pl.BlockSpec((tm, d), lambda j, i: (i, 0)), pl.BlockSpec((tn, d), lambda j, i: (j, 0)), blk, blk],
        out_specs=[blk, blk],
        compiler_params=_params(("arbitrary", "arbitrary")),
    )(dyb, wd, g, u)


def _loss_grad(o, tgt, t_real, name):
    tp, d = o.shape
    tm = _tile(tp, ROW_TILE_CAP)

    def body(o_ref, t_ref, l_ref, d_ref):
        i = pl.program_id(0)

        @pl.when(i == 0)
        def _():
            l_ref[...] = jnp.zeros_like(l_ref)
        rows = i * tm + lax.broadcasted_iota(jnp.int32, (tm, 1), 0)
        real = jnp.logical_and(rows >= N_META, rows < t_real)
        err = jnp.where(real, o_ref[...] - t_ref[...], 0.0)
        d_ref[...] = err * (1.0 / d)
        l_ref[...] += jnp.sum(err * err, axis=0, keepdims=True) * (0.5 / d)

    row = lambda i: (i, 0)
    return pl.pallas_call(
        body, name=name, out_shape=(_sds((1, d), F32), _sds((tp, d), F32)), grid=(tp // tm,),
        in_specs=[pl.BlockSpec((tm, d), row), pl.BlockSpec((tm, d), row)],
        out_specs=[pl.BlockSpec((1, d), lambda i: (0, 0)), pl.BlockSpec((tm, d), row)],
        compiler_params=_params(("arbitrary",)),
    )(o, tgt)


CONV_TC = 256


def _silu_grad(y):
    s = jax.nn.sigmoid(y)
    return s * (1.0 + y * (1.0 - s))


def _conv_taps(x_ref, w, r0, rb):
    cur = x_ref[r0:r0 + rb, :]
    prev = x_ref[r0 - 8:r0, :] if r0 > 0 else jnp.zeros((8, cur.shape[1]), F32)
    xcat = jnp.concatenate([prev, cur], axis=0)
    taps = [xcat[5 + j:5 + j + rb] for j in range(CONV_K - 1)] + [cur]
    y = w[0:1] * taps[0]
    for j in range(1, CONV_K):
        y = y + w[j:j + 1] * taps[j]
    return y, taps


def _conv_fwd(proj, w, b, name):
    tp = proj.shape[0]
    rb = _tile(tp, ROW_TILE_CAP, 8)

    def body(x_ref, w_ref, b_ref, o_ref):
        gated = pl.program_id(0) > 0
        wv = w_ref[...]
        bv = b_ref[...]
        for r0 in range(0, tp, rb):
            y, _ = _conv_taps(x_ref, wv, r0, rb)
            y = y + bv
            o_ref[r0:r0 + rb, :] = jnp.where(gated, y * jax.nn.sigmoid(y), y)

    col = lambda j: (0, j)
    return _call(
        body, 25, name=name, out_shape=_sds((tp, CONV_W), F32), grid=(CONV_W // CONV_TC,),
        in_specs=[pl.BlockSpec((tp, CONV_TC), col), pl.BlockSpec((CONV_K, CONV_TC), col), pl.BlockSpec((1, CONV_TC), col)],
        out_specs=pl.BlockSpec((tp, CONV_TC), col),
        compiler_params=_params(("arbitrary",)),
    )(proj, w, b)


def _conv_bwd(proj, w, b, dxc, name):
    tp = proj.shape[0]
    rb = _tile(tp, ROW_TILE_CAP, 8)

    def body(x_ref, w_ref, b_ref, d_ref, dx_ref, dw_ref, db_ref, dy_scr):
        gated = pl.program_id(0) > 0
        wv = w_ref[...]
        bv = b_ref[...]
        dw = [jnp.zeros((1, CONV_TC), F32) for _ in range(CONV_K)]
        db = jnp.zeros((1, CONV_TC), F32)
        for r0 in range(0, tp, rb):
            y, taps = _conv_taps(x_ref, wv, r0, rb)
            y = y + bv
            d = d_ref[r0:r0 + rb, :]
            dy = jnp.where(gated, d * _silu_grad(y), d)
            dy_scr[r0:r0 + rb, :] = dy
            for j in range(CONV_K):
                dw[j] = dw[j] + jnp.sum(dy * taps[j], axis=0, keepdims=True)
            db = db + jnp.sum(dy, axis=0, keepdims=True)
        for j in range(CONV_K):
            dw_ref[j:j + 1, :] = dw[j]
        db_ref[...] = db
        for r0 in range(0, tp, rb):
            cur = dy_scr[r0:r0 + rb, :]
            nxt = dy_scr[r0 + rb:r0 + rb + 8, :] if r0 + rb < tp else jnp.zeros((8, CONV_TC), F32)
            dcat = jnp.concatenate([cur, nxt], axis=0)
            dx = wv[3:4] * cur
            for s in range(1, CONV_K):
                dx = dx + wv[3 - s:4 - s] * dcat[s:s + rb]
            dx_ref[r0:r0 + rb, :] = dx

    col = lambda j: (0, j)
    return _call(
        body, 70, name=name,
        out_shape=(_sds((tp, CONV_W), F32), _sds((CONV_K, CONV_W), F32), _sds((1, CONV_W), F32)),
        grid=(CONV_W // CONV_TC,),
        in_specs=[pl.BlockSpec((tp, CONV_TC), col), pl.BlockSpec((CONV_K, CONV_TC), col), pl.BlockSpec((1, CONV_TC), col),
                  pl.BlockSpec((tp, CONV_TC), col)],
        out_specs=[pl.BlockSpec((tp, CONV_TC), col), pl.BlockSpec((CONV_K, CONV_TC), col), pl.BlockSpec((1, CONV_TC), col)],
        scratch_shapes=[pltpu.VMEM((tp, CONV_TC), F32)],
        compiler_params=_params(("arbitrary",)),
    )(proj, w, b, dxc)


def _neg_expm1(x):
    series = -x * (1.0 + x * (0.5 + x * (1.0 / 6.0 + x * (1.0 / 24.0))))
    return jnp.where(jnp.abs(x) < 0.03, series, 1.0 - jnp.exp(x))


def _lru_gates(u, wa, ba, wx, bx, lam):
    r = jax.nn.sigmoid(jnp.dot(u, wa, precision=HI, preferred_element_type=F32) + ba)
    i = jax.nn.sigmoid(jnp.dot(u, wx, precision=HI, preferred_element_type=F32) + bx)
    log_a = -LRU_C * r * jax.nn.softplus(-lam)
    a = jnp.exp(log_a)
    b = jnp.sqrt(_neg_expm1(2.0 * log_a)) * (i * u)
    return a, b


def _lru_specs(tm, flip, n_tiles):
    idx = (lambda i: (n_tiles - 1 - i, 0)) if flip else (lambda i: (i, 0))
    return idx, lambda i: (0, 0)


def _lru_fwd(xc, proj, wa, ba, wx, bx, lam, name):
    tp = xc.shape[0]
    w = LRU_WIDTH
    tm = _tile(tp, ROW_TILE_CAP, 8)
    ycol = OFF_LRU_Y // w

    def body(u_ref, y_ref, wa_ref, ba_ref, wx_ref, bx_ref, lam_ref, o_ref, h_ref, a_scr, b_scr, carry):
        @pl.when(pl.program_id(0) == 0)
        def _():
            carry[...] = jnp.zeros_like(carry)
        a, b = _lru_gates(u_ref[...], wa_ref[...], ba_ref[...], wx_ref[...], bx_ref[...], lam_ref[...])
        a_scr[...] = a
        b_scr[...] = b

        def step(t, h):
            h = a_scr[pl.ds(t, 1), :] * h + b_scr[pl.ds(t, 1), :]
            h_ref[pl.ds(t, 1), :] = h
            return h

        carry[0:1, :] = lax.fori_loop(0, tm, step, carry[0:1, :])
        o_ref[...] = (h_ref[...] * jax.nn.gelu(y_ref[...])).astype(BF16)

    fix = lambda i: (0, 0)
    return _call(
        body, 40, name=name, out_shape=(_sds((tp, w), BF16), _sds((tp, w), F32)), grid=(tp // tm,),
        in_specs=[pl.BlockSpec((tm, w), lambda i: (i, 0)), pl.BlockSpec((tm, w), lambda i: (i, ycol)),
                  pl.BlockSpec((w, w), fix), pl.BlockSpec((1, w), fix), pl.BlockSpec((w, w), fix),
                  pl.BlockSpec((1, w), fix), pl.BlockSpec((1, w), fix)],
        out_specs=[pl.BlockSpec((tm, w), lambda i: (i, 0))] * 2,
        scratch_shapes=[pltpu.VMEM((tm, w), F32), pltpu.VMEM((tm, w), F32), pltpu.VMEM((8, w), F32)],
        compiler_params=_params(("arbitrary",)),
    )(xc, proj, wa, ba, wx, bx, lam)


def _lru_bwd(xc, proj, h, dout, wa, ba, wx, bx, lam, name):
    tp = xc.shape[0]
    w = LRU_WIDTH
    tm = _tile(tp, ROW_TILE_CAP, 8)
    nt = tp // tm
    ycol = OFF_LRU_Y // w
    rev = lambda i: (nt - 1 - i, 0)
    prev8 = lambda i: (jnp.maximum((nt - 1 - i) * (tm // 8) - 1, 0), 0)

    def body(u_ref, y_ref, h_ref, hp_ref, do_ref, wa_ref, ba_ref, wx_ref, bx_ref, lam_ref,
             du_ref, dy_ref, dwa_ref, dba_ref, dwx_ref, dbx_ref, dlam_ref,
             a_scr, dh_scr, g_scr, da_scr, hext, carry):
        i = pl.program_id(0)

        @pl.when(i == 0)
        def _():
            carry[...] = jnp.zeros_like(carry)
            for r in (dwa_ref, dba_ref, dwx_ref, dbx_ref, dlam_ref):
                r[...] = jnp.zeros_like(r)
        params = (wa_ref[...], ba_ref[...], wx_ref[...], bx_ref[...], lam_ref[...])
        (a, _), gates_vjp = jax.vjp(_lru_gates, u_ref[...], *params)
        gel, gelu_vjp = jax.vjp(jax.nn.gelu, y_ref[...])
        do = do_ref[...]
        hv = h_ref[...]
        dy_ref[...] = gelu_vjp(do * hv)[0]
        a_scr[...] = a
        dh_scr[...] = do * gel
        hext[0:8, :] = jnp.where(i == nt - 1, 0.0, hp_ref[...])
        hext[8:8 + tm, :] = hv

        def step(s, c):
            t = tm - 1 - s
            g = dh_scr[pl.ds(t, 1), :] + c
            g_scr[pl.ds(t, 1), :] = g
            da_scr[pl.ds(t, 1), :] = g * hext[pl.ds(t + 7, 1), :]
            return a_scr[pl.ds(t, 1), :] * g

        carry[0:1, :] = lax.fori_loop(0, tm, step, carry[0:1, :])
        du, dwa, dba, dwx, dbx, dlam = gates_vjp((da_scr[...], g_scr[...]))
        du_ref[...] = du
        dwa_ref[...] += dwa
        dba_ref[...] += dba
        dwx_ref[...] += dwx
        dbx_ref[...] += dbx
        dlam_ref[...] += dlam

    fix = lambda i: (0, 0)
    tile = pl.BlockSpec((tm, w), rev)
    mat = pl.BlockSpec((w, w), fix)
    vec = pl.BlockSpec((1, w), fix)
    return _call(
        body, 90, name=name,
        out_shape=(_sds((tp, w), F32), _sds((tp, w), F32), _sds((w, w), F32), _sds((1, w), F32), _sds((w, w), F32),
                   _sds((1, w), F32), _sds((1, w), F32)),
        grid=(nt,),
        in_specs=[tile, pl.BlockSpec((tm, w), lambda i: (nt - 1 - i, ycol)), tile, pl.BlockSpec((8, w), prev8), tile,
                  mat, vec, mat, vec, vec],
        out_specs=[tile, tile, mat, vec, mat, vec, vec],
        scratch_shapes=[pltpu.VMEM((tm, w), F32)] * 4 + [pltpu.VMEM((tm + 8, w), F32), pltpu.VMEM((8, w), F32)],
        compiler_params=_params(("arbitrary",)),
    )(xc, proj, h, h, dout, wa, ba, wx, bx, lam)


def _dot3(a, b, dims):
    ah = a.astype(BF16)
    al = (a - ah.astype(F32)).astype(BF16)
    bh = b.astype(BF16)
    bl = (b - bh.astype(F32)).astype(BF16)
    dot = lambda x, y: lax.dot_general(x, y, (dims, ((0,), (0,))), preferred_element_type=F32)
    return dot(ah, bh) + (dot(ah, bl) + dot(al, bh))


@jax.custom_vjp
def _bmm(a, b):
    return _dot3(a, b, ((2,), (1,)))


@jax.custom_vjp
def _bmm_nt(a, b):
    return _dot3(a, b, ((2,), (2,)))


@jax.custom_vjp
def _bmm_tn(a, b):
    return _dot3(a, b, ((1,), (1,)))


_bmm.defvjp(lambda a, b: (_bmm(a, b), (a, b)), lambda r, g: (_bmm_nt(g, r[1]), _bmm_tn(r[0], g)))
_bmm_nt.defvjp(lambda a, b: (_bmm_nt(a, b), (a, b)), lambda r, g: (_bmm(g, r[1]), _bmm_tn(g, r[0])))
_bmm_tn.defvjp(lambda a, b: (_bmm_tn(a, b), (a, b)), lambda r, g: (_bmm_nt(r[1], g), _bmm(r[0], g)))


def _dot1(a, b, dims):
    return lax.dot_general(a.astype(BF16), b.astype(BF16), (dims, ((0,), (0,))), preferred_element_type=F32)


@jax.custom_vjp
def _bmm1(a, b):
    return _dot1(a, b, ((2,), (1,)))


@jax.custom_vjp
def _bmm1_nt(a, b):
    return _dot1(a, b, ((2,), (2,)))


@jax.custom_vjp
def _bmm1_tn(a, b):
    return _dot1(a, b, ((1,), (1,)))


_bmm1.defvjp(lambda a, b: (_bmm1(a, b), (a, b)), lambda r, g: (_bmm1_nt(g, r[1]), _bmm1_tn(r[0], g)))
_bmm1_nt.defvjp(lambda a, b: (_bmm1_nt(a, b), (a, b)), lambda r, g: (_bmm1(g, r[1]), _bmm1_tn(g, r[0])))
_bmm1_tn.defvjp(lambda a, b: (_bmm1_tn(a, b), (a, b)), lambda r, g: (_bmm1_nt(r[1], g), _bmm1(r[0], g)))


def _chunk_masks(nh):
    r = lax.broadcasted_iota(jnp.int32, (CHUNK, CHUNK), 0)
    c = lax.broadcasted_iota(jnp.int32, (CHUNK, CHUNK), 1)
    full = lambda m: jnp.broadcast_to(m[None], (nh, CHUNK, CHUNK))
    return r, c, full


def _decay_terms(g, nh):
    r, c, full = _chunk_masks(nh)
    incl = r >= c
    cs = _bmm(full(incl.astype(F32)), g)
    cs_t = jnp.swapaxes(cs, 1, 2)
    tot = jnp.broadcast_to(jnp.sum(g, axis=1, keepdims=True), g.shape)
    m = full(incl)
    decay = jnp.where(m, jnp.exp(jnp.where(m, cs - cs_t, 0.0)), 0.0)
    return cs, decay, tot


def _rep_groups(x):
    return jnp.concatenate([jnp.broadcast_to(x[g:g + 1], (SSD_HPG,) + x.shape[1:]) for g in range(SSD_GROUPS)], axis=0)


def _ssd_chunk(xs, bm, cm, dtr, z, p_dtb, p_alog, p_d, p_nw, state):
    nh = SSD_HEADS
    dt = jax.nn.softplus(dtr + p_dtb)
    a = dt * (-jnp.exp(p_alog))
    x = xs * dt
    cs, decay, tot = _decay_terms(a, nh)
    b8 = _rep_groups(bm)
    c8 = _rep_groups(cm)
    y = _bmm1(_rep_groups(_bmm1_nt(cm, bm)) * decay, x)
    y = y + _bmm1_nt(c8, state) * jnp.exp(cs)
    new_state = state * jnp.exp(tot) + _bmm1_tn(x * jnp.exp(tot - cs), b8)
    y = y + p_d * xs
    y = y * (z * jax.nn.sigmoid(z))
    ss = jnp.sum(y * y, axis=-1, keepdims=True)
    ssg = jnp.concatenate(
        [jnp.broadcast_to(jnp.sum(ss[g * SSD_HPG:(g + 1) * SSD_HPG], axis=0, keepdims=True), (SSD_HPG, CHUNK, 1))
         for g in range(SSD_GROUPS)], axis=0)
    y = y * lax.rsqrt(ssg * (1.0 / (SSD_HPG * HEAD)) + RMS_EPS) * p_nw
    return y, new_state


def _unit_lower_inverse(m, nh):
    r, c, full = _chunk_masks(nh)
    eye = full((r == c).astype(F32))
    md = jnp.where(full((r // 16) == (c // 16)), m, 0.0)
    mo = m - md
    x = eye - md
    p = _bmm(md, md)
    x = x + _bmm(x, p)
    p = _bmm(p, p)
    x = x + _bmm(x, p)
    p = _bmm(p, p)
    x = x + _bmm(x, p)
    n = _bmm(x, mo)
    y = x - _bmm(n, x)
    return y + _bmm(_bmm(n, n), y)


def _dn_chunk(q, k, v, gate, braw, araw, p_alog, p_dtb, p_nw, state):
    nh = DN_HEADS
    r, c, full = _chunk_masks(nh)
    q = q * lax.rsqrt(jnp.sum(q * q, axis=-1, keepdims=True) + RMS_EPS) * (HEAD ** -0.5)
    k = k * lax.rsqrt(jnp.sum(k * k, axis=-1, keepdims=True) + RMS_EPS)
    beta = jax.nn.sigmoid(braw)
    g = -jnp.exp(p_alog) * jax.nn.softplus(araw + p_dtb)
    gcs, decay, tot = _decay_terms(g, nh)
    kb = k * beta
    vb = v * beta
    m = jnp.where(full(r > c), _bmm1_nt(kb, k) * decay, 0.0)
    t = _unit_lower_inverse(m, nh)
    egcs = jnp.exp(gcs)
    u = _bmm(t, vb)
    w = _bmm(t, kb * egcs)
    attn = _bmm1_nt(q, k) * decay
    v_new = u - _bmm1(w, state)
    out = _bmm1(q * egcs, state) + _bmm1(attn, v_new)
    new_state = state * jnp.exp(tot) + _bmm1_tn(k * jnp.exp(tot - gcs), v_new)
    out = out * lax.rsqrt(jnp.mean(out * out, axis=-1, keepdims=True) + RMS_EPS) * p_nw
    return out * (gate * jax.nn.sigmoid(gate)), new_state


def _chunks_per_step(nc):
    return max(n for n in range(1, 6) if nc % n == 0)


def _heads(t, n):
    return jnp.stack([t[:, HEAD * h:HEAD * (h + 1)] for h in range(n)])


def _unheads(a):
    return jnp.concatenate([a[h] for h in range(a.shape[0])], axis=-1)


def _head_scalars(t, off, n):
    return jnp.stack([jnp.broadcast_to(t[:, off + h:off + h + 1], (CHUNK, HEAD)) for h in range(n)])


def _unhead_scalars(d, off):
    red = jnp.sum(d, axis=-1, keepdims=True)
    lane = lax.broadcasted_iota(jnp.int32, (CHUNK, LANE), 1)
    out = jnp.zeros((CHUNK, LANE), F32)
    for h in range(d.shape[0]):
        out = out + jnp.where(lane == off + h, red[h], 0.0)
    return out


def _chunk_scan_fwd(chunk_fn, srcs, read, params, nh, host_us, name):
    tp = srcs[0][0].shape[0]
    nc = tp // CHUNK
    nb = _chunks_per_step(nc)
    rows = nb * CHUNK
    ns, npar = len(srcs), len(params)

    def body(*refs):
        s_refs, p_refs = refs[:ns], refs[ns:ns + npar]
        y_ref, st_ref, state = refs[ns + npar:]

        @pl.when(pl.program_id(0) == 0)
        def _():
            state[...] = jnp.zeros_like(state)
        par = [r[...] for r in p_refs]
        st = state[...]
        for k in range(nb):
            sl = slice(k * CHUNK, (k + 1) * CHUNK)
            st_ref[k] = st
            y, st = chunk_fn(*read(s_refs, sl), *par, st)
            y_ref[sl, :] = _unheads(y).astype(BF16)
        state[...] = st

    src_spec = lambda s: pl.BlockSpec((rows, s[1]), lambda c: (c, s[2]))
    par_spec = lambda a: pl.BlockSpec(a.shape, lambda c: (0, 0, 0))
    return _call(
        body, host_us, name=name,
        out_shape=(_sds((tp, nh * HEAD), BF16), _sds((nc, nh, HEAD, HEAD), F32)),
        grid=(nc // nb,),
        in_specs=[src_spec(s) for s in srcs] + [par_spec(a) for a in params],
        out_specs=[pl.BlockSpec((rows, nh * HEAD), lambda c: (c, 0)),
                   pl.BlockSpec((nb, nh, HEAD, HEAD), lambda c: (c, 0, 0, 0))],
        scratch_shapes=[pltpu.VMEM((nh, HEAD, HEAD), F32)],
        compiler_params=_params(("arbitrary",)),
    )(*[s[0] for s in srcs], *params)


def _chunk_scan_bwd(chunk_fn, srcs, read, params, states, dsrc, write, out_widths, nh, host_us, name):
    tp = srcs[0][0].shape[0]
    nc = tp // CHUNK
    nb = _chunks_per_step(nc)
    rows = nb * CHUNK
    steps = nc // nb
    ns, npar, nout = len(srcs), len(params), len(out_widths)
    d_arr, d_off = dsrc

    def body(*refs):
        s_refs, p_refs = refs[:ns], refs[ns:ns + npar]
        st_ref, dy_ref = refs[ns + npar:ns + npar + 2]
        o_refs = refs[ns + npar + 2:ns + npar + 2 + nout]
        dp_refs = refs[ns + npar + 2 + nout:ns + 2 * npar + 2 + nout]
        dstate = refs[-1]

        @pl.when(pl.program_id(0) == 0)
        def _():
            dstate[...] = jnp.zeros_like(dstate)
            for r in dp_refs:
                r[...] = jnp.zeros_like(r)
        par = [r[...] for r in p_refs]
        dst = dstate[...]
        dpar = None
        for k in reversed(range(nb)):
            sl = slice(k * CHUNK, (k + 1) * CHUNK)
            seqs = read(s_refs, sl)
            _, vjp = jax.vjp(chunk_fn, *seqs, *par, st_ref[k])
            grads = vjp((_heads(dy_ref[sl, d_off:d_off + nh * HEAD], nh), dst))
            for r, tile in zip(o_refs, write(*grads[:len(seqs)])):
                r[sl, :] = tile
            gp = grads[len(seqs):len(seqs) + npar]
            dpar = gp if dpar is None else [a + b for a, b in zip(dpar, gp)]
            dst = grads[-1]
        for r, gr in zip(dp_refs, dpar):
            r[...] += gr
        dstate[...] = dst

    rev = lambda c: steps - 1 - c
    src_spec = lambda s: pl.BlockSpec((rows, s[1]), lambda c: (rev(c), s[2]))
    par_spec = lambda a: pl.BlockSpec(a.shape, lambda c: (0, 0, 0))
    out_spec = lambda w: pl.BlockSpec((rows, w), lambda c: (rev(c), 0))
    return tuple(_call(
        body, host_us, name=name,
        out_shape=tuple(_sds((tp, w), F32) for w in out_widths) + tuple(_sds(a.shape, F32) for a in params),
        grid=(steps,),
        in_specs=[src_spec(s) for s in srcs] + [par_spec(a) for a in params]
        + [pl.BlockSpec((nb, nh, HEAD, HEAD), lambda c: (rev(c), 0, 0, 0)), out_spec(d_arr.shape[1])],
        out_specs=[out_spec(w) for w in out_widths] + [par_spec(a) for a in params],
        scratch_shapes=[pltpu.VMEM((nh, HEAD, HEAD), F32)],
        compiler_params=_params(("arbitrary",)),
    )(*[s[0] for s in srcs], *params, states, d_arr))


def _lane_param(p):
    return jnp.broadcast_to(p[:, None, None], (p.shape[0], 1, HEAD))


def _block_diag(w):
    out = jnp.zeros((LRU_WIDTH, LRU_WIDTH), F32)
    for h in range(LRU_HEADS):
        out = out.at[h * HEAD:(h + 1) * HEAD, h * HEAD:(h + 1) * HEAD].set(w[h])
    return out


def _block_diag_inv(w):
    return jnp.stack([w[h * HEAD:(h + 1) * HEAD, h * HEAD:(h + 1) * HEAD] for h in range(LRU_HEADS)])


def _ssd_inputs(xc, proj, lp):
    srcs = ((xc, CONV_W, 0), (proj, SSD_INNER, OFF_SSD_Z // SSD_INNER), (proj, LANE, OFF_SMALL // LANE))
    params = (_lane_param(lp["ssd_dt_bias"]), _lane_param(lp["ssd_a_log"]), _lane_param(lp["ssd_d"]),
              lp["ssd_norm_w"].reshape(SSD_HEADS, 1, HEAD))
    return srcs, params


def _ssd_read(refs, rows):
    xc, z, sm = refs
    return (_heads(xc[rows, 256:768], SSD_HEADS), _heads(xc[rows, 768:896], SSD_GROUPS),
            _heads(xc[rows, 896:1024], SSD_GROUPS), _head_scalars(sm[rows, :], 0, SSD_HEADS), _heads(z[rows, :], SSD_HEADS))


def _ssd_write(dxs, dbm, dcm, ddtr, dz):
    return (jnp.concatenate([_unheads(dxs), _unheads(dbm), _unheads(dcm)], axis=-1), _unheads(dz), _unhead_scalars(ddtr, 0))


def _dn_inputs(xc, proj, lp):
    srcs = ((xc, CONV_W, 0), (proj, 256, OFF_DN_GATE // 256), (proj, LANE, OFF_SMALL // LANE))
    params = (_lane_param(lp["dn_a_log"]), _lane_param(lp["dn_dt_bias"]),
              jnp.broadcast_to(lp["dn_norm_w"][None, None, :], (DN_HEADS, 1, HEAD)))
    return srcs, params


def _dn_read(refs, rows):
    xc, gate, sm = refs
    return (_heads(xc[rows, 1024:1280], DN_HEADS), _heads(xc[rows, 1280:1536], DN_HEADS),
            _heads(xc[rows, 1536:1792], DN_HEADS), _heads(gate[rows, :], DN_HEADS),
            _head_scalars(sm[rows, :], 8, DN_HEADS), _head_scalars(sm[rows, :], 12, DN_HEADS))


def _dn_write(dq, dk, dv, dgate, dbraw, daraw):
    return (jnp.concatenate([_unheads(dq), _unheads(dk), _unheads(dv)], axis=-1), _unheads(dgate),
            _unhead_scalars(dbraw, 8) + _unhead_scalars(daraw, 12))


def _lru_params(lp):
    return (_block_diag(lp["lru_w_a"]), lp["lru_b_a"][None], _block_diag(lp["lru_w_x"]), lp["lru_b_x"][None],
            lp["lru_lambda"][None])


def _conv_params(lp):
    w = jnp.concatenate([lp["lru_conv_w"], lp["ssd_conv_w"], lp["dn_conv_w"]], axis=1)
    b = jnp.concatenate([lp["lru_conv_b"], lp["ssd_conv_b"], jnp.zeros((768,), F32)])[None]
    return w, b


def _mixers_fwd(proj, lp, tag):
    cw, cb = _conv_params(lp)
    xc = _conv_fwd(proj, cw, cb, tag + "_conv")
    o_lru, h_lru = _lru_fwd(xc, proj, *_lru_params(lp), name=tag + "_lru")
    s_srcs, s_par = _ssd_inputs(xc, proj, lp)
    o_ssd, s_states = _chunk_scan_fwd(_ssd_chunk, s_srcs, _ssd_read, s_par, SSD_HEADS, 70, tag + "_ssd")
    d_srcs, d_par = _dn_inputs(xc, proj, lp)
    o_dn, d_states = _chunk_scan_fwd(_dn_chunk, d_srcs, _dn_read, d_par, DN_HEADS, 140, tag + "_dn")
    cat = jnp.concatenate([o_lru, o_ssd, o_dn], axis=1)
    return cat, (xc, h_lru, s_states, d_states)


def _mixers_bwd(proj, lp, saved, dcat, tag):
    xc, h_lru, s_states, d_states = saved
    tp = proj.shape[0]
    cw, cb = _conv_params(lp)
    g = {}
    du, dyraw, dwa, dba, dwx, dbx, dlam = _lru_bwd(xc, proj, h_lru, dcat, *_lru_params(lp), name=tag + "_lru_bwd")
    g["lru_w_a"], g["lru_b_a"], g["lru_w_x"], g["lru_b_x"], g["lru_lambda"] = (
        _block_diag_inv(dwa), dba[0], _block_diag_inv(dwx), dbx[0], dlam[0])

    s_srcs, s_par = _ssd_inputs(xc, proj, lp)
    dxbc, dz, dsm_ssd, dp_dtb, dp_alog, dp_d, dp_nw = _chunk_scan_bwd(
        _ssd_chunk, s_srcs, _ssd_read, s_par, s_states, (dcat, 256), _ssd_write, (768, SSD_INNER, LANE), SSD_HEADS, 125,
        tag + "_ssd_bwd")
    g["ssd_dt_bias"], g["ssd_a_log"], g["ssd_d"] = (jnp.sum(p, axis=(1, 2)) for p in (dp_dtb, dp_alog, dp_d))
    g["ssd_norm_w"] = dp_nw.reshape(SSD_INNER)

    d_srcs, d_par = _dn_inputs(xc, proj, lp)
    dqkv, dgate, dsm_dn, dq_alog, dq_dtb, dq_nw = _chunk_scan_bwd(
        _dn_chunk, d_srcs, _dn_read, d_par, d_states, (dcat, 768), _dn_write, (768, 256, LANE), DN_HEADS, 380,
        tag + "_dn_bwd")
    g["dn_a_log"], g["dn_dt_bias"] = (jnp.sum(p, axis=(1, 2)) for p in (dq_alog, dq_dtb))
    g["dn_norm_w"] = jnp.sum(dq_nw, axis=(0, 1))

    dconv, dcw, dcb = _conv_bwd(proj, cw, cb, jnp.concatenate([du, dxbc, dqkv], axis=1), tag + "_conv_bwd")
    g["lru_conv_w"], g["ssd_conv_w"], g["dn_conv_w"] = dcw[:, :256], dcw[:, 256:1024], dcw[:, 1024:]
    g["lru_conv_b"], g["ssd_conv_b"] = dcb[0, :256], dcb[0, 256:1024]
    dproj = jnp.concatenate([dconv, dyraw, dz, dgate, dsm_ssd + dsm_dn], axis=1)
    return dproj, g


MIXER_PARAMS = ("lru_conv_w", "lru_conv_b", "lru_w_a", "lru_b_a", "lru_w_x", "lru_b_x", "lru_lambda",
                "ssd_conv_w", "ssd_conv_b", "ssd_dt_bias", "ssd_a_log", "ssd_d", "ssd_norm_w",
                "dn_conv_w", "dn_a_log", "dn_dt_bias", "dn_norm_w")


UNITS = ("gate0", "up0", "down0", "win", "wout", "gate1", "up1", "down1")


def _layer_shards(w, l):
    out = {"win": w["w_in"][l], "wout": w["w_out"][l]}
    for j in range(2):
        out[f"gate{j}"], out[f"up{j}"], out[f"down{j}"] = w["ffn_w_gate"][l, j], w["ffn_w_up"][l, j], w["ffn_w_down"][l, j]
    return {k: a.astype(BF16) for k, a in out.items()}


def _enqueue_matrix(name, arr, slab):
    if name == "wout":
        return [_enqueue(arr, slab)], 1
    return _enqueue_halves(arr, slab, 1 if slab else 0)


def _enqueue_layer(shards):
    return {k: _enqueue_matrix(k, shards[k], False) for k in UNITS}


def _gathered_cols(a):
    return a.transpose(1, 0, 2).reshape(a.shape[1], -1)


def _col_slabs(a):
    return a.reshape(a.shape[0], N_DEV, -1).transpose(1, 0, 2).astype(BF16)


def _row_slabs(a):
    return a.reshape(N_DEV, -1, a.shape[1]).astype(BF16)


def _local_step(x, tgt, small, first, shards):
    s = x.shape[0]
    t_real = N_META + s
    tp = -(-t_real // CHUNK) * CHUNK
    depth = len(first) if shards is None else len(shards)
    h = jnp.concatenate([small["meta"], x, jnp.zeros((tp - t_real, D_MODEL), F32)], axis=0)
    hb = h.astype(BF16)
    ln_g, ln_b = small["ln_g"], small["ln_b"]
    saved, weights = [], []
    queued = first
    for l in range(depth):
        lp = {k: small[k][l] for k in MIXER_PARAMS}
        t = f"l{l}"
        if shards is None:
            get = lambda k, g=first[l]: g[k]
        else:
            get = lambda k, g=queued: _collect_halves(g[k])
            if l + 1 < depth:
                queued = _enqueue_layer(shards[l + 1])

        def ffn_weights(j):
            up = _gathered_cols(get(f"up{j}"))
            return jnp.concatenate([_gathered_cols(get(f"gate{j}")), up], axis=-1)

        wgu0 = ffn_weights(0)
        g0, u0, a0 = _ffn_up(hb, wgu0, t + "_ffn0_up")
        wd0 = get("down0").reshape(-1, D_MODEL)
        y1, h1, h1b = _mm_resid_ln(a0, wd0, h, ln_g[l, 0][None], ln_b[l, 0][None], FFN_RES, t + "_ffn0_down")
        win = _proj_cols(_gathered_cols(get("win")))
        proj = _mm_nn(h1b, win, t + "_in_proj")
        cat, mix_saved = _mixers_fwd(proj, lp, t)
        catb = cat.astype(BF16)
        wout = get("wout").reshape(D_MODEL, D_MODEL)
        y2, h2, h2b = _mm_resid_ln(catb, wout, h1, ln_g[l, 1][None], ln_b[l, 1][None], 1.0, t + "_out_proj")
        wgu1 = ffn_weights(1)
        g1, u1, a1 = _ffn_up(h2b, wgu1, t + "_ffn1_up")
        wd1 = get("down1").reshape(-1, D_MODEL)
        y3, h3, h3b = _mm_resid_ln(a1, wd1, h2, ln_g[l, 2][None], ln_b[l, 2][None], FFN_RES, t + "_ffn1_down")
        weights.append(([wgu0, wgu1], [wd0, wd1], win, wout))
        saved.append((hb, g0, u0, a0, y1, h1b, proj, mix_saved, catb, y2, h2b, g1, u1, a1, y3))
        h, hb = h3, h3b

    tgt_p = jnp.pad(tgt, ((N_META, tp - t_real), (0, 0)))
    lossv, dh = _loss_grad(h, tgt_p, t_real, "loss")
    loss = jnp.sum(lossv)

    gs = {k: [None] * depth for k in MIXER_PARAMS}
    d_ln_g = [[None] * 3 for _ in range(depth)]
    d_ln_b = [[None] * 3 for _ in range(depth)]
    slabs = [{} for _ in range(depth)]
    send = (lambda k, a: a) if shards is None else (lambda k, a: _enqueue_matrix(k, a, True))

    def ffn_bwd(l, j, xb_in, g, u, a, y, dout):
        t = f"l{l}_ffn{j}"
        wgu, wd = weights[l][0][j], weights[l][1][j]
        dres, dyb, dgam, dbet = _ln_bwd(y, dout, ln_g[l, 2 * j][None], FFN_RES, t + "_ln_bwd")
        d_ln_g[l][2 * j], d_ln_b[l][2 * j] = dgam[0], dbet[0]
        slabs[l][f"down{j}"] = send("down", _row_slabs(_mm_tn(a, dyb, t + "_dwd")))
        dg, du = _ffn_dact(dyb, wd, g, u, t + "_dact")
        slabs[l][f"gate{j}"] = send("gate", _col_slabs(_mm_tn(xb_in, dg, t + "_dwg")))
        slabs[l][f"up{j}"] = send("up", _col_slabs(_mm_tn(xb_in, du, t + "_dwu")))
        return _ffn_dx(dg, du, wgu, dres, t + "_dx")

    for l in reversed(range(depth)):
        hb_in, g0, u0, a0, y1, h1b, proj, mix_saved, catb, y2, h2b, g1, u1, a1, y3 = saved[l]
        lp = {k: small[k][l] for k in MIXER_PARAMS}
        t = f"l{l}"
        _, _, win, wout = weights[l]
        dh2 = ffn_bwd(l, 1, h2b, g1, u1, a1, y3, dh)
        dres, dyb, dgam, dbet = _ln_bwd(y2, dh2, ln_g[l, 1][None], 1.0, t + "_mix_ln_bwd")
        d_ln_g[l][1], d_ln_b[l][1] = dgam[0], dbet[0]
        slabs[l]["wout"] = send("wout", _row_slabs(_mm_tn(catb, dyb, t + "_dwout")))
        dcat = _mm_nt_add(dyb, wout, None, t + "_dcat")
        dproj, mg = _mixers_bwd(proj, lp, mix_saved, dcat, t)
        for k in MIXER_PARAMS:
            gs[k][l] = mg[k]
        dprojb = dproj.astype(BF16)
        slabs[l]["win"] = send("win", _col_slabs(_proj_cols_inv(_mm_tn(h1b, dprojb, t + "_dwin"))))
        dh1 = _mm_nt_add(dprojb, win, dres, t + "_dh1")
        dh = ffn_bwd(l, 0, hb_in, g0, u0, a0, y1, dh1)

    small_grads = {k: jnp.stack(v) for k, v in gs.items()}
    small_grads["ln_g"] = jnp.stack([jnp.stack(r) for r in d_ln_g])
    small_grads["ln_b"] = jnp.stack([jnp.stack(r) for r in d_ln_b])
    small_grads["meta"] = dh[:N_META]
    return loss, dh[N_META:t_real], small_grads, slabs


def _mesh_pos():
    return lax.axis_index("x"), lax.axis_index("y"), lax.axis_index("c")


def _flip(pos, k):
    x, y, c = pos
    return (1 - x if k & 4 else x, 1 - y if k & 2 else y, 1 - c if k & 1 else c)


def _flat(pos):
    return 4 * pos[0] + 2 * pos[1] + pos[2]


def _exchange_copies(ins, outs, slabs, send, recv, loc):
    pos = _mesh_pos()
    me = _flat(pos)
    local, sends, receives = [], [], []
    for a in range(len(ins)):
        local.append(pltpu.make_async_copy(ins[a].at[me] if slabs[a] else ins[a], outs[a].at[me], loc.at[a]))
    for k in range(1, N_DEV):
        peer = _flip(pos, k)
        for a in range(len(ins)):
            if not slabs[a] and k not in GATHER_FIRST:
                continue
            sem = dict(send_sem=send.at[a, k - 1], recv_sem=recv.at[a, k - 1], device_id=peer, device_id_type=MESH_IDS)
            sends.append(pltpu.make_async_remote_copy(
                src_ref=ins[a].at[_flat(peer)] if slabs[a] else ins[a], dst_ref=outs[a].at[me], **sem))
            receives.append(pltpu.make_async_remote_copy(
                src_ref=ins[a].at[me] if slabs[a] else ins[a], dst_ref=outs[a].at[_flat(peer)], **sem))
    return local, sends, receives


GATHER_FIRST = (1, 2, 4, 6)


def _gather_forwards(ins, outs, slabs, send, recv):
    pos = _mesh_pos()
    sibling = _flip(pos, 1)
    forwards, arrivals = [], []
    for a in range(len(ins)):
        if slabs[a]:
            continue
        for k in GATHER_FIRST[1:]:
            sem = dict(send_sem=send.at[a, k], recv_sem=recv.at[a, k], device_id=sibling, device_id_type=MESH_IDS)
            mine, theirs = _flat(_flip(pos, k)), _flat(_flip(pos, k + 1))
            forwards.append(pltpu.make_async_remote_copy(src_ref=outs[a].at[mine], dst_ref=outs[a].at[mine], **sem))
            arrivals.append(pltpu.make_async_remote_copy(src_ref=outs[a].at[theirs], dst_ref=outs[a].at[theirs], **sem))
    return forwards, arrivals


def _exchange_start(ins, outs, slabs, send, recv, loc):
    local, sends, _ = _exchange_copies(ins, outs, slabs, send, recv, loc)
    for cp in local + sends:
        cp.start()


def _exchange_wait(ins, outs, slabs, send, recv, loc):
    local, sends, receives = _exchange_copies(ins, outs, slabs, send, recv, loc)
    forwards, arrivals = _gather_forwards(ins, outs, slabs, send, recv)
    for cp in receives:
        cp.wait_recv()
    for cp in forwards:
        cp.start()
    for cp in arrivals:
        cp.wait_recv()
    for cp in sends + forwards:
        cp.wait_send()
    for cp in local:
        cp.wait()


def _exchange_shapes(arrs, slabs):
    return tuple(_sds(a.shape if s else (N_DEV,) + a.shape, a.dtype) for a, s in zip(arrs, slabs))


def _exchange_sems(n):
    return [pltpu.SemaphoreType.DMA((n, N_DEV - 1)), pltpu.SemaphoreType.DMA((n, N_DEV - 1)),
            pltpu.SemaphoreType.DMA((n,))]


def _exchange(arrs, slabs, name):
    n = len(arrs)

    def body(*refs):
        ins, outs, sems = refs[:n], refs[n:2 * n], refs[2 * n:]
        _exchange_start(ins, outs, slabs, *sems)
        _exchange_wait(ins, outs, slabs, *sems)

    hbm = pl.BlockSpec(memory_space=pl.ANY)
    return pl.pallas_call(
        body, name=name, out_shape=_exchange_shapes(arrs, slabs), in_specs=[hbm] * n, out_specs=[hbm] * n,
        scratch_shapes=_exchange_sems(n),
    )(*arrs)


def _adam_math(w, g, m, v):
    m = ADAM_B1 * m + (1.0 - ADAM_B1) * g
    v = ADAM_B2 * v + (1.0 - ADAM_B2) * (g * g)
    m_hat = m / (1.0 - ADAM_B1 ** ADAM_STEP)
    v_hat = v / (1.0 - ADAM_B2 ** ADAM_STEP)
    delta = -ADAM_LR * (m_hat / (jnp.sqrt(v_hat) + ADAM_EPS) + ADAM_WD * w)
    return delta, m, v


def _adam(w, g, m, v, name):
    r, c = w.shape
    parts = g.ndim == 3
    tr = _tile(r, 512, 8)

    def body(w_ref, g_ref, m_ref, v_ref, go_ref, d_ref, mo_ref, vo_ref):
        if parts:
            gv = g_ref[0].astype(F32)
            for d in range(1, N_DEV):
                gv = gv + g_ref[d].astype(F32)
        else:
            gv = g_ref[...]
        delta, mn, vn = _adam_math(w_ref[...], gv, m_ref[...], v_ref[...])
        go_ref[...] = gv
        d_ref[...] = delta
        mo_ref[...] = mn
        vo_ref[...] = vn

    blk = pl.BlockSpec((tr, c), lambda i: (i, 0))
    gblk = pl.BlockSpec((N_DEV, tr, c), lambda i: (0, i, 0)) if parts else blk
    out = _sds((r, c), F32)
    return pl.pallas_call(
        body, name=name, out_shape=(out,) * 4, grid=(r // tr,),
        in_specs=[blk, gblk, blk, blk], out_specs=[blk] * 4,
        compiler_params=_params(("arbitrary",)),
    )(w, g, m, v)


def _sum_parts(parts, name):
    _, r, c = parts.shape

    def body(p_ref, o_ref):
        acc = p_ref[0]
        for d in range(1, N_DEV):
            acc = acc + p_ref[d]
        o_ref[...] = acc

    return pl.pallas_call(body, name=name, out_shape=_sds((r, c), F32), compiler_params=_params())(parts)


SMALL_SHARD_AXIS = {
    "meta": 1, "ln_g": 2, "ln_b": 2, "lru_conv_w": 2, "lru_conv_b": None, "lru_w_a": None, "lru_b_a": None,
    "lru_w_x": None, "lru_b_x": None, "lru_lambda": None, "ssd_conv_w": 2, "ssd_conv_b": None, "ssd_dt_bias": None,
    "ssd_a_log": None, "ssd_d": None, "ssd_norm_w": None, "dn_conv_w": 2, "dn_a_log": None, "dn_dt_bias": None,
    "dn_norm_w": None,
}
BIG = ("ffn_w_gate", "ffn_w_up", "ffn_w_down", "w_in", "w_out")
WEIGHT_ORDER = ("meta", "ln_g", "ln_b", "ffn_w_gate", "ffn_w_up", "ffn_w_down", "w_in", "lru_conv_w", "lru_conv_b",
                "lru_w_a", "lru_b_a", "lru_w_x", "lru_b_x", "lru_lambda", "ssd_conv_w", "ssd_conv_b", "ssd_dt_bias",
                "ssd_a_log", "ssd_d", "ssd_norm_w", "dn_conv_w", "dn_a_log", "dn_dt_bias", "dn_norm_w", "w_out")


def _pack(arrs):
    flat = jnp.concatenate([a.reshape(-1) for a in arrs])
    rows = -(-flat.shape[0] // (8 * LANE)) * 8
    return jnp.pad(flat, (0, rows * LANE - flat.shape[0])).reshape(rows, LANE)


def _unpack(buf, shapes, lead=()):
    flat = buf.reshape(lead + (-1,))
    out, off = [], 0
    for s in shapes:
        n = math.prod(s)
        out.append(flat[..., off:off + n].reshape(lead + tuple(s)))
        off += n
    return out


def _proj_cols(w):
    pad = jnp.zeros(w.shape[:-1] + (PROJ_W - D_IN,), w.dtype)
    return jnp.concatenate([w[..., 0:256], w[..., 1024:1792], w[..., 1800:2568], w[..., 256:512], w[..., 512:1024],
                            w[..., 2568:2824], w[..., 1792:1800], w[..., 2824:2832], pad], axis=-1)


def _proj_cols_inv(w):
    return jnp.concatenate([w[..., 0:256], w[..., 1792:2048], w[..., 2048:2560], w[..., 256:1024], w[..., 2816:2824],
                            w[..., 1024:1792], w[..., 2560:2816], w[..., 2824:2832]], axis=-1)


def kernel(x, meta, ln_g, ln_b, ffn_w_gate, ffn_w_up, ffn_w_down, w_in, lru_conv_w, lru_conv_b, lru_w_a, lru_b_a, lru_w_x, lru_b_x, lru_lambda, ssd_conv_w, ssd_conv_b, ssd_dt_bias, ssd_a_log, ssd_d, ssd_norm_w, dn_conv_w, dn_a_log, dn_dt_bias, dn_norm_w, w_out, loss_target, m_meta, m_ln_g, m_ln_b, m_ffn_w_gate, m_ffn_w_up, m_ffn_w_down, m_w_in, m_lru_conv_w, m_lru_conv_b, m_lru_w_a, m_lru_b_a, m_lru_w_x, m_lru_b_x, m_lru_lambda, m_ssd_conv_w, m_ssd_conv_b, m_ssd_dt_bias, m_ssd_a_log, m_ssd_d, m_ssd_norm_w, m_dn_conv_w, m_dn_a_log, m_dn_dt_bias, m_dn_norm_w, m_w_out, v_meta, v_ln_g, v_ln_b, v_ffn_w_gate, v_ffn_w_up, v_ffn_w_down, v_w_in, v_lru_conv_w, v_lru_conv_b, v_lru_w_a, v_lru_b_a, v_lru_w_x, v_lru_b_x, v_lru_lambda, v_ssd_conv_w, v_ssd_conv_b, v_ssd_dt_bias, v_ssd_a_log, v_ssd_d, v_ssd_norm_w, v_dn_conv_w, v_dn_a_log, v_dn_dt_bias, v_dn_norm_w, v_w_out):
    w = dict(meta=meta, ln_g=ln_g, ln_b=ln_b, ffn_w_gate=ffn_w_gate, ffn_w_up=ffn_w_up, ffn_w_down=ffn_w_down, w_in=w_in,
             lru_conv_w=lru_conv_w, lru_conv_b=lru_conv_b, lru_w_a=lru_w_a, lru_b_a=lru_b_a, lru_w_x=lru_w_x,
             lru_b_x=lru_b_x, lru_lambda=lru_lambda, ssd_conv_w=ssd_conv_w, ssd_conv_b=ssd_conv_b, ssd_dt_bias=ssd_dt_bias,
             ssd_a_log=ssd_a_log, ssd_d=ssd_d, ssd_norm_w=ssd_norm_w, dn_conv_w=dn_conv_w, dn_a_log=dn_a_log,
             dn_dt_bias=dn_dt_bias, dn_norm_w=dn_norm_w, w_out=w_out)
    m = dict(meta=m_meta, ln_g=m_ln_g, ln_b=m_ln_b, ffn_w_gate=m_ffn_w_gate, ffn_w_up=m_ffn_w_up, ffn_w_down=m_ffn_w_down,
             w_in=m_w_in, lru_conv_w=m_lru_conv_w, lru_conv_b=m_lru_conv_b, lru_w_a=m_lru_w_a, lru_b_a=m_lru_b_a,
             lru_w_x=m_lru_w_x, lru_b_x=m_lru_b_x, lru_lambda=m_lru_lambda, ssd_conv_w=m_ssd_conv_w, ssd_conv_b=m_ssd_conv_b,
             ssd_dt_bias=m_ssd_dt_bias, ssd_a_log=m_ssd_a_log, ssd_d=m_ssd_d, ssd_norm_w=m_ssd_norm_w, dn_conv_w=m_dn_conv_w,
             dn_a_log=m_dn_a_log, dn_dt_bias=m_dn_dt_bias, dn_norm_w=m_dn_norm_w, w_out=m_w_out)
    v = dict(meta=v_meta, ln_g=v_ln_g, ln_b=v_ln_b, ffn_w_gate=v_ffn_w_gate, ffn_w_up=v_ffn_w_up, ffn_w_down=v_ffn_w_down,
             w_in=v_w_in, lru_conv_w=v_lru_conv_w, lru_conv_b=v_lru_conv_b, lru_w_a=v_lru_w_a, lru_b_a=v_lru_b_a,
             lru_w_x=v_lru_w_x, lru_b_x=v_lru_b_x, lru_lambda=v_lru_lambda, ssd_conv_w=v_ssd_conv_w, ssd_conv_b=v_ssd_conv_b,
             ssd_dt_bias=v_ssd_dt_bias, ssd_a_log=v_ssd_a_log, ssd_d=v_ssd_d, ssd_norm_w=v_ssd_norm_w, dn_conv_w=v_dn_conv_w,
             dn_a_log=v_dn_a_log, dn_dt_bias=v_dn_dt_bias, dn_norm_w=v_dn_norm_w, w_out=v_w_out)
    depth = ln_g.shape[0]
    me = _flat(_mesh_pos())
    small_names = tuple(SMALL_SHARD_AXIS)
    sharded = tuple(k for k in small_names if SMALL_SHARD_AXIS[k] is not None)

    del _QUEUE[:]
    _STANDALONE[0] = 0
    shards = [_layer_shards(w, l) for l in range(depth)]
    small_unit = _enqueue(_pack([w[k] for k in sharded]), False)
    first = _enqueue_layer(shards[0])
    g_small = _collect(small_unit)
    small = {k: w[k] for k in small_names if SMALL_SHARD_AXIS[k] is None}
    for k, piece in zip(sharded, _unpack(g_small, [w[k].shape for k in sharded], lead=(N_DEV,))):
        ax = SMALL_SHARD_AXIS[k]
        full = jnp.moveaxis(piece, 0, ax)
        small[k] = full.reshape(full.shape[:ax] + (N_DEV * w[k].shape[ax],) + full.shape[ax + 2:])

    loss, dx, small_grads, slabs = _local_step(x[0], loss_target[0], small, first, shards)

    r_small = _collect(_enqueue(_pack([small_grads[k] for k in small_names]), False))
    got = [{k: _collect_halves(g) for k, g in layer.items()} for layer in slabs]

    outs = {}
    ffn = lambda name: jnp.stack([jnp.stack([got[l][f"{name}{j}"] for j in range(2)], axis=1) for l in range(depth)], axis=1)
    one = lambda name: jnp.stack([got[l][name] for l in range(depth)], axis=1)
    parts_of = {"ffn_w_gate": ffn("gate"), "ffn_w_up": ffn("up"), "ffn_w_down": ffn("down"), "w_in": one("win"),
                "w_out": one("wout")}
    for k in BIG:
        shp = w[k].shape
        two = lambda a: a.reshape(-1, shp[-1])
        res = _adam(two(w[k]), parts_of[k].reshape(N_DEV, -1, shp[-1]), two(m[k]), two(v[k]), "adam_" + k)
        outs[k] = [a.reshape(shp) for a in res]
    g_full = _unpack(_sum_parts(r_small, "sum_small_grads"), [small[k].shape for k in small_names])
    g_loc = {}
    for k, gf in zip(small_names, g_full):
        ax = SMALL_SHARD_AXIS[k]
        g_loc[k] = gf if ax is None else lax.dynamic_slice_in_dim(gf, me * w[k].shape[ax], w[k].shape[ax], axis=ax)
    res = _adam(_pack([w[k] for k in small_names]), _pack([g_loc[k] for k in small_names]),
                _pack([m[k] for k in small_names]), _pack([v[k] for k in small_names]), "adam_small")
    shapes = [w[k].shape for k in small_names]
    for i, k in enumerate(small_names):
        outs[k] = [_unpack(r, shapes)[i] for r in res]

    loss = lax.psum(loss, ("x", "y", "c"))
    return (loss, dx[None], *[outs[k][0] for k in WEIGHT_ORDER], *[outs[k][1] for k in WEIGHT_ORDER],
            *[outs[k][2] for k in WEIGHT_ORDER], *[outs[k][3] for k in WEIGHT_ORDER])
```

```python
import functools
import math

import jax
import jax.numpy as jnp
from jax import lax
from jax.experimental import pallas as pl
from jax.experimental.pallas import tpu as pltpu

F32 = jnp.float32
BF16 = jnp.bfloat16
HI = lax.Precision.HIGHEST
MESH_IDS = pl.DeviceIdType.MESH

N_DEV = 8
D_MODEL = 1024
DEPTH = 4
N_META = 16
CHUNK = 64
CONV_K = 4
D_FF = 2816
LRU_WIDTH = 256
LRU_HEADS = 4
LRU_C = 8.0
SSD_HEADS = 8
SSD_GROUPS = 2
SSD_HPG = 4
SSD_INNER = 512
DN_HEADS = 4
HEAD = 64
CONV_W = 1792
PROJ_W = 2944
OFF_LRU_Y, OFF_SSD_Z, OFF_DN_GATE, OFF_SMALL = 1792, 2048, 2560, 2816
D_IN = 2832
ALPHA = (2 * DEPTH) ** 0.25
FFN_RES = 0.5
LN_EPS = 1e-5
RMS_EPS = 1e-6
ADAM_LR, ADAM_B1, ADAM_B2, ADAM_EPS, ADAM_WD, ADAM_STEP = 0.001, 0.9, 0.999, 1e-08, 0.01, 10

VMEM_LIMIT = 56 * 1024 * 1024
ROW_TILE_CAP = 832
LANE = 128


def _tile(n, cap, mult=16):
    best = None
    for t in range(mult, min(n, cap) + 1, mult):
        if n % t == 0:
            best = t
    assert best is not None, (n, cap, mult)
    return best


def _params(sem=None):
    return pltpu.CompilerParams(dimension_semantics=sem, vmem_limit_bytes=VMEM_LIMIT)


def _resident(shape, index_map):
    return pl.BlockSpec(shape, index_map, pipeline_mode=pl.Buffered(1))


def _dot(a, b):
    return jnp.dot(a, b, preferred_element_type=F32)


def _dot_nt(a, b):
    return lax.dot_general(a, b, (((1,), (1,)), ((), ())), preferred_element_type=F32)


def _dot_tn(a, b):
    return lax.dot_general(a, b, (((0,), (0,)), ((), ())), preferred_element_type=F32)


def _sds(shape, dtype):
    return jax.ShapeDtypeStruct(shape, dtype)


US_PER_MB = 92.5


class _Unit:
    def __init__(self, arr, slab):
        self.arr, self.slab, self.out = arr, slab, None
        per_peer = arr.size * arr.dtype.itemsize / (N_DEV if slab else 1)
        self.us = per_peer / 1e6 * US_PER_MB * (1.0 if slab else 0.5)


_QUEUE = []
_STANDALONE = [0]


def _enqueue(arr, slab):
    unit = _Unit(arr, slab)
    _QUEUE.append(unit)
    return unit


def _enqueue_halves(arr, slab, axis):
    half = arr.shape[axis] // 2
    parts = (lax.slice_in_dim(arr, 0, half, axis=axis), lax.slice_in_dim(arr, half, arr.shape[axis], axis=axis))
    return [_enqueue(p, slab) for p in parts], axis + (0 if slab else 1)


def _collect_halves(group):
    units, axis = group
    return jnp.concatenate([_collect(u) for u in units], axis=axis)


def _take_units(host_us):
    units = []
    while _QUEUE and host_us >= 0.5 * _QUEUE[0].us:
        host_us -= _QUEUE[0].us
        units.append(_QUEUE.pop(0))
    return units


def _collect(unit):
    if unit.out is None:
        n = _QUEUE.index(unit) + 1
        units = [_QUEUE.pop(0) for _ in range(n)]
        _STANDALONE[0] += 1
        res = _exchange([u.arr for u in units], [u.slab for u in units], f"exchange_{_STANDALONE[0]}")
        for u, r in zip(units, res):
            u.out = r
    return unit.out


def _call(body, host_us, *, name, out_shape, in_specs, out_specs, grid=(), scratch_shapes=(), compiler_params=None):
    units = _take_units(host_us)
    kw = dict(name=name, grid=grid, compiler_params=compiler_params)
    if not units:
        return pl.pallas_call(body, out_shape=out_shape, in_specs=in_specs, out_specs=out_specs,
                              scratch_shapes=list(scratch_shapes), **kw)
    single = not isinstance(out_shape, (tuple, list))
    outs = (out_shape,) if single else tuple(out_shape)
    ospecs = [out_specs] if single else list(out_specs)
    nin, nout, nscr, ncm = len(in_specs), len(outs), len(scratch_shapes), len(units)
    slabs = [u.slab for u in units]

    def hosted(*refs):
        ins, c_in = refs[:nin], refs[nin:nin + ncm]
        o, c_out = refs[nin + ncm:nin + ncm + nout], refs[nin + ncm + nout:nin + 2 * ncm + nout]
        scr, sems = refs[nin + 2 * ncm + nout:nin + 2 * ncm + nout + nscr], refs[nin + 2 * ncm + nout + nscr:]
        ids = [pl.program_id(d) for d in range(len(grid))]
        first = functools.reduce(jnp.logical_and, [i == 0 for i in ids])
        last = functools.reduce(jnp.logical_and, [i == g - 1 for i, g in zip(ids, grid)])

        @pl.when(first)
        def _():
            _exchange_start(c_in, c_out, slabs, *sems)
        body(*ins, *o, *scr)

        @pl.when(last)
        def _():
            _exchange_wait(c_in, c_out, slabs, *sems)

    hbm = pl.BlockSpec(memory_space=pl.ANY)
    fn = pl.pallas_call(
        hosted, out_shape=outs + _exchange_shapes([u.arr for u in units], slabs), in_specs=list(in_specs) + [hbm] * ncm,
        out_specs=ospecs + [hbm] * ncm, scratch_shapes=list(scratch_shapes) + _exchange_sems(ncm), **kw)

    def run(*args):
        res = fn(*args, *[u.arr for u in units])
        for u, r in zip(units, res[nout:]):
            u.out = r
        return res[0] if single else tuple(res[:nout])

    return run


def _ffn_up(xb, wgu, name):
    tp, d = xb.shape
    f = wgu.shape[1] // 2
    tm = _tile(tp, ROW_TILE_CAP)
    tn = _tile(f, 1408, LANE)
    nj = f // tn

    def body(x_ref, wg_ref, wu_ref, g_ref, u_ref, a_ref):
        x = x_ref[...]
        g = _dot(x, wg_ref[...])
        u = _dot(x, wu_ref[...])
        g_ref[...] = g.astype(BF16)
        u_ref[...] = u.astype(BF16)
        a_ref[...] = (g * jax.nn.sigmoid(g) * u).astype(BF16)

    out = _sds((tp, f), BF16)
    return _call(
        body, 60, name=name, out_shape=(out, out, out), grid=(nj, tp // tm),
        in_specs=[pl.BlockSpec((tm, d), lambda j, i: (i, 0)),
                  pl.BlockSpec((d, tn), lambda j, i: (0, j)),
                  pl.BlockSpec((d, tn), lambda j, i: (0, j + nj))],
        out_specs=[pl.BlockSpec((tm, tn), lambda j, i: (i, j))] * 3,
        compiler_params=_params(("arbitrary", "arbitrary")),
    )(xb, wgu, wgu)


def _mm_resid_ln(a, w, h, gamma, beta, scale, name):
    tp, k = a.shape
    d = w.shape[1]
    tm = _tile(tp, ROW_TILE_CAP)

    def body(a_ref, w_ref, h_ref, g_ref, b_ref, y_ref, o_ref, ob_ref):
        y = ALPHA * h_ref[...] + scale * _dot(a_ref[...], w_ref[...])
        mu = jnp.mean(y, axis=-1, keepdims=True)
        yc = y - mu
        var = jnp.mean(yc * yc, axis=-1, keepdims=True)
        o = yc * lax.rsqrt(var + LN_EPS) * g_ref[...] + b_ref[...]
        y_ref[...] = y
        o_ref[...] = o
        ob_ref[...] = o.astype(BF16)

    row = lambda i: (i, 0)
    fix = lambda i: (0, 0)
    return _call(
        body, 30, name=name, out_shape=(_sds((tp, d), F32), _sds((tp, d), F32), _sds((tp, d), BF16)),
        grid=(tp // tm,),
        in_specs=[pl.BlockSpec((tm, k), row), _resident((k, d), fix), pl.BlockSpec((tm, d), row),
                  pl.BlockSpec((1, d), fix), pl.BlockSpec((1, d), fix)],
        out_specs=[pl.BlockSpec((tm, d), row)] * 3,
        compiler_params=_params(("arbitrary",)),
    )(a, w, h, gamma, beta)


def _mm_nn(xb, w, name):
    tp, k = xb.shape
    n = w.shape[1]
    tm = _tile(tp, ROW_TILE_CAP)

    def body(x_ref, w_ref, o_ref):
        o_ref[...] = _dot(x_ref[...], w_ref[...])

    return _call(
        body, 30, name=name, out_shape=_sds((tp, n), F32), grid=(tp // tm,),
        in_specs=[pl.BlockSpec((tm, k), lambda i: (i, 0)), _resident((k, n), lambda i: (0, 0))],
        out_specs=pl.BlockSpec((tm, n), lambda i: (i, 0)),
        compiler_params=_params(("arbitrary",)),
    )(xb, w)


def _mm_nt_add(a, w, resid, name):
    tp, k = a.shape
    n = w.shape[0]
    tm = _tile(tp, ROW_TILE_CAP)
    has_resid = resid is not None

    def body(*refs):
        if has_resid:
            a_ref, w_ref, r_ref, o_ref = refs
            o_ref[...] = r_ref[...] + _dot_nt(a_ref[...], w_ref[...])
        else:
            a_ref, w_ref, o_ref = refs
            o_ref[...] = _dot_nt(a_ref[...], w_ref[...])

    in_specs = [pl.BlockSpec((tm, k), lambda i: (i, 0)), _resident((n, k), lambda i: (0, 0))]
    args = [a, w]
    if has_resid:
        in_specs.append(pl.BlockSpec((tm, n), lambda i: (i, 0)))
        args.append(resid)
    return _call(
        body, 25, name=name, out_shape=_sds((tp, n), F32), grid=(tp // tm,),
        in_specs=in_specs, out_specs=pl.BlockSpec((tm, n), lambda i: (i, 0)),
        compiler_params=_params(("arbitrary",)),
    )(*args)


def _ffn_dx(dg, du, wgu, resid, name):
    tp, f = dg.shape
    d = wgu.shape[0]
    tm = _tile(tp, ROW_TILE_CAP)

    def body(dg_ref, du_ref, w_ref, r_ref, o_ref):
        acc = r_ref[...] + _dot_nt(dg_ref[...], w_ref[:, 0:f])
        o_ref[...] = acc + _dot_nt(du_ref[...], w_ref[:, f:2 * f])

    row = lambda i: (i, 0)
    return _call(
        body, 55, name=name, out_shape=_sds((tp, d), F32), grid=(tp // tm,),
        in_specs=[pl.BlockSpec((tm, f), row), pl.BlockSpec((tm, f), row), _resident((d, 2 * f), lambda i: (0, 0)),
                  pl.BlockSpec((tm, d), row)],
        out_specs=pl.BlockSpec((tm, d), row),
        compiler_params=_params(("arbitrary",)),
    )(dg, du, wgu, resid)


def _mm_tn(a, b, name):
    tp, ka = a.shape
    nb = b.shape[1]
    tt = _tile(tp, ROW_TILE_CAP)
    tk = _tile(ka, 1408, LANE)
    tn = _tile(nb, 1536, LANE)
    if tn < 512:
        tn = nb
    nt = tp // tt

    def body(a_ref, b_ref, o_ref):
        @pl.when(pl.program_id(2) == 0)
        def _():
            o_ref[...] = jnp.zeros_like(o_ref)
        o_ref[...] += _dot_tn(a_ref[...], b_ref[...])

    return _call(
        body, 40, name=name, out_shape=_sds((ka, nb), F32), grid=(ka // tk, nb // tn, nt),
        in_specs=[pl.BlockSpec((tt, tk), lambda i, j, t: (t, i)), pl.BlockSpec((tt, tn), lambda i, j, t: (t, j))],
        out_specs=pl.BlockSpec((tk, tn), lambda i, j, t: (i, j)),
        compiler_params=_params(("arbitrary", "arbitrary", "arbitrary")),
    )(a, b)


def _ln_bwd(y, dout, gamma, scale, name):
    tp, d = y.shape
    tm = _tile(tp, ROW_TILE_CAP)

    def body(y_ref, do_ref, g_ref, dres_ref, dyb_ref, dg_ref, db_ref):
        @pl.when(pl.program_id(0) == 0)
        def _():
            dg_ref[...] = jnp.zeros_like(dg_ref)
            db_ref[...] = jnp.zeros_like(db_ref)
        yv = y_ref[...]
        do = do_ref[...]
        mu = jnp.mean(yv, axis=-1, keepdims=True)
        yc = yv - mu
        var = jnp.mean(yc * yc, axis=-1, keepdims=True)
        rstd = lax.rsqrt(var + LN_EPS)
        xhat = yc * rstd
        dxh = do * g_ref[...]
        m1 = jnp.mean(dxh, axis=-1, keepdims=True)
        m2 = jnp.mean(dxh * xhat, axis=-1, keepdims=True)
        dy = rstd * (dxh - m1 - xhat * m2)
        dres_ref[...] = ALPHA * dy
        dyb_ref[...] = (scale * dy).astype(BF16)
        dg_ref[...] += jnp.sum(do * xhat, axis=0, keepdims=True)
        db_ref[...] += jnp.sum(do, axis=0, keepdims=True)

    row = lambda i: (i, 0)
    fix = lambda i: (0, 0)
    return _call(
        body, 22, name=name,
        out_shape=(_sds((tp, d), F32), _sds((tp, d), BF16), _sds((1, d), F32), _sds((1, d), F32)),
        grid=(tp // tm,),
        in_specs=[pl.BlockSpec((tm, d), row), pl.BlockSpec((tm, d), row), pl.BlockSpec((1, d), fix)],
        out_specs=[pl.BlockSpec((tm, d), row), pl.BlockSpec((tm, d), row), pl.BlockSpec((1, d), fix),
                   pl.BlockSpec((1, d), fix)],
        compiler_params=_params(("arbitrary",)),
    )(y, dout, gamma)


def _ffn_dact(dyb, wd, g, u, name):
    tp, d = dyb.shape
    f = wd.shape[0]
    tm = _tile(tp, ROW_TILE_CAP)
    tn = _tile(f, 1408, LANE)

    def body(dy_ref, w_ref, g_ref, u_ref, dg_ref, du_ref):
        dact = _dot_nt(dy_ref[...], w_ref[...])
        gv = g_ref[...].astype(F32)
        uv = u_ref[...].astype(F32)
        sg = jax.nn.sigmoid(gv)
        dg_ref[...] = (dact * uv * (sg * (1.0 + gv * (1.0 - sg)))).astype(BF16)
        du_ref[...] = (dact * (gv * sg)).astype(BF16)

    out = _sds((tp, f), BF16)
    blk = pl.BlockSpec((tm, tn), lambda j, i: (i, j))
    return _call(
        body, 55, name=name, out_shape=(out, out), grid=(f // tn, tp // tm),
        in_specs=[pl.BlockSpec((tm, d), lambda j, i: (i, 0)), pl.BlockSpec((tn, d), lambda j, i: (j, 0)), blk, blk],
        out_specs=[blk, blk],
        compiler_params=_params(("arbitrary", "arbitrary")),
    )(dyb, wd, g, u)


def _loss_grad(o, tgt, t_real, name):
    tp, d = o.shape
    tm = _tile(tp, ROW_TILE_CAP)

    def body(o_ref, t_ref, l_ref, d_ref):
        i = pl.program_id(0)

        @pl.when(i == 0)
        def _():
            l_ref[...] = jnp.zeros_like(l_ref)
        rows = i * tm + lax.broadcasted_iota(jnp.int32, (tm, 1), 0)
        real = jnp.logical_and(rows >= N_META, rows < t_real)
        err = jnp.where(real, o_ref[...] - t_ref[...], 0.0)
        d_ref[...] = err * (1.0 / d)
        l_ref[...] += jnp.sum(err * err, axis=0, keepdims=True) * (0.5 / d)

    row = lambda i: (i, 0)
    return pl.pallas_call(
        body, name=name, out_shape=(_sds((1, d), F32), _sds((tp, d), F32)), grid=(tp // tm,),
        in_specs=[pl.BlockSpec((tm, d), row), pl.BlockSpec((tm, d), row)],
        out_specs=[pl.BlockSpec((1, d), lambda i: (0, 0)), pl.BlockSpec((tm, d), row)],
        compiler_params=_params(("arbitrary",)),
    )(o, tgt)


CONV_TC = 256


def _silu_grad(y):
    s = jax.nn.sigmoid(y)
    return s * (1.0 + y * (1.0 - s))


def _conv_taps(x_ref, w, r0, rb):
    cur = x_ref[r0:r0 + rb, :]
    prev = x_ref[r0 - 8:r0, :] if r0 > 0 else jnp.zeros((8, cur.shape[1]), F32)
    xcat = jnp.concatenate([prev, cur], axis=0)
    taps = [xcat[5 + j:5 + j + rb] for j in range(CONV_K - 1)] + [cur]
    y = w[0:1] * taps[0]
    for j in range(1, CONV_K):
        y = y + w[j:j + 1] * taps[j]
    return y, taps


def _conv_fwd(proj, w, b, name):
    tp = proj.shape[0]
    rb = _tile(tp, ROW_TILE_CAP, 8)

    def body(x_ref, w_ref, b_ref, o_ref):
        gated = pl.program_id(0) > 0
        wv = w_ref[...]
        bv = b_ref[...]
        for r0 in range(0, tp, rb):
            y, _ = _conv_taps(x_ref, wv, r0, rb)
            y = y + bv
            o_ref[r0:r0 + rb, :] = jnp.where(gated, y * jax.nn.sigmoid(y), y)

    col = lambda j: (0, j)
    return _call(
        body, 25, name=name, out_shape=_sds((tp, CONV_W), F32), grid=(CONV_W // CONV_TC,),
        in_specs=[pl.BlockSpec((tp, CONV_TC), col), pl.BlockSpec((CONV_K, CONV_TC), col), pl.BlockSpec((1, CONV_TC), col)],
        out_specs=pl.BlockSpec((tp, CONV_TC), col),
        compiler_params=_params(("arbitrary",)),
    )(proj, w, b)


def _conv_bwd(proj, w, b, dxc, name):
    tp = proj.shape[0]
    rb = _tile(tp, ROW_TILE_CAP, 8)

    def body(x_ref, w_ref, b_ref, d_ref, dx_ref, dw_ref, db_ref, dy_scr):
        gated = pl.program_id(0) > 0
        wv = w_ref[...]
        bv = b_ref[...]
        dw = [jnp.zeros((1, CONV_TC), F32) for _ in range(CONV_K)]
        db = jnp.zeros((1, CONV_TC), F32)
        for r0 in range(0, tp, rb):
            y, taps = _conv_taps(x_ref, wv, r0, rb)
            y = y + bv
            d = d_ref[r0:r0 + rb, :]
            dy = jnp.where(gated, d * _silu_grad(y), d)
            dy_scr[r0:r0 + rb, :] = dy
            for j in range(CONV_K):
                dw[j] = dw[j] + jnp.sum(dy * taps[j], axis=0, keepdims=True)
            db = db + jnp.sum(dy, axis=0, keepdims=True)
        for j in range(CONV_K):
            dw_ref[j:j + 1, :] = dw[j]
        db_ref[...] = db
        for r0 in range(0, tp, rb):
            cur = dy_scr[r0:r0 + rb, :]
            nxt = dy_scr[r0 + rb:r0 + rb + 8, :] if r0 + rb < tp else jnp.zeros((8, CONV_TC), F32)
            dcat = jnp.concatenate([cur, nxt], axis=0)
            dx = wv[3:4] * cur
            for s in range(1, CONV_K):
                dx = dx + wv[3 - s:4 - s] * dcat[s:s + rb]
            dx_ref[r0:r0 + rb, :] = dx

    col = lambda j: (0, j)
    return _call(
        body, 70, name=name,
        out_shape=(_sds((tp, CONV_W), F32), _sds((CONV_K, CONV_W), F32), _sds((1, CONV_W), F32)),
        grid=(CONV_W // CONV_TC,),
        in_specs=[pl.BlockSpec((tp, CONV_TC), col), pl.BlockSpec((CONV_K, CONV_TC), col), pl.BlockSpec((1, CONV_TC), col),
                  pl.BlockSpec((tp, CONV_TC), col)],
        out_specs=[pl.BlockSpec((tp, CONV_TC), col), pl.BlockSpec((CONV_K, CONV_TC), col), pl.BlockSpec((1, CONV_TC), col)],
        scratch_shapes=[pltpu.VMEM((tp, CONV_TC), F32)],
        compiler_params=_params(("arbitrary",)),
    )(proj, w, b, dxc)


def _neg_expm1(x):
    series = -x * (1.0 + x * (0.5 + x * (1.0 / 6.0 + x * (1.0 / 24.0))))
    return jnp.where(jnp.abs(x) < 0.03, series, 1.0 - jnp.exp(x))


def _lru_gates(u, wa, ba, wx, bx, lam):
    r = jax.nn.sigmoid(jnp.dot(u, wa, precision=HI, preferred_element_type=F32) + ba)
    i = jax.nn.sigmoid(jnp.dot(u, wx, precision=HI, preferred_element_type=F32) + bx)
    log_a = -LRU_C * r * jax.nn.softplus(-lam)
    a = jnp.exp(log_a)
    b = jnp.sqrt(_neg_expm1(2.0 * log_a)) * (i * u)
    return a, b


def _lru_specs(tm, flip, n_tiles):
    idx = (lambda i: (n_tiles - 1 - i, 0)) if flip else (lambda i: (i, 0))
    return idx, lambda i: (0, 0)


def _lru_fwd(xc, proj, wa, ba, wx, bx, lam, name):
    tp = xc.shape[0]
    w = LRU_WIDTH
    tm = _tile(tp, ROW_TILE_CAP, 8)
    ycol = OFF_LRU_Y // w

    def body(u_ref, y_ref, wa_ref, ba_ref, wx_ref, bx_ref, lam_ref, o_ref, h_ref, a_scr, b_scr, carry):
        @pl.when(pl.program_id(0) == 0)
        def _():
            carry[...] = jnp.zeros_like(carry)
        a, b = _lru_gates(u_ref[...], wa_ref[...], ba_ref[...], wx_ref[...], bx_ref[...], lam_ref[...])
        a_scr[...] = a
        b_scr[...] = b

        def step(t, h):
            h = a_scr[pl.ds(t, 1), :] * h + b_scr[pl.ds(t, 1), :]
            h_ref[pl.ds(t, 1), :] = h
            return h

        carry[0:1, :] = lax.fori_loop(0, tm, step, carry[0:1, :])
        o_ref[...] = (h_ref[...] * jax.nn.gelu(y_ref[...])).astype(BF16)

    fix = lambda i: (0, 0)
    return _call(
        body, 40, name=name, out_shape=(_sds((tp, w), BF16), _sds((tp, w), F32)), grid=(tp // tm,),
        in_specs=[pl.BlockSpec((tm, w), lambda i: (i, 0)), pl.BlockSpec((tm, w), lambda i: (i, ycol)),
                  pl.BlockSpec((w, w), fix), pl.BlockSpec((1, w), fix), pl.BlockSpec((w, w), fix),
                  pl.BlockSpec((1, w), fix), pl.BlockSpec((1, w), fix)],
        out_specs=[pl.BlockSpec((tm, w), lambda i: (i, 0))] * 2,
        scratch_shapes=[pltpu.VMEM((tm, w), F32), pltpu.VMEM((tm, w), F32), pltpu.VMEM((8, w), F32)],
        compiler_params=_params(("arbitrary",)),
    )(xc, proj, wa, ba, wx, bx, lam)


def _lru_bwd(xc, proj, h, dout, wa, ba, wx, bx, lam, name):
    tp = xc.shape[0]
    w = LRU_WIDTH
    tm = _tile(tp, ROW_TILE_CAP, 8)
    nt = tp // tm
    ycol = OFF_LRU_Y // w
    rev = lambda i: (nt - 1 - i, 0)
    prev8 = lambda i: (jnp.maximum((nt - 1 - i) * (tm // 8) - 1, 0), 0)

    def body(u_ref, y_ref, h_ref, hp_ref, do_ref, wa_ref, ba_ref, wx_ref, bx_ref, lam_ref,
             du_ref, dy_ref, dwa_ref, dba_ref, dwx_ref, dbx_ref, dlam_ref,
             a_scr, dh_scr, g_scr, da_scr, hext, carry):
        i = pl.program_id(0)

        @pl.when(i == 0)
        def _():
            carry[...] = jnp.zeros_like(carry)
            for r in (dwa_ref, dba_ref, dwx_ref, dbx_ref, dlam_ref):
                r[...] = jnp.zeros_like(r)
        params = (wa_ref[...], ba_ref[...], wx_ref[...], bx_ref[...], lam_ref[...])
        (a, _), gates_vjp = jax.vjp(_lru_gates, u_ref[...], *params)
        gel, gelu_vjp = jax.vjp(jax.nn.gelu, y_ref[...])
        do = do_ref[...]
        hv = h_ref[...]
        dy_ref[...] = gelu_vjp(do * hv)[0]
        a_scr[...] = a
        dh_scr[...] = do * gel
        hext[0:8, :] = jnp.where(i == nt - 1, 0.0, hp_ref[...])
        hext[8:8 + tm, :] = hv

        def step(s, c):
            t = tm - 1 - s
            g = dh_scr[pl.ds(t, 1), :] + c
            g_scr[pl.ds(t, 1), :] = g
            da_scr[pl.ds(t, 1), :] = g * hext[pl.ds(t + 7, 1), :]
            return a_scr[pl.ds(t, 1), :] * g

        carry[0:1, :] = lax.fori_loop(0, tm, step, carry[0:1, :])
        du, dwa, dba, dwx, dbx, dlam = gates_vjp((da_scr[...], g_scr[...]))
        du_ref[...] = du
        dwa_ref[...] += dwa
        dba_ref[...] += dba
        dwx_ref[...] += dwx
        dbx_ref[...] += dbx
        dlam_ref[...] += dlam

    fix = lambda i: (0, 0)
    tile = pl.BlockSpec((tm, w), rev)
    mat = pl.BlockSpec((w, w), fix)
    vec = pl.BlockSpec((1, w), fix)
    return _call(
        body, 90, name=name,
        out_shape=(_sds((tp, w), F32), _sds((tp, w), F32), _sds((w, w), F32), _sds((1, w), F32), _sds((w, w), F32),
                   _sds((1, w), F32), _sds((1, w), F32)),
        grid=(nt,),
        in_specs=[tile, pl.BlockSpec((tm, w), lambda i: (nt - 1 - i, ycol)), tile, pl.BlockSpec((8, w), prev8), tile,
                  mat, vec, mat, vec, vec],
        out_specs=[tile, tile, mat, vec, mat, vec, vec],
        scratch_shapes=[pltpu.VMEM((tm, w), F32)] * 4 + [pltpu.VMEM((tm + 8, w), F32), pltpu.VMEM((8, w), F32)],
        compiler_params=_params(("arbitrary",)),
    )(xc, proj, h, h, dout, wa, ba, wx, bx, lam)


def _dot3(a, b, dims):
    ah = a.astype(BF16)
    al = (a - ah.astype(F32)).astype(BF16)
    bh = b.astype(BF16)
    bl = (b - bh.astype(F32)).astype(BF16)
    dot = lambda x, y: lax.dot_general(x, y, (dims, ((0,), (0,))), preferred_element_type=F32)
    return dot(ah, bh) + (dot(ah, bl) + dot(al, bh))


@jax.custom_vjp
def _bmm(a, b):
    return _dot3(a, b, ((2,), (1,)))


@jax.custom_vjp
def _bmm_nt(a, b):
    return _dot3(a, b, ((2,), (2,)))


@jax.custom_vjp
def _bmm_tn(a, b):
    return _dot3(a, b, ((1,), (1,)))


_bmm.defvjp(lambda a, b: (_bmm(a, b), (a, b)), lambda r, g: (_bmm_nt(g, r[1]), _bmm_tn(r[0], g)))
_bmm_nt.defvjp(lambda a, b: (_bmm_nt(a, b), (a, b)), lambda r, g: (_bmm(g, r[1]), _bmm_tn(g, r[0])))
_bmm_tn.defvjp(lambda a, b: (_bmm_tn(a, b), (a, b)), lambda r, g: (_bmm_nt(r[1], g), _bmm(r[0], g)))


def _dot1(a, b, dims):
    return lax.dot_general(a.astype(BF16), b.astype(BF16), (dims, ((0,), (0,))), preferred_element_type=F32)


@jax.custom_vjp
def _bmm1(a, b):
    return _dot1(a, b, ((2,), (1,)))


@jax.custom_vjp
def _bmm1_nt(a, b):
    return _dot1(a, b, ((2,), (2,)))


@jax.custom_vjp
def _bmm1_tn(a, b):
    return _dot1(a, b, ((1,), (1,)))


_bmm1.defvjp(lambda a, b: (_bmm1(a, b), (a, b)), lambda r, g: (_bmm1_nt(g, r[1]), _bmm1_tn(r[0], g)))
_bmm1_nt.defvjp(lambda a, b: (_bmm1_nt(a, b), (a, b)), lambda r, g: (_bmm1(g, r[1]), _bmm1_tn(g, r[0])))
_bmm1_tn.defvjp(lambda a, b: (_bmm1_tn(a, b), (a, b)), lambda r, g: (_bmm1_nt(r[1], g), _bmm1(r[0], g)))


def _chunk_masks(nh):
    r = lax.broadcasted_iota(jnp.int32, (CHUNK, CHUNK), 0)
    c = lax.broadcasted_iota(jnp.int32, (CHUNK, CHUNK), 1)
    full = lambda m: jnp.broadcast_to(m[None], (nh, CHUNK, CHUNK))
    return r, c, full


def _decay_terms(g, nh):
    r, c, full = _chunk_masks(nh)
    incl = r >= c
    cs = _bmm(full(incl.astype(F32)), g)
    cs_t = jnp.swapaxes(cs, 1, 2)
    tot = jnp.broadcast_to(jnp.sum(g, axis=1, keepdims=True), g.shape)
    m = full(incl)
    decay = jnp.where(m, jnp.exp(jnp.where(m, cs - cs_t, 0.0)), 0.0)
    return cs, decay, tot


def _rep_groups(x):
    return jnp.concatenate([jnp.broadcast_to(x[g:g + 1], (SSD_HPG,) + x.shape[1:]) for g in range(SSD_GROUPS)], axis=0)


def _ssd_chunk(xs, bm, cm, dtr, z, p_dtb, p_alog, p_d, p_nw, state):
    nh = SSD_HEADS
    dt = jax.nn.softplus(dtr + p_dtb)
    a = dt * (-jnp.exp(p_alog))
    x = xs * dt
    cs, decay, tot = _decay_terms(a, nh)
    b8 = _rep_groups(bm)
    c8 = _rep_groups(cm)
    y = _bmm1(_rep_groups(_bmm1_nt(cm, bm)) * decay, x)
    y = y + _bmm1_nt(c8, state) * jnp.exp(cs)
    new_state = state * jnp.exp(tot) + _bmm1_tn(x * jnp.exp(tot - cs), b8)
    y = y + p_d * xs
    y = y * (z * jax.nn.sigmoid(z))
    ss = jnp.sum(y * y, axis=-1, keepdims=True)
    ssg = jnp.concatenate(
        [jnp.broadcast_to(jnp.sum(ss[g * SSD_HPG:(g + 1) * SSD_HPG], axis=0, keepdims=True), (SSD_HPG, CHUNK, 1))
         for g in range(SSD_GROUPS)], axis=0)
    y = y * lax.rsqrt(ssg * (1.0 / (SSD_HPG * HEAD)) + RMS_EPS) * p_nw
    return y, new_state


@jax.custom_vjp
def _unit_lower_inverse(m):
    r, c, full = _chunk_masks(m.shape[0])
    eye = full((r == c).astype(F32))
    md = jnp.where(full((r // 16) == (c // 16)), m, 0.0)
    mo = m - md
    x = eye - md
    p = _bmm(md, md)
    x = x + _bmm(x, p)
    p = _bmm(p, p)
    x = x + _bmm(x, p)
    p = _bmm(p, p)
    x = x + _bmm(x, p)
    n = _bmm(x, mo)
    y = x - _bmm(n, x)
    return y + _bmm(_bmm(n, n), y)


def _unit_lower_inverse_fwd(m):
    t = _unit_lower_inverse(m)
    return t, t


_unit_lower_inverse.defvjp(_unit_lower_inverse_fwd, lambda t, g: (-_bmm_nt(_bmm_tn(t, g), t),))


def _dn_chunk(q, k, v, gate, braw, araw, p_alog, p_dtb, p_nw, state):
    nh = DN_HEADS
    r, c, full = _chunk_masks(nh)
    q = q * lax.rsqrt(jnp.sum(q * q, axis=-1, keepdims=True) + RMS_EPS) * (HEAD ** -0.5)
    k = k * lax.rsqrt(jnp.sum(k * k, axis=-1, keepdims=True) + RMS_EPS)
    beta = jax.nn.sigmoid(braw)
    g = -jnp.exp(p_alog) * jax.nn.softplus(araw + p_dtb)
    gcs, decay, tot = _decay_terms(g, nh)
    kb = k * beta
    vb = v * beta
    m = jnp.where(full(r > c), _bmm1_nt(kb, k) * decay, 0.0)
    t = _unit_lower_inverse(m)
    egcs = jnp.exp(gcs)
    u = _bmm(t, vb)
    w = _bmm(t, kb * egcs)
    attn = _bmm1_nt(q, k) * decay
    v_new = u - _bmm1(w, state)
    out = _bmm1(q * egcs, state) + _bmm1(attn, v_new)
    new_state = state * jnp.exp(tot) + _bmm1_tn(k * jnp.exp(tot - gcs), v_new)
    out = out * lax.rsqrt(jnp.mean(out * out, axis=-1, keepdims=True) + RMS_EPS) * p_nw
    return out * (gate * jax.nn.sigmoid(gate)), new_state


def _chunks_per_step(nc):
    return max(n for n in range(1, 6) if nc % n == 0)


def _heads(t, n):
    return jnp.stack([t[:, HEAD * h:HEAD * (h + 1)] for h in range(n)])


def _unheads(a):
    return jnp.concatenate([a[h] for h in range(a.shape[0])], axis=-1)


def _head_scalars(t, off, n):
    return jnp.stack([jnp.broadcast_to(t[:, off + h:off + h + 1], (CHUNK, HEAD)) for h in range(n)])


def _unhead_scalars(d, off):
    red = jnp.sum(d, axis=-1, keepdims=True)
    lane = lax.broadcasted_iota(jnp.int32, (CHUNK, LANE), 1)
    out = jnp.zeros((CHUNK, LANE), F32)
    for h in range(d.shape[0]):
        out = out + jnp.where(lane == off + h, red[h], 0.0)
    return out


def _chunk_scan_fwd(chunk_fn, srcs, read, params, nh, host_us, name):
    tp = srcs[0][0].shape[0]
    nc = tp // CHUNK
    nb = _chunks_per_step(nc)
    rows = nb * CHUNK
    ns, npar = len(srcs), len(params)

    def body(*refs):
        s_refs, p_refs = refs[:ns], refs[ns:ns + npar]
        y_ref, st_ref, state = refs[ns + npar:]

        @pl.when(pl.program_id(0) == 0)
        def _():
            state[...] = jnp.zeros_like(state)
        par = [r[...] for r in p_refs]
        st = state[...]
        for k in range(nb):
            sl = slice(k * CHUNK, (k + 1) * CHUNK)
            st_ref[k] = st
            y, st = chunk_fn(*read(s_refs, sl), *par, st)
            y_ref[sl, :] = _unheads(y).astype(BF16)
        state[...] = st

    src_spec = lambda s: pl.BlockSpec((rows, s[1]), lambda c: (c, s[2]))
    par_spec = lambda a: pl.BlockSpec(a.shape, lambda c: (0, 0, 0))
    return _call(
        body, host_us, name=name,
        out_shape=(_sds((tp, nh * HEAD), BF16), _sds((nc, nh, HEAD, HEAD), F32)),
        grid=(nc // nb,),
        in_specs=[src_spec(s) for s in srcs] + [par_spec(a) for a in params],
        out_specs=[pl.BlockSpec((rows, nh * HEAD), lambda c: (c, 0)),
                   pl.BlockSpec((nb, nh, HEAD, HEAD), lambda c: (c, 0, 0, 0))],
        scratch_shapes=[pltpu.VMEM((nh, HEAD, HEAD), F32)],
        compiler_params=_params(("arbitrary",)),
    )(*[s[0] for s in srcs], *params)


def _chunk_scan_bwd(chunk_fn, srcs, read, params, states, dsrc, write, out_widths, nh, host_us, name):
    tp = srcs[0][0].shape[0]
    nc = tp // CHUNK
    nb = _chunks_per_step(nc)
    rows = nb * CHUNK
    steps = nc // nb
    ns, npar, nout = len(srcs), len(params), len(out_widths)
    d_arr, d_off = dsrc

    def body(*refs):
        s_refs, p_refs = refs[:ns], refs[ns:ns + npar]
        st_ref, dy_ref = refs[ns + npar:ns + npar + 2]
        o_refs = refs[ns + npar + 2:ns + npar + 2 + nout]
        dp_refs = refs[ns + npar + 2 + nout:ns + 2 * npar + 2 + nout]
        dstate = refs[-1]

        @pl.when(pl.program_id(0) == 0)
        def _():
            dstate[...] = jnp.zeros_like(dstate)
            for r in dp_refs:
                r[...] = jnp.zeros_like(r)
        par = [r[...] for r in p_refs]
        dst = dstate[...]
        dpar = None
        for k in reversed(range(nb)):
            sl = slice(k * CHUNK, (k + 1) * CHUNK)
            seqs = read(s_refs, sl)
            _, vjp = jax.vjp(chunk_fn, *seqs, *par, st_ref[k])
            grads = vjp((_heads(dy_ref[sl, d_off:d_off + nh * HEAD], nh), dst))
            for r, tile in zip(o_refs, write(*grads[:len(seqs)])):
                r[sl, :] = tile
            gp = grads[len(seqs):len(seqs) + npar]
            dpar = gp if dpar is None else [a + b for a, b in zip(dpar, gp)]
            dst = grads[-1]
        for r, gr in zip(dp_refs, dpar):
            r[...] += gr
        dstate[...] = dst

    rev = lambda c: steps - 1 - c
    src_spec = lambda s: pl.BlockSpec((rows, s[1]), lambda c: (rev(c), s[2]))
    par_spec = lambda a: pl.BlockSpec(a.shape, lambda c: (0, 0, 0))
    out_spec = lambda w: pl.BlockSpec((rows, w), lambda c: (rev(c), 0))
    return tuple(_call(
        body, host_us, name=name,
        out_shape=tuple(_sds((tp, w), F32) for w in out_widths) + tuple(_sds(a.shape, F32) for a in params),
        grid=(steps,),
        in_specs=[src_spec(s) for s in srcs] + [par_spec(a) for a in params]
        + [pl.BlockSpec((nb, nh, HEAD, HEAD), lambda c: (rev(c), 0, 0, 0)), out_spec(d_arr.shape[1])],
        out_specs=[out_spec(w) for w in out_widths] + [par_spec(a) for a in params],
        scratch_shapes=[pltpu.VMEM((nh, HEAD, HEAD), F32)],
        compiler_params=_params(("arbitrary",)),
    )(*[s[0] for s in srcs], *params, states, d_arr))


def _lane_param(p):
    return jnp.broadcast_to(p[:, None, None], (p.shape[0], 1, HEAD))


def _block_diag(w):
    out = jnp.zeros((LRU_WIDTH, LRU_WIDTH), F32)
    for h in range(LRU_HEADS):
        out = out.at[h * HEAD:(h + 1) * HEAD, h * HEAD:(h + 1) * HEAD].set(w[h])
    return out


def _block_diag_inv(w):
    return jnp.stack([w[h * HEAD:(h + 1) * HEAD, h * HEAD:(h + 1) * HEAD] for h in range(LRU_HEADS)])


def _ssd_inputs(xc, proj, lp):
    srcs = ((xc, CONV_W, 0), (proj, SSD_INNER, OFF_SSD_Z // SSD_INNER), (proj, LANE, OFF_SMALL // LANE))
    params = (_lane_param(lp["ssd_dt_bias"]), _lane_param(lp["ssd_a_log"]), _lane_param(lp["ssd_d"]),
              lp["ssd_norm_w"].reshape(SSD_HEADS, 1, HEAD))
    return srcs, params


def _ssd_read(refs, rows):
    xc, z, sm = refs
    return (_heads(xc[rows, 256:768], SSD_HEADS), _heads(xc[rows, 768:896], SSD_GROUPS),
            _heads(xc[rows, 896:1024], SSD_GROUPS), _head_scalars(sm[rows, :], 0, SSD_HEADS), _heads(z[rows, :], SSD_HEADS))


def _ssd_write(dxs, dbm, dcm, ddtr, dz):
    return (jnp.concatenate([_unheads(dxs), _unheads(dbm), _unheads(dcm)], axis=-1), _unheads(dz), _unhead_scalars(ddtr, 0))


def _dn_inputs(xc, proj, lp):
    srcs = ((xc, CONV_W, 0), (proj, 256, OFF_DN_GATE // 256), (proj, LANE, OFF_SMALL // LANE))
    params = (_lane_param(lp["dn_a_log"]), _lane_param(lp["dn_dt_bias"]),
              jnp.broadcast_to(lp["dn_norm_w"][None, None, :], (DN_HEADS, 1, HEAD)))
    return srcs, params


def _dn_read(refs, rows):
    xc, gate, sm = refs
    return (_heads(xc[rows, 1024:1280], DN_HEADS), _heads(xc[rows, 1280:1536], DN_HEADS),
            _heads(xc[rows, 1536:1792], DN_HEADS), _heads(gate[rows, :], DN_HEADS),
            _head_scalars(sm[rows, :], 8, DN_HEADS), _head_scalars(sm[rows, :], 12, DN_HEADS))


def _dn_write(dq, dk, dv, dgate, dbraw, daraw):
    return (jnp.concatenate([_unheads(dq), _unheads(dk), _unheads(dv)], axis=-1), _unheads(dgate),
            _unhead_scalars(dbraw, 8) + _unhead_scalars(daraw, 12))


def _lru_params(lp):
    return (_block_diag(lp["lru_w_a"]), lp["lru_b_a"][None], _block_diag(lp["lru_w_x"]), lp["lru_b_x"][None],
            lp["lru_lambda"][None])


def _conv_params(lp):
    w = jnp.concatenate([lp["lru_conv_w"], lp["ssd_conv_w"], lp["dn_conv_w"]], axis=1)
    b = jnp.concatenate([lp["lru_conv_b"], lp["ssd_conv_b"], jnp.zeros((768,), F32)])[None]
    return w, b


def _mixers_fwd(proj, lp, tag):
    cw, cb = _conv_params(lp)
    xc = _conv_fwd(proj, cw, cb, tag + "_conv")
    o_lru, h_lru = _lru_fwd(xc, proj, *_lru_params(lp), name=tag + "_lru")
    s_srcs, s_par = _ssd_inputs(xc, proj, lp)
    o_ssd, s_states = _chunk_scan_fwd(_ssd_chunk, s_srcs, _ssd_read, s_par, SSD_HEADS, 70, tag + "_ssd")
    d_srcs, d_par = _dn_inputs(xc, proj, lp)
    o_dn, d_states = _chunk_scan_fwd(_dn_chunk, d_srcs, _dn_read, d_par, DN_HEADS, 140, tag + "_dn")
    cat = jnp.concatenate([o_lru, o_ssd, o_dn], axis=1)
    return cat, (xc, h_lru, s_states, d_states)


def _mixers_bwd(proj, lp, saved, dcat, tag):
    xc, h_lru, s_states, d_states = saved
    tp = proj.shape[0]
    cw, cb = _conv_params(lp)
    g = {}
    du, dyraw, dwa, dba, dwx, dbx, dlam = _lru_bwd(xc, proj, h_lru, dcat, *_lru_params(lp), name=tag + "_lru_bwd")
    g["lru_w_a"], g["lru_b_a"], g["lru_w_x"], g["lru_b_x"], g["lru_lambda"] = (
        _block_diag_inv(dwa), dba[0], _block_diag_inv(dwx), dbx[0], dlam[0])

    s_srcs, s_par = _ssd_inputs(xc, proj, lp)
    dxbc, dz, dsm_ssd, dp_dtb, dp_alog, dp_d, dp_nw = _chunk_scan_bwd(
        _ssd_chunk, s_srcs, _ssd_read, s_par, s_states, (dcat, 256), _ssd_write, (768, SSD_INNER, LANE), SSD_HEADS, 125,
        tag + "_ssd_bwd")
    g["ssd_dt_bias"], g["ssd_a_log"], g["ssd_d"] = (jnp.sum(p, axis=(1, 2)) for p in (dp_dtb, dp_alog, dp_d))
    g["ssd_norm_w"] = dp_nw.reshape(SSD_INNER)

    d_srcs, d_par = _dn_inputs(xc, proj, lp)
    dqkv, dgate, dsm_dn, dq_alog, dq_dtb, dq_nw = _chunk_scan_bwd(
        _dn_chunk, d_srcs, _dn_read, d_par, d_states, (dcat, 768), _dn_write, (768, 256, LANE), DN_HEADS, 380,
        tag + "_dn_bwd")
    g["dn_a_log"], g["dn_dt_bias"] = (jnp.sum(p, axis=(1, 2)) for p in (dq_alog, dq_dtb))
    g["dn_norm_w"] = jnp.sum(dq_nw, axis=(0, 1))

    dconv, dcw, dcb = _conv_bwd(proj, cw, cb, jnp.concatenate([du, dxbc, dqkv], axis=1), tag + "_conv_bwd")
    g["lru_conv_w"], g["ssd_conv_w"], g["dn_conv_w"] = dcw[:, :256], dcw[:, 256:1024], dcw[:, 1024:]
    g["lru_conv_b"], g["ssd_conv_b"] = dcb[0, :256], dcb[0, 256:1024]
    dproj = jnp.concatenate([dconv, dyraw, dz, dgate, dsm_ssd + dsm_dn], axis=1)
    return dproj, g


MIXER_PARAMS = ("lru_conv_w", "lru_conv_b", "lru_w_a", "lru_b_a", "lru_w_x", "lru_b_x", "lru_lambda",
                "ssd_conv_w", "ssd_conv_b", "ssd_dt_bias", "ssd_a_log", "ssd_d", "ssd_norm_w",
                "dn_conv_w", "dn_a_log", "dn_dt_bias", "dn_norm_w")


UNITS = ("gate0", "up0", "down0", "win", "wout", "gate1", "up1", "down1")


def _layer_shards(w, l):
    out = {"win": w["w_in"][l], "wout": w["w_out"][l]}
    for j in range(2):
        out[f"gate{j}"], out[f"up{j}"], out[f"down{j}"] = w["ffn_w_gate"][l, j], w["ffn_w_up"][l, j], w["ffn_w_down"][l, j]
    return {k: a.astype(BF16) for k, a in out.items()}


def _enqueue_matrix(name, arr, slab):
    if name == "wout" or not slab:
        return [_enqueue(arr, slab)], 1
    return _enqueue_halves(arr, slab, 1)


def _enqueue_layer(shards):
    return {k: _enqueue_matrix(k, shards[k], False) for k in UNITS}


def _gathered_cols(a):
    return a.transpose(1, 0, 2).reshape(a.shape[1], -1)


def _col_slabs(a):
    return a.reshape(a.shape[0], N_DEV, -1).transpose(1, 0, 2).astype(BF16)


def _row_slabs(a):
    return a.reshape(N_DEV, -1, a.shape[1]).astype(BF16)


def _local_step(x, tgt, small, first, shards):
    s = x.shape[0]
    t_real = N_META + s
    tp = -(-t_real // CHUNK) * CHUNK
    depth = len(first) if shards is None else len(shards)
    h = jnp.concatenate([small["meta"], x, jnp.zeros((tp - t_real, D_MODEL), F32)], axis=0)
    hb = h.astype(BF16)
    ln_g, ln_b = small["ln_g"], small["ln_b"]
    saved, weights = [], []
    queued = first
    for l in range(depth):
        lp = {k: small[k][l] for k in MIXER_PARAMS}
        t = f"l{l}"
        if shards is None:
            get = lambda k, g=first[l]: g[k]
        else:
            get = lambda k, g=queued: _collect_halves(g[k])
            if l + 1 < depth:
                queued = _enqueue_layer(shards[l + 1])

        def ffn_weights(j):
            up = _gathered_cols(get(f"up{j}"))
            return jnp.concatenate([_gathered_cols(get(f"gate{j}")), up], axis=-1)

        wgu0 = ffn_weights(0)
        g0, u0, a0 = _ffn_up(hb, wgu0, t + "_ffn0_up")
        wd0 = get("down0").reshape(-1, D_MODEL)
        y1, h1, h1b = _mm_resid_ln(a0, wd0, h, ln_g[l, 0][None], ln_b[l, 0][None], FFN_RES, t + "_ffn0_down")
        win = _proj_cols(_gathered_cols(get("win")))
        proj = _mm_nn(h1b, win, t + "_in_proj")
        cat, mix_saved = _mixers_fwd(proj, lp, t)
        catb = cat.astype(BF16)
        wout = get("wout").reshape(D_MODEL, D_MODEL)
        y2, h2, h2b = _mm_resid_ln(catb, wout, h1, ln_g[l, 1][None], ln_b[l, 1][None], 1.0, t + "_out_proj")
        wgu1 = ffn_weights(1)
        g1, u1, a1 = _ffn_up(h2b, wgu1, t + "_ffn1_up")
        wd1 = get("down1").reshape(-1, D_MODEL)
        y3, h3, h3b = _mm_resid_ln(a1, wd1, h2, ln_g[l, 2][None], ln_b[l, 2][None], FFN_RES, t + "_ffn1_down")
        weights.append(([wgu0, wgu1], [wd0, wd1], win, wout))
        saved.append((hb, g0, u0, a0, y1, h1b, proj, mix_saved, catb, y2, h2b, g1, u1, a1, y3))
        h, hb = h3, h3b

    tgt_p = jnp.pad(tgt, ((N_META, tp - t_real), (0, 0)))
    lossv, dh = _loss_grad(h, tgt_p, t_real, "loss")
    loss = jnp.sum(lossv)

    gs = {k: [None] * depth for k in MIXER_PARAMS}
    d_ln_g = [[None] * 3 for _ in range(depth)]
    d_ln_b = [[None] * 3 for _ in range(depth)]
    slabs = [{} for _ in range(depth)]
    send = (lambda k, a: a) if shards is None else (lambda k, a: _enqueue_matrix(k, a, True))

    def ffn_bwd(l, j, xb_in, g, u, a, y, dout):
        t = f"l{l}_ffn{j}"
        wgu, wd = weights[l][0][j], weights[l][1][j]
        dres, dyb, dgam, dbet = _ln_bwd(y, dout, ln_g[l, 2 * j][None], FFN_RES, t + "_ln_bwd")
        d_ln_g[l][2 * j], d_ln_b[l][2 * j] = dgam[0], dbet[0]
        slabs[l][f"down{j}"] = send("down", _row_slabs(_mm_tn(a, dyb, t + "_dwd")))
        dg, du = _ffn_dact(dyb, wd, g, u, t + "_dact")
        slabs[l][f"gate{j}"] = send("gate", _col_slabs(_mm_tn(xb_in, dg, t + "_dwg")))
        slabs[l][f"up{j}"] = send("up", _col_slabs(_mm_tn(xb_in, du, t + "_dwu")))
        return _ffn_dx(dg, du, wgu, dres, t + "_dx")

    for l in reversed(range(depth)):
        hb_in, g0, u0, a0, y1, h1b, proj, mix_saved, catb, y2, h2b, g1, u1, a1, y3 = saved[l]
        lp = {k: small[k][l] for k in MIXER_PARAMS}
        t = f"l{l}"
        _, _, win, wout = weights[l]
        dh2 = ffn_bwd(l, 1, h2b, g1, u1, a1, y3, dh)
        dres, dyb, dgam, dbet = _ln_bwd(y2, dh2, ln_g[l, 1][None], 1.0, t + "_mix_ln_bwd")
        d_ln_g[l][1], d_ln_b[l][1] = dgam[0], dbet[0]
        slabs[l]["wout"] = send("wout", _row_slabs(_mm_tn(catb, dyb, t + "_dwout")))
        dcat = _mm_nt_add(dyb, wout, None, t + "_dcat")
        dproj, mg = _mixers_bwd(proj, lp, mix_saved, dcat, t)
        for k in MIXER_PARAMS:
            gs[k][l] = mg[k]
        dprojb = dproj.astype(BF16)
        slabs[l]["win"] = send("win", _col_slabs(_proj_cols_inv(_mm_tn(h1b, dprojb, t + "_dwin"))))
        dh1 = _mm_nt_add(dprojb, win, dres, t + "_dh1")
        dh = ffn_bwd(l, 0, hb_in, g0, u0, a0, y1, dh1)

    small_grads = {k: jnp.stack(v) for k, v in gs.items()}
    small_grads["ln_g"] = jnp.stack([jnp.stack(r) for r in d_ln_g])
    small_grads["ln_b"] = jnp.stack([jnp.stack(r) for r in d_ln_b])
    small_grads["meta"] = dh[:N_META]
    return loss, dh[N_META:t_real], small_grads, slabs


def _mesh_pos():
    return lax.axis_index("x"), lax.axis_index("y"), lax.axis_index("c")


def _flip(pos, k):
    x, y, c = pos
    return (1 - x if k & 4 else x, 1 - y if k & 2 else y, 1 - c if k & 1 else c)


def _flat(pos):
    return 4 * pos[0] + 2 * pos[1] + pos[2]


def _exchange_copies(ins, outs, slabs, send, recv, loc):
    pos = _mesh_pos()
    me = _flat(pos)
    local, sends, receives = [], [], []
    for a in range(len(ins)):
        local.append(pltpu.make_async_copy(ins[a].at[me] if slabs[a] else ins[a], outs[a].at[me], loc.at[a]))
    for k in range(1, N_DEV):
        peer = _flip(pos, k)
        for a in range(len(ins)):
            if not slabs[a] and k not in GATHER_FIRST:
                continue
            sem = dict(send_sem=send.at[a, k - 1], recv_sem=recv.at[a, k - 1], device_id=peer, device_id_type=MESH_IDS)
            sends.append(pltpu.make_async_remote_copy(
                src_ref=ins[a].at[_flat(peer)] if slabs[a] else ins[a], dst_ref=outs[a].at[me], **sem))
            receives.append(pltpu.make_async_remote_copy(
                src_ref=ins[a].at[me] if slabs[a] else ins[a], dst_ref=outs[a].at[_flat(peer)], **sem))
    return local, sends, receives


GATHER_FIRST = (1, 2, 4, 6)


def _gather_forwards(ins, outs, slabs, send, recv):
    pos = _mesh_pos()
    sibling = _flip(pos, 1)
    forwards, arrivals = [], []
    for a in range(len(ins)):
        if slabs[a]:
            continue
        for k in GATHER_FIRST[1:]:
            sem = dict(send_sem=send.at[a, k], recv_sem=recv.at[a, k], device_id=sibling, device_id_type=MESH_IDS)
            mine, theirs = _flat(_flip(pos, k)), _flat(_flip(pos, k + 1))
            forwards.append(pltpu.make_async_remote_copy(src_ref=outs[a].at[mine], dst_ref=outs[a].at[mine], **sem))
            arrivals.append(pltpu.make_async_remote_copy(src_ref=outs[a].at[theirs], dst_ref=outs[a].at[theirs], **sem))
    return forwards, arrivals


def _exchange_start(ins, outs, slabs, send, recv, loc):
    local, sends, _ = _exchange_copies(ins, outs, slabs, send, recv, loc)
    for cp in local + sends:
        cp.start()


def _exchange_wait(ins, outs, slabs, send, recv, loc):
    local, sends, receives = _exchange_copies(ins, outs, slabs, send, recv, loc)
    forwards, arrivals = _gather_forwards(ins, outs, slabs, send, recv)
    for cp in receives:
        cp.wait_recv()
    for cp in forwards:
        cp.start()
    for cp in arrivals:
        cp.wait_recv()
    for cp in sends + forwards:
        cp.wait_send()
    for cp in local:
        cp.wait()


def _exchange_shapes(arrs, slabs):
    return tuple(_sds(a.shape if s else (N_DEV,) + a.shape, a.dtype) for a, s in zip(arrs, slabs))


def _exchange_sems(n):
    return [pltpu.SemaphoreType.DMA((n, N_DEV - 1)), pltpu.SemaphoreType.DMA((n, N_DEV - 1)),
            pltpu.SemaphoreType.DMA((n,))]


def _exchange(arrs, slabs, name):
    n = len(arrs)

    def body(*refs):
        ins, outs, sems = refs[:n], refs[n:2 * n], refs[2 * n:]
        _exchange_start(ins, outs, slabs, *sems)
        _exchange_wait(ins, outs, slabs, *sems)

    hbm = pl.BlockSpec(memory_space=pl.ANY)
    return pl.pallas_call(
        body, name=name, out_shape=_exchange_shapes(arrs, slabs), in_specs=[hbm] * n, out_specs=[hbm] * n,
        scratch_shapes=_exchange_sems(n),
    )(*arrs)


def _adam_math(w, g, m, v):
    m = ADAM_B1 * m + (1.0 - ADAM_B1) * g
    v = ADAM_B2 * v + (1.0 - ADAM_B2) * (g * g)
    m_hat = m / (1.0 - ADAM_B1 ** ADAM_STEP)
    v_hat = v / (1.0 - ADAM_B2 ** ADAM_STEP)
    delta = -ADAM_LR * (m_hat / (jnp.sqrt(v_hat) + ADAM_EPS) + ADAM_WD * w)
    return delta, m, v


def _adam(w, g, m, v, name):
    r, c = w.shape
    parts = g.ndim == 3
    tr = _tile(r, 512, 8)

    def body(w_ref, g_ref, m_ref, v_ref, go_ref, d_ref, mo_ref, vo_ref):
        if parts:
            gv = g_ref[0].astype(F32)
            for d in range(1, N_DEV):
                gv = gv + g_ref[d].astype(F32)
        else:
            gv = g_ref[...]
        delta, mn, vn = _adam_math(w_ref[...], gv, m_ref[...], v_ref[...])
        go_ref[...] = gv
        d_ref[...] = delta
        mo_ref[...] = mn
        vo_ref[...] = vn

    blk = pl.BlockSpec((tr, c), lambda i: (i, 0))
    gblk = pl.BlockSpec((N_DEV, tr, c), lambda i: (0, i, 0)) if parts else blk
    out = _sds((r, c), F32)
    return pl.pallas_call(
        body, name=name, out_shape=(out,) * 4, grid=(r // tr,),
        in_specs=[blk, gblk, blk, blk], out_specs=[blk] * 4,
        compiler_params=_params(("arbitrary",)),
    )(w, g, m, v)


def _sum_parts(parts, name):
    _, r, c = parts.shape

    def body(p_ref, o_ref):
        acc = p_ref[0]
        for d in range(1, N_DEV):
            acc = acc + p_ref[d]
        o_ref[...] = acc

    return pl.pallas_call(body, name=name, out_shape=_sds((r, c), F32), compiler_params=_params())(parts)


SMALL_SHARD_AXIS = {
    "meta": 1, "ln_g": 2, "ln_b": 2, "lru_conv_w": 2, "lru_conv_b": None, "lru_w_a": None, "lru_b_a": None,
    "lru_w_x": None, "lru_b_x": None, "lru_lambda": None, "ssd_conv_w": 2, "ssd_conv_b": None, "ssd_dt_bias": None,
    "ssd_a_log": None, "ssd_d": None, "ssd_norm_w": None, "dn_conv_w": 2, "dn_a_log": None, "dn_dt_bias": None,
    "dn_norm_w": None,
}
BIG = ("ffn_w_gate", "ffn_w_up", "ffn_w_down", "w_in", "w_out")
WEIGHT_ORDER = ("meta", "ln_g", "ln_b", "ffn_w_gate", "ffn_w_up", "ffn_w_down", "w_in", "lru_conv_w", "lru_conv_b",
                "lru_w_a", "lru_b_a", "lru_w_x", "lru_b_x", "lru_lambda", "ssd_conv_w", "ssd_conv_b", "ssd_dt_bias",
                "ssd_a_log", "ssd_d", "ssd_norm_w", "dn_conv_w", "dn_a_log", "dn_dt_bias", "dn_norm_w", "w_out")


def _pack(arrs):
    flat = jnp.concatenate([a.reshape(-1) for a in arrs])
    rows = -(-flat.shape[0] // (8 * LANE)) * 8
    return jnp.pad(flat, (0, rows * LANE - flat.shape[0])).reshape(rows, LANE)


def _unpack(buf, shapes, lead=()):
    flat = buf.reshape(lead + (-1,))
    out, off = [], 0
    for s in shapes:
        n = math.prod(s)
        out.append(flat[..., off:off + n].reshape(lead + tuple(s)))
        off += n
    return out


def _proj_cols(w):
    pad = jnp.zeros(w.shape[:-1] + (PROJ_W - D_IN,), w.dtype)
    return jnp.concatenate([w[..., 0:256], w[..., 1024:1792], w[..., 1800:2568], w[..., 256:512], w[..., 512:1024],
                            w[..., 2568:2824], w[..., 1792:1800], w[..., 2824:2832], pad], axis=-1)


def _proj_cols_inv(w):
    return jnp.concatenate([w[..., 0:256], w[..., 1792:2048], w[..., 2048:2560], w[..., 256:1024], w[..., 2816:2824],
                            w[..., 1024:1792], w[..., 2560:2816], w[..., 2824:2832]], axis=-1)


def kernel(x, meta, ln_g, ln_b, ffn_w_gate, ffn_w_up, ffn_w_down, w_in, lru_conv_w, lru_conv_b, lru_w_a, lru_b_a, lru_w_x, lru_b_x, lru_lambda, ssd_conv_w, ssd_conv_b, ssd_dt_bias, ssd_a_log, ssd_d, ssd_norm_w, dn_conv_w, dn_a_log, dn_dt_bias, dn_norm_w, w_out, loss_target, m_meta, m_ln_g, m_ln_b, m_ffn_w_gate, m_ffn_w_up, m_ffn_w_down, m_w_in, m_lru_conv_w, m_lru_conv_b, m_lru_w_a, m_lru_b_a, m_lru_w_x, m_lru_b_x, m_lru_lambda, m_ssd_conv_w, m_ssd_conv_b, m_ssd_dt_bias, m_ssd_a_log, m_ssd_d, m_ssd_norm_w, m_dn_conv_w, m_dn_a_log, m_dn_dt_bias, m_dn_norm_w, m_w_out, v_meta, v_ln_g, v_ln_b, v_ffn_w_gate, v_ffn_w_up, v_ffn_w_down, v_w_in, v_lru_conv_w, v_lru_conv_b, v_lru_w_a, v_lru_b_a, v_lru_w_x, v_lru_b_x, v_lru_lambda, v_ssd_conv_w, v_ssd_conv_b, v_ssd_dt_bias, v_ssd_a_log, v_ssd_d, v_ssd_norm_w, v_dn_conv_w, v_dn_a_log, v_dn_dt_bias, v_dn_norm_w, v_w_out):
    w = dict(meta=meta, ln_g=ln_g, ln_b=ln_b, ffn_w_gate=ffn_w_gate, ffn_w_up=ffn_w_up, ffn_w_down=ffn_w_down, w_in=w_in,
             lru_conv_w=lru_conv_w, lru_conv_b=lru_conv_b, lru_w_a=lru_w_a, lru_b_a=lru_b_a, lru_w_x=lru_w_x,
             lru_b_x=lru_b_x, lru_lambda=lru_lambda, ssd_conv_w=ssd_conv_w, ssd_conv_b=ssd_conv_b, ssd_dt_bias=ssd_dt_bias,
             ssd_a_log=ssd_a_log, ssd_d=ssd_d, ssd_norm_w=ssd_norm_w, dn_conv_w=dn_conv_w, dn_a_log=dn_a_log,
             dn_dt_bias=dn_dt_bias, dn_norm_w=dn_norm_w, w_out=w_out)
    m = dict(meta=m_meta, ln_g=m_ln_g, ln_b=m_ln_b, ffn_w_gate=m_ffn_w_gate, ffn_w_up=m_ffn_w_up, ffn_w_down=m_ffn_w_down,
             w_in=m_w_in, lru_conv_w=m_lru_conv_w, lru_conv_b=m_lru_conv_b, lru_w_a=m_lru_w_a, lru_b_a=m_lru_b_a,
             lru_w_x=m_lru_w_x, lru_b_x=m_lru_b_x, lru_lambda=m_lru_lambda, ssd_conv_w=m_ssd_conv_w, ssd_conv_b=m_ssd_conv_b,
             ssd_dt_bias=m_ssd_dt_bias, ssd_a_log=m_ssd_a_log, ssd_d=m_ssd_d, ssd_norm_w=m_ssd_norm_w, dn_conv_w=m_dn_conv_w,
             dn_a_log=m_dn_a_log, dn_dt_bias=m_dn_dt_bias, dn_norm_w=m_dn_norm_w, w_out=m_w_out)
    v = dict(meta=v_meta, ln_g=v_ln_g, ln_b=v_ln_b, ffn_w_gate=v_ffn_w_gate, ffn_w_up=v_ffn_w_up, ffn_w_down=v_ffn_w_down,
             w_in=v_w_in, lru_conv_w=v_lru_conv_w, lru_conv_b=v_lru_conv_b, lru_w_a=v_lru_w_a, lru_b_a=v_lru_b_a,
             lru_w_x=v_lru_w_x, lru_b_x=v_lru_b_x, lru_lambda=v_lru_lambda, ssd_conv_w=v_ssd_conv_w, ssd_conv_b=v_ssd_conv_b,
             ssd_dt_bias=v_ssd_dt_bias, ssd_a_log=v_ssd_a_log, ssd_d=v_ssd_d, ssd_norm_w=v_ssd_norm_w, dn_conv_w=v_dn_conv_w,
             dn_a_log=v_dn_a_log, dn_dt_bias=v_dn_dt_bias, dn_norm_w=v_dn_norm_w, w_out=v_w_out)
    depth = ln_g.shape[0]
    me = _flat(_mesh_pos())
    small_names = tuple(SMALL_SHARD_AXIS)
    sharded = tuple(k for k in small_names if SMALL_SHARD_AXIS[k] is not None)

    del _QUEUE[:]
    _STANDALONE[0] = 0
    shards = [_layer_shards(w, l) for l in range(depth)]
    small_unit = _enqueue(_pack([w[k] for k in sharded]), False)
    first = _enqueue_layer(shards[0])
    g_small = _collect(small_unit)
    small = {k: w[k] for k in small_names if SMALL_SHARD_AXIS[k] is None}
    for k, piece in zip(sharded, _unpack(g_small, [w[k].shape for k in sharded], lead=(N_DEV,))):
        ax = SMALL_SHARD_AXIS[k]
        full = jnp.moveaxis(piece, 0, ax)
        small[k] = full.reshape(full.shape[:ax] + (N_DEV * w[k].shape[ax],) + full.shape[ax + 2:])

    loss, dx, small_grads, slabs = _local_step(x[0], loss_target[0], small, first, shards)

    r_small = _collect(_enqueue(_pack([small_grads[k] for k in small_names]), False))
    got = [{k: _collect_halves(g) for k, g in layer.items()} for layer in slabs]

    outs = {}
    ffn = lambda name: jnp.stack([jnp.stack([got[l][f"{name}{j}"] for j in range(2)], axis=1) for l in range(depth)], axis=1)
    one = lambda name: jnp.stack([got[l][name] for l in range(depth)], axis=1)
    parts_of = {"ffn_w_gate": ffn("gate"), "ffn_w_up": ffn("up"), "ffn_w_down": ffn("down"), "w_in": one("win"),
                "w_out": one("wout")}
    for k in BIG:
        shp = w[k].shape
        two = lambda a: a.reshape(-1, shp[-1])
        res = _adam(two(w[k]), parts_of[k].reshape(N_DEV, -1, shp[-1]), two(m[k]), two(v[k]), "adam_" + k)
        outs[k] = [a.reshape(shp) for a in res]
    g_full = _unpack(_sum_parts(r_small, "sum_small_grads"), [small[k].shape for k in small_names])
    g_loc = {}
    for k, gf in zip(small_names, g_full):
        ax = SMALL_SHARD_AXIS[k]
        g_loc[k] = gf if ax is None else lax.dynamic_slice_in_dim(gf, me * w[k].shape[ax], w[k].shape[ax], axis=ax)
    res = _adam(_pack([w[k] for k in small_names]), _pack([g_loc[k] for k in small_names]),
                _pack([m[k] for k in small_names]), _pack([v[k] for k in small_names]), "adam_small")
    shapes = [w[k].shape for k in small_names]
    for i, k in enumerate(small_names):
        outs[k] = [_unpack(r, shapes)[i] for r in res]

    loss = lax.psum(loss, ("x", "y", "c"))
    return (loss, dx[None], *[outs[k][0] for k in WEIGHT_ORDER], *[outs[k][1] for k in WEIGHT_ORDER],
            *[outs[k][2] for k in WEIGHT_ORDER], *[outs[k][3] for k in WEIGHT_ORDER])
```

```python
import functools
import math

import jax
import jax.numpy as jnp
from jax import lax
from jax.experimental import pallas as pl
from jax.experimental.pallas import tpu as pltpu

F32 = jnp.float32
BF16 = jnp.bfloat16
HI = lax.Precision.HIGHEST
MESH_IDS = pl.DeviceIdType.MESH

N_DEV = 8
D_MODEL = 1024
DEPTH = 4
N_META = 16
CHUNK = 64
CONV_K = 4
D_FF = 2816
LRU_WIDTH = 256
LRU_HEADS = 4
LRU_C = 8.0
SSD_HEADS = 8
SSD_GROUPS = 2
SSD_HPG = 4
SSD_INNER = 512
DN_HEADS = 4
HEAD = 64
CONV_W = 1792
PROJ_W = 2944
OFF_LRU_Y, OFF_SSD_Z, OFF_DN_GATE, OFF_SMALL = 1792, 2048, 2560, 2816
D_IN = 2832
ALPHA = (2 * DEPTH) ** 0.25
FFN_RES = 0.5
LN_EPS = 1e-5
RMS_EPS = 1e-6
ADAM_LR, ADAM_B1, ADAM_B2, ADAM_EPS, ADAM_WD, ADAM_STEP = 0.001, 0.9, 0.999, 1e-08, 0.01, 10

VMEM_LIMIT = 56 * 1024 * 1024
ROW_TILE_CAP = 832
LANE = 128


def _tile(n, cap, mult=16):
    best = None
    for t in range(mult, min(n, cap) + 1, mult):
        if n % t == 0:
            best = t
    assert best is not None, (n, cap, mult)
    return best


def _params(sem=None):
    return pltpu.CompilerParams(dimension_semantics=sem, vmem_limit_bytes=VMEM_LIMIT)


def _resident(shape, index_map):
    return pl.BlockSpec(shape, index_map, pipeline_mode=pl.Buffered(1))


def _dot(a, b):
    return jnp.dot(a, b, preferred_element_type=F32)


def _dot_nt(a, b):
    return lax.dot_general(a, b, (((1,), (1,)), ((), ())), preferred_element_type=F32)


def _dot_tn(a, b):
    return lax.dot_general(a, b, (((0,), (0,)), ((), ())), preferred_element_type=F32)


def _sds(shape, dtype):
    return jax.ShapeDtypeStruct(shape, dtype)


US_PER_MB = 92.5


class _Unit:
    def __init__(self, arr, slab):
        self.arr, self.slab, self.out = arr, slab, None
        per_peer = arr.size * arr.dtype.itemsize / (N_DEV if slab else 1)
        self.us = per_peer / 1e6 * US_PER_MB * (1.0 if slab else 0.5)


_QUEUE = []
_STANDALONE = [0]


def _enqueue(arr, slab):
    unit = _Unit(arr, slab)
    _QUEUE.append(unit)
    return unit


def _enqueue_halves(arr, slab, axis):
    half = arr.shape[axis] // 2
    parts = (lax.slice_in_dim(arr, 0, half, axis=axis), lax.slice_in_dim(arr, half, arr.shape[axis], axis=axis))
    return [_enqueue(p, slab) for p in parts], axis + (0 if slab else 1)


def _collect_halves(group):
    units, axis = group
    return jnp.concatenate([_collect(u) for u in units], axis=axis)


def _take_units(host_us):
    units = []
    while _QUEUE and host_us >= 0.5 * _QUEUE[0].us:
        host_us -= _QUEUE[0].us
        units.append(_QUEUE.pop(0))
    return units


def _collect(unit):
    if unit.out is None:
        n = _QUEUE.index(unit) + 1
        units = [_QUEUE.pop(0) for _ in range(n)]
        _STANDALONE[0] += 1
        res = _exchange([u.arr for u in units], [u.slab for u in units], f"exchange_{_STANDALONE[0]}")
        for u, r in zip(units, res):
            u.out = r
    return unit.out


def _call(body, host_us, *, name, out_shape, in_specs, out_specs, grid=(), scratch_shapes=(), compiler_params=None):
    units = _take_units(host_us)
    kw = dict(name=name, grid=grid, compiler_params=compiler_params)
    if not units:
        return pl.pallas_call(body, out_shape=out_shape, in_specs=in_specs, out_specs=out_specs,
                              scratch_shapes=list(scratch_shapes), **kw)
    single = not isinstance(out_shape, (tuple, list))
    outs = (out_shape,) if single else tuple(out_shape)
    ospecs = [out_specs] if single else list(out_specs)
    nin, nout, nscr, ncm = len(in_specs), len(outs), len(scratch_shapes), len(units)
    slabs = [u.slab for u in units]

    def hosted(*refs):
        ins, c_in = refs[:nin], refs[nin:nin + ncm]
        o, c_out = refs[nin + ncm:nin + ncm + nout], refs[nin + ncm + nout:nin + 2 * ncm + nout]
        scr, sems = refs[nin + 2 * ncm + nout:nin + 2 * ncm + nout + nscr], refs[nin + 2 * ncm + nout + nscr:]
        ids = [pl.program_id(d) for d in range(len(grid))]
        first = functools.reduce(jnp.logical_and, [i == 0 for i in ids])
        last = functools.reduce(jnp.logical_and, [i == g - 1 for i, g in zip(ids, grid)])

        @pl.when(first)
        def _():
            _exchange_start(c_in, c_out, slabs, *sems)
        body(*ins, *o, *scr)

        @pl.when(last)
        def _():
            _exchange_wait(c_in, c_out, slabs, *sems)

    hbm = pl.BlockSpec(memory_space=pl.ANY)
    fn = pl.pallas_call(
        hosted, out_shape=outs + _exchange_shapes([u.arr for u in units], slabs), in_specs=list(in_specs) + [hbm] * ncm,
        out_specs=ospecs + [hbm] * ncm, scratch_shapes=list(scratch_shapes) + _exchange_sems(ncm), **kw)

    def run(*args):
        res = fn(*args, *[u.arr for u in units])
        for u, r in zip(units, res[nout:]):
            u.out = r
        return res[0] if single else tuple(res[:nout])

    return run


def _ffn_up(xb, wg, wu, name):
    tp, d = xb.shape
    f = wg.shape[1]
    tm = _tile(tp, ROW_TILE_CAP)
    tn = _tile(f, 1408, LANE)
    nj = f // tn

    def body(x_ref, wg_ref, wu_ref, g_ref, u_ref, a_ref):
        x = x_ref[...]
        g = _dot(x, wg_ref[...])
        u = _dot(x, wu_ref[...])
        g_ref[...] = g.astype(BF16)
        u_ref[...] = u.astype(BF16)
        a_ref[...] = (g * jax.nn.sigmoid(g) * u).astype(BF16)

    out = _sds((tp, f), BF16)
    return _call(
        body, 60, name=name, out_shape=(out, out, out), grid=(nj, tp // tm),
        in_specs=[pl.BlockSpec((tm, d), lambda j, i: (i, 0)),
                  pl.BlockSpec((d, tn), lambda j, i: (0, j)),
                  pl.BlockSpec((d, tn), lambda j, i: (0, j))],
        out_specs=[pl.BlockSpec((tm, tn), lambda j, i: (i, j))] * 3,
        compiler_params=_params(("arbitrary", "arbitrary")),
    )(xb, wg, wu)


def _mm_resid_ln(a, w, h, gamma, beta, scale, name):
    tp, k = a.shape
    d = w.shape[1]
    tm = _tile(tp, ROW_TILE_CAP)

    def body(a_ref, w_ref, h_ref, g_ref, b_ref, y_ref, o_ref, ob_ref):
        y = ALPHA * h_ref[...] + scale * _dot(a_ref[...], w_ref[...])
        mu = jnp.mean(y, axis=-1, keepdims=True)
        yc = y - mu
        var = jnp.mean(yc * yc, axis=-1, keepdims=True)
        o = yc * lax.rsqrt(var + LN_EPS) * g_ref[...] + b_ref[...]
        y_ref[...] = y
        o_ref[...] = o
        ob_ref[...] = o.astype(BF16)

    row = lambda i: (i, 0)
    fix = lambda i: (0, 0)
    return _call(
        body, 30, name=name, out_shape=(_sds((tp, d), F32), _sds((tp, d), F32), _sds((tp, d), BF16)),
        grid=(tp // tm,),
        in_specs=[pl.BlockSpec((tm, k), row), _resident((k, d), fix), pl.BlockSpec((tm, d), row),
                  pl.BlockSpec((1, d), fix), pl.BlockSpec((1, d), fix)],
        out_specs=[pl.BlockSpec((tm, d), row)] * 3,
        compiler_params=_params(("arbitrary",)),
    )(a, w, h, gamma, beta)


def _mm_nn(xb, w, name):
    tp, k = xb.shape
    n = w.shape[1]
    tm = _tile(tp, ROW_TILE_CAP)

    def body(x_ref, w_ref, o_ref):
        o_ref[...] = _dot(x_ref[...], w_ref[...])

    return _call(
        body, 30, name=name, out_shape=_sds((tp, n), F32), grid=(tp // tm,),
        in_specs=[pl.BlockSpec((tm, k), lambda i: (i, 0)), _resident((k, n), lambda i: (0, 0))],
        out_specs=pl.BlockSpec((tm, n), lambda i: (i, 0)),
        compiler_params=_params(("arbitrary",)),
    )(xb, w)


def _mm_nt_add(a, w, resid, name):
    tp, k = a.shape
    n = w.shape[0]
    tm = _tile(tp, ROW_TILE_CAP)
    has_resid = resid is not None

    def body(*refs):
        if has_resid:
            a_ref, w_ref, r_ref, o_ref = refs
            o_ref[...] = r_ref[...] + _dot_nt(a_ref[...], w_ref[...])
        else:
            a_ref, w_ref, o_ref = refs
            o_ref[...] = _dot_nt(a_ref[...], w_ref[...])

    in_specs = [pl.BlockSpec((tm, k), lambda i: (i, 0)), _resident((n, k), lambda i: (0, 0))]
    args = [a, w]
    if has_resid:
        in_specs.append(pl.BlockSpec((tm, n), lambda i: (i, 0)))
        args.append(resid)
    return _call(
        body, 25, name=name, out_shape=_sds((tp, n), F32), grid=(tp // tm,),
        in_specs=in_specs, out_specs=pl.BlockSpec((tm, n), lambda i: (i, 0)),
        compiler_params=_params(("arbitrary",)),
    )(*args)


def _ffn_dx(dg, du, wg, wu, resid, name):
    tp, f = dg.shape
    d = wg.shape[0]
    tm = _tile(tp, ROW_TILE_CAP)

    def body(dg_ref, du_ref, wg_ref, wu_ref, r_ref, o_ref):
        acc = r_ref[...] + _dot_nt(dg_ref[...], wg_ref[...])
        o_ref[...] = acc + _dot_nt(du_ref[...], wu_ref[...])

    row = lambda i: (i, 0)
    fix = lambda i: (0, 0)
    return _call(
        body, 55, name=name, out_shape=_sds((tp, d), F32), grid=(tp // tm,),
        in_specs=[pl.BlockSpec((tm, f), row), pl.BlockSpec((tm, f), row), _resident((d, f), fix), _resident((d, f), fix),
                  pl.BlockSpec((tm, d), row)],
        out_specs=pl.BlockSpec((tm, d), row),
        compiler_params=_params(("arbitrary",)),
    )(dg, du, wg, wu, resid)


def _mm_tn(a, b, name):
    tp, ka = a.shape
    nb = b.shape[1]
    tt = _tile(tp, ROW_TILE_CAP)
    tk = _tile(ka, 1408, LANE)
    tn = _tile(nb, 1536, LANE)
    if tn < 512:
        tn = nb
    nt = tp // tt

    def body(a_ref, b_ref, o_ref):
        @pl.when(pl.program_id(2) == 0)
        def _():
            o_ref[...] = jnp.zeros_like(o_ref)
        o_ref[...] += _dot_tn(a_ref[...], b_ref[...])

    return _call(
        body, 40, name=name, out_shape=_sds((ka, nb), F32), grid=(ka // tk, nb // tn, nt),
        in_specs=[pl.BlockSpec((tt, tk), lambda i, j, t: (t, i)), pl.BlockSpec((tt, tn), lambda i, j, t: (t, j))],
        out_specs=pl.BlockSpec((tk, tn), lambda i, j, t: (i, j)),
        compiler_params=_params(("arbitrary", "arbitrary", "arbitrary")),
    )(a, b)


def _ln_bwd(y, dout, gamma, scale, name):
    tp, d = y.shape
    tm = _tile(tp, ROW_TILE_CAP)

    def body(y_ref, do_ref, g_ref, dres_ref, dyb_ref, dg_ref, db_ref):
        @pl.when(pl.program_id(0) == 0)
        def _():
            dg_ref[...] = jnp.zeros_like(dg_ref)
            db_ref[...] = jnp.zeros_like(db_ref)
        yv = y_ref[...]
        do = do_ref[...]
        mu = jnp.mean(yv, axis=-1, keepdims=True)
        yc = yv - mu
        var = jnp.mean(yc * yc, axis=-1, keepdims=True)
        rstd = lax.rsqrt(var + LN_EPS)
        xhat = yc * rstd
        dxh = do * g_ref[...]
        m1 = jnp.mean(dxh, axis=-1, keepdims=True)
        m2 = jnp.mean(dxh * xhat, axis=-1, keepdims=True)
        dy = rstd * (dxh - m1 - xhat * m2)
        dres_ref[...] = ALPHA * dy
        dyb_ref[...] = (scale * dy).astype(BF16)
        dg_ref[...] += jnp.sum(do * xhat, axis=0, keepdims=True)
        db_ref[...] += jnp.sum(do, axis=0, keepdims=True)

    row = lambda i: (i, 0)
    fix = lambda i: (0, 0)
    return _call(
        body, 22, name=name,
        out_shape=(_sds((tp, d), F32), _sds((tp, d), BF16), _sds((1, d), F32), _sds((1, d), F32)),
        grid=(tp // tm,),
        in_specs=[pl.BlockSpec((tm, d), row), pl.BlockSpec((tm, d), row), pl.BlockSpec((1, d), fix)],
        out_specs=[pl.BlockSpec((tm, d), row), pl.BlockSpec((tm, d), row), pl.BlockSpec((1, d), fix),
                   pl.BlockSpec((1, d), fix)],
        compiler_params=_params(("arbitrary",)),
    )(y, dout, gamma)


def _ffn_dact(dyb, wd, g, u, name):
    tp, d = dyb.shape
    f = wd.shape[0]
    tm = _tile(tp, ROW_TILE_CAP)
    tn = _tile(f, 1408, LANE)

    def body(dy_ref, w_ref, g_ref, u_ref, dg_ref, du_ref):
        dact = _dot_nt(dy_ref[...], w_ref[...])
        gv = g_ref[...].astype(F32)
        uv = u_ref[...].astype(F32)
        sg = jax.nn.sigmoid(gv)
        dg_ref[...] = (dact * uv * (sg * (1.0 + gv * (1.0 - sg)))).astype(BF16)
        du_ref[...] = (dact * (gv * sg)).astype(BF16)

    out = _sds((tp, f), BF16)
    blk = pl.BlockSpec((tm, tn), lambda j, i: (i, j))
    return _call(
        body, 55, name=name, out_shape=(out, out), grid=(f // tn, tp // tm),
        in_specs=[pl.BlockSpec((tm, d), lambda j, i: (i, 0)), pl.BlockSpec((tn, d), lambda j, i: (j, 0)), blk, blk],
        out_specs=[blk, blk],
        compiler_params=_params(("arbitrary", "arbitrary")),
    )(dyb, wd, g, u)


def _loss_grad(o, tgt, t_real, name):
    tp, d = o.shape
    tm = _tile(tp, ROW_TILE_CAP)

    def body(o_ref, t_ref, l_ref, d_ref):
        i = pl.program_id(0)

        @pl.when(i == 0)
        def _():
            l_ref[...] = jnp.zeros_like(l_ref)
        rows = i * tm + lax.broadcasted_iota(jnp.int32, (tm, 1), 0)
        real = jnp.logical_and(rows >= N_META, rows < t_real)
        err = jnp.where(real, o_ref[...] - t_ref[...], 0.0)
        d_ref[...] = err * (1.0 / d)
        l_ref[...] += jnp.sum(err * err, axis=0, keepdims=True) * (0.5 / d)

    row = lambda i: (i, 0)
    return pl.pallas_call(
        body, name=name, out_shape=(_sds((1, d), F32), _sds((tp, d), F32)), grid=(tp // tm,),
        in_specs=[pl.BlockSpec((tm, d), row), pl.BlockSpec((tm, d), row)],
        out_specs=[pl.BlockSpec((1, d), lambda i: (0, 0)), pl.BlockSpec((tm, d), row)],
        compiler_params=_params(("arbitrary",)),
    )(o, tgt)


CONV_TC = 256


def _silu_grad(y):
    s = jax.nn.sigmoid(y)
    return s * (1.0 + y * (1.0 - s))


def _conv_taps(x_ref, w, r0, rb):
    cur = x_ref[r0:r0 + rb, :]
    prev = x_ref[r0 - 8:r0, :] if r0 > 0 else jnp.zeros((8, cur.shape[1]), F32)
    xcat = jnp.concatenate([prev, cur], axis=0)
    taps = [xcat[5 + j:5 + j + rb] for j in range(CONV_K - 1)] + [cur]
    y = w[0:1] * taps[0]
    for j in range(1, CONV_K):
        y = y + w[j:j + 1] * taps[j]
    return y, taps


def _conv_fwd(proj, w, b, name):
    tp = proj.shape[0]
    rb = _tile(tp, ROW_TILE_CAP, 8)

    def body(x_ref, w_ref, b_ref, o_ref):
        gated = pl.program_id(0) > 0
        wv = w_ref[...]
        bv = b_ref[...]
        for r0 in range(0, tp, rb):
            y, _ = _conv_taps(x_ref, wv, r0, rb)
            y = y + bv
            o_ref[r0:r0 + rb, :] = jnp.where(gated, y * jax.nn.sigmoid(y), y)

    col = lambda j: (0, j)
    return _call(
        body, 25, name=name, out_shape=_sds((tp, CONV_W), F32), grid=(CONV_W // CONV_TC,),
        in_specs=[pl.BlockSpec((tp, CONV_TC), col), pl.BlockSpec((CONV_K, CONV_TC), col), pl.BlockSpec((1, CONV_TC), col)],
        out_specs=pl.BlockSpec((tp, CONV_TC), col),
        compiler_params=_params(("arbitrary",)),
    )(proj, w, b)


def _conv_bwd(proj, w, b, dxc, name):
    tp = proj.shape[0]
    rb = _tile(tp, ROW_TILE_CAP, 8)

    def body(x_ref, w_ref, b_ref, d_ref, dx_ref, dw_ref, db_ref, dy_scr):
        gated = pl.program_id(0) > 0
        wv = w_ref[...]
        bv = b_ref[...]
        dw = [jnp.zeros((1, CONV_TC), F32) for _ in range(CONV_K)]
        db = jnp.zeros((1, CONV_TC), F32)
        for r0 in range(0, tp, rb):
            y, taps = _conv_taps(x_ref, wv, r0, rb)
            y = y + bv
            d = d_ref[r0:r0 + rb, :]
            dy = jnp.where(gated, d * _silu_grad(y), d)
            dy_scr[r0:r0 + rb, :] = dy
            for j in range(CONV_K):
                dw[j] = dw[j] + jnp.sum(dy * taps[j], axis=0, keepdims=True)
            db = db + jnp.sum(dy, axis=0, keepdims=True)
        for j in range(CONV_K):
            dw_ref[j:j + 1, :] = dw[j]
        db_ref[...] = db
        for r0 in range(0, tp, rb):
            cur = dy_scr[r0:r0 + rb, :]
            nxt = dy_scr[r0 + rb:r0 + rb + 8, :] if r0 + rb < tp else jnp.zeros((8, CONV_TC), F32)
            dcat = jnp.concatenate([cur, nxt], axis=0)
            dx = wv[3:4] * cur
            for s in range(1, CONV_K):
                dx = dx + wv[3 - s:4 - s] * dcat[s:s + rb]
            dx_ref[r0:r0 + rb, :] = dx.astype(BF16)

    col = lambda j: (0, j)
    return _call(
        body, 70, name=name,
        out_shape=(_sds((tp, CONV_W), BF16), _sds((CONV_K, CONV_W), F32), _sds((1, CONV_W), F32)),
        grid=(CONV_W // CONV_TC,),
        in_specs=[pl.BlockSpec((tp, CONV_TC), col), pl.BlockSpec((CONV_K, CONV_TC), col), pl.BlockSpec((1, CONV_TC), col),
                  pl.BlockSpec((tp, CONV_TC), col)],
        out_specs=[pl.BlockSpec((tp, CONV_TC), col), pl.BlockSpec((CONV_K, CONV_TC), col), pl.BlockSpec((1, CONV_TC), col)],
        scratch_shapes=[pltpu.VMEM((tp, CONV_TC), F32)],
        compiler_params=_params(("arbitrary",)),
    )(proj, w, b, dxc)


def _neg_expm1(x):
    series = -x * (1.0 + x * (0.5 + x * (1.0 / 6.0 + x * (1.0 / 24.0))))
    return jnp.where(jnp.abs(x) < 0.03, series, 1.0 - jnp.exp(x))


def _lru_gates(u, wa, ba, wx, bx, lam):
    r = jax.nn.sigmoid(jnp.dot(u, wa, precision=HI, preferred_element_type=F32) + ba)
    i = jax.nn.sigmoid(jnp.dot(u, wx, precision=HI, preferred_element_type=F32) + bx)
    log_a = -LRU_C * r * jax.nn.softplus(-lam)
    a = jnp.exp(log_a)
    b = jnp.sqrt(_neg_expm1(2.0 * log_a)) * (i * u)
    return a, b


def _lru_specs(tm, flip, n_tiles):
    idx = (lambda i: (n_tiles - 1 - i, 0)) if flip else (lambda i: (i, 0))
    return idx, lambda i: (0, 0)


def _lru_fwd(xc, proj, wa, ba, wx, bx, lam, name):
    tp = xc.shape[0]
    w = LRU_WIDTH
    tm = _tile(tp, ROW_TILE_CAP, 8)
    ycol = OFF_LRU_Y // w

    def body(u_ref, y_ref, wa_ref, ba_ref, wx_ref, bx_ref, lam_ref, o_ref, h_ref, a_scr, b_scr, carry):
        @pl.when(pl.program_id(0) == 0)
        def _():
            carry[...] = jnp.zeros_like(carry)
        a, b = _lru_gates(u_ref[...], wa_ref[...], ba_ref[...], wx_ref[...], bx_ref[...], lam_ref[...])
        a_scr[...] = a
        b_scr[...] = b

        def step(t, h):
            h = a_scr[pl.ds(t, 1), :] * h + b_scr[pl.ds(t, 1), :]
            h_ref[pl.ds(t, 1), :] = h
            return h

        carry[0:1, :] = lax.fori_loop(0, tm, step, carry[0:1, :])
        o_ref[...] = (h_ref[...] * jax.nn.gelu(y_ref[...])).astype(BF16)

    fix = lambda i: (0, 0)
    return _call(
        body, 40, name=name, out_shape=(_sds((tp, w), BF16), _sds((tp, w), F32)), grid=(tp // tm,),
        in_specs=[pl.BlockSpec((tm, w), lambda i: (i, 0)), pl.BlockSpec((tm, w), lambda i: (i, ycol)),
                  pl.BlockSpec((w, w), fix), pl.BlockSpec((1, w), fix), pl.BlockSpec((w, w), fix),
                  pl.BlockSpec((1, w), fix), pl.BlockSpec((1, w), fix)],
        out_specs=[pl.BlockSpec((tm, w), lambda i: (i, 0))] * 2,
        scratch_shapes=[pltpu.VMEM((tm, w), F32), pltpu.VMEM((tm, w), F32), pltpu.VMEM((8, w), F32)],
        compiler_params=_params(("arbitrary",)),
    )(xc, proj, wa, ba, wx, bx, lam)


def _lru_bwd(xc, proj, h, dout, wa, ba, wx, bx, lam, name):
    tp = xc.shape[0]
    w = LRU_WIDTH
    tm = _tile(tp, ROW_TILE_CAP, 8)
    nt = tp // tm
    ycol = OFF_LRU_Y // w
    rev = lambda i: (nt - 1 - i, 0)
    prev8 = lambda i: (jnp.maximum((nt - 1 - i) * (tm // 8) - 1, 0), 0)

    def body(u_ref, y_ref, h_ref, hp_ref, do_ref, wa_ref, ba_ref, wx_ref, bx_ref, lam_ref,
             du_ref, dy_ref, dwa_ref, dba_ref, dwx_ref, dbx_ref, dlam_ref,
             a_scr, dh_scr, g_scr, da_scr, hext, carry):
        i = pl.program_id(0)

        @pl.when(i == 0)
        def _():
            carry[...] = jnp.zeros_like(carry)
            for r in (dwa_ref, dba_ref, dwx_ref, dbx_ref, dlam_ref):
                r[...] = jnp.zeros_like(r)
        params = (wa_ref[...], ba_ref[...], wx_ref[...], bx_ref[...], lam_ref[...])
        (a, _), gates_vjp = jax.vjp(_lru_gates, u_ref[...], *params)
        gel, gelu_vjp = jax.vjp(jax.nn.gelu, y_ref[...])
        do = do_ref[...]
        hv = h_ref[...]
        dy_ref[...] = gelu_vjp(do * hv)[0].astype(BF16)
        a_scr[...] = a
        dh_scr[...] = do * gel
        hext[0:8, :] = jnp.where(i == nt - 1, 0.0, hp_ref[...])
        hext[8:8 + tm, :] = hv

        def step(s, c):
            t = tm - 1 - s
            g = dh_scr[pl.ds(t, 1), :] + c
            g_scr[pl.ds(t, 1), :] = g
            da_scr[pl.ds(t, 1), :] = g * hext[pl.ds(t + 7, 1), :]
            return a_scr[pl.ds(t, 1), :] * g

        carry[0:1, :] = lax.fori_loop(0, tm, step, carry[0:1, :])
        du, dwa, dba, dwx, dbx, dlam = gates_vjp((da_scr[...], g_scr[...]))
        du_ref[...] = du
        dwa_ref[...] += dwa
        dba_ref[...] += dba
        dwx_ref[...] += dwx
        dbx_ref[...] += dbx
        dlam_ref[...] += dlam

    fix = lambda i: (0, 0)
    tile = pl.BlockSpec((tm, w), rev)
    mat = pl.BlockSpec((w, w), fix)
    vec = pl.BlockSpec((1, w), fix)
    return _call(
        body, 90, name=name,
        out_shape=(_sds((tp, w), F32), _sds((tp, w), BF16), _sds((w, w), F32), _sds((1, w), F32), _sds((w, w), F32),
                   _sds((1, w), F32), _sds((1, w), F32)),
        grid=(nt,),
        in_specs=[tile, pl.BlockSpec((tm, w), lambda i: (nt - 1 - i, ycol)), tile, pl.BlockSpec((8, w), prev8), tile,
                  mat, vec, mat, vec, vec],
        out_specs=[tile, tile, mat, vec, mat, vec, vec],
        scratch_shapes=[pltpu.VMEM((tm, w), F32)] * 4 + [pltpu.VMEM((tm + 8, w), F32), pltpu.VMEM((8, w), F32)],
        compiler_params=_params(("arbitrary",)),
    )(xc, proj, h, h, dout, wa, ba, wx, bx, lam)


def _dot3(a, b, dims):
    ah = a.astype(BF16)
    al = (a - ah.astype(F32)).astype(BF16)
    bh = b.astype(BF16)
    bl = (b - bh.astype(F32)).astype(BF16)
    dot = lambda x, y: lax.dot_general(x, y, (dims, ((0,), (0,))), preferred_element_type=F32)
    return dot(ah, bh) + (dot(ah, bl) + dot(al, bh))


@jax.custom_vjp
def _bmm(a, b):
    return _dot3(a, b, ((2,), (1,)))


@jax.custom_vjp
def _bmm_nt(a, b):
    return _dot3(a, b, ((2,), (2,)))


@jax.custom_vjp
def _bmm_tn(a, b):
    return _dot3(a, b, ((1,), (1,)))


_bmm.defvjp(lambda a, b: (_bmm(a, b), (a, b)), lambda r, g: (_bmm_nt(g, r[1]), _bmm_tn(r[0], g)))
_bmm_nt.defvjp(lambda a, b: (_bmm_nt(a, b), (a, b)), lambda r, g: (_bmm(g, r[1]), _bmm_tn(g, r[0])))
_bmm_tn.defvjp(lambda a, b: (_bmm_tn(a, b), (a, b)), lambda r, g: (_bmm_nt(r[1], g), _bmm(r[0], g)))


def _dot1(a, b, dims):
    return lax.dot_general(a.astype(BF16), b.astype(BF16), (dims, ((0,), (0,))), preferred_element_type=F32)


@jax.custom_vjp
def _bmm1(a, b):
    return _dot1(a, b, ((2,), (1,)))


@jax.custom_vjp
def _bmm1_nt(a, b):
    return _dot1(a, b, ((2,), (2,)))


@jax.custom_vjp
def _bmm1_tn(a, b):
    return _dot1(a, b, ((1,), (1,)))


_bmm1.defvjp(lambda a, b: (_bmm1(a, b), (a, b)), lambda r, g: (_bmm1_nt(g, r[1]), _bmm1_tn(r[0], g)))
_bmm1_nt.defvjp(lambda a, b: (_bmm1_nt(a, b), (a, b)), lambda r, g: (_bmm1(g, r[1]), _bmm1_tn(g, r[0])))
_bmm1_tn.defvjp(lambda a, b: (_bmm1_tn(a, b), (a, b)), lambda r, g: (_bmm1_nt(r[1], g), _bmm1(r[0], g)))


def _chunk_masks(nh):
    r = lax.broadcasted_iota(jnp.int32, (CHUNK, CHUNK), 0)
    c = lax.broadcasted_iota(jnp.int32, (CHUNK, CHUNK), 1)
    full = lambda m: jnp.broadcast_to(m[None], (nh, CHUNK, CHUNK))
    return r, c, full


def _decay_terms(g, nh):
    r, c, full = _chunk_masks(nh)
    incl = r >= c
    cs = _bmm(full(incl.astype(F32)), g)
    cs_t = jnp.swapaxes(cs, 1, 2)
    tot = jnp.broadcast_to(jnp.sum(g, axis=1, keepdims=True), g.shape)
    m = full(incl)
    decay = jnp.where(m, jnp.exp(jnp.where(m, cs - cs_t, 0.0)), 0.0)
    return cs, decay, tot


def _rep_groups(x):
    return jnp.concatenate([jnp.broadcast_to(x[g:g + 1], (SSD_HPG,) + x.shape[1:]) for g in range(SSD_GROUPS)], axis=0)


def _ssd_chunk(xs, bm, cm, dtr, z, p_dtb, p_alog, p_d, p_nw, state):
    nh = SSD_HEADS
    dt = jax.nn.softplus(dtr + p_dtb)
    a = dt * (-jnp.exp(p_alog))
    x = xs * dt
    cs, decay, tot = _decay_terms(a, nh)
    b8 = _rep_groups(bm)
    c8 = _rep_groups(cm)
    y = _bmm1(_rep_groups(_bmm1_nt(cm, bm)) * decay, x)
    y = y + _bmm1_nt(c8, state) * jnp.exp(cs)
    new_state = state * jnp.exp(tot) + _bmm1_tn(x * jnp.exp(tot - cs), b8)
    y = y + p_d * xs
    y = y * (z * jax.nn.sigmoid(z))
    ss = jnp.sum(y * y, axis=-1, keepdims=True)
    ssg = jnp.concatenate(
        [jnp.broadcast_to(jnp.sum(ss[g * SSD_HPG:(g + 1) * SSD_HPG], axis=0, keepdims=True), (SSD_HPG, CHUNK, 1))
         for g in range(SSD_GROUPS)], axis=0)
    y = y * lax.rsqrt(ssg * (1.0 / (SSD_HPG * HEAD)) + RMS_EPS) * p_nw
    return y, new_state


@jax.custom_vjp
def _unit_lower_inverse(m):
    r, c, full = _chunk_masks(m.shape[0])
    eye = full((r == c).astype(F32))
    md = jnp.where(full((r // 16) == (c // 16)), m, 0.0)
    mo = m - md
    x = eye - md
    p = _bmm(md, md)
    x = x + _bmm(x, p)
    p = _bmm(p, p)
    x = x + _bmm(x, p)
    p = _bmm(p, p)
    x = x + _bmm(x, p)
    n = _bmm(x, mo)
    y = x - _bmm(n, x)
    return y + _bmm(_bmm(n, n), y)


def _unit_lower_inverse_fwd(m):
    t = _unit_lower_inverse(m)
    return t, t


_unit_lower_inverse.defvjp(_unit_lower_inverse_fwd, lambda t, g: (-_bmm_nt(_bmm_tn(t, g), t),))


def _dn_chunk(q, k, v, gate, braw, araw, p_alog, p_dtb, p_nw, state):
    nh = DN_HEADS
    r, c, full = _chunk_masks(nh)
    q = q * lax.rsqrt(jnp.sum(q * q, axis=-1, keepdims=True) + RMS_EPS) * (HEAD ** -0.5)
    k = k * lax.rsqrt(jnp.sum(k * k, axis=-1, keepdims=True) + RMS_EPS)
    beta = jax.nn.sigmoid(braw)
    g = -jnp.exp(p_alog) * jax.nn.softplus(araw + p_dtb)
    gcs, decay, tot = _decay_terms(g, nh)
    kb = k * beta
    vb = v * beta
    m = jnp.where(full(r > c), _bmm1_nt(kb, k) * decay, 0.0)
    t = _unit_lower_inverse(m)
    egcs = jnp.exp(gcs)
    u = _bmm(t, vb)
    w = _bmm(t, kb * egcs)
    attn = _bmm1_nt(q, k) * decay
    v_new = u - _bmm1(w, state)
    out = _bmm1(q * egcs, state) + _bmm1(attn, v_new)
    new_state = state * jnp.exp(tot) + _bmm1_tn(k * jnp.exp(tot - gcs), v_new)
    out = out * lax.rsqrt(jnp.mean(out * out, axis=-1, keepdims=True) + RMS_EPS) * p_nw
    return out * (gate * jax.nn.sigmoid(gate)), new_state


def _chunks_per_step(nc):
    return max(n for n in range(1, 6) if nc % n == 0)


def _heads(t, n):
    return jnp.stack([t[:, HEAD * h:HEAD * (h + 1)] for h in range(n)])


def _unheads(a):
    return jnp.concatenate([a[h] for h in range(a.shape[0])], axis=-1)


def _head_scalars(t, off, n):
    return jnp.stack([jnp.broadcast_to(t[:, off + h:off + h + 1], (CHUNK, HEAD)) for h in range(n)])


def _unhead_scalars(d, off):
    red = jnp.sum(d, axis=-1, keepdims=True)
    lane = lax.broadcasted_iota(jnp.int32, (CHUNK, LANE), 1)
    out = jnp.zeros((CHUNK, LANE), F32)
    for h in range(d.shape[0]):
        out = out + jnp.where(lane == off + h, red[h], 0.0)
    return out


def _chunk_scan_fwd(chunk_fn, srcs, read, params, nh, host_us, name):
    tp = srcs[0][0].shape[0]
    nc = tp // CHUNK
    nb = _chunks_per_step(nc)
    rows = nb * CHUNK
    ns, npar = len(srcs), len(params)

    def body(*refs):
        s_refs, p_refs = refs[:ns], refs[ns:ns + npar]
        y_ref, st_ref, state = refs[ns + npar:]

        @pl.when(pl.program_id(0) == 0)
        def _():
            state[...] = jnp.zeros_like(state)
        par = [r[...] for r in p_refs]
        st = state[...]
        for k in range(nb):
            sl = slice(k * CHUNK, (k + 1) * CHUNK)
            st_ref[k] = st
            y, st = chunk_fn(*read(s_refs, sl), *par, st)
            y_ref[sl, :] = _unheads(y).astype(BF16)
        state[...] = st

    src_spec = lambda s: pl.BlockSpec((rows, s[1]), lambda c: (c, s[2]))
    par_spec = lambda a: pl.BlockSpec(a.shape, lambda c: (0, 0, 0))
    return _call(
        body, host_us, name=name,
        out_shape=(_sds((tp, nh * HEAD), BF16), _sds((nc, nh, HEAD, HEAD), F32)),
        grid=(nc // nb,),
        in_specs=[src_spec(s) for s in srcs] + [par_spec(a) for a in params],
        out_specs=[pl.BlockSpec((rows, nh * HEAD), lambda c: (c, 0)),
                   pl.BlockSpec((nb, nh, HEAD, HEAD), lambda c: (c, 0, 0, 0))],
        scratch_shapes=[pltpu.VMEM((nh, HEAD, HEAD), F32)],
        compiler_params=_params(("arbitrary",)),
    )(*[s[0] for s in srcs], *params)


def _chunk_scan_bwd(chunk_fn, srcs, read, params, states, dsrc, write, out_widths, nh, host_us, name):
    tp = srcs[0][0].shape[0]
    nc = tp // CHUNK
    nb = _chunks_per_step(nc)
    rows = nb * CHUNK
    steps = nc // nb
    ns, npar, nout = len(srcs), len(params), len(out_widths)
    d_arr, d_off = dsrc

    def body(*refs):
        s_refs, p_refs = refs[:ns], refs[ns:ns + npar]
        st_ref, dy_ref = refs[ns + npar:ns + npar + 2]
        o_refs = refs[ns + npar + 2:ns + npar + 2 + nout]
        dp_refs = refs[ns + npar + 2 + nout:ns + 2 * npar + 2 + nout]
        dstate = refs[-1]

        @pl.when(pl.program_id(0) == 0)
        def _():
            dstate[...] = jnp.zeros_like(dstate)
            for r in dp_refs:
                r[...] = jnp.zeros_like(r)
        par = [r[...] for r in p_refs]
        dst = dstate[...]
        dpar = None
        for k in reversed(range(nb)):
            sl = slice(k * CHUNK, (k + 1) * CHUNK)
            seqs = read(s_refs, sl)
            _, vjp = jax.vjp(chunk_fn, *seqs, *par, st_ref[k])
            grads = vjp((_heads(dy_ref[sl, d_off:d_off + nh * HEAD], nh), dst))
            for r, tile in zip(o_refs, write(*grads[:len(seqs)])):
                r[sl, :] = tile.astype(r.dtype)
            gp = grads[len(seqs):len(seqs) + npar]
            dpar = gp if dpar is None else [a + b for a, b in zip(dpar, gp)]
            dst = grads[-1]
        for r, gr in zip(dp_refs, dpar):
            r[...] += gr
        dstate[...] = dst

    rev = lambda c: steps - 1 - c
    src_spec = lambda s: pl.BlockSpec((rows, s[1]), lambda c: (rev(c), s[2]))
    par_spec = lambda a: pl.BlockSpec(a.shape, lambda c: (0, 0, 0))
    out_spec = lambda w: pl.BlockSpec((rows, w), lambda c: (rev(c), 0))
    return tuple(_call(
        body, host_us, name=name,
        out_shape=tuple(_sds((tp, w), dt) for w, dt in out_widths) + tuple(_sds(a.shape, F32) for a in params),
        grid=(steps,),
        in_specs=[src_spec(s) for s in srcs] + [par_spec(a) for a in params]
        + [pl.BlockSpec((nb, nh, HEAD, HEAD), lambda c: (rev(c), 0, 0, 0)), out_spec(d_arr.shape[1])],
        out_specs=[out_spec(w) for w, _ in out_widths] + [par_spec(a) for a in params],
        scratch_shapes=[pltpu.VMEM((nh, HEAD, HEAD), F32)],
        compiler_params=_params(("arbitrary",)),
    )(*[s[0] for s in srcs], *params, states, d_arr))


def _lane_param(p):
    return jnp.broadcast_to(p[:, None, None], (p.shape[0], 1, HEAD))


def _block_diag(w):
    out = jnp.zeros((LRU_WIDTH, LRU_WIDTH), F32)
    for h in range(LRU_HEADS):
        out = out.at[h * HEAD:(h + 1) * HEAD, h * HEAD:(h + 1) * HEAD].set(w[h])
    return out


def _block_diag_inv(w):
    return jnp.stack([w[h * HEAD:(h + 1) * HEAD, h * HEAD:(h + 1) * HEAD] for h in range(LRU_HEADS)])


def _ssd_inputs(xc, proj, lp):
    srcs = ((xc, CONV_W, 0), (proj, SSD_INNER, OFF_SSD_Z // SSD_INNER), (proj, LANE, OFF_SMALL // LANE))
    params = (_lane_param(lp["ssd_dt_bias"]), _lane_param(lp["ssd_a_log"]), _lane_param(lp["ssd_d"]),
              lp["ssd_norm_w"].reshape(SSD_HEADS, 1, HEAD))
    return srcs, params


def _ssd_read(refs, rows):
    xc, z, sm = refs
    return (_heads(xc[rows, 256:768], SSD_HEADS), _heads(xc[rows, 768:896], SSD_GROUPS),
            _heads(xc[rows, 896:1024], SSD_GROUPS), _head_scalars(sm[rows, :], 0, SSD_HEADS), _heads(z[rows, :], SSD_HEADS))


def _ssd_write(dxs, dbm, dcm, ddtr, dz):
    return (jnp.concatenate([_unheads(dxs), _unheads(dbm), _unheads(dcm)], axis=-1), _unheads(dz), _unhead_scalars(ddtr, 0))


def _dn_inputs(xc, proj, lp):
    srcs = ((xc, CONV_W, 0), (proj, 256, OFF_DN_GATE // 256), (proj, LANE, OFF_SMALL // LANE))
    params = (_lane_param(lp["dn_a_log"]), _lane_param(lp["dn_dt_bias"]),
              jnp.broadcast_to(lp["dn_norm_w"][None, None, :], (DN_HEADS, 1, HEAD)))
    return srcs, params


def _dn_read(refs, rows):
    xc, gate, sm = refs
    return (_heads(xc[rows, 1024:1280], DN_HEADS), _heads(xc[rows, 1280:1536], DN_HEADS),
            _heads(xc[rows, 1536:1792], DN_HEADS), _heads(gate[rows, :], DN_HEADS),
            _head_scalars(sm[rows, :], 8, DN_HEADS), _head_scalars(sm[rows, :], 12, DN_HEADS))


def _dn_write(dq, dk, dv, dgate, dbraw, daraw):
    return (jnp.concatenate([_unheads(dq), _unheads(dk), _unheads(dv)], axis=-1), _unheads(dgate),
            _unhead_scalars(dbraw, 8) + _unhead_scalars(daraw, 12))


def _lru_params(lp):
    return (_block_diag(lp["lru_w_a"]), lp["lru_b_a"][None], _block_diag(lp["lru_w_x"]), lp["lru_b_x"][None],
            lp["lru_lambda"][None])


def _conv_params(lp):
    w = jnp.concatenate([lp["lru_conv_w"], lp["ssd_conv_w"], lp["dn_conv_w"]], axis=1)
    b = jnp.concatenate([lp["lru_conv_b"], lp["ssd_conv_b"], jnp.zeros((768,), F32)])[None]
    return w, b


def _mixers_fwd(proj, lp, tag):
    cw, cb = _conv_params(lp)
    xc = _conv_fwd(proj, cw, cb, tag + "_conv")
    o_lru, h_lru = _lru_fwd(xc, proj, *_lru_params(lp), name=tag + "_lru")
    s_srcs, s_par = _ssd_inputs(xc, proj, lp)
    o_ssd, s_states = _chunk_scan_fwd(_ssd_chunk, s_srcs, _ssd_read, s_par, SSD_HEADS, 70, tag + "_ssd")
    d_srcs, d_par = _dn_inputs(xc, proj, lp)
    o_dn, d_states = _chunk_scan_fwd(_dn_chunk, d_srcs, _dn_read, d_par, DN_HEADS, 140, tag + "_dn")
    cat = jnp.concatenate([o_lru, o_ssd, o_dn], axis=1)
    return cat, (xc, h_lru, s_states, d_states)


def _mixers_bwd(proj, lp, saved, dcat, tag):
    xc, h_lru, s_states, d_states = saved
    tp = proj.shape[0]
    cw, cb = _conv_params(lp)
    g = {}
    du, dyraw, dwa, dba, dwx, dbx, dlam = _lru_bwd(xc, proj, h_lru, dcat, *_lru_params(lp), name=tag + "_lru_bwd")
    g["lru_w_a"], g["lru_b_a"], g["lru_w_x"], g["lru_b_x"], g["lru_lambda"] = (
        _block_diag_inv(dwa), dba[0], _block_diag_inv(dwx), dbx[0], dlam[0])

    s_srcs, s_par = _ssd_inputs(xc, proj, lp)
    dxbc, dz, dsm_ssd, dp_dtb, dp_alog, dp_d, dp_nw = _chunk_scan_bwd(
        _ssd_chunk, s_srcs, _ssd_read, s_par, s_states, (dcat, 256), _ssd_write, ((768, F32), (SSD_INNER, BF16), (LANE, F32)), SSD_HEADS, 125,
        tag + "_ssd_bwd")
    g["ssd_dt_bias"], g["ssd_a_log"], g["ssd_d"] = (jnp.sum(p, axis=(1, 2)) for p in (dp_dtb, dp_alog, dp_d))
    g["ssd_norm_w"] = dp_nw.reshape(SSD_INNER)

    d_srcs, d_par = _dn_inputs(xc, proj, lp)
    dqkv, dgate, dsm_dn, dq_alog, dq_dtb, dq_nw = _chunk_scan_bwd(
        _dn_chunk, d_srcs, _dn_read, d_par, d_states, (dcat, 768), _dn_write, ((768, F32), (256, BF16), (LANE, F32)), DN_HEADS, 300,
        tag + "_dn_bwd")
    g["dn_a_log"], g["dn_dt_bias"] = (jnp.sum(p, axis=(1, 2)) for p in (dq_alog, dq_dtb))
    g["dn_norm_w"] = jnp.sum(dq_nw, axis=(0, 1))

    dconv, dcw, dcb = _conv_bwd(proj, cw, cb, jnp.concatenate([du, dxbc, dqkv], axis=1), tag + "_conv_bwd")
    g["lru_conv_w"], g["ssd_conv_w"], g["dn_conv_w"] = dcw[:, :256], dcw[:, 256:1024], dcw[:, 1024:]
    g["lru_conv_b"], g["ssd_conv_b"] = dcb[0, :256], dcb[0, 256:1024]
    dproj = jnp.concatenate([dconv, dyraw, dz, dgate, (dsm_ssd + dsm_dn).astype(BF16)], axis=1)
    return dproj, g


MIXER_PARAMS = ("lru_conv_w", "lru_conv_b", "lru_w_a", "lru_b_a", "lru_w_x", "lru_b_x", "lru_lambda",
                "ssd_conv_w", "ssd_conv_b", "ssd_dt_bias", "ssd_a_log", "ssd_d", "ssd_norm_w",
                "dn_conv_w", "dn_a_log", "dn_dt_bias", "dn_norm_w")


UNITS = ("gate0", "up0", "down0", "win", "wout", "gate1", "up1", "down1")


def _layer_shards(w, l):
    out = {"win": w["w_in"][l], "wout": w["w_out"][l]}
    for j in range(2):
        out[f"gate{j}"], out[f"up{j}"], out[f"down{j}"] = w["ffn_w_gate"][l, j], w["ffn_w_up"][l, j], w["ffn_w_down"][l, j]
    return {k: a.astype(BF16) for k, a in out.items()}


def _enqueue_matrix(name, arr, slab):
    if name == "wout" or not slab:
        return [_enqueue(arr, slab)], 1
    return _enqueue_halves(arr, slab, 1)


def _enqueue_layer(shards):
    return {k: _enqueue_matrix(k, shards[k], False) for k in UNITS}


def _gathered_cols(a):
    return a.transpose(1, 0, 2).reshape(a.shape[1], -1)


def _col_slabs(a):
    return a.reshape(a.shape[0], N_DEV, -1).transpose(1, 0, 2).astype(BF16)


def _row_slabs(a):
    return a.reshape(N_DEV, -1, a.shape[1]).astype(BF16)


def _local_step(x, tgt, small, first, shards):
    s = x.shape[0]
    t_real = N_META + s
    tp = -(-t_real // CHUNK) * CHUNK
    depth = len(first) if shards is None else len(shards)
    h = jnp.concatenate([small["meta"], x, jnp.zeros((tp - t_real, D_MODEL), F32)], axis=0)
    hb = h.astype(BF16)
    ln_g, ln_b = small["ln_g"], small["ln_b"]
    saved, weights = [], []
    queued = first
    for l in range(depth):
        lp = {k: small[k][l] for k in MIXER_PARAMS}
        t = f"l{l}"
        if shards is None:
            get = lambda k, g=first[l]: g[k]
        else:
            get = lambda k, g=queued: _collect_halves(g[k])
            if l + 1 < depth:
                queued = _enqueue_layer(shards[l + 1])

        def ffn_weights(j):
            up = _gathered_cols(get(f"up{j}"))
            return _gathered_cols(get(f"gate{j}")), up

        wgu0 = ffn_weights(0)
        g0, u0, a0 = _ffn_up(hb, *wgu0, t + "_ffn0_up")
        wd0 = get("down0").reshape(-1, D_MODEL)
        y1, h1, h1b = _mm_resid_ln(a0, wd0, h, ln_g[l, 0][None], ln_b[l, 0][None], FFN_RES, t + "_ffn0_down")
        win = _proj_cols(_gathered_cols(get("win")))
        proj = _mm_nn(h1b, win, t + "_in_proj")
        cat, mix_saved = _mixers_fwd(proj, lp, t)
        catb = cat.astype(BF16)
        wout = get("wout").reshape(D_MODEL, D_MODEL)
        y2, h2, h2b = _mm_resid_ln(catb, wout, h1, ln_g[l, 1][None], ln_b[l, 1][None], 1.0, t + "_out_proj")
        wgu1 = ffn_weights(1)
        g1, u1, a1 = _ffn_up(h2b, *wgu1, t + "_ffn1_up")
        wd1 = get("down1").reshape(-1, D_MODEL)
        y3, h3, h3b = _mm_resid_ln(a1, wd1, h2, ln_g[l, 2][None], ln_b[l, 2][None], FFN_RES, t + "_ffn1_down")
        weights.append(([wgu0, wgu1], [wd0, wd1], win, wout))
        saved.append((hb, g0, u0, a0, y1, h1b, proj, mix_saved, catb, y2, h2b, g1, u1, a1, y3))
        h, hb = h3, h3b

    tgt_p = jnp.pad(tgt, ((N_META, tp - t_real), (0, 0)))
    lossv, dh = _loss_grad(h, tgt_p, t_real, "loss")
    loss = jnp.sum(lossv)

    gs = {k: [None] * depth for k in MIXER_PARAMS}
    d_ln_g = [[None] * 3 for _ in range(depth)]
    d_ln_b = [[None] * 3 for _ in range(depth)]
    slabs = [{} for _ in range(depth)]
    send = (lambda k, a: a) if shards is None else (lambda k, a: _enqueue_matrix(k, a, True))

    def ffn_bwd(l, j, xb_in, g, u, a, y, dout):
        t = f"l{l}_ffn{j}"
        wgu, wd = weights[l][0][j], weights[l][1][j]
        dres, dyb, dgam, dbet = _ln_bwd(y, dout, ln_g[l, 2 * j][None], FFN_RES, t + "_ln_bwd")
        d_ln_g[l][2 * j], d_ln_b[l][2 * j] = dgam[0], dbet[0]
        slabs[l][f"down{j}"] = send("down", _row_slabs(_mm_tn(a, dyb, t + "_dwd")))
        dg, du = _ffn_dact(dyb, wd, g, u, t + "_dact")
        slabs[l][f"gate{j}"] = send("gate", _col_slabs(_mm_tn(xb_in, dg, t + "_dwg")))
        slabs[l][f"up{j}"] = send("up", _col_slabs(_mm_tn(xb_in, du, t + "_dwu")))
        return _ffn_dx(dg, du, *wgu, dres, t + "_dx")

    for l in reversed(range(depth)):
        hb_in, g0, u0, a0, y1, h1b, proj, mix_saved, catb, y2, h2b, g1, u1, a1, y3 = saved[l]
        lp = {k: small[k][l] for k in MIXER_PARAMS}
        t = f"l{l}"
        _, _, win, wout = weights[l]
        dh2 = ffn_bwd(l, 1, h2b, g1, u1, a1, y3, dh)
        dres, dyb, dgam, dbet = _ln_bwd(y2, dh2, ln_g[l, 1][None], 1.0, t + "_mix_ln_bwd")
        d_ln_g[l][1], d_ln_b[l][1] = dgam[0], dbet[0]
        slabs[l]["wout"] = send("wout", _row_slabs(_mm_tn(catb, dyb, t + "_dwout")))
        dcat = _mm_nt_add(dyb, wout, None, t + "_dcat")
        dproj, mg = _mixers_bwd(proj, lp, mix_saved, dcat, t)
        for k in MIXER_PARAMS:
            gs[k][l] = mg[k]
        dprojb = dproj.astype(BF16)
        slabs[l]["win"] = send("win", _col_slabs(_proj_cols_inv(_mm_tn(h1b, dprojb, t + "_dwin"))))
        dh1 = _mm_nt_add(dprojb, win, dres, t + "_dh1")
        dh = ffn_bwd(l, 0, hb_in, g0, u0, a0, y1, dh1)

    small_grads = {k: jnp.stack(v) for k, v in gs.items()}
    small_grads["ln_g"] = jnp.stack([jnp.stack(r) for r in d_ln_g])
    small_grads["ln_b"] = jnp.stack([jnp.stack(r) for r in d_ln_b])
    small_grads["meta"] = dh[:N_META]
    return loss, dh[N_META:t_real], small_grads, slabs


def _mesh_pos():
    return lax.axis_index("x"), lax.axis_index("y"), lax.axis_index("c")


def _flip(pos, k):
    x, y, c = pos
    return (1 - x if k & 4 else x, 1 - y if k & 2 else y, 1 - c if k & 1 else c)


def _flat(pos):
    return 4 * pos[0] + 2 * pos[1] + pos[2]


def _exchange_copies(ins, outs, slabs, send, recv, loc):
    pos = _mesh_pos()
    me = _flat(pos)
    local, sends, receives = [], [], []
    for a in range(len(ins)):
        local.append(pltpu.make_async_copy(ins[a].at[me] if slabs[a] else ins[a], outs[a].at[me], loc.at[a]))
    for k in range(1, N_DEV):
        peer = _flip(pos, k)
        for a in range(len(ins)):
            if not slabs[a] and k not in GATHER_FIRST:
                continue
            sem = dict(send_sem=send.at[a, k - 1], recv_sem=recv.at[a, k - 1], device_id=peer, device_id_type=MESH_IDS)
            sends.append(pltpu.make_async_remote_copy(
                src_ref=ins[a].at[_flat(peer)] if slabs[a] else ins[a], dst_ref=outs[a].at[me], **sem))
            receives.append(pltpu.make_async_remote_copy(
                src_ref=ins[a].at[me] if slabs[a] else ins[a], dst_ref=outs[a].at[_flat(peer)], **sem))
    return local, sends, receives


GATHER_FIRST = (1, 2, 4, 6)


def _gather_forwards(ins, outs, slabs, send, recv):
    pos = _mesh_pos()
    sibling = _flip(pos, 1)
    forwards, arrivals = [], []
    for a in range(len(ins)):
        if slabs[a]:
            continue
        for k in GATHER_FIRST[1:]:
            sem = dict(send_sem=send.at[a, k], recv_sem=recv.at[a, k], device_id=sibling, device_id_type=MESH_IDS)
            mine, theirs = _flat(_flip(pos, k)), _flat(_flip(pos, k + 1))
            forwards.append(pltpu.make_async_remote_copy(src_ref=outs[a].at[mine], dst_ref=outs[a].at[mine], **sem))
            arrivals.append(pltpu.make_async_remote_copy(src_ref=outs[a].at[theirs], dst_ref=outs[a].at[theirs], **sem))
    return forwards, arrivals


def _exchange_start(ins, outs, slabs, send, recv, loc):
    local, sends, _ = _exchange_copies(ins, outs, slabs, send, recv, loc)
    for cp in local + sends:
        cp.start()


def _exchange_wait(ins, outs, slabs, send, recv, loc):
    local, sends, receives = _exchange_copies(ins, outs, slabs, send, recv, loc)
    forwards, arrivals = _gather_forwards(ins, outs, slabs, send, recv)
    for cp in receives:
        cp.wait_recv()
    for cp in forwards:
        cp.start()
    for cp in arrivals:
        cp.wait_recv()
    for cp in sends + forwards:
        cp.wait_send()
    for cp in local:
        cp.wait()


def _exchange_shapes(arrs, slabs):
    return tuple(_sds(a.shape if s else (N_DEV,) + a.shape, a.dtype) for a, s in zip(arrs, slabs))


def _exchange_sems(n):
    return [pltpu.SemaphoreType.DMA((n, N_DEV - 1)), pltpu.SemaphoreType.DMA((n, N_DEV - 1)),
            pltpu.SemaphoreType.DMA((n,))]


def _exchange(arrs, slabs, name):
    n = len(arrs)

    def body(*refs):
        ins, outs, sems = refs[:n], refs[n:2 * n], refs[2 * n:]
        _exchange_start(ins, outs, slabs, *sems)
        _exchange_wait(ins, outs, slabs, *sems)

    hbm = pl.BlockSpec(memory_space=pl.ANY)
    return pl.pallas_call(
        body, name=name, out_shape=_exchange_shapes(arrs, slabs), in_specs=[hbm] * n, out_specs=[hbm] * n,
        scratch_shapes=_exchange_sems(n),
    )(*arrs)


def _adam_math(w, g, m, v):
    m = ADAM_B1 * m + (1.0 - ADAM_B1) * g
    v = ADAM_B2 * v + (1.0 - ADAM_B2) * (g * g)
    m_hat = m / (1.0 - ADAM_B1 ** ADAM_STEP)
    v_hat = v / (1.0 - ADAM_B2 ** ADAM_STEP)
    delta = -ADAM_LR * (m_hat / (jnp.sqrt(v_hat) + ADAM_EPS) + ADAM_WD * w)
    return delta, m, v


def _adam(w, g, m, v, name):
    r, c = w.shape
    parts = g.ndim == 3
    tr = _tile(r, 512, 8)

    def body(w_ref, g_ref, m_ref, v_ref, go_ref, d_ref, mo_ref, vo_ref):
        if parts:
            gv = g_ref[0].astype(F32)
            for d in range(1, N_DEV):
                gv = gv + g_ref[d].astype(F32)
        else:
            gv = g_ref[...]
        delta, mn, vn = _adam_math(w_ref[...], gv, m_ref[...], v_ref[...])
        go_ref[...] = gv
        d_ref[...] = delta
        mo_ref[...] = mn
        vo_ref[...] = vn

    blk = pl.BlockSpec((tr, c), lambda i: (i, 0))
    gblk = pl.BlockSpec((N_DEV, tr, c), lambda i: (0, i, 0)) if parts else blk
    out = _sds((r, c), F32)
    return pl.pallas_call(
        body, name=name, out_shape=(out,) * 4, grid=(r // tr,),
        in_specs=[blk, gblk, blk, blk], out_specs=[blk] * 4,
        compiler_params=_params(("arbitrary",)),
    )(w, g, m, v)


def _sum_parts(parts, name):
    _, r, c = parts.shape

    def body(p_ref, o_ref):
        acc = p_ref[0]
        for d in range(1, N_DEV):
            acc = acc + p_ref[d]
        o_ref[...] = acc

    return pl.pallas_call(body, name=name, out_shape=_sds((r, c), F32), compiler_params=_params())(parts)


SMALL_SHARD_AXIS = {
    "meta": 1, "ln_g": 2, "ln_b": 2, "lru_conv_w": 2, "lru_conv_b": None, "lru_w_a": None, "lru_b_a": None,
    "lru_w_x": None, "lru_b_x": None, "lru_lambda": None, "ssd_conv_w": 2, "ssd_conv_b": None, "ssd_dt_bias": None,
    "ssd_a_log": None, "ssd_d": None, "ssd_norm_w": None, "dn_conv_w": 2, "dn_a_log": None, "dn_dt_bias": None,
    "dn_norm_w": None,
}
BIG = ("ffn_w_gate", "ffn_w_up", "ffn_w_down", "w_in", "w_out")
WEIGHT_ORDER = ("meta", "ln_g", "ln_b", "ffn_w_gate", "ffn_w_up", "ffn_w_down", "w_in", "lru_conv_w", "lru_conv_b",
                "lru_w_a", "lru_b_a", "lru_w_x", "lru_b_x", "lru_lambda", "ssd_conv_w", "ssd_conv_b", "ssd_dt_bias",
                "ssd_a_log", "ssd_d", "ssd_norm_w", "dn_conv_w", "dn_a_log", "dn_dt_bias", "dn_norm_w", "w_out")


def _pack(arrs):
    flat = jnp.concatenate([a.reshape(-1) for a in arrs])
    rows = -(-flat.shape[0] // (8 * LANE)) * 8
    return jnp.pad(flat, (0, rows * LANE - flat.shape[0])).reshape(rows, LANE)


def _unpack(buf, shapes, lead=()):
    flat = buf.reshape(lead + (-1,))
    out, off = [], 0
    for s in shapes:
        n = math.prod(s)
        out.append(flat[..., off:off + n].reshape(lead + tuple(s)))
        off += n
    return out


def _proj_cols(w):
    pad = jnp.zeros(w.shape[:-1] + (PROJ_W - D_IN,), w.dtype)
    return jnp.concatenate([w[..., 0:256], w[..., 1024:1792], w[..., 1800:2568], w[..., 256:512], w[..., 512:1024],
                            w[..., 2568:2824], w[..., 1792:1800], w[..., 2824:2832], pad], axis=-1)


def _proj_cols_inv(w):
    return jnp.concatenate([w[..., 0:256], w[..., 1792:2048], w[..., 2048:2560], w[..., 256:1024], w[..., 2816:2824],
                            w[..., 1024:1792], w[..., 2560:2816], w[..., 2824:2832]], axis=-1)


def kernel(x, meta, ln_g, ln_b, ffn_w_gate, ffn_w_up, ffn_w_down, w_in, lru_conv_w, lru_conv_b, lru_w_a, lru_b_a, lru_w_x, lru_b_x, lru_lambda, ssd_conv_w, ssd_conv_b, ssd_dt_bias, ssd_a_log, ssd_d, ssd_norm_w, dn_conv_w, dn_a_log, dn_dt_bias, dn_norm_w, w_out, loss_target, m_meta, m_ln_g, m_ln_b, m_ffn_w_gate, m_ffn_w_up, m_ffn_w_down, m_w_in, m_lru_conv_w, m_lru_conv_b, m_lru_w_a, m_lru_b_a, m_lru_w_x, m_lru_b_x, m_lru_lambda, m_ssd_conv_w, m_ssd_conv_b, m_ssd_dt_bias, m_ssd_a_log, m_ssd_d, m_ssd_norm_w, m_dn_conv_w, m_dn_a_log, m_dn_dt_bias, m_dn_norm_w, m_w_out, v_meta, v_ln_g, v_ln_b, v_ffn_w_gate, v_ffn_w_up, v_ffn_w_down, v_w_in, v_lru_conv_w, v_lru_conv_b, v_lru_w_a, v_lru_b_a, v_lru_w_x, v_lru_b_x, v_lru_lambda, v_ssd_conv_w, v_ssd_conv_b, v_ssd_dt_bias, v_ssd_a_log, v_ssd_d, v_ssd_norm_w, v_dn_conv_w, v_dn_a_log, v_dn_dt_bias, v_dn_norm_w, v_w_out):
    w = dict(meta=meta, ln_g=ln_g, ln_b=ln_b, ffn_w_gate=ffn_w_gate, ffn_w_up=ffn_w_up, ffn_w_down=ffn_w_down, w_in=w_in,
             lru_conv_w=lru_conv_w, lru_conv_b=lru_conv_b, lru_w_a=lru_w_a, lru_b_a=lru_b_a, lru_w_x=lru_w_x,
             lru_b_x=lru_b_x, lru_lambda=lru_lambda, ssd_conv_w=ssd_conv_w, ssd_conv_b=ssd_conv_b, ssd_dt_bias=ssd_dt_bias,
             ssd_a_log=ssd_a_log, ssd_d=ssd_d, ssd_norm_w=ssd_norm_w, dn_conv_w=dn_conv_w, dn_a_log=dn_a_log,
             dn_dt_bias=dn_dt_bias, dn_norm_w=dn_norm_w, w_out=w_out)
    m = dict(meta=m_meta, ln_g=m_ln_g, ln_b=m_ln_b, ffn_w_gate=m_ffn_w_gate, ffn_w_up=m_ffn_w_up, ffn_w_down=m_ffn_w_down,
             w_in=m_w_in, lru_conv_w=m_lru_conv_w, lru_conv_b=m_lru_conv_b, lru_w_a=m_lru_w_a, lru_b_a=m_lru_b_a,
             lru_w_x=m_lru_w_x, lru_b_x=m_lru_b_x, lru_lambda=m_lru_lambda, ssd_conv_w=m_ssd_conv_w, ssd_conv_b=m_ssd_conv_b,
             ssd_dt_bias=m_ssd_dt_bias, ssd_a_log=m_ssd_a_log, ssd_d=m_ssd_d, ssd_norm_w=m_ssd_norm_w, dn_conv_w=m_dn_conv_w,
             dn_a_log=m_dn_a_log, dn_dt_bias=m_dn_dt_bias, dn_norm_w=m_dn_norm_w, w_out=m_w_out)
    v = dict(meta=v_meta, ln_g=v_ln_g, ln_b=v_ln_b, ffn_w_gate=v_ffn_w_gate, ffn_w_up=v_ffn_w_up, ffn_w_down=v_ffn_w_down,
             w_in=v_w_in, lru_conv_w=v_lru_conv_w, lru_conv_b=v_lru_conv_b, lru_w_a=v_lru_w_a, lru_b_a=v_lru_b_a,
             lru_w_x=v_lru_w_x, lru_b_x=v_lru_b_x, lru_lambda=v_lru_lambda, ssd_conv_w=v_ssd_conv_w, ssd_conv_b=v_ssd_conv_b,
             ssd_dt_bias=v_ssd_dt_bias, ssd_a_log=v_ssd_a_log, ssd_d=v_ssd_d, ssd_norm_w=v_ssd_norm_w, dn_conv_w=v_dn_conv_w,
             dn_a_log=v_dn_a_log, dn_dt_bias=v_dn_dt_bias, dn_norm_w=v_dn_norm_w, w_out=v_w_out)
    depth = ln_g.shape[0]
    me = _flat(_mesh_pos())
    small_names = tuple(SMALL_SHARD_AXIS)
    sharded = tuple(k for k in small_names if SMALL_SHARD_AXIS[k] is not None)

    del _QUEUE[:]
    _STANDALONE[0] = 0
    shards = [_layer_shards(w, l) for l in range(depth)]
    small_unit = _enqueue(_pack([w[k] for k in sharded]), False)
    first = _enqueue_layer(shards[0])
    g_small = _collect(small_unit)
    small = {k: w[k] for k in small_names if SMALL_SHARD_AXIS[k] is None}
    for k, piece in zip(sharded, _unpack(g_small, [w[k].shape for k in sharded], lead=(N_DEV,))):
        ax = SMALL_SHARD_AXIS[k]
        full = jnp.moveaxis(piece, 0, ax)
        small[k] = full.reshape(full.shape[:ax] + (N_DEV * w[k].shape[ax],) + full.shape[ax + 2:])

    loss, dx, small_grads, slabs = _local_step(x[0], loss_target[0], small, first, shards)

    r_small = _collect(_enqueue(_pack([small_grads[k] for k in small_names]), False))
    got = [{k: _collect_halves(g) for k, g in layer.items()} for layer in slabs]

    outs = {}
    ffn = lambda name: jnp.stack([jnp.stack([got[l][f"{name}{j}"] for j in range(2)], axis=1) for l in range(depth)], axis=1)
    one = lambda name: jnp.stack([got[l][name] for l in range(depth)], axis=1)
    parts_of = {"ffn_w_gate": ffn("gate"), "ffn_w_up": ffn("up"), "ffn_w_down": ffn("down"), "w_in": one("win"),
                "w_out": one("wout")}
    for k in BIG:
        shp = w[k].shape
        two = lambda a: a.reshape(-1, shp[-1])
        res = _adam(two(w[k]), parts_of[k].reshape(N_DEV, -1, shp[-1]), two(m[k]), two(v[k]), "adam_" + k)
        outs[k] = [a.reshape(shp) for a in res]
    g_full = _unpack(_sum_parts(r_small, "sum_small_grads"), [small[k].shape for k in small_names])
    g_loc = {}
    for k, gf in zip(small_names, g_full):
        ax = SMALL_SHARD_AXIS[k]
        g_loc[k] = gf if ax is None else lax.dynamic_slice_in_dim(gf, me * w[k].shape[ax], w[k].shape[ax], axis=ax)
    res = _adam(_pack([w[k] for k in small_names]), _pack([g_loc[k] for k in small_names]),
                _pack([m[k] for k in small_names]), _pack([v[k] for k in small_names]), "adam_small")
    shapes = [w[k].shape for k in small_names]
    for i, k in enumerate(small_names):
        outs[k] = [_unpack(r, shapes)[i] for r in res]

    loss = lax.psum(loss, ("x", "y", "c"))
    return (loss, dx[None], *[outs[k][0] for k in WEIGHT_ORDER], *[outs[k][1] for k in WEIGHT_ORDER],
            *[outs[k][2] for k in WEIGHT_ORDER], *[outs[k][3] for k in WEIGHT_ORDER])
```

```python
import functools
import math

import jax
import jax.numpy as jnp
from jax import lax
from jax.experimental import pallas as pl
from jax.experimental.pallas import tpu as pltpu

F32 = jnp.float32
BF16 = jnp.bfloat16
HI = lax.Precision.HIGHEST
MESH_IDS = pl.DeviceIdType.MESH

N_DEV = 8
D_MODEL = 1024
DEPTH = 4
N_META = 16
CHUNK = 64
CONV_K = 4
D_FF = 2816
LRU_WIDTH = 256
LRU_HEADS = 4
LRU_C = 8.0
SSD_HEADS = 8
SSD_GROUPS = 2
SSD_HPG = 4
SSD_INNER = 512
DN_HEADS = 4
HEAD = 64
CONV_W = 1792
PROJ_W = 2944
OFF_LRU_Y, OFF_SSD_Z, OFF_DN_GATE, OFF_SMALL = 1792, 2048, 2560, 2816
D_IN = 2832
ALPHA = (2 * DEPTH) ** 0.25
FFN_RES = 0.5
LN_EPS = 1e-5
RMS_EPS = 1e-6
ADAM_LR, ADAM_B1, ADAM_B2, ADAM_EPS, ADAM_WD, ADAM_STEP = 0.001, 0.9, 0.999, 1e-08, 0.01, 10

VMEM_LIMIT = 56 * 1024 * 1024
ROW_TILE_CAP = 832
LANE = 128


def _tile(n, cap, mult=16):
    best = None
    for t in range(mult, min(n, cap) + 1, mult):
        if n % t == 0:
            best = t
    assert best is not None, (n, cap, mult)
    return best


def _params(sem=None):
    return pltpu.CompilerParams(dimension_semantics=sem, vmem_limit_bytes=VMEM_LIMIT)


def _resident(shape, index_map):
    return pl.BlockSpec(shape, index_map, pipeline_mode=pl.Buffered(1))


def _dot(a, b):
    return jnp.dot(a, b, preferred_element_type=F32)


def _dot_nt(a, b):
    return lax.dot_general(a, b, (((1,), (1,)), ((), ())), preferred_element_type=F32)


def _dot_tn(a, b):
    return lax.dot_general(a, b, (((0,), (0,)), ((), ())), preferred_element_type=F32)


def _sds(shape, dtype):
    return jax.ShapeDtypeStruct(shape, dtype)


US_PER_MB = 92.5


class _Unit:
    def __init__(self, arr, slab):
        self.arr, self.slab, self.out = arr, slab, None
        per_peer = arr.size * arr.dtype.itemsize / (N_DEV if slab else 1)
        self.us = per_peer / 1e6 * US_PER_MB * (1.0 if slab else 0.5)


_QUEUE = []
_STANDALONE = [0]


def _enqueue(arr, slab):
    unit = _Unit(arr, slab)
    _QUEUE.append(unit)
    return unit


def _enqueue_halves(arr, slab, axis):
    half = arr.shape[axis] // 2
    parts = (lax.slice_in_dim(arr, 0, half, axis=axis), lax.slice_in_dim(arr, half, arr.shape[axis], axis=axis))
    return [_enqueue(p, slab) for p in parts], axis + (0 if slab else 1)


def _collect_halves(group):
    units, axis = group
    return jnp.concatenate([_collect(u) for u in units], axis=axis)


def _take_units(host_us):
    units = []
    while _QUEUE and host_us >= 0.5 * _QUEUE[0].us:
        host_us -= _QUEUE[0].us
        units.append(_QUEUE.pop(0))
    return units


def _collect(unit):
    if unit.out is None:
        n = _QUEUE.index(unit) + 1
        units = [_QUEUE.pop(0) for _ in range(n)]
        _STANDALONE[0] += 1
        res = _exchange([u.arr for u in units], [u.slab for u in units], f"exchange_{_STANDALONE[0]}")
        for u, r in zip(units, res):
            u.out = r
    return unit.out


def _call(body, host_us, *, name, out_shape, in_specs, out_specs, grid=(), scratch_shapes=(), compiler_params=None):
    units = _take_units(0.7 * host_us)
    kw = dict(name=name, grid=grid, compiler_params=compiler_params)
    if not units:
        return pl.pallas_call(body, out_shape=out_shape, in_specs=in_specs, out_specs=out_specs,
                              scratch_shapes=list(scratch_shapes), **kw)
    single = not isinstance(out_shape, (tuple, list))
    outs = (out_shape,) if single else tuple(out_shape)
    ospecs = [out_specs] if single else list(out_specs)
    nin, nout, nscr, ncm = len(in_specs), len(outs), len(scratch_shapes), len(units)
    slabs = [u.slab for u in units]

    def hosted(*refs):
        ins, c_in = refs[:nin], refs[nin:nin + ncm]
        o, c_out = refs[nin + ncm:nin + ncm + nout], refs[nin + ncm + nout:nin + 2 * ncm + nout]
        scr, sems = refs[nin + 2 * ncm + nout:nin + 2 * ncm + nout + nscr], refs[nin + 2 * ncm + nout + nscr:]
        ids = [pl.program_id(d) for d in range(len(grid))]
        first = functools.reduce(jnp.logical_and, [i == 0 for i in ids])
        last = functools.reduce(jnp.logical_and, [i == g - 1 for i, g in zip(ids, grid)])

        @pl.when(first)
        def _():
            _exchange_start(c_in, c_out, slabs, *sems)
        body(*ins, *o, *scr)

        @pl.when(last)
        def _():
            _exchange_wait(c_in, c_out, slabs, *sems)

    hbm = pl.BlockSpec(memory_space=pl.ANY)
    fn = pl.pallas_call(
        hosted, out_shape=outs + _exchange_shapes([u.arr for u in units], slabs), in_specs=list(in_specs) + [hbm] * ncm,
        out_specs=ospecs + [hbm] * ncm, scratch_shapes=list(scratch_shapes) + _exchange_sems(ncm), **kw)

    def run(*args):
        res = fn(*args, *[u.arr for u in units])
        for u, r in zip(units, res[nout:]):
            u.out = r
        return res[0] if single else tuple(res[:nout])

    return run


def _ffn_up(xb, wg, wu, name):
    tp, d = xb.shape
    f = wg.shape[1]
    tm = _tile(tp, ROW_TILE_CAP)
    tn = _tile(f, 1408, LANE)
    nj = f // tn

    def body(x_ref, wg_ref, wu_ref, g_ref, u_ref, a_ref):
        x = x_ref[...]
        g = _dot(x, wg_ref[...])
        u = _dot(x, wu_ref[...])
        g_ref[...] = g.astype(BF16)
        u_ref[...] = u.astype(BF16)
        a_ref[...] = (g * jax.nn.sigmoid(g) * u).astype(BF16)

    out = _sds((tp, f), BF16)
    return _call(
        body, 60, name=name, out_shape=(out, out, out), grid=(nj, tp // tm),
        in_specs=[pl.BlockSpec((tm, d), lambda j, i: (i, 0)),
                  pl.BlockSpec((d, tn), lambda j, i: (0, j)),
                  pl.BlockSpec((d, tn), lambda j, i: (0, j))],
        out_specs=[pl.BlockSpec((tm, tn), lambda j, i: (i, j))] * 3,
        compiler_params=_params(("arbitrary", "arbitrary")),
    )(xb, wg, wu)


def _mm_resid_ln(a, w, h, gamma, beta, scale, name):
    tp, k = a.shape
    d = w.shape[1]
    tm = _tile(tp, ROW_TILE_CAP)

    def body(a_ref, w_ref, h_ref, g_ref, b_ref, y_ref, o_ref, ob_ref):
        y = ALPHA * h_ref[...] + scale * _dot(a_ref[...], w_ref[...])
        mu = jnp.mean(y, axis=-1, keepdims=True)
        yc = y - mu
        var = jnp.mean(yc * yc, axis=-1, keepdims=True)
        o = yc * lax.rsqrt(var + LN_EPS) * g_ref[...] + b_ref[...]
        y_ref[...] = y
        o_ref[...] = o
        ob_ref[...] = o.astype(BF16)

    row = lambda i: (i, 0)
    fix = lambda i: (0, 0)
    return _call(
        body, 30, name=name, out_shape=(_sds((tp, d), F32), _sds((tp, d), F32), _sds((tp, d), BF16)),
        grid=(tp // tm,),
        in_specs=[pl.BlockSpec((tm, k), row), _resident((k, d), fix), pl.BlockSpec((tm, d), row),
                  pl.BlockSpec((1, d), fix), pl.BlockSpec((1, d), fix)],
        out_specs=[pl.BlockSpec((tm, d), row)] * 3,
        compiler_params=_params(("arbitrary",)),
    )(a, w, h, gamma, beta)


def _mm_nn(xb, w, name):
    tp, k = xb.shape
    n = w.shape[1]
    tm = _tile(tp, ROW_TILE_CAP)

    def body(x_ref, w_ref, o_ref):
        o_ref[...] = _dot(x_ref[...], w_ref[...])

    return _call(
        body, 30, name=name, out_shape=_sds((tp, n), F32), grid=(tp // tm,),
        in_specs=[pl.BlockSpec((tm, k), lambda i: (i, 0)), _resident((k, n), lambda i: (0, 0))],
        out_specs=pl.BlockSpec((tm, n), lambda i: (i, 0)),
        compiler_params=_params(("arbitrary",)),
    )(xb, w)


def _mm_nt_add(a, w, resid, name):
    tp, k = a.shape
    n = w.shape[0]
    tm = _tile(tp, ROW_TILE_CAP)
    has_resid = resid is not None

    def body(*refs):
        if has_resid:
            a_ref, w_ref, r_ref, o_ref = refs
            o_ref[...] = r_ref[...] + _dot_nt(a_ref[...], w_ref[...])
        else:
            a_ref, w_ref, o_ref = refs
            o_ref[...] = _dot_nt(a_ref[...], w_ref[...])

    in_specs = [pl.BlockSpec((tm, k), lambda i: (i, 0)), _resident((n, k), lambda i: (0, 0))]
    args = [a, w]
    if has_resid:
        in_specs.append(pl.BlockSpec((tm, n), lambda i: (i, 0)))
        args.append(resid)
    return _call(
        body, 25, name=name, out_shape=_sds((tp, n), F32), grid=(tp // tm,),
        in_specs=in_specs, out_specs=pl.BlockSpec((tm, n), lambda i: (i, 0)),
        compiler_params=_params(("arbitrary",)),
    )(*args)


def _ffn_dx(dg, du, wg, wu, resid, name):
    tp, f = dg.shape
    d = wg.shape[0]
    tm = _tile(tp, ROW_TILE_CAP)

    def body(dg_ref, du_ref, wg_ref, wu_ref, r_ref, o_ref):
        acc = r_ref[...] + _dot_nt(dg_ref[...], wg_ref[...])
        o_ref[...] = acc + _dot_nt(du_ref[...], wu_ref[...])

    row = lambda i: (i, 0)
    fix = lambda i: (0, 0)
    return _call(
        body, 55, name=name, out_shape=_sds((tp, d), F32), grid=(tp // tm,),
        in_specs=[pl.BlockSpec((tm, f), row), pl.BlockSpec((tm, f), row), _resident((d, f), fix), _resident((d, f), fix),
                  pl.BlockSpec((tm, d), row)],
        out_specs=pl.BlockSpec((tm, d), row),
        compiler_params=_params(("arbitrary",)),
    )(dg, du, wg, wu, resid)


def _mm_tn(a, b, name):
    tp, ka = a.shape
    nb = b.shape[1]
    tt = _tile(tp, ROW_TILE_CAP)
    tk = _tile(ka, 1408, LANE)
    tn = _tile(nb, 1536, LANE)
    if tn < 512:
        tn = nb
    nt = tp // tt

    def body(a_ref, b_ref, o_ref):
        @pl.when(pl.program_id(2) == 0)
        def _():
            o_ref[...] = jnp.zeros_like(o_ref)
        o_ref[...] += _dot_tn(a_ref[...], b_ref[...])

    return _call(
        body, 40, name=name, out_shape=_sds((ka, nb), F32), grid=(ka // tk, nb // tn, nt),
        in_specs=[pl.BlockSpec((tt, tk), lambda i, j, t: (t, i)), pl.BlockSpec((tt, tn), lambda i, j, t: (t, j))],
        out_specs=pl.BlockSpec((tk, tn), lambda i, j, t: (i, j)),
        compiler_params=_params(("arbitrary", "arbitrary", "arbitrary")),
    )(a, b)


def _ln_bwd(y, dout, gamma, scale, name):
    tp, d = y.shape
    tm = _tile(tp, ROW_TILE_CAP)

    def body(y_ref, do_ref, g_ref, dres_ref, dyb_ref, dg_ref, db_ref):
        @pl.when(pl.program_id(0) == 0)
        def _():
            dg_ref[...] = jnp.zeros_like(dg_ref)
            db_ref[...] = jnp.zeros_like(db_ref)
        yv = y_ref[...]
        do = do_ref[...]
        mu = jnp.mean(yv, axis=-1, keepdims=True)
        yc = yv - mu
        var = jnp.mean(yc * yc, axis=-1, keepdims=True)
        rstd = lax.rsqrt(var + LN_EPS)
        xhat = yc * rstd
        dxh = do * g_ref[...]
        m1 = jnp.mean(dxh, axis=-1, keepdims=True)
        m2 = jnp.mean(dxh * xhat, axis=-1, keepdims=True)
        dy = rstd * (dxh - m1 - xhat * m2)
        dres_ref[...] = ALPHA * dy
        dyb_ref[...] = (scale * dy).astype(BF16)
        dg_ref[...] += jnp.sum(do * xhat, axis=0, keepdims=True)
        db_ref[...] += jnp.sum(do, axis=0, keepdims=True)

    row = lambda i: (i, 0)
    fix = lambda i: (0, 0)
    return _call(
        body, 22, name=name,
        out_shape=(_sds((tp, d), F32), _sds((tp, d), BF16), _sds((1, d), F32), _sds((1, d), F32)),
        grid=(tp // tm,),
        in_specs=[pl.BlockSpec((tm, d), row), pl.BlockSpec((tm, d), row), pl.BlockSpec((1, d), fix)],
        out_specs=[pl.BlockSpec((tm, d), row), pl.BlockSpec((tm, d), row), pl.BlockSpec((1, d), fix),
                   pl.BlockSpec((1, d), fix)],
        compiler_params=_params(("arbitrary",)),
    )(y, dout, gamma)


def _ffn_dact(dyb, wd, g, u, name):
    tp, d = dyb.shape
    f = wd.shape[0]
    tm = _tile(tp, ROW_TILE_CAP)
    tn = _tile(f, 1408, LANE)

    def body(dy_ref, w_ref, g_ref, u_ref, dg_ref, du_ref):
        dact = _dot_nt(dy_ref[...], w_ref[...])
        gv = g_ref[...].astype(F32)
        uv = u_ref[...].astype(F32)
        sg = jax.nn.sigmoid(gv)
        dg_ref[...] = (dact * uv * (sg * (1.0 + gv * (1.0 - sg)))).astype(BF16)
        du_ref[...] = (dact * (gv * sg)).astype(BF16)

    out = _sds((tp, f), BF16)
    blk = pl.BlockSpec((tm, tn), lambda j, i: (i, j))
    return _call(
        body, 55, name=name, out_shape=(out, out), grid=(f // tn, tp // tm),
        in_specs=[pl.BlockSpec((tm, d), lambda j, i: (i, 0)), pl.BlockSpec((tn, d), lambda j, i: (j, 0)), blk, blk],
        out_specs=[blk, blk],
        compiler_params=_params(("arbitrary", "arbitrary")),
    )(dyb, wd, g, u)


def _loss_grad(o, tgt, t_real, name):
    tp, d = o.shape
    tm = _tile(tp, ROW_TILE_CAP)

    def body(o_ref, t_ref, l_ref, d_ref):
        i = pl.program_id(0)

        @pl.when(i == 0)
        def _():
            l_ref[...] = jnp.zeros_like(l_ref)
        rows = i * tm + lax.broadcasted_iota(jnp.int32, (tm, 1), 0)
        real = jnp.logical_and(rows >= N_META, rows < t_real)
        err = jnp.where(real, o_ref[...] - t_ref[...], 0.0)
        d_ref[...] = err * (1.0 / d)
        l_ref[...] += jnp.sum(err * err, axis=0, keepdims=True) * (0.5 / d)

    row = lambda i: (i, 0)
    return pl.pallas_call(
        body, name=name, out_shape=(_sds((1, d), F32), _sds((tp, d), F32)), grid=(tp // tm,),
        in_specs=[pl.BlockSpec((tm, d), row), pl.BlockSpec((tm, d), row)],
        out_specs=[pl.BlockSpec((1, d), lambda i: (0, 0)), pl.BlockSpec((tm, d), row)],
        compiler_params=_params(("arbitrary",)),
    )(o, tgt)


CONV_TC = 256


def _silu_grad(y):
    s = jax.nn.sigmoid(y)
    return s * (1.0 + y * (1.0 - s))


def _conv_taps(x_ref, w, r0, rb):
    cur = x_ref[r0:r0 + rb, :]
    prev = x_ref[r0 - 8:r0, :] if r0 > 0 else jnp.zeros((8, cur.shape[1]), F32)
    xcat = jnp.concatenate([prev, cur], axis=0)
    taps = [xcat[5 + j:5 + j + rb] for j in range(CONV_K - 1)] + [cur]
    y = w[0:1] * taps[0]
    for j in range(1, CONV_K):
        y = y + w[j:j + 1] * taps[j]
    return y, taps


def _conv_fwd(proj, w, b, name):
    tp = proj.shape[0]
    rb = _tile(tp, ROW_TILE_CAP, 8)

    def body(x_ref, w_ref, b_ref, o_ref):
        gated = pl.program_id(0) > 0
        wv = w_ref[...]
        bv = b_ref[...]
        for r0 in range(0, tp, rb):
            y, _ = _conv_taps(x_ref, wv, r0, rb)
            y = y + bv
            o_ref[r0:r0 + rb, :] = jnp.where(gated, y * jax.nn.sigmoid(y), y)

    col = lambda j: (0, j)
    return _call(
        body, 25, name=name, out_shape=_sds((tp, CONV_W), F32), grid=(CONV_W // CONV_TC,),
        in_specs=[pl.BlockSpec((tp, CONV_TC), col), pl.BlockSpec((CONV_K, CONV_TC), col), pl.BlockSpec((1, CONV_TC), col)],
        out_specs=pl.BlockSpec((tp, CONV_TC), col),
        compiler_params=_params(("arbitrary",)),
    )(proj, w, b)


def _conv_bwd(proj, w, b, dxc, name):
    tp = proj.shape[0]
    rb = _tile(tp, ROW_TILE_CAP, 8)

    def body(x_ref, w_ref, b_ref, d_ref, dx_ref, dw_ref, db_ref, dy_scr):
        gated = pl.program_id(0) > 0
        wv = w_ref[...]
        bv = b_ref[...]
        dw = [jnp.zeros((1, CONV_TC), F32) for _ in range(CONV_K)]
        db = jnp.zeros((1, CONV_TC), F32)
        for r0 in range(0, tp, rb):
            y, taps = _conv_taps(x_ref, wv, r0, rb)
            y = y + bv
            d = d_ref[r0:r0 + rb, :]
            dy = jnp.where(gated, d * _silu_grad(y), d)
            dy_scr[r0:r0 + rb, :] = dy
            for j in range(CONV_K):
                dw[j] = dw[j] + jnp.sum(dy * taps[j], axis=0, keepdims=True)
            db = db + jnp.sum(dy, axis=0, keepdims=True)
        for j in range(CONV_K):
            dw_ref[j:j + 1, :] = dw[j]
        db_ref[...] = db
        for r0 in range(0, tp, rb):
            cur = dy_scr[r0:r0 + rb, :]
            nxt = dy_scr[r0 + rb:r0 + rb + 8, :] if r0 + rb < tp else jnp.zeros((8, CONV_TC), F32)
            dcat = jnp.concatenate([cur, nxt], axis=0)
            dx = wv[3:4] * cur
            for s in range(1, CONV_K):
                dx = dx + wv[3 - s:4 - s] * dcat[s:s + rb]
            dx_ref[r0:r0 + rb, :] = dx.astype(BF16)

    col = lambda j: (0, j)
    return _call(
        body, 70, name=name,
        out_shape=(_sds((tp, CONV_W), BF16), _sds((CONV_K, CONV_W), F32), _sds((1, CONV_W), F32)),
        grid=(CONV_W // CONV_TC,),
        in_specs=[pl.BlockSpec((tp, CONV_TC), col), pl.BlockSpec((CONV_K, CONV_TC), col), pl.BlockSpec((1, CONV_TC), col),
                  pl.BlockSpec((tp, CONV_TC), col)],
        out_specs=[pl.BlockSpec((tp, CONV_TC), col), pl.BlockSpec((CONV_K, CONV_TC), col), pl.BlockSpec((1, CONV_TC), col)],
        scratch_shapes=[pltpu.VMEM((tp, CONV_TC), F32)],
        compiler_params=_params(("arbitrary",)),
    )(proj, w, b, dxc)


def _neg_expm1(x):
    series = -x * (1.0 + x * (0.5 + x * (1.0 / 6.0 + x * (1.0 / 24.0))))
    return jnp.where(jnp.abs(x) < 0.03, series, 1.0 - jnp.exp(x))


def _lru_gates(u, wa, ba, wx, bx, lam):
    r = jax.nn.sigmoid(jnp.dot(u, wa, precision=HI, preferred_element_type=F32) + ba)
    i = jax.nn.sigmoid(jnp.dot(u, wx, precision=HI, preferred_element_type=F32) + bx)
    log_a = -LRU_C * r * jax.nn.softplus(-lam)
    a = jnp.exp(log_a)
    b = jnp.sqrt(_neg_expm1(2.0 * log_a)) * (i * u)
    return a, b


def _lru_specs(tm, flip, n_tiles):
    idx = (lambda i: (n_tiles - 1 - i, 0)) if flip else (lambda i: (i, 0))
    return idx, lambda i: (0, 0)


def _lru_fwd(xc, proj, wa, ba, wx, bx, lam, name):
    tp = xc.shape[0]
    w = LRU_WIDTH
    tm = _tile(tp, ROW_TILE_CAP, 8)
    ycol = OFF_LRU_Y // w

    def body(u_ref, y_ref, wa_ref, ba_ref, wx_ref, bx_ref, lam_ref, o_ref, h_ref, a_scr, b_scr, carry):
        @pl.when(pl.program_id(0) == 0)
        def _():
            carry[...] = jnp.zeros_like(carry)
        a, b = _lru_gates(u_ref[...], wa_ref[...], ba_ref[...], wx_ref[...], bx_ref[...], lam_ref[...])
        a_scr[...] = a
        b_scr[...] = b

        def step(t, h):
            h = a_scr[pl.ds(t, 1), :] * h + b_scr[pl.ds(t, 1), :]
            h_ref[pl.ds(t, 1), :] = h
            return h

        carry[0:1, :] = lax.fori_loop(0, tm, step, carry[0:1, :])
        o_ref[...] = (h_ref[...] * jax.nn.gelu(y_ref[...])).astype(BF16)

    fix = lambda i: (0, 0)
    return _call(
        body, 40, name=name, out_shape=(_sds((tp, w), BF16), _sds((tp, w), F32)), grid=(tp // tm,),
        in_specs=[pl.BlockSpec((tm, w), lambda i: (i, 0)), pl.BlockSpec((tm, w), lambda i: (i, ycol)),
                  pl.BlockSpec((w, w), fix), pl.BlockSpec((1, w), fix), pl.BlockSpec((w, w), fix),
                  pl.BlockSpec((1, w), fix), pl.BlockSpec((1, w), fix)],
        out_specs=[pl.BlockSpec((tm, w), lambda i: (i, 0))] * 2,
        scratch_shapes=[pltpu.VMEM((tm, w), F32), pltpu.VMEM((tm, w), F32), pltpu.VMEM((8, w), F32)],
        compiler_params=_params(("arbitrary",)),
    )(xc, proj, wa, ba, wx, bx, lam)


def _lru_bwd(xc, proj, h, dout, wa, ba, wx, bx, lam, name):
    tp = xc.shape[0]
    w = LRU_WIDTH
    tm = _tile(tp, ROW_TILE_CAP, 8)
    nt = tp // tm
    ycol = OFF_LRU_Y // w
    rev = lambda i: (nt - 1 - i, 0)
    prev8 = lambda i: (jnp.maximum((nt - 1 - i) * (tm // 8) - 1, 0), 0)

    def body(u_ref, y_ref, h_ref, hp_ref, do_ref, wa_ref, ba_ref, wx_ref, bx_ref, lam_ref,
             du_ref, dy_ref, dwa_ref, dba_ref, dwx_ref, dbx_ref, dlam_ref,
             a_scr, dh_scr, g_scr, da_scr, hext, carry):
        i = pl.program_id(0)

        @pl.when(i == 0)
        def _():
            carry[...] = jnp.zeros_like(carry)
            for r in (dwa_ref, dba_ref, dwx_ref, dbx_ref, dlam_ref):
                r[...] = jnp.zeros_like(r)
        params = (wa_ref[...], ba_ref[...], wx_ref[...], bx_ref[...], lam_ref[...])
        (a, _), gates_vjp = jax.vjp(_lru_gates, u_ref[...], *params)
        gel, gelu_vjp = jax.vjp(jax.nn.gelu, y_ref[...])
        do = do_ref[...]
        hv = h_ref[...]
        dy_ref[...] = gelu_vjp(do * hv)[0].astype(BF16)
        a_scr[...] = a
        dh_scr[...] = do * gel
        hext[0:8, :] = jnp.where(i == nt - 1, 0.0, hp_ref[...])
        hext[8:8 + tm, :] = hv

        def step(s, c):
            t = tm - 1 - s
            g = dh_scr[pl.ds(t, 1), :] + c
            g_scr[pl.ds(t, 1), :] = g
            da_scr[pl.ds(t, 1), :] = g * hext[pl.ds(t + 7, 1), :]
            return a_scr[pl.ds(t, 1), :] * g

        carry[0:1, :] = lax.fori_loop(0, tm, step, carry[0:1, :])
        du, dwa, dba, dwx, dbx, dlam = gates_vjp((da_scr[...], g_scr[...]))
        du_ref[...] = du
        dwa_ref[...] += dwa
        dba_ref[...] += dba
        dwx_ref[...] += dwx
        dbx_ref[...] += dbx
        dlam_ref[...] += dlam

    fix = lambda i: (0, 0)
    tile = pl.BlockSpec((tm, w), rev)
    mat = pl.BlockSpec((w, w), fix)
    vec = pl.BlockSpec((1, w), fix)
    return _call(
        body, 90, name=name,
        out_shape=(_sds((tp, w), F32), _sds((tp, w), BF16), _sds((w, w), F32), _sds((1, w), F32), _sds((w, w), F32),
                   _sds((1, w), F32), _sds((1, w), F32)),
        grid=(nt,),
        in_specs=[tile, pl.BlockSpec((tm, w), lambda i: (nt - 1 - i, ycol)), tile, pl.BlockSpec((8, w), prev8), tile,
                  mat, vec, mat, vec, vec],
        out_specs=[tile, tile, mat, vec, mat, vec, vec],
        scratch_shapes=[pltpu.VMEM((tm, w), F32)] * 4 + [pltpu.VMEM((tm + 8, w), F32), pltpu.VMEM((8, w), F32)],
        compiler_params=_params(("arbitrary",)),
    )(xc, proj, h, h, dout, wa, ba, wx, bx, lam)


def _dot3(a, b, dims):
    ah = a.astype(BF16)
    al = (a - ah.astype(F32)).astype(BF16)
    bh = b.astype(BF16)
    bl = (b - bh.astype(F32)).astype(BF16)
    dot = lambda x, y: lax.dot_general(x, y, (dims, ((0,), (0,))), preferred_element_type=F32)
    return dot(ah, bh) + (dot(ah, bl) + dot(al, bh))


@jax.custom_vjp
def _bmm(a, b):
    return _dot3(a, b, ((2,), (1,)))


@jax.custom_vjp
def _bmm_nt(a, b):
    return _dot3(a, b, ((2,), (2,)))


@jax.custom_vjp
def _bmm_tn(a, b):
    return _dot3(a, b, ((1,), (1,)))


_bmm.defvjp(lambda a, b: (_bmm(a, b), (a, b)), lambda r, g: (_bmm_nt(g, r[1]), _bmm_tn(r[0], g)))
_bmm_nt.defvjp(lambda a, b: (_bmm_nt(a, b), (a, b)), lambda r, g: (_bmm(g, r[1]), _bmm_tn(g, r[0])))
_bmm_tn.defvjp(lambda a, b: (_bmm_tn(a, b), (a, b)), lambda r, g: (_bmm_nt(r[1], g), _bmm(r[0], g)))


def _dot1(a, b, dims):
    return lax.dot_general(a.astype(BF16), b.astype(BF16), (dims, ((0,), (0,))), preferred_element_type=F32)


@jax.custom_vjp
def _bmm1(a, b):
    return _dot1(a, b, ((2,), (1,)))


@jax.custom_vjp
def _bmm1_nt(a, b):
    return _dot1(a, b, ((2,), (2,)))


@jax.custom_vjp
def _bmm1_tn(a, b):
    return _dot1(a, b, ((1,), (1,)))


_bmm1.defvjp(lambda a, b: (_bmm1(a, b), (a, b)), lambda r, g: (_bmm1_nt(g, r[1]), _bmm1_tn(r[0], g)))
_bmm1_nt.defvjp(lambda a, b: (_bmm1_nt(a, b), (a, b)), lambda r, g: (_bmm1(g, r[1]), _bmm1_tn(g, r[0])))
_bmm1_tn.defvjp(lambda a, b: (_bmm1_tn(a, b), (a, b)), lambda r, g: (_bmm1_nt(r[1], g), _bmm1(r[0], g)))


def _chunk_masks(nh):
    r = lax.broadcasted_iota(jnp.int32, (CHUNK, CHUNK), 0)
    c = lax.broadcasted_iota(jnp.int32, (CHUNK, CHUNK), 1)
    full = lambda m: jnp.broadcast_to(m[None], (nh, CHUNK, CHUNK))
    return r, c, full


def _decay_terms(g, nh):
    r, c, full = _chunk_masks(nh)
    incl = r >= c
    cs = _bmm(full(incl.astype(F32)), g)
    cs_t = jnp.swapaxes(cs, 1, 2)
    tot = jnp.broadcast_to(jnp.sum(g, axis=1, keepdims=True), g.shape)
    m = full(incl)
    decay = jnp.where(m, jnp.exp(jnp.where(m, cs - cs_t, 0.0)), 0.0)
    return cs, decay, tot


def _rep_groups(x):
    return jnp.concatenate([jnp.broadcast_to(x[g:g + 1], (SSD_HPG,) + x.shape[1:]) for g in range(SSD_GROUPS)], axis=0)


def _ssd_chunk(xs, bm, cm, dtr, z, p_dtb, p_alog, p_d, p_nw, state, saved=None):
    nh = SSD_HEADS
    dt = jax.nn.softplus(dtr + p_dtb)
    a = dt * (-jnp.exp(p_alog))
    x = xs * dt
    cs, decay, tot = _decay_terms(a, nh)
    b8 = _rep_groups(bm)
    c8 = _rep_groups(cm)
    y = _bmm1(_rep_groups(_bmm1_nt(cm, bm)) * decay, x)
    y = y + _bmm1_nt(c8, state) * jnp.exp(cs)
    new_state = state * jnp.exp(tot) + _bmm1_tn(x * jnp.exp(tot - cs), b8)
    y = y + p_d * xs
    y = y * (z * jax.nn.sigmoid(z))
    ss = jnp.sum(y * y, axis=-1, keepdims=True)
    ssg = jnp.concatenate(
        [jnp.broadcast_to(jnp.sum(ss[g * SSD_HPG:(g + 1) * SSD_HPG], axis=0, keepdims=True), (SSD_HPG, CHUNK, 1))
         for g in range(SSD_GROUPS)], axis=0)
    y = y * lax.rsqrt(ssg * (1.0 / (SSD_HPG * HEAD)) + RMS_EPS) * p_nw
    return y, new_state, None


@jax.custom_vjp
def _unit_lower_inverse(m):
    r, c, full = _chunk_masks(m.shape[0])
    eye = full((r == c).astype(F32))
    md = jnp.where(full((r // 16) == (c // 16)), m, 0.0)
    mo = m - md
    x = eye - md
    p = _bmm(md, md)
    x = x + _bmm(x, p)
    p = _bmm(p, p)
    x = x + _bmm(x, p)
    p = _bmm(p, p)
    x = x + _bmm(x, p)
    n = _bmm(x, mo)
    y = x - _bmm(n, x)
    return y + _bmm(_bmm(n, n), y)


def _unit_lower_inverse_fwd(m):
    t = _unit_lower_inverse(m)
    return t, t


_unit_lower_inverse.defvjp(_unit_lower_inverse_fwd, lambda t, g: (-_bmm_nt(_bmm_tn(t, g), t),))


@jax.custom_vjp
def _saved_inverse(m, t):
    return t


_saved_inverse.defvjp(lambda m, t: (t, t), lambda t, g: (-_bmm_nt(_bmm_tn(t, g), t), jnp.zeros_like(t)))


def _dn_chunk(q, k, v, gate, braw, araw, p_alog, p_dtb, p_nw, state, saved=None):
    nh = DN_HEADS
    r, c, full = _chunk_masks(nh)
    q = q * lax.rsqrt(jnp.sum(q * q, axis=-1, keepdims=True) + RMS_EPS) * (HEAD ** -0.5)
    k = k * lax.rsqrt(jnp.sum(k * k, axis=-1, keepdims=True) + RMS_EPS)
    beta = jax.nn.sigmoid(braw)
    g = -jnp.exp(p_alog) * jax.nn.softplus(araw + p_dtb)
    gcs, decay, tot = _decay_terms(g, nh)
    kb = k * beta
    vb = v * beta
    m = jnp.where(full(r > c), _bmm1_nt(kb, k) * decay, 0.0)
    t = _unit_lower_inverse(m) if saved is None else _saved_inverse(m, saved)
    egcs = jnp.exp(gcs)
    u = _bmm(t, vb)
    w = _bmm(t, kb * egcs)
    attn = _bmm1_nt(q, k) * decay
    v_new = u - _bmm1(w, state)
    out = _bmm1(q * egcs, state) + _bmm1(attn, v_new)
    new_state = state * jnp.exp(tot) + _bmm1_tn(k * jnp.exp(tot - gcs), v_new)
    out = out * lax.rsqrt(jnp.mean(out * out, axis=-1, keepdims=True) + RMS_EPS) * p_nw
    return out * (gate * jax.nn.sigmoid(gate)), new_state, t


def _chunks_per_step(nc):
    return max(n for n in range(1, 6) if nc % n == 0)


def _heads(t, n):
    return jnp.stack([t[:, HEAD * h:HEAD * (h + 1)] for h in range(n)])


def _unheads(a):
    return jnp.concatenate([a[h] for h in range(a.shape[0])], axis=-1)


def _head_scalars(t, off, n):
    return jnp.stack([jnp.broadcast_to(t[:, off + h:off + h + 1], (CHUNK, HEAD)) for h in range(n)])


def _unhead_scalars(d, off):
    red = jnp.sum(d, axis=-1, keepdims=True)
    lane = lax.broadcasted_iota(jnp.int32, (CHUNK, LANE), 1)
    out = jnp.zeros((CHUNK, LANE), F32)
    for h in range(d.shape[0]):
        out = out + jnp.where(lane == off + h, red[h], 0.0)
    return out


def _chunk_scan_fwd(chunk_fn, srcs, read, params, nh, keeps, host_us, name):
    tp = srcs[0][0].shape[0]
    nc = tp // CHUNK
    nb = _chunks_per_step(nc)
    rows = nb * CHUNK
    ns, npar = len(srcs), len(params)

    def body(*refs):
        s_refs, p_refs = refs[:ns], refs[ns:ns + npar]
        y_ref, st_ref = refs[ns + npar:ns + npar + 2]
        keep_ref = refs[ns + npar + 2] if keeps else None
        state = refs[-1]

        @pl.when(pl.program_id(0) == 0)
        def _():
            state[...] = jnp.zeros_like(state)
        par = [r[...] for r in p_refs]
        st = state[...]
        for k in range(nb):
            sl = slice(k * CHUNK, (k + 1) * CHUNK)
            st_ref[k] = st
            y, st, kept = chunk_fn(*read(s_refs, sl), *par, st)
            y_ref[sl, :] = _unheads(y).astype(BF16)
            if keeps:
                keep_ref[k] = kept
        state[...] = st

    src_spec = lambda s: pl.BlockSpec((rows, s[1]), lambda c: (c, s[2]))
    par_spec = lambda a: pl.BlockSpec(a.shape, lambda c: (0, 0, 0))
    per_chunk = (_sds((nc, nh, HEAD, HEAD), F32), pl.BlockSpec((nb, nh, HEAD, HEAD), lambda c: (c, 0, 0, 0)))
    n_chunk_outs = 2 if keeps else 1
    return _call(
        body, host_us, name=name,
        out_shape=(_sds((tp, nh * HEAD), BF16),) + (per_chunk[0],) * n_chunk_outs,
        grid=(nc // nb,),
        in_specs=[src_spec(s) for s in srcs] + [par_spec(a) for a in params],
        out_specs=[pl.BlockSpec((rows, nh * HEAD), lambda c: (c, 0))] + [per_chunk[1]] * n_chunk_outs,
        scratch_shapes=[pltpu.VMEM((nh, HEAD, HEAD), F32)],
        compiler_params=_params(("arbitrary",)),
    )(*[s[0] for s in srcs], *params)


def _chunk_scan_bwd(chunk_fn, srcs, read, params, states, kept, dsrc, write, out_widths, nh, host_us, name):
    tp = srcs[0][0].shape[0]
    nc = tp // CHUNK
    nb = _chunks_per_step(nc)
    rows = nb * CHUNK
    steps = nc // nb
    ns, npar, nout = len(srcs), len(params), len(out_widths)
    d_arr, d_off = dsrc
    per_chunk = [states] if kept is None else [states, kept]
    nin = ns + npar + len(per_chunk) + 1

    def body(*refs):
        s_refs, p_refs = refs[:ns], refs[ns:ns + npar]
        st_ref, dy_ref = refs[ns + npar], refs[nin - 1]
        kept_ref = None if kept is None else refs[ns + npar + 1]
        o_refs = refs[nin:nin + nout]
        dp_refs = refs[nin + nout:nin + nout + npar]
        dstate = refs[-1]

        @pl.when(pl.program_id(0) == 0)
        def _():
            dstate[...] = jnp.zeros_like(dstate)
            for r in dp_refs:
                r[...] = jnp.zeros_like(r)
        par = [r[...] for r in p_refs]
        dst = dstate[...]
        dpar = None
        for k in reversed(range(nb)):
            sl = slice(k * CHUNK, (k + 1) * CHUNK)
            seqs = read(s_refs, sl)
            saved = None if kept is None else kept_ref[k]
            _, vjp = jax.vjp(lambda *a: chunk_fn(*a, saved=saved)[:2], *seqs, *par, st_ref[k])
            grads = vjp((_heads(dy_ref[sl, d_off:d_off + nh * HEAD], nh), dst))
            for r, tile in zip(o_refs, write(*grads[:len(seqs)])):
                r[sl, :] = tile.astype(r.dtype)
            gp = grads[len(seqs):len(seqs) + npar]
            dpar = gp if dpar is None else [a + b for a, b in zip(dpar, gp)]
            dst = grads[-1]
        for r, gr in zip(dp_refs, dpar):
            r[...] += gr
        dstate[...] = dst

    rev = lambda c: steps - 1 - c
    src_spec = lambda s: pl.BlockSpec((rows, s[1]), lambda c: (rev(c), s[2]))
    par_spec = lambda a: pl.BlockSpec(a.shape, lambda c: (0, 0, 0))
    out_spec = lambda w: pl.BlockSpec((rows, w), lambda c: (rev(c), 0))
    return tuple(_call(
        body, host_us, name=name,
        out_shape=tuple(_sds((tp, w), dt) for w, dt in out_widths) + tuple(_sds(a.shape, F32) for a in params),
        grid=(steps,),
        in_specs=[src_spec(s) for s in srcs] + [par_spec(a) for a in params]
        + [pl.BlockSpec((nb, nh, HEAD, HEAD), lambda c: (rev(c), 0, 0, 0))] * len(per_chunk) + [out_spec(d_arr.shape[1])],
        out_specs=[out_spec(w) for w, _ in out_widths] + [par_spec(a) for a in params],
        scratch_shapes=[pltpu.VMEM((nh, HEAD, HEAD), F32)],
        compiler_params=_params(("arbitrary",)),
    )(*[s[0] for s in srcs], *params, *per_chunk, d_arr))


def _lane_param(p):
    return jnp.broadcast_to(p[:, None, None], (p.shape[0], 1, HEAD))


def _block_diag(w):
    out = jnp.zeros((LRU_WIDTH, LRU_WIDTH), F32)
    for h in range(LRU_HEADS):
        out = out.at[h * HEAD:(h + 1) * HEAD, h * HEAD:(h + 1) * HEAD].set(w[h])
    return out


def _block_diag_inv(w):
    return jnp.stack([w[h * HEAD:(h + 1) * HEAD, h * HEAD:(h + 1) * HEAD] for h in range(LRU_HEADS)])


def _ssd_inputs(xc, proj, lp):
    srcs = ((xc, CONV_W, 0), (proj, SSD_INNER, OFF_SSD_Z // SSD_INNER), (proj, LANE, OFF_SMALL // LANE))
    params = (_lane_param(lp["ssd_dt_bias"]), _lane_param(lp["ssd_a_log"]), _lane_param(lp["ssd_d"]),
              lp["ssd_norm_w"].reshape(SSD_HEADS, 1, HEAD))
    return srcs, params


def _ssd_read(refs, rows):
    xc, z, sm = refs
    return (_heads(xc[rows, 256:768], SSD_HEADS), _heads(xc[rows, 768:896], SSD_GROUPS),
            _heads(xc[rows, 896:1024], SSD_GROUPS), _head_scalars(sm[rows, :], 0, SSD_HEADS), _heads(z[rows, :], SSD_HEADS))


def _ssd_write(dxs, dbm, dcm, ddtr, dz):
    return (jnp.concatenate([_unheads(dxs), _unheads(dbm), _unheads(dcm)], axis=-1), _unheads(dz), _unhead_scalars(ddtr, 0))


def _dn_inputs(xc, proj, lp):
    srcs = ((xc, CONV_W, 0), (proj, 256, OFF_DN_GATE // 256), (proj, LANE, OFF_SMALL // LANE))
    params = (_lane_param(lp["dn_a_log"]), _lane_param(lp["dn_dt_bias"]),
              jnp.broadcast_to(lp["dn_norm_w"][None, None, :], (DN_HEADS, 1, HEAD)))
    return srcs, params


def _dn_read(refs, rows):
    xc, gate, sm = refs
    return (_heads(xc[rows, 1024:1280], DN_HEADS), _heads(xc[rows, 1280:1536], DN_HEADS),
            _heads(xc[rows, 1536:1792], DN_HEADS), _heads(gate[rows, :], DN_HEADS),
            _head_scalars(sm[rows, :], 8, DN_HEADS), _head_scalars(sm[rows, :], 12, DN_HEADS))


def _dn_write(dq, dk, dv, dgate, dbraw, daraw):
    return (jnp.concatenate([_unheads(dq), _unheads(dk), _unheads(dv)], axis=-1), _unheads(dgate),
            _unhead_scalars(dbraw, 8) + _unhead_scalars(daraw, 12))


def _lru_params(lp):
    return (_block_diag(lp["lru_w_a"]), lp["lru_b_a"][None], _block_diag(lp["lru_w_x"]), lp["lru_b_x"][None],
            lp["lru_lambda"][None])


def _conv_params(lp):
    w = jnp.concatenate([lp["lru_conv_w"], lp["ssd_conv_w"], lp["dn_conv_w"]], axis=1)
    b = jnp.concatenate([lp["lru_conv_b"], lp["ssd_conv_b"], jnp.zeros((768,), F32)])[None]
    return w, b


def _mixers_fwd(proj, lp, tag):
    cw, cb = _conv_params(lp)
    xc = _conv_fwd(proj, cw, cb, tag + "_conv")
    o_lru, h_lru = _lru_fwd(xc, proj, *_lru_params(lp), name=tag + "_lru")
    s_srcs, s_par = _ssd_inputs(xc, proj, lp)
    o_ssd, s_states = _chunk_scan_fwd(_ssd_chunk, s_srcs, _ssd_read, s_par, SSD_HEADS, False, 70, tag + "_ssd")
    d_srcs, d_par = _dn_inputs(xc, proj, lp)
    o_dn, d_states, d_inv = _chunk_scan_fwd(_dn_chunk, d_srcs, _dn_read, d_par, DN_HEADS, True, 140, tag + "_dn")
    cat = jnp.concatenate([o_lru, o_ssd, o_dn], axis=1)
    return cat, (xc, h_lru, s_states, d_states, d_inv)


def _mixers_bwd(proj, lp, saved, dcat, tag):
    xc, h_lru, s_states, d_states, d_inv = saved
    tp = proj.shape[0]
    cw, cb = _conv_params(lp)
    g = {}
    du, dyraw, dwa, dba, dwx, dbx, dlam = _lru_bwd(xc, proj, h_lru, dcat, *_lru_params(lp), name=tag + "_lru_bwd")
    g["lru_w_a"], g["lru_b_a"], g["lru_w_x"], g["lru_b_x"], g["lru_lambda"] = (
        _block_diag_inv(dwa), dba[0], _block_diag_inv(dwx), dbx[0], dlam[0])

    s_srcs, s_par = _ssd_inputs(xc, proj, lp)
    dxbc, dz, dsm_ssd, dp_dtb, dp_alog, dp_d, dp_nw = _chunk_scan_bwd(
        _ssd_chunk, s_srcs, _ssd_read, s_par, s_states, None, (dcat, 256), _ssd_write, ((768, F32), (SSD_INNER, BF16), (LANE, F32)), SSD_HEADS, 125,
        tag + "_ssd_bwd")
    g["ssd_dt_bias"], g["ssd_a_log"], g["ssd_d"] = (jnp.sum(p, axis=(1, 2)) for p in (dp_dtb, dp_alog, dp_d))
    g["ssd_norm_w"] = dp_nw.reshape(SSD_INNER)

    d_srcs, d_par = _dn_inputs(xc, proj, lp)
    dqkv, dgate, dsm_dn, dq_alog, dq_dtb, dq_nw = _chunk_scan_bwd(
        _dn_chunk, d_srcs, _dn_read, d_par, d_states, d_inv, (dcat, 768), _dn_write, ((768, F32), (256, BF16), (LANE, F32)), DN_HEADS, 300,
        tag + "_dn_bwd")
    g["dn_a_log"], g["dn_dt_bias"] = (jnp.sum(p, axis=(1, 2)) for p in (dq_alog, dq_dtb))
    g["dn_norm_w"] = jnp.sum(dq_nw, axis=(0, 1))

    dconv, dcw, dcb = _conv_bwd(proj, cw, cb, jnp.concatenate([du, dxbc, dqkv], axis=1), tag + "_conv_bwd")
    g["lru_conv_w"], g["ssd_conv_w"], g["dn_conv_w"] = dcw[:, :256], dcw[:, 256:1024], dcw[:, 1024:]
    g["lru_conv_b"], g["ssd_conv_b"] = dcb[0, :256], dcb[0, 256:1024]
    dproj = jnp.concatenate([dconv, dyraw, dz, dgate, (dsm_ssd + dsm_dn).astype(BF16)], axis=1)
    return dproj, g


MIXER_PARAMS = ("lru_conv_w", "lru_conv_b", "lru_w_a", "lru_b_a", "lru_w_x", "lru_b_x", "lru_lambda",
                "ssd_conv_w", "ssd_conv_b", "ssd_dt_bias", "ssd_a_log", "ssd_d", "ssd_norm_w",
                "dn_conv_w", "dn_a_log", "dn_dt_bias", "dn_norm_w")


UNITS = ("gate0", "up0", "down0", "win", "wout", "gate1", "up1", "down1")


def _layer_shards(w, l):
    out = {"win": w["w_in"][l], "wout": w["w_out"][l]}
    for j in range(2):
        out[f"gate{j}"], out[f"up{j}"], out[f"down{j}"] = w["ffn_w_gate"][l, j], w["ffn_w_up"][l, j], w["ffn_w_down"][l, j]
    return {k: a.astype(BF16) for k, a in out.items()}


def _enqueue_matrix(name, arr, slab):
    if name == "wout" or not slab:
        return [_enqueue(arr, slab)], 1
    return _enqueue_halves(arr, slab, 1)


def _enqueue_layer(shards):
    return {k: _enqueue_matrix(k, shards[k], False) for k in UNITS}


def _gathered_cols(a):
    return a.transpose(1, 0, 2).reshape(a.shape[1], -1)


def _col_slabs(a):
    return a.reshape(a.shape[0], N_DEV, -1).transpose(1, 0, 2).astype(BF16)


def _row_slabs(a):
    return a.reshape(N_DEV, -1, a.shape[1]).astype(BF16)


def _local_step(x, tgt, small, first, shards):
    s = x.shape[0]
    t_real = N_META + s
    tp = -(-t_real // CHUNK) * CHUNK
    depth = len(first) if shards is None else len(shards)
    h = jnp.concatenate([small["meta"], x, jnp.zeros((tp - t_real, D_MODEL), F32)], axis=0)
    hb = h.astype(BF16)
    ln_g, ln_b = small["ln_g"], small["ln_b"]
    saved, weights = [], []
    queued = first
    for l in range(depth):
        lp = {k: small[k][l] for k in MIXER_PARAMS}
        t = f"l{l}"
        if shards is None:
            get = lambda k, g=first[l]: g[k]
        else:
            get = lambda k, g=queued: _collect_halves(g[k])
            if l + 1 < depth:
                queued = _enqueue_layer(shards[l + 1])

        def ffn_weights(j):
            up = _gathered_cols(get(f"up{j}"))
            return _gathered_cols(get(f"gate{j}")), up

        wgu0 = ffn_weights(0)
        g0, u0, a0 = _ffn_up(hb, *wgu0, t + "_ffn0_up")
        wd0 = get("down0").reshape(-1, D_MODEL)
        y1, h1, h1b = _mm_resid_ln(a0, wd0, h, ln_g[l, 0][None], ln_b[l, 0][None], FFN_RES, t + "_ffn0_down")
        win = _proj_cols(_gathered_cols(get("win")))
        proj = _mm_nn(h1b, win, t + "_in_proj")
        cat, mix_saved = _mixers_fwd(proj, lp, t)
        catb = cat.astype(BF16)
        wout = get("wout").reshape(D_MODEL, D_MODEL)
        y2, h2, h2b = _mm_resid_ln(catb, wout, h1, ln_g[l, 1][None], ln_b[l, 1][None], 1.0, t + "_out_proj")
        wgu1 = ffn_weights(1)
        g1, u1, a1 = _ffn_up(h2b, *wgu1, t + "_ffn1_up")
        wd1 = get("down1").reshape(-1, D_MODEL)
        y3, h3, h3b = _mm_resid_ln(a1, wd1, h2, ln_g[l, 2][None], ln_b[l, 2][None], FFN_RES, t + "_ffn1_down")
        weights.append(([wgu0, wgu1], [wd0, wd1], win, wout))
        saved.append((hb, g0, u0, a0, y1, h1b, proj, mix_saved, catb, y2, h2b, g1, u1, a1, y3))
        h, hb = h3, h3b

    tgt_p = jnp.pad(tgt, ((N_META, tp - t_real), (0, 0)))
    lossv, dh = _loss_grad(h, tgt_p, t_real, "loss")
    loss = jnp.sum(lossv)

    gs = {k: [None] * depth for k in MIXER_PARAMS}
    d_ln_g = [[None] * 3 for _ in range(depth)]
    d_ln_b = [[None] * 3 for _ in range(depth)]
    slabs = [{} for _ in range(depth)]
    send = (lambda k, a: a) if shards is None else (lambda k, a: _enqueue_matrix(k, a, True))

    def ffn_bwd(l, j, xb_in, g, u, a, y, dout):
        t = f"l{l}_ffn{j}"
        wgu, wd = weights[l][0][j], weights[l][1][j]
        dres, dyb, dgam, dbet = _ln_bwd(y, dout, ln_g[l, 2 * j][None], FFN_RES, t + "_ln_bwd")
        d_ln_g[l][2 * j], d_ln_b[l][2 * j] = dgam[0], dbet[0]
        slabs[l][f"down{j}"] = send("down", _row_slabs(_mm_tn(a, dyb, t + "_dwd")))
        dg, du = _ffn_dact(dyb, wd, g, u, t + "_dact")
        slabs[l][f"gate{j}"] = send("gate", _col_slabs(_mm_tn(xb_in, dg, t + "_dwg")))
        slabs[l][f"up{j}"] = send("up", _col_slabs(_mm_tn(xb_in, du, t + "_dwu")))
        return _ffn_dx(dg, du, *wgu, dres, t + "_dx")

    for l in reversed(range(depth)):
        hb_in, g0, u0, a0, y1, h1b, proj, mix_saved, catb, y2, h2b, g1, u1, a1, y3 = saved[l]
        lp = {k: small[k][l] for k in MIXER_PARAMS}
        t = f"l{l}"
        _, _, win, wout = weights[l]
        dh2 = ffn_bwd(l, 1, h2b, g1, u1, a1, y3, dh)
        dres, dyb, dgam, dbet = _ln_bwd(y2, dh2, ln_g[l, 1][None], 1.0, t + "_mix_ln_bwd")
        d_ln_g[l][1], d_ln_b[l][1] = dgam[0], dbet[0]
        slabs[l]["wout"] = send("wout", _row_slabs(_mm_tn(catb, dyb, t + "_dwout")))
        dcat = _mm_nt_add(dyb, wout, None, t + "_dcat")
        dproj, mg = _mixers_bwd(proj, lp, mix_saved, dcat, t)
        for k in MIXER_PARAMS:
            gs[k][l] = mg[k]
        dprojb = dproj.astype(BF16)
        slabs[l]["win"] = send("win", _col_slabs(_proj_cols_inv(_mm_tn(h1b, dprojb, t + "_dwin"))))
        dh1 = _mm_nt_add(dprojb, win, dres, t + "_dh1")
        dh = ffn_bwd(l, 0, hb_in, g0, u0, a0, y1, dh1)

    small_grads = {k: jnp.stack(v) for k, v in gs.items()}
    small_grads["ln_g"] = jnp.stack([jnp.stack(r) for r in d_ln_g])
    small_grads["ln_b"] = jnp.stack([jnp.stack(r) for r in d_ln_b])
    small_grads["meta"] = dh[:N_META]
    return loss, dh[N_META:t_real], small_grads, slabs


def _mesh_pos():
    return lax.axis_index("x"), lax.axis_index("y"), lax.axis_index("c")


def _flip(pos, k):
    x, y, c = pos
    return (1 - x if k & 4 else x, 1 - y if k & 2 else y, 1 - c if k & 1 else c)


def _flat(pos):
    return 4 * pos[0] + 2 * pos[1] + pos[2]


def _exchange_copies(ins, outs, slabs, send, recv, loc):
    pos = _mesh_pos()
    me = _flat(pos)
    local, sends, receives = [], [], []
    for a in range(len(ins)):
        local.append(pltpu.make_async_copy(ins[a].at[me] if slabs[a] else ins[a], outs[a].at[me], loc.at[a]))
    for k in range(1, N_DEV):
        peer = _flip(pos, k)
        for a in range(len(ins)):
            if not slabs[a] and k not in GATHER_FIRST:
                continue
            sem = dict(send_sem=send.at[a, k - 1], recv_sem=recv.at[a, k - 1], device_id=peer, device_id_type=MESH_IDS)
            sends.append(pltpu.make_async_remote_copy(
                src_ref=ins[a].at[_flat(peer)] if slabs[a] else ins[a], dst_ref=outs[a].at[me], **sem))
            receives.append(pltpu.make_async_remote_copy(
                src_ref=ins[a].at[me] if slabs[a] else ins[a], dst_ref=outs[a].at[_flat(peer)], **sem))
    return local, sends, receives


GATHER_FIRST = (1, 2, 4, 6)


def _gather_forwards(ins, outs, slabs, send, recv):
    pos = _mesh_pos()
    sibling = _flip(pos, 1)
    forwards, arrivals = [], []
    for a in range(len(ins)):
        if slabs[a]:
            continue
        for k in GATHER_FIRST[1:]:
            sem = dict(send_sem=send.at[a, k], recv_sem=recv.at[a, k], device_id=sibling, device_id_type=MESH_IDS)
            mine, theirs = _flat(_flip(pos, k)), _flat(_flip(pos, k + 1))
            forwards.append(pltpu.make_async_remote_copy(src_ref=outs[a].at[mine], dst_ref=outs[a].at[mine], **sem))
            arrivals.append(pltpu.make_async_remote_copy(src_ref=outs[a].at[theirs], dst_ref=outs[a].at[theirs], **sem))
    return forwards, arrivals


def _exchange_start(ins, outs, slabs, send, recv, loc):
    local, sends, _ = _exchange_copies(ins, outs, slabs, send, recv, loc)
    for cp in local + sends:
        cp.start()


def _exchange_wait(ins, outs, slabs, send, recv, loc):
    local, sends, receives = _exchange_copies(ins, outs, slabs, send, recv, loc)
    forwards, arrivals = _gather_forwards(ins, outs, slabs, send, recv)
    for cp in receives:
        cp.wait_recv()
    for cp in forwards:
        cp.start()
    for cp in arrivals:
        cp.wait_recv()
    for cp in sends + forwards:
        cp.wait_send()
    for cp in local:
        cp.wait()


def _exchange_shapes(arrs, slabs):
    return tuple(_sds(a.shape if s else (N_DEV,) + a.shape, a.dtype) for a, s in zip(arrs, slabs))


def _exchange_sems(n):
    return [pltpu.SemaphoreType.DMA((n, N_DEV - 1)), pltpu.SemaphoreType.DMA((n, N_DEV - 1)),
            pltpu.SemaphoreType.DMA((n,))]


def _exchange(arrs, slabs, name):
    n = len(arrs)

    def body(*refs):
        ins, outs, sems = refs[:n], refs[n:2 * n], refs[2 * n:]
        _exchange_start(ins, outs, slabs, *sems)
        _exchange_wait(ins, outs, slabs, *sems)

    hbm = pl.BlockSpec(memory_space=pl.ANY)
    return pl.pallas_call(
        body, name=name, out_shape=_exchange_shapes(arrs, slabs), in_specs=[hbm] * n, out_specs=[hbm] * n,
        scratch_shapes=_exchange_sems(n),
    )(*arrs)


def _adam_math(w, g, m, v):
    m = ADAM_B1 * m + (1.0 - ADAM_B1) * g
    v = ADAM_B2 * v + (1.0 - ADAM_B2) * (g * g)
    m_hat = m / (1.0 - ADAM_B1 ** ADAM_STEP)
    v_hat = v / (1.0 - ADAM_B2 ** ADAM_STEP)
    delta = -ADAM_LR * (m_hat / (jnp.sqrt(v_hat) + ADAM_EPS) + ADAM_WD * w)
    return delta, m, v


def _adam(w, g, m, v, name):
    r, c = w.shape
    parts = g.ndim == 3
    tr = _tile(r, 512, 8)

    def body(w_ref, g_ref, m_ref, v_ref, go_ref, d_ref, mo_ref, vo_ref):
        if parts:
            gv = g_ref[0].astype(F32)
            for d in range(1, N_DEV):
                gv = gv + g_ref[d].astype(F32)
        else:
            gv = g_ref[...]
        delta, mn, vn = _adam_math(w_ref[...], gv, m_ref[...], v_ref[...])
        go_ref[...] = gv
        d_ref[...] = delta
        mo_ref[...] = mn
        vo_ref[...] = vn

    blk = pl.BlockSpec((tr, c), lambda i: (i, 0))
    gblk = pl.BlockSpec((N_DEV, tr, c), lambda i: (0, i, 0)) if parts else blk
    out = _sds((r, c), F32)
    return pl.pallas_call(
        body, name=name, out_shape=(out,) * 4, grid=(r // tr,),
        in_specs=[blk, gblk, blk, blk], out_specs=[blk] * 4,
        compiler_params=_params(("arbitrary",)),
    )(w, g, m, v)


def _sum_parts(parts, name):
    _, r, c = parts.shape

    def body(p_ref, o_ref):
        acc = p_ref[0]
        for d in range(1, N_DEV):
            acc = acc + p_ref[d]
        o_ref[...] = acc

    return pl.pallas_call(body, name=name, out_shape=_sds((r, c), F32), compiler_params=_params())(parts)


SMALL_SHARD_AXIS = {
    "meta": 1, "ln_g": 2, "ln_b": 2, "lru_conv_w": 2, "lru_conv_b": None, "lru_w_a": None, "lru_b_a": None,
    "lru_w_x": None, "lru_b_x": None, "lru_lambda": None, "ssd_conv_w": 2, "ssd_conv_b": None, "ssd_dt_bias": None,
    "ssd_a_log": None, "ssd_d": None, "ssd_norm_w": None, "dn_conv_w": 2, "dn_a_log": None, "dn_dt_bias": None,
    "dn_norm_w": None,
}
BIG = ("ffn_w_gate", "ffn_w_up", "ffn_w_down", "w_in", "w_out")
WEIGHT_ORDER = ("meta", "ln_g", "ln_b", "ffn_w_gate", "ffn_w_up", "ffn_w_down", "w_in", "lru_conv_w", "lru_conv_b",
                "lru_w_a", "lru_b_a", "lru_w_x", "lru_b_x", "lru_lambda", "ssd_conv_w", "ssd_conv_b", "ssd_dt_bias",
                "ssd_a_log", "ssd_d", "ssd_norm_w", "dn_conv_w", "dn_a_log", "dn_dt_bias", "dn_norm_w", "w_out")


def _pack(arrs):
    flat = jnp.concatenate([a.reshape(-1) for a in arrs])
    rows = -(-flat.shape[0] // (8 * LANE)) * 8
    return jnp.pad(flat, (0, rows * LANE - flat.shape[0])).reshape(rows, LANE)


def _unpack(buf, shapes, lead=()):
    flat = buf.reshape(lead + (-1,))
    out, off = [], 0
    for s in shapes:
        n = math.prod(s)
        out.append(flat[..., off:off + n].reshape(lead + tuple(s)))
        off += n
    return out


def _proj_cols(w):
    pad = jnp.zeros(w.shape[:-1] + (PROJ_W - D_IN,), w.dtype)
    return jnp.concatenate([w[..., 0:256], w[..., 1024:1792], w[..., 1800:2568], w[..., 256:512], w[..., 512:1024],
                            w[..., 2568:2824], w[..., 1792:1800], w[..., 2824:2832], pad], axis=-1)


def _proj_cols_inv(w):
    return jnp.concatenate([w[..., 0:256], w[..., 1792:2048], w[..., 2048:2560], w[..., 256:1024], w[..., 2816:2824],
                            w[..., 1024:1792], w[..., 2560:2816], w[..., 2824:2832]], axis=-1)


def kernel(x, meta, ln_g, ln_b, ffn_w_gate, ffn_w_up, ffn_w_down, w_in, lru_conv_w, lru_conv_b, lru_w_a, lru_b_a, lru_w_x, lru_b_x, lru_lambda, ssd_conv_w, ssd_conv_b, ssd_dt_bias, ssd_a_log, ssd_d, ssd_norm_w, dn_conv_w, dn_a_log, dn_dt_bias, dn_norm_w, w_out, loss_target, m_meta, m_ln_g, m_ln_b, m_ffn_w_gate, m_ffn_w_up, m_ffn_w_down, m_w_in, m_lru_conv_w, m_lru_conv_b, m_lru_w_a, m_lru_b_a, m_lru_w_x, m_lru_b_x, m_lru_lambda, m_ssd_conv_w, m_ssd_conv_b, m_ssd_dt_bias, m_ssd_a_log, m_ssd_d, m_ssd_norm_w, m_dn_conv_w, m_dn_a_log, m_dn_dt_bias, m_dn_norm_w, m_w_out, v_meta, v_ln_g, v_ln_b, v_ffn_w_gate, v_ffn_w_up, v_ffn_w_down, v_w_in, v_lru_conv_w, v_lru_conv_b, v_lru_w_a, v_lru_b_a, v_lru_w_x, v_lru_b_x, v_lru_lambda, v_ssd_conv_w, v_ssd_conv_b, v_ssd_dt_bias, v_ssd_a_log, v_ssd_d, v_ssd_norm_w, v_dn_conv_w, v_dn_a_log, v_dn_dt_bias, v_dn_norm_w, v_w_out):
    w = dict(meta=meta, ln_g=ln_g, ln_b=ln_b, ffn_w_gate=ffn_w_gate, ffn_w_up=ffn_w_up, ffn_w_down=ffn_w_down, w_in=w_in,
             lru_conv_w=lru_conv_w, lru_conv_b=lru_conv_b, lru_w_a=lru_w_a, lru_b_a=lru_b_a, lru_w_x=lru_w_x,
             lru_b_x=lru_b_x, lru_lambda=lru_lambda, ssd_conv_w=ssd_conv_w, ssd_conv_b=ssd_conv_b, ssd_dt_bias=ssd_dt_bias,
             ssd_a_log=ssd_a_log, ssd_d=ssd_d, ssd_norm_w=ssd_norm_w, dn_conv_w=dn_conv_w, dn_a_log=dn_a_log,
             dn_dt_bias=dn_dt_bias, dn_norm_w=dn_norm_w, w_out=w_out)
    m = dict(meta=m_meta, ln_g=m_ln_g, ln_b=m_ln_b, ffn_w_gate=m_ffn_w_gate, ffn_w_up=m_ffn_w_up, ffn_w_down=m_ffn_w_down,
             w_in=m_w_in, lru_conv_w=m_lru_conv_w, lru_conv_b=m_lru_conv_b, lru_w_a=m_lru_w_a, lru_b_a=m_lru_b_a,
             lru_w_x=m_lru_w_x, lru_b_x=m_lru_b_x, lru_lambda=m_lru_lambda, ssd_conv_w=m_ssd_conv_w, ssd_conv_b=m_ssd_conv_b,
             ssd_dt_bias=m_ssd_dt_bias, ssd_a_log=m_ssd_a_log, ssd_d=m_ssd_d, ssd_norm_w=m_ssd_norm_w, dn_conv_w=m_dn_conv_w,
             dn_a_log=m_dn_a_log, dn_dt_bias=m_dn_dt_bias, dn_norm_w=m_dn_norm_w, w_out=m_w_out)
    v = dict(meta=v_meta, ln_g=v_ln_g, ln_b=v_ln_b, ffn_w_gate=v_ffn_w_gate, ffn_w_up=v_ffn_w_up, ffn_w_down=v_ffn_w_down,
             w_in=v_w_in, lru_conv_w=v_lru_conv_w, lru_conv_b=v_lru_conv_b, lru_w_a=v_lru_w_a, lru_b_a=v_lru_b_a,
             lru_w_x=v_lru_w_x, lru_b_x=v_lru_b_x, lru_lambda=v_lru_lambda, ssd_conv_w=v_ssd_conv_w, ssd_conv_b=v_ssd_conv_b,
             ssd_dt_bias=v_ssd_dt_bias, ssd_a_log=v_ssd_a_log, ssd_d=v_ssd_d, ssd_norm_w=v_ssd_norm_w, dn_conv_w=v_dn_conv_w,
             dn_a_log=v_dn_a_log, dn_dt_bias=v_dn_dt_bias, dn_norm_w=v_dn_norm_w, w_out=v_w_out)
    depth = ln_g.shape[0]
    me = _flat(_mesh_pos())
    small_names = tuple(SMALL_SHARD_AXIS)
    sharded = tuple(k for k in small_names if SMALL_SHARD_AXIS[k] is not None)

    del _QUEUE[:]
    _STANDALONE[0] = 0
    shards = [_layer_shards(w, l) for l in range(depth)]
    small_unit = _enqueue(_pack([w[k] for k in sharded]), False)
    first = _enqueue_layer(shards[0])
    g_small = _collect(small_unit)
    small = {k: w[k] for k in small_names if SMALL_SHARD_AXIS[k] is None}
    for k, piece in zip(sharded, _unpack(g_small, [w[k].shape for k in sharded], lead=(N_DEV,))):
        ax = SMALL_SHARD_AXIS[k]
        full = jnp.moveaxis(piece, 0, ax)
        small[k] = full.reshape(full.shape[:ax] + (N_DEV * w[k].shape[ax],) + full.shape[ax + 2:])

    loss, dx, small_grads, slabs = _local_step(x[0], loss_target[0], small, first, shards)

    r_small = _collect(_enqueue(_pack([small_grads[k] for k in small_names]), False))
    got = [{k: _collect_halves(g) for k, g in layer.items()} for layer in slabs]

    outs = {}
    ffn = lambda name: jnp.stack([jnp.stack([got[l][f"{name}{j}"] for j in range(2)], axis=1) for l in range(depth)], axis=1)
    one = lambda name: jnp.stack([got[l][name] for l in range(depth)], axis=1)
    parts_of = {"ffn_w_gate": ffn("gate"), "ffn_w_up": ffn("up"), "ffn_w_down": ffn("down"), "w_in": one("win"),
                "w_out": one("wout")}
    for k in BIG:
        shp = w[k].shape
        two = lambda a: a.reshape(-1, shp[-1])
        res = _adam(two(w[k]), parts_of[k].reshape(N_DEV, -1, shp[-1]), two(m[k]), two(v[k]), "adam_" + k)
        outs[k] = [a.reshape(shp) for a in res]
    g_full = _unpack(_sum_parts(r_small, "sum_small_grads"), [small[k].shape for k in small_names])
    g_loc = {}
    for k, gf in zip(small_names, g_full):
        ax = SMALL_SHARD_AXIS[k]
        g_loc[k] = gf if ax is None else lax.dynamic_slice_in_dim(gf, me * w[k].shape[ax], w[k].shape[ax], axis=ax)
    res = _adam(_pack([w[k] for k in small_names]), _pack([g_loc[k] for k in small_names]),
                _pack([m[k] for k in small_names]), _pack([v[k] for k in small_names]), "adam_small")
    shapes = [w[k].shape for k in small_names]
    for i, k in enumerate(small_names):
        outs[k] = [_unpack(r, shapes)[i] for r in res]

    loss = lax.psum(loss, ("x", "y", "c"))
    return (loss, dx[None], *[outs[k][0] for k in WEIGHT_ORDER], *[outs[k][1] for k in WEIGHT_ORDER],
            *[outs[k][2] for k in WEIGHT_ORDER], *[outs[k][3] for k in WEIGHT_ORDER])
```

```python
import functools
import math

import jax
import jax.numpy as jnp
from jax import lax
from jax.experimental import pallas as pl
from jax.experimental.pallas import tpu as pltpu

F32 = jnp.float32
BF16 = jnp.bfloat16
HI = lax.Precision.HIGHEST
MESH_IDS = pl.DeviceIdType.MESH

N_DEV = 8
D_MODEL = 1024
DEPTH = 4
N_META = 16
CHUNK = 64
CONV_K = 4
D_FF = 2816
LRU_WIDTH = 256
LRU_HEADS = 4
LRU_C = 8.0
SSD_HEADS = 8
SSD_GROUPS = 2
SSD_HPG = 4
SSD_INNER = 512
DN_HEADS = 4
HEAD = 64
CONV_W = 1792
PROJ_W = 2944
OFF_LRU_Y, OFF_SSD_Z, OFF_DN_GATE, OFF_SMALL = 1792, 2048, 2560, 2816
D_IN = 2832
ALPHA = (2 * DEPTH) ** 0.25
FFN_RES = 0.5
LN_EPS = 1e-5
RMS_EPS = 1e-6
ADAM_LR, ADAM_B1, ADAM_B2, ADAM_EPS, ADAM_WD, ADAM_STEP = 0.001, 0.9, 0.999, 1e-08, 0.01, 10

VMEM_LIMIT = 56 * 1024 * 1024
ROW_TILE_CAP = 832
LANE = 128


def _tile(n, cap, mult=16):
    best = None
    for t in range(mult, min(n, cap) + 1, mult):
        if n % t == 0:
            best = t
    assert best is not None, (n, cap, mult)
    return best


def _params(sem=None):
    return pltpu.CompilerParams(dimension_semantics=sem, vmem_limit_bytes=VMEM_LIMIT)


def _resident(shape, index_map):
    return pl.BlockSpec(shape, index_map, pipeline_mode=pl.Buffered(1))


def _dot(a, b):
    return jnp.dot(a, b, preferred_element_type=F32)


def _dot_nt(a, b):
    return lax.dot_general(a, b, (((1,), (1,)), ((), ())), preferred_element_type=F32)


def _dot_tn(a, b):
    return lax.dot_general(a, b, (((0,), (0,)), ((), ())), preferred_element_type=F32)


def _sds(shape, dtype):
    return jax.ShapeDtypeStruct(shape, dtype)


US_PER_MB = 92.5


class _Unit:
    def __init__(self, arr, slab):
        self.arr, self.slab, self.out = arr, slab, None
        per_peer = arr.size * arr.dtype.itemsize / (N_DEV if slab else 1)
        self.us = per_peer / 1e6 * US_PER_MB * (1.0 if slab else 0.5)


_QUEUE = []
_STANDALONE = [0]


def _enqueue(arr, slab):
    unit = _Unit(arr, slab)
    _QUEUE.append(unit)
    return unit


def _enqueue_halves(arr, slab, axis):
    half = arr.shape[axis] // 2
    parts = (lax.slice_in_dim(arr, 0, half, axis=axis), lax.slice_in_dim(arr, half, arr.shape[axis], axis=axis))
    return [_enqueue(p, slab) for p in parts], axis + (0 if slab else 1)


def _collect_halves(group):
    units, axis = group
    return jnp.concatenate([_collect(u) for u in units], axis=axis)


def _take_units(host_us):
    units = []
    while _QUEUE and host_us >= 0.5 * _QUEUE[0].us:
        host_us -= _QUEUE[0].us
        units.append(_QUEUE.pop(0))
    return units


def _collect(unit):
    if unit.out is None:
        n = _QUEUE.index(unit) + 1
        units = [_QUEUE.pop(0) for _ in range(n)]
        _STANDALONE[0] += 1
        res = _exchange([u.arr for u in units], [u.slab for u in units], f"exchange_{_STANDALONE[0]}")
        for u, r in zip(units, res):
            u.out = r
    return unit.out


def _call(body, host_us, *, name, out_shape, in_specs, out_specs, grid=(), scratch_shapes=(), compiler_params=None):
    units = _take_units(0.7 * host_us)
    kw = dict(name=name, grid=grid, compiler_params=compiler_params)
    if not units:
        return pl.pallas_call(body, out_shape=out_shape, in_specs=in_specs, out_specs=out_specs,
                              scratch_shapes=list(scratch_shapes), **kw)
    single = not isinstance(out_shape, (tuple, list))
    outs = (out_shape,) if single else tuple(out_shape)
    ospecs = [out_specs] if single else list(out_specs)
    nin, nout, nscr, ncm = len(in_specs), len(outs), len(scratch_shapes), len(units)
    slabs = [u.slab for u in units]

    def hosted(*refs):
        ins, c_in = refs[:nin], refs[nin:nin + ncm]
        o, c_out = refs[nin + ncm:nin + ncm + nout], refs[nin + ncm + nout:nin + 2 * ncm + nout]
        scr, sems = refs[nin + 2 * ncm + nout:nin + 2 * ncm + nout + nscr], refs[nin + 2 * ncm + nout + nscr:]
        ids = [pl.program_id(d) for d in range(len(grid))]
        first = functools.reduce(jnp.logical_and, [i == 0 for i in ids])
        last = functools.reduce(jnp.logical_and, [i == g - 1 for i, g in zip(ids, grid)])

        @pl.when(first)
        def _():
            _exchange_start(c_in, c_out, slabs, *sems)
        body(*ins, *o, *scr)

        @pl.when(last)
        def _():
            _exchange_wait(c_in, c_out, slabs, *sems)

    hbm = pl.BlockSpec(memory_space=pl.ANY)
    fn = pl.pallas_call(
        hosted, out_shape=outs + _exchange_shapes([u.arr for u in units], slabs), in_specs=list(in_specs) + [hbm] * ncm,
        out_specs=ospecs + [hbm] * ncm, scratch_shapes=list(scratch_shapes) + _exchange_sems(ncm), **kw)

    def run(*args):
        res = fn(*args, *[u.arr for u in units])
        for u, r in zip(units, res[nout:]):
            u.out = r
        return res[0] if single else tuple(res[:nout])

    return run


def _ffn_up(xb, wg, wu, name):
    tp, d = xb.shape
    f = wg.shape[1]
    tm = _tile(tp, ROW_TILE_CAP)
    tn = _tile(f, 1408, LANE)
    nj = f // tn

    def body(x_ref, wg_ref, wu_ref, g_ref, u_ref, a_ref):
        x = x_ref[...]
        g = _dot(x, wg_ref[...])
        u = _dot(x, wu_ref[...])
        g_ref[...] = g.astype(BF16)
        u_ref[...] = u.astype(BF16)
        a_ref[...] = (g * jax.nn.sigmoid(g) * u).astype(BF16)

    out = _sds((tp, f), BF16)
    return _call(
        body, 60, name=name, out_shape=(out, out, out), grid=(nj, tp // tm),
        in_specs=[pl.BlockSpec((tm, d), lambda j, i: (i, 0)),
                  pl.BlockSpec((d, tn), lambda j, i: (0, j)),
                  pl.BlockSpec((d, tn), lambda j, i: (0, j))],
        out_specs=[pl.BlockSpec((tm, tn), lambda j, i: (i, j))] * 3,
        compiler_params=_params(("arbitrary", "arbitrary")),
    )(xb, wg, wu)


def _mm_resid_ln(a, w, h, gamma, beta, scale, name):
    tp, k = a.shape
    d = w.shape[1]
    tm = _tile(tp, ROW_TILE_CAP)

    def body(a_ref, w_ref, h_ref, g_ref, b_ref, y_ref, o_ref, ob_ref):
        y = ALPHA * h_ref[...] + scale * _dot(a_ref[...], w_ref[...])
        mu = jnp.mean(y, axis=-1, keepdims=True)
        yc = y - mu
        var = jnp.mean(yc * yc, axis=-1, keepdims=True)
        o = yc * lax.rsqrt(var + LN_EPS) * g_ref[...] + b_ref[...]
        y_ref[...] = y
        o_ref[...] = o
        ob_ref[...] = o.astype(BF16)

    row = lambda i: (i, 0)
    fix = lambda i: (0, 0)
    return _call(
        body, 30, name=name, out_shape=(_sds((tp, d), F32), _sds((tp, d), F32), _sds((tp, d), BF16)),
        grid=(tp // tm,),
        in_specs=[pl.BlockSpec((tm, k), row), _resident((k, d), fix), pl.BlockSpec((tm, d), row),
                  pl.BlockSpec((1, d), fix), pl.BlockSpec((1, d), fix)],
        out_specs=[pl.BlockSpec((tm, d), row)] * 3,
        compiler_params=_params(("arbitrary",)),
    )(a, w, h, gamma, beta)


def _mm_nn(xb, w, name):
    tp, k = xb.shape
    n = w.shape[1]
    tm = _tile(tp, ROW_TILE_CAP)

    def body(x_ref, w_ref, o_ref):
        o_ref[...] = _dot(x_ref[...], w_ref[...])

    return _call(
        body, 30, name=name, out_shape=_sds((tp, n), F32), grid=(tp // tm,),
        in_specs=[pl.BlockSpec((tm, k), lambda i: (i, 0)), _resident((k, n), lambda i: (0, 0))],
        out_specs=pl.BlockSpec((tm, n), lambda i: (i, 0)),
        compiler_params=_params(("arbitrary",)),
    )(xb, w)


def _mm_nt_add(a, w, resid, name):
    tp, k = a.shape
    n = w.shape[0]
    tm = _tile(tp, ROW_TILE_CAP)
    has_resid = resid is not None

    def body(*refs):
        if has_resid:
            a_ref, w_ref, r_ref, o_ref = refs
            o_ref[...] = r_ref[...] + _dot_nt(a_ref[...], w_ref[...])
        else:
            a_ref, w_ref, o_ref = refs
            o_ref[...] = _dot_nt(a_ref[...], w_ref[...])

    in_specs = [pl.BlockSpec((tm, k), lambda i: (i, 0)), _resident((n, k), lambda i: (0, 0))]
    args = [a, w]
    if has_resid:
        in_specs.append(pl.BlockSpec((tm, n), lambda i: (i, 0)))
        args.append(resid)
    return _call(
        body, 25, name=name, out_shape=_sds((tp, n), F32), grid=(tp // tm,),
        in_specs=in_specs, out_specs=pl.BlockSpec((tm, n), lambda i: (i, 0)),
        compiler_params=_params(("arbitrary",)),
    )(*args)


def _ffn_dx(dg, du, wg, wu, resid, name):
    tp, f = dg.shape
    d = wg.shape[0]
    tm = _tile(tp, ROW_TILE_CAP)

    def body(dg_ref, du_ref, wg_ref, wu_ref, r_ref, o_ref):
        acc = r_ref[...] + _dot_nt(dg_ref[...], wg_ref[...])
        o_ref[...] = acc + _dot_nt(du_ref[...], wu_ref[...])

    row = lambda i: (i, 0)
    fix = lambda i: (0, 0)
    return _call(
        body, 55, name=name, out_shape=_sds((tp, d), F32), grid=(tp // tm,),
        in_specs=[pl.BlockSpec((tm, f), row), pl.BlockSpec((tm, f), row), _resident((d, f), fix), _resident((d, f), fix),
                  pl.BlockSpec((tm, d), row)],
        out_specs=pl.BlockSpec((tm, d), row),
        compiler_params=_params(("arbitrary",)),
    )(dg, du, wg, wu, resid)


def _mm_tn(a, b, name):
    tp, ka = a.shape
    nb = b.shape[1]
    tt = _tile(tp, ROW_TILE_CAP)
    tk = _tile(ka, 1408, LANE)
    tn = _tile(nb, 1536, LANE)
    if tn < 512:
        tn = nb
    nt = tp // tt

    def body(a_ref, b_ref, o_ref):
        @pl.when(pl.program_id(2) == 0)
        def _():
            o_ref[...] = jnp.zeros_like(o_ref)
        o_ref[...] += _dot_tn(a_ref[...], b_ref[...])

    return _call(
        body, 40, name=name, out_shape=_sds((ka, nb), F32), grid=(ka // tk, nb // tn, nt),
        in_specs=[pl.BlockSpec((tt, tk), lambda i, j, t: (t, i)), pl.BlockSpec((tt, tn), lambda i, j, t: (t, j))],
        out_specs=pl.BlockSpec((tk, tn), lambda i, j, t: (i, j)),
        compiler_params=_params(("arbitrary", "arbitrary", "arbitrary")),
    )(a, b)


def _ln_bwd(y, dout, gamma, scale, name):
    tp, d = y.shape
    tm = _tile(tp, ROW_TILE_CAP)

    def body(y_ref, do_ref, g_ref, dres_ref, dyb_ref, dg_ref, db_ref):
        @pl.when(pl.program_id(0) == 0)
        def _():
            dg_ref[...] = jnp.zeros_like(dg_ref)
            db_ref[...] = jnp.zeros_like(db_ref)
        yv = y_ref[...]
        do = do_ref[...]
        mu = jnp.mean(yv, axis=-1, keepdims=True)
        yc = yv - mu
        var = jnp.mean(yc * yc, axis=-1, keepdims=True)
        rstd = lax.rsqrt(var + LN_EPS)
        xhat = yc * rstd
        dxh = do * g_ref[...]
        m1 = jnp.mean(dxh, axis=-1, keepdims=True)
        m2 = jnp.mean(dxh * xhat, axis=-1, keepdims=True)
        dy = rstd * (dxh - m1 - xhat * m2)
        dres_ref[...] = ALPHA * dy
        dyb_ref[...] = (scale * dy).astype(BF16)
        dg_ref[...] += jnp.sum(do * xhat, axis=0, keepdims=True)
        db_ref[...] += jnp.sum(do, axis=0, keepdims=True)

    row = lambda i: (i, 0)
    fix = lambda i: (0, 0)
    return _call(
        body, 22, name=name,
        out_shape=(_sds((tp, d), F32), _sds((tp, d), BF16), _sds((1, d), F32), _sds((1, d), F32)),
        grid=(tp // tm,),
        in_specs=[pl.BlockSpec((tm, d), row), pl.BlockSpec((tm, d), row), pl.BlockSpec((1, d), fix)],
        out_specs=[pl.BlockSpec((tm, d), row), pl.BlockSpec((tm, d), row), pl.BlockSpec((1, d), fix),
                   pl.BlockSpec((1, d), fix)],
        compiler_params=_params(("arbitrary",)),
    )(y, dout, gamma)


def _ffn_dact(dyb, wd, g, u, name):
    tp, d = dyb.shape
    f = wd.shape[0]
    tm = _tile(tp, ROW_TILE_CAP)
    tn = _tile(f, 1408, LANE)

    def body(dy_ref, w_ref, g_ref, u_ref, dg_ref, du_ref):
        dact = _dot_nt(dy_ref[...], w_ref[...])
        gv = g_ref[...].astype(F32)
        uv = u_ref[...].astype(F32)
        sg = jax.nn.sigmoid(gv)
        dg_ref[...] = (dact * uv * (sg * (1.0 + gv * (1.0 - sg)))).astype(BF16)
        du_ref[...] = (dact * (gv * sg)).astype(BF16)

    out = _sds((tp, f), BF16)
    blk = pl.BlockSpec((tm, tn), lambda j, i: (i, j))
    return _call(
        body, 55, name=name, out_shape=(out, out), grid=(f // tn, tp // tm),
        in_specs=[pl.BlockSpec((tm, d), lambda j, i: (i, 0)), pl.BlockSpec((tn, d), lambda j, i: (j, 0)), blk, blk],
        out_specs=[blk, blk],
        compiler_params=_params(("arbitrary", "arbitrary")),
    )(dyb, wd, g, u)


def _loss_grad(o, tgt, t_real, name):
    tp, d = o.shape
    tm = _tile(tp, ROW_TILE_CAP)

    def body(o_ref, t_ref, l_ref, d_ref):
        i = pl.program_id(0)

        @pl.when(i == 0)
        def _():
            l_ref[...] = jnp.zeros_like(l_ref)
        rows = i * tm + lax.broadcasted_iota(jnp.int32, (tm, 1), 0)
        real = jnp.logical_and(rows >= N_META, rows < t_real)
        err = jnp.where(real, o_ref[...] - t_ref[...], 0.0)
        d_ref[...] = err * (1.0 / d)
        l_ref[...] += jnp.sum(err * err, axis=0, keepdims=True) * (0.5 / d)

    row = lambda i: (i, 0)
    return pl.pallas_call(
        body, name=name, out_shape=(_sds((1, d), F32), _sds((tp, d), F32)), grid=(tp // tm,),
        in_specs=[pl.BlockSpec((tm, d), row), pl.BlockSpec((tm, d), row)],
        out_specs=[pl.BlockSpec((1, d), lambda i: (0, 0)), pl.BlockSpec((tm, d), row)],
        compiler_params=_params(("arbitrary",)),
    )(o, tgt)


CONV_TC = 256


def _silu_grad(y):
    s = jax.nn.sigmoid(y)
    return s * (1.0 + y * (1.0 - s))


def _conv_taps(x_ref, w, r0, rb):
    cur = x_ref[r0:r0 + rb, :]
    prev = x_ref[r0 - 8:r0, :] if r0 > 0 else jnp.zeros((8, cur.shape[1]), F32)
    xcat = jnp.concatenate([prev, cur], axis=0)
    taps = [xcat[5 + j:5 + j + rb] for j in range(CONV_K - 1)] + [cur]
    y = w[0:1] * taps[0]
    for j in range(1, CONV_K):
        y = y + w[j:j + 1] * taps[j]
    return y, taps


def _conv_fwd(proj, w, b, name):
    tp = proj.shape[0]
    rb = _tile(tp, ROW_TILE_CAP, 8)

    def body(x_ref, w_ref, b_ref, o_ref):
        gated = pl.program_id(0) > 0
        wv = w_ref[...]
        bv = b_ref[...]
        for r0 in range(0, tp, rb):
            y, _ = _conv_taps(x_ref, wv, r0, rb)
            y = y + bv
            o_ref[r0:r0 + rb, :] = jnp.where(gated, y * jax.nn.sigmoid(y), y)

    col = lambda j: (0, j)
    return _call(
        body, 25, name=name, out_shape=_sds((tp, CONV_W), F32), grid=(CONV_W // CONV_TC,),
        in_specs=[pl.BlockSpec((tp, CONV_TC), col), pl.BlockSpec((CONV_K, CONV_TC), col), pl.BlockSpec((1, CONV_TC), col)],
        out_specs=pl.BlockSpec((tp, CONV_TC), col),
        compiler_params=_params(("arbitrary",)),
    )(proj, w, b)


def _conv_bwd(proj, w, b, dxc, name):
    tp = proj.shape[0]
    rb = _tile(tp, ROW_TILE_CAP, 8)

    def body(x_ref, w_ref, b_ref, d_ref, dx_ref, dw_ref, db_ref, dy_scr):
        gated = pl.program_id(0) > 0
        wv = w_ref[...]
        bv = b_ref[...]
        dw = [jnp.zeros((1, CONV_TC), F32) for _ in range(CONV_K)]
        db = jnp.zeros((1, CONV_TC), F32)
        for r0 in range(0, tp, rb):
            y, taps = _conv_taps(x_ref, wv, r0, rb)
            y = y + bv
            d = d_ref[r0:r0 + rb, :]
            dy = jnp.where(gated, d * _silu_grad(y), d)
            dy_scr[r0:r0 + rb, :] = dy
            for j in range(CONV_K):
                dw[j] = dw[j] + jnp.sum(dy * taps[j], axis=0, keepdims=True)
            db = db + jnp.sum(dy, axis=0, keepdims=True)
        for j in range(CONV_K):
            dw_ref[j:j + 1, :] = dw[j]
        db_ref[...] = db
        for r0 in range(0, tp, rb):
            cur = dy_scr[r0:r0 + rb, :]
            nxt = dy_scr[r0 + rb:r0 + rb + 8, :] if r0 + rb < tp else jnp.zeros((8, CONV_TC), F32)
            dcat = jnp.concatenate([cur, nxt], axis=0)
            dx = wv[3:4] * cur
            for s in range(1, CONV_K):
                dx = dx + wv[3 - s:4 - s] * dcat[s:s + rb]
            dx_ref[r0:r0 + rb, :] = dx.astype(BF16)

    col = lambda j: (0, j)
    return _call(
        body, 70, name=name,
        out_shape=(_sds((tp, CONV_W), BF16), _sds((CONV_K, CONV_W), F32), _sds((1, CONV_W), F32)),
        grid=(CONV_W // CONV_TC,),
        in_specs=[pl.BlockSpec((tp, CONV_TC), col), pl.BlockSpec((CONV_K, CONV_TC), col), pl.BlockSpec((1, CONV_TC), col),
                  pl.BlockSpec((tp, CONV_TC), col)],
        out_specs=[pl.BlockSpec((tp, CONV_TC), col), pl.BlockSpec((CONV_K, CONV_TC), col), pl.BlockSpec((1, CONV_TC), col)],
        scratch_shapes=[pltpu.VMEM((tp, CONV_TC), F32)],
        compiler_params=_params(("arbitrary",)),
    )(proj, w, b, dxc)


def _neg_expm1(x):
    series = -x * (1.0 + x * (0.5 + x * (1.0 / 6.0 + x * (1.0 / 24.0))))
    return jnp.where(jnp.abs(x) < 0.03, series, 1.0 - jnp.exp(x))


def _lru_gates(u, wa, ba, wx, bx, lam):
    r = jax.nn.sigmoid(jnp.dot(u, wa, precision=HI, preferred_element_type=F32) + ba)
    i = jax.nn.sigmoid(jnp.dot(u, wx, precision=HI, preferred_element_type=F32) + bx)
    log_a = -LRU_C * r * jax.nn.softplus(-lam)
    a = jnp.exp(log_a)
    b = jnp.sqrt(_neg_expm1(2.0 * log_a)) * (i * u)
    return a, b


def _lru_specs(tm, flip, n_tiles):
    idx = (lambda i: (n_tiles - 1 - i, 0)) if flip else (lambda i: (i, 0))
    return idx, lambda i: (0, 0)


def _lru_fwd(xc, proj, wa, ba, wx, bx, lam, name):
    tp = xc.shape[0]
    w = LRU_WIDTH
    tm = _tile(tp, ROW_TILE_CAP, 8)
    ycol = OFF_LRU_Y // w

    def body(u_ref, y_ref, wa_ref, ba_ref, wx_ref, bx_ref, lam_ref, o_ref, h_ref, a_scr, b_scr, carry):
        @pl.when(pl.program_id(0) == 0)
        def _():
            carry[...] = jnp.zeros_like(carry)
        a, b = _lru_gates(u_ref[...], wa_ref[...], ba_ref[...], wx_ref[...], bx_ref[...], lam_ref[...])
        a_scr[...] = a
        b_scr[...] = b

        def step(t, h):
            h = a_scr[pl.ds(t, 1), :] * h + b_scr[pl.ds(t, 1), :]
            h_ref[pl.ds(t, 1), :] = h
            return h

        carry[0:1, :] = lax.fori_loop(0, tm, step, carry[0:1, :], unroll=8)
        o_ref[...] = (h_ref[...] * jax.nn.gelu(y_ref[...])).astype(BF16)

    fix = lambda i: (0, 0)
    return _call(
        body, 40, name=name, out_shape=(_sds((tp, w), BF16), _sds((tp, w), F32)), grid=(tp // tm,),
        in_specs=[pl.BlockSpec((tm, w), lambda i: (i, 0)), pl.BlockSpec((tm, w), lambda i: (i, ycol)),
                  pl.BlockSpec((w, w), fix), pl.BlockSpec((1, w), fix), pl.BlockSpec((w, w), fix),
                  pl.BlockSpec((1, w), fix), pl.BlockSpec((1, w), fix)],
        out_specs=[pl.BlockSpec((tm, w), lambda i: (i, 0))] * 2,
        scratch_shapes=[pltpu.VMEM((tm, w), F32), pltpu.VMEM((tm, w), F32), pltpu.VMEM((8, w), F32)],
        compiler_params=_params(("arbitrary",)),
    )(xc, proj, wa, ba, wx, bx, lam)


def _lru_bwd(xc, proj, h, dout, wa, ba, wx, bx, lam, name):
    tp = xc.shape[0]
    w = LRU_WIDTH
    tm = _tile(tp, ROW_TILE_CAP, 8)
    nt = tp // tm
    ycol = OFF_LRU_Y // w
    rev = lambda i: (nt - 1 - i, 0)
    prev8 = lambda i: (jnp.maximum((nt - 1 - i) * (tm // 8) - 1, 0), 0)

    def body(u_ref, y_ref, h_ref, hp_ref, do_ref, wa_ref, ba_ref, wx_ref, bx_ref, lam_ref,
             du_ref, dy_ref, dwa_ref, dba_ref, dwx_ref, dbx_ref, dlam_ref,
             a_scr, dh_scr, g_scr, da_scr, hext, carry):
        i = pl.program_id(0)

        @pl.when(i == 0)
        def _():
            carry[...] = jnp.zeros_like(carry)
            for r in (dwa_ref, dba_ref, dwx_ref, dbx_ref, dlam_ref):
                r[...] = jnp.zeros_like(r)
        params = (wa_ref[...], ba_ref[...], wx_ref[...], bx_ref[...], lam_ref[...])
        (a, _), gates_vjp = jax.vjp(_lru_gates, u_ref[...], *params)
        gel, gelu_vjp = jax.vjp(jax.nn.gelu, y_ref[...])
        do = do_ref[...]
        hv = h_ref[...]
        dy_ref[...] = gelu_vjp(do * hv)[0].astype(BF16)
        a_scr[...] = a
        dh_scr[...] = do * gel
        hext[0:8, :] = jnp.where(i == nt - 1, 0.0, hp_ref[...])
        hext[8:8 + tm, :] = hv

        def step(s, c):
            t = tm - 1 - s
            g = dh_scr[pl.ds(t, 1), :] + c
            g_scr[pl.ds(t, 1), :] = g
            da_scr[pl.ds(t, 1), :] = g * hext[pl.ds(t + 7, 1), :]
            return a_scr[pl.ds(t, 1), :] * g

        carry[0:1, :] = lax.fori_loop(0, tm, step, carry[0:1, :], unroll=8)
        du, dwa, dba, dwx, dbx, dlam = gates_vjp((da_scr[...], g_scr[...]))
        du_ref[...] = du
        dwa_ref[...] += dwa
        dba_ref[...] += dba
        dwx_ref[...] += dwx
        dbx_ref[...] += dbx
        dlam_ref[...] += dlam

    fix = lambda i: (0, 0)
    tile = pl.BlockSpec((tm, w), rev)
    mat = pl.BlockSpec((w, w), fix)
    vec = pl.BlockSpec((1, w), fix)
    return _call(
        body, 90, name=name,
        out_shape=(_sds((tp, w), F32), _sds((tp, w), BF16), _sds((w, w), F32), _sds((1, w), F32), _sds((w, w), F32),
                   _sds((1, w), F32), _sds((1, w), F32)),
        grid=(nt,),
        in_specs=[tile, pl.BlockSpec((tm, w), lambda i: (nt - 1 - i, ycol)), tile, pl.BlockSpec((8, w), prev8), tile,
                  mat, vec, mat, vec, vec],
        out_specs=[tile, tile, mat, vec, mat, vec, vec],
        scratch_shapes=[pltpu.VMEM((tm, w), F32)] * 4 + [pltpu.VMEM((tm + 8, w), F32), pltpu.VMEM((8, w), F32)],
        compiler_params=_params(("arbitrary",)),
    )(xc, proj, h, h, dout, wa, ba, wx, bx, lam)


def _dot3(a, b, dims):
    ah = a.astype(BF16)
    al = (a - ah.astype(F32)).astype(BF16)
    bh = b.astype(BF16)
    bl = (b - bh.astype(F32)).astype(BF16)
    dot = lambda x, y: lax.dot_general(x, y, (dims, ((0,), (0,))), preferred_element_type=F32)
    return dot(ah, bh) + (dot(ah, bl) + dot(al, bh))


@jax.custom_vjp
def _bmm(a, b):
    return _dot3(a, b, ((2,), (1,)))


@jax.custom_vjp
def _bmm_nt(a, b):
    return _dot3(a, b, ((2,), (2,)))


@jax.custom_vjp
def _bmm_tn(a, b):
    return _dot3(a, b, ((1,), (1,)))


_bmm.defvjp(lambda a, b: (_bmm(a, b), (a, b)), lambda r, g: (_bmm_nt(g, r[1]), _bmm_tn(r[0], g)))
_bmm_nt.defvjp(lambda a, b: (_bmm_nt(a, b), (a, b)), lambda r, g: (_bmm(g, r[1]), _bmm_tn(g, r[0])))
_bmm_tn.defvjp(lambda a, b: (_bmm_tn(a, b), (a, b)), lambda r, g: (_bmm_nt(r[1], g), _bmm(r[0], g)))


def _dot1(a, b, dims):
    return lax.dot_general(a.astype(BF16), b.astype(BF16), (dims, ((0,), (0,))), preferred_element_type=F32)


@jax.custom_vjp
def _bmm1(a, b):
    return _dot1(a, b, ((2,), (1,)))


@jax.custom_vjp
def _bmm1_nt(a, b):
    return _dot1(a, b, ((2,), (2,)))


@jax.custom_vjp
def _bmm1_tn(a, b):
    return _dot1(a, b, ((1,), (1,)))


_bmm1.defvjp(lambda a, b: (_bmm1(a, b), (a, b)), lambda r, g: (_bmm1_nt(g, r[1]), _bmm1_tn(r[0], g)))
_bmm1_nt.defvjp(lambda a, b: (_bmm1_nt(a, b), (a, b)), lambda r, g: (_bmm1(g, r[1]), _bmm1_tn(g, r[0])))
_bmm1_tn.defvjp(lambda a, b: (_bmm1_tn(a, b), (a, b)), lambda r, g: (_bmm1_nt(r[1], g), _bmm1(r[0], g)))


def _chunk_masks(nh):
    r = lax.broadcasted_iota(jnp.int32, (CHUNK, CHUNK), 0)
    c = lax.broadcasted_iota(jnp.int32, (CHUNK, CHUNK), 1)
    full = lambda m: jnp.broadcast_to(m[None], (nh, CHUNK, CHUNK))
    return r, c, full


def _decay_terms(g, nh):
    r, c, full = _chunk_masks(nh)
    incl = r >= c
    cs = _bmm(full(incl.astype(F32)), g)
    cs_t = jnp.swapaxes(cs, 1, 2)
    tot = jnp.broadcast_to(jnp.sum(g, axis=1, keepdims=True), g.shape)
    m = full(incl)
    decay = jnp.where(m, jnp.exp(jnp.where(m, cs - cs_t, 0.0)), 0.0)
    return cs, decay, tot


def _rep_groups(x):
    return jnp.concatenate([jnp.broadcast_to(x[g:g + 1], (SSD_HPG,) + x.shape[1:]) for g in range(SSD_GROUPS)], axis=0)


def _ssd_chunk(xs, bm, cm, dtr, z, p_dtb, p_alog, p_d, p_nw, state, saved=None):
    nh = SSD_HEADS
    dt = jax.nn.softplus(dtr + p_dtb)
    a = dt * (-jnp.exp(p_alog))
    x = xs * dt
    cs, decay, tot = _decay_terms(a, nh)
    b8 = _rep_groups(bm)
    c8 = _rep_groups(cm)
    y = _bmm1(_rep_groups(_bmm1_nt(cm, bm)) * decay, x)
    y = y + _bmm1_nt(c8, state) * jnp.exp(cs)
    new_state = state * jnp.exp(tot) + _bmm1_tn(x * jnp.exp(tot - cs), b8)
    y = y + p_d * xs
    y = y * (z * jax.nn.sigmoid(z))
    ss = jnp.sum(y * y, axis=-1, keepdims=True)
    ssg = jnp.concatenate(
        [jnp.broadcast_to(jnp.sum(ss[g * SSD_HPG:(g + 1) * SSD_HPG], axis=0, keepdims=True), (SSD_HPG, CHUNK, 1))
         for g in range(SSD_GROUPS)], axis=0)
    y = y * lax.rsqrt(ssg * (1.0 / (SSD_HPG * HEAD)) + RMS_EPS) * p_nw
    return y, new_state, None


@jax.custom_vjp
def _unit_lower_inverse(m):
    r, c, full = _chunk_masks(m.shape[0])
    eye = full((r == c).astype(F32))
    md = jnp.where(full((r // 16) == (c // 16)), m, 0.0)
    mo = m - md
    x = eye - md
    p = _bmm(md, md)
    x = x + _bmm(x, p)
    p = _bmm(p, p)
    x = x + _bmm(x, p)
    p = _bmm(p, p)
    x = x + _bmm(x, p)
    n = _bmm(x, mo)
    y = x - _bmm(n, x)
    return y + _bmm(_bmm(n, n), y)


def _unit_lower_inverse_fwd(m):
    t = _unit_lower_inverse(m)
    return t, t


_unit_lower_inverse.defvjp(_unit_lower_inverse_fwd, lambda t, g: (-_bmm_nt(_bmm_tn(t, g), t),))


@jax.custom_vjp
def _saved_inverse(m, t):
    return t


_saved_inverse.defvjp(lambda m, t: (t, t), lambda t, g: (-_bmm_nt(_bmm_tn(t, g), t), jnp.zeros_like(t)))


def _dn_chunk(q, k, v, gate, braw, araw, p_alog, p_dtb, p_nw, state, saved=None):
    nh = DN_HEADS
    r, c, full = _chunk_masks(nh)
    q = q * lax.rsqrt(jnp.sum(q * q, axis=-1, keepdims=True) + RMS_EPS) * (HEAD ** -0.5)
    k = k * lax.rsqrt(jnp.sum(k * k, axis=-1, keepdims=True) + RMS_EPS)
    beta = jax.nn.sigmoid(braw)
    g = -jnp.exp(p_alog) * jax.nn.softplus(araw + p_dtb)
    gcs, decay, tot = _decay_terms(g, nh)
    kb = k * beta
    vb = v * beta
    m = jnp.where(full(r > c), _bmm1_nt(kb, k) * decay, 0.0)
    t = _unit_lower_inverse(m) if saved is None else _saved_inverse(m, saved)
    egcs = jnp.exp(gcs)
    u = _bmm(t, vb)
    w = _bmm(t, kb * egcs)
    attn = _bmm1_nt(q, k) * decay
    v_new = u - _bmm1(w, state)
    out = _bmm1(q * egcs, state) + _bmm1(attn, v_new)
    new_state = state * jnp.exp(tot) + _bmm1_tn(k * jnp.exp(tot - gcs), v_new)
    out = out * lax.rsqrt(jnp.mean(out * out, axis=-1, keepdims=True) + RMS_EPS) * p_nw
    return out * (gate * jax.nn.sigmoid(gate)), new_state, t


def _chunks_per_step(nc):
    return max(n for n in range(1, 6) if nc % n == 0)


def _heads(t, n):
    return jnp.stack([t[:, HEAD * h:HEAD * (h + 1)] for h in range(n)])


def _unheads(a):
    return jnp.concatenate([a[h] for h in range(a.shape[0])], axis=-1)


def _head_scalars(t, off, n):
    return jnp.stack([jnp.broadcast_to(t[:, off + h:off + h + 1], (CHUNK, HEAD)) for h in range(n)])


def _unhead_scalars(d, off):
    red = jnp.sum(d, axis=-1, keepdims=True)
    lane = lax.broadcasted_iota(jnp.int32, (CHUNK, LANE), 1)
    out = jnp.zeros((CHUNK, LANE), F32)
    for h in range(d.shape[0]):
        out = out + jnp.where(lane == off + h, red[h], 0.0)
    return out


def _chunk_scan_fwd(chunk_fn, srcs, read, params, nh, keeps, host_us, name):
    tp = srcs[0][0].shape[0]
    nc = tp // CHUNK
    nb = _chunks_per_step(nc)
    rows = nb * CHUNK
    ns, npar = len(srcs), len(params)

    def body(*refs):
        s_refs, p_refs = refs[:ns], refs[ns:ns + npar]
        y_ref, st_ref = refs[ns + npar:ns + npar + 2]
        keep_ref = refs[ns + npar + 2] if keeps else None
        state = refs[-1]

        @pl.when(pl.program_id(0) == 0)
        def _():
            state[...] = jnp.zeros_like(state)
        par = [r[...] for r in p_refs]
        st = state[...]
        for k in range(nb):
            sl = slice(k * CHUNK, (k + 1) * CHUNK)
            st_ref[k] = st
            y, st, kept = chunk_fn(*read(s_refs, sl), *par, st)
            y_ref[sl, :] = _unheads(y).astype(BF16)
            if keeps:
                keep_ref[k] = kept
        state[...] = st

    src_spec = lambda s: pl.BlockSpec((rows, s[1]), lambda c: (c, s[2]))
    par_spec = lambda a: pl.BlockSpec(a.shape, lambda c: (0, 0, 0))
    per_chunk = (_sds((nc, nh, HEAD, HEAD), F32), pl.BlockSpec((nb, nh, HEAD, HEAD), lambda c: (c, 0, 0, 0)))
    n_chunk_outs = 2 if keeps else 1
    return _call(
        body, host_us, name=name,
        out_shape=(_sds((tp, nh * HEAD), BF16),) + (per_chunk[0],) * n_chunk_outs,
        grid=(nc // nb,),
        in_specs=[src_spec(s) for s in srcs] + [par_spec(a) for a in params],
        out_specs=[pl.BlockSpec((rows, nh * HEAD), lambda c: (c, 0))] + [per_chunk[1]] * n_chunk_outs,
        scratch_shapes=[pltpu.VMEM((nh, HEAD, HEAD), F32)],
        compiler_params=_params(("arbitrary",)),
    )(*[s[0] for s in srcs], *params)


def _chunk_scan_bwd(chunk_fn, srcs, read, params, states, kept, dsrc, write, out_widths, nh, host_us, name):
    tp = srcs[0][0].shape[0]
    nc = tp // CHUNK
    nb = _chunks_per_step(nc)
    rows = nb * CHUNK
    steps = nc // nb
    ns, npar, nout = len(srcs), len(params), len(out_widths)
    d_arr, d_off = dsrc
    per_chunk = [states] if kept is None else [states, kept]
    nin = ns + npar + len(per_chunk) + 1

    def body(*refs):
        s_refs, p_refs = refs[:ns], refs[ns:ns + npar]
        st_ref, dy_ref = refs[ns + npar], refs[nin - 1]
        kept_ref = None if kept is None else refs[ns + npar + 1]
        o_refs = refs[nin:nin + nout]
        dp_refs = refs[nin + nout:nin + nout + npar]
        dstate = refs[-1]

        @pl.when(pl.program_id(0) == 0)
        def _():
            dstate[...] = jnp.zeros_like(dstate)
            for r in dp_refs:
                r[...] = jnp.zeros_like(r)
        par = [r[...] for r in p_refs]
        dst = dstate[...]
        dpar = None
        for k in reversed(range(nb)):
            sl = slice(k * CHUNK, (k + 1) * CHUNK)
            seqs = read(s_refs, sl)
            saved = None if kept is None else kept_ref[k]
            _, vjp = jax.vjp(lambda *a: chunk_fn(*a, saved=saved)[:2], *seqs, *par, st_ref[k])
            grads = vjp((_heads(dy_ref[sl, d_off:d_off + nh * HEAD], nh), dst))
            for r, tile in zip(o_refs, write(*grads[:len(seqs)])):
                r[sl, :] = tile.astype(r.dtype)
            gp = grads[len(seqs):len(seqs) + npar]
            dpar = gp if dpar is None else [a + b for a, b in zip(dpar, gp)]
            dst = grads[-1]
        for r, gr in zip(dp_refs, dpar):
            r[...] += gr
        dstate[...] = dst

    rev = lambda c: steps - 1 - c
    src_spec = lambda s: pl.BlockSpec((rows, s[1]), lambda c: (rev(c), s[2]))
    par_spec = lambda a: pl.BlockSpec(a.shape, lambda c: (0, 0, 0))
    out_spec = lambda w: pl.BlockSpec((rows, w), lambda c: (rev(c), 0))
    return tuple(_call(
        body, host_us, name=name,
        out_shape=tuple(_sds((tp, w), dt) for w, dt in out_widths) + tuple(_sds(a.shape, F32) for a in params),
        grid=(steps,),
        in_specs=[src_spec(s) for s in srcs] + [par_spec(a) for a in params]
        + [pl.BlockSpec((nb, nh, HEAD, HEAD), lambda c: (rev(c), 0, 0, 0))] * len(per_chunk) + [out_spec(d_arr.shape[1])],
        out_specs=[out_spec(w) for w, _ in out_widths] + [par_spec(a) for a in params],
        scratch_shapes=[pltpu.VMEM((nh, HEAD, HEAD), F32)],
        compiler_params=_params(("arbitrary",)),
    )(*[s[0] for s in srcs], *params, *per_chunk, d_arr))


def _lane_param(p):
    return jnp.broadcast_to(p[:, None, None], (p.shape[0], 1, HEAD))


def _block_diag(w):
    out = jnp.zeros((LRU_WIDTH, LRU_WIDTH), F32)
    for h in range(LRU_HEADS):
        out = out.at[h * HEAD:(h + 1) * HEAD, h * HEAD:(h + 1) * HEAD].set(w[h])
    return out


def _block_diag_inv(w):
    return jnp.stack([w[h * HEAD:(h + 1) * HEAD, h * HEAD:(h + 1) * HEAD] for h in range(LRU_HEADS)])


def _ssd_inputs(xc, proj, lp):
    srcs = ((xc, CONV_W, 0), (proj, SSD_INNER, OFF_SSD_Z // SSD_INNER), (proj, LANE, OFF_SMALL // LANE))
    params = (_lane_param(lp["ssd_dt_bias"]), _lane_param(lp["ssd_a_log"]), _lane_param(lp["ssd_d"]),
              lp["ssd_norm_w"].reshape(SSD_HEADS, 1, HEAD))
    return srcs, params


def _ssd_read(refs, rows):
    xc, z, sm = refs
    return (_heads(xc[rows, 256:768], SSD_HEADS), _heads(xc[rows, 768:896], SSD_GROUPS),
            _heads(xc[rows, 896:1024], SSD_GROUPS), _head_scalars(sm[rows, :], 0, SSD_HEADS), _heads(z[rows, :], SSD_HEADS))


def _ssd_write(dxs, dbm, dcm, ddtr, dz):
    return (jnp.concatenate([_unheads(dxs), _unheads(dbm), _unheads(dcm)], axis=-1), _unheads(dz), _unhead_scalars(ddtr, 0))


def _dn_inputs(xc, proj, lp):
    srcs = ((xc, CONV_W, 0), (proj, 256, OFF_DN_GATE // 256), (proj, LANE, OFF_SMALL // LANE))
    params = (_lane_param(lp["dn_a_log"]), _lane_param(lp["dn_dt_bias"]),
              jnp.broadcast_to(lp["dn_norm_w"][None, None, :], (DN_HEADS, 1, HEAD)))
    return srcs, params


def _dn_read(refs, rows):
    xc, gate, sm = refs
    return (_heads(xc[rows, 1024:1280], DN_HEADS), _heads(xc[rows, 1280:1536], DN_HEADS),
            _heads(xc[rows, 1536:1792], DN_HEADS), _heads(gate[rows, :], DN_HEADS),
            _head_scalars(sm[rows, :], 8, DN_HEADS), _head_scalars(sm[rows, :], 12, DN_HEADS))


def _dn_write(dq, dk, dv, dgate, dbraw, daraw):
    return (jnp.concatenate([_unheads(dq), _unheads(dk), _unheads(dv)], axis=-1), _unheads(dgate),
            _unhead_scalars(dbraw, 8) + _unhead_scalars(daraw, 12))


def _lru_params(lp):
    return (_block_diag(lp["lru_w_a"]), lp["lru_b_a"][None], _block_diag(lp["lru_w_x"]), lp["lru_b_x"][None],
            lp["lru_lambda"][None])


def _conv_params(lp):
    w = jnp.concatenate([lp["lru_conv_w"], lp["ssd_conv_w"], lp["dn_conv_w"]], axis=1)
    b = jnp.concatenate([lp["lru_conv_b"], lp["ssd_conv_b"], jnp.zeros((768,), F32)])[None]
    return w, b


def _mixers_fwd(proj, lp, tag):
    cw, cb = _conv_params(lp)
    xc = _conv_fwd(proj, cw, cb, tag + "_conv")
    o_lru, h_lru = _lru_fwd(xc, proj, *_lru_params(lp), name=tag + "_lru")
    s_srcs, s_par = _ssd_inputs(xc, proj, lp)
    o_ssd, s_states = _chunk_scan_fwd(_ssd_chunk, s_srcs, _ssd_read, s_par, SSD_HEADS, False, 70, tag + "_ssd")
    d_srcs, d_par = _dn_inputs(xc, proj, lp)
    o_dn, d_states, d_inv = _chunk_scan_fwd(_dn_chunk, d_srcs, _dn_read, d_par, DN_HEADS, True, 140, tag + "_dn")
    cat = jnp.concatenate([o_lru, o_ssd, o_dn], axis=1)
    return cat, (xc, h_lru, s_states, d_states, d_inv)


def _mixers_bwd(proj, lp, saved, dcat, tag):
    xc, h_lru, s_states, d_states, d_inv = saved
    tp = proj.shape[0]
    cw, cb = _conv_params(lp)
    g = {}
    du, dyraw, dwa, dba, dwx, dbx, dlam = _lru_bwd(xc, proj, h_lru, dcat, *_lru_params(lp), name=tag + "_lru_bwd")
    g["lru_w_a"], g["lru_b_a"], g["lru_w_x"], g["lru_b_x"], g["lru_lambda"] = (
        _block_diag_inv(dwa), dba[0], _block_diag_inv(dwx), dbx[0], dlam[0])

    s_srcs, s_par = _ssd_inputs(xc, proj, lp)
    dxbc, dz, dsm_ssd, dp_dtb, dp_alog, dp_d, dp_nw = _chunk_scan_bwd(
        _ssd_chunk, s_srcs, _ssd_read, s_par, s_states, None, (dcat, 256), _ssd_write, ((768, F32), (SSD_INNER, BF16), (LANE, F32)), SSD_HEADS, 125,
        tag + "_ssd_bwd")
    g["ssd_dt_bias"], g["ssd_a_log"], g["ssd_d"] = (jnp.sum(p, axis=(1, 2)) for p in (dp_dtb, dp_alog, dp_d))
    g["ssd_norm_w"] = dp_nw.reshape(SSD_INNER)

    d_srcs, d_par = _dn_inputs(xc, proj, lp)
    dqkv, dgate, dsm_dn, dq_alog, dq_dtb, dq_nw = _chunk_scan_bwd(
        _dn_chunk, d_srcs, _dn_read, d_par, d_states, d_inv, (dcat, 768), _dn_write, ((768, F32), (256, BF16), (LANE, F32)), DN_HEADS, 300,
        tag + "_dn_bwd")
    g["dn_a_log"], g["dn_dt_bias"] = (jnp.sum(p, axis=(1, 2)) for p in (dq_alog, dq_dtb))
    g["dn_norm_w"] = jnp.sum(dq_nw, axis=(0, 1))

    dconv, dcw, dcb = _conv_bwd(proj, cw, cb, jnp.concatenate([du, dxbc, dqkv], axis=1), tag + "_conv_bwd")
    g["lru_conv_w"], g["ssd_conv_w"], g["dn_conv_w"] = dcw[:, :256], dcw[:, 256:1024], dcw[:, 1024:]
    g["lru_conv_b"], g["ssd_conv_b"] = dcb[0, :256], dcb[0, 256:1024]
    dproj = jnp.concatenate([dconv, dyraw, dz, dgate, (dsm_ssd + dsm_dn).astype(BF16)], axis=1)
    return dproj, g


MIXER_PARAMS = ("lru_conv_w", "lru_conv_b", "lru_w_a", "lru_b_a", "lru_w_x", "lru_b_x", "lru_lambda",
                "ssd_conv_w", "ssd_conv_b", "ssd_dt_bias", "ssd_a_log", "ssd_d", "ssd_norm_w",
                "dn_conv_w", "dn_a_log", "dn_dt_bias", "dn_norm_w")


UNITS = ("gate0", "up0", "down0", "win", "wout", "gate1", "up1", "down1")


def _layer_shards(w, l):
    out = {"win": w["w_in"][l], "wout": w["w_out"][l]}
    for j in range(2):
        out[f"gate{j}"], out[f"up{j}"], out[f"down{j}"] = w["ffn_w_gate"][l, j], w["ffn_w_up"][l, j], w["ffn_w_down"][l, j]
    return {k: a.astype(BF16) for k, a in out.items()}


def _enqueue_matrix(name, arr, slab):
    if name == "wout" or not slab:
        return [_enqueue(arr, slab)], 1
    return _enqueue_halves(arr, slab, 1)


def _enqueue_layer(shards):
    return {k: _enqueue_matrix(k, shards[k], False) for k in UNITS}


def _gathered_cols(a):
    return a.transpose(1, 0, 2).reshape(a.shape[1], -1)


def _col_slabs(a):
    return a.reshape(a.shape[0], N_DEV, -1).transpose(1, 0, 2).astype(BF16)


def _row_slabs(a):
    return a.reshape(N_DEV, -1, a.shape[1]).astype(BF16)


def _local_step(x, tgt, small, first, shards):
    s = x.shape[0]
    t_real = N_META + s
    tp = -(-t_real // CHUNK) * CHUNK
    depth = len(first) if shards is None else len(shards)
    h = jnp.concatenate([small["meta"], x, jnp.zeros((tp - t_real, D_MODEL), F32)], axis=0)
    hb = h.astype(BF16)
    ln_g, ln_b = small["ln_g"], small["ln_b"]
    saved, weights = [], []
    queued = first
    for l in range(depth):
        lp = {k: small[k][l] for k in MIXER_PARAMS}
        t = f"l{l}"
        if shards is None:
            get = lambda k, g=first[l]: g[k]
        else:
            get = lambda k, g=queued: _collect_halves(g[k])
            if l + 1 < depth:
                queued = _enqueue_layer(shards[l + 1])

        def ffn_weights(j):
            up = _gathered_cols(get(f"up{j}"))
            return _gathered_cols(get(f"gate{j}")), up

        wgu0 = ffn_weights(0)
        g0, u0, a0 = _ffn_up(hb, *wgu0, t + "_ffn0_up")
        wd0 = get("down0").reshape(-1, D_MODEL)
        y1, h1, h1b = _mm_resid_ln(a0, wd0, h, ln_g[l, 0][None], ln_b[l, 0][None], FFN_RES, t + "_ffn0_down")
        win = _proj_cols(_gathered_cols(get("win")))
        proj = _mm_nn(h1b, win, t + "_in_proj")
        cat, mix_saved = _mixers_fwd(proj, lp, t)
        catb = cat.astype(BF16)
        wout = get("wout").reshape(D_MODEL, D_MODEL)
        y2, h2, h2b = _mm_resid_ln(catb, wout, h1, ln_g[l, 1][None], ln_b[l, 1][None], 1.0, t + "_out_proj")
        wgu1 = ffn_weights(1)
        g1, u1, a1 = _ffn_up(h2b, *wgu1, t + "_ffn1_up")
        wd1 = get("down1").reshape(-1, D_MODEL)
        y3, h3, h3b = _mm_resid_ln(a1, wd1, h2, ln_g[l, 2][None], ln_b[l, 2][None], FFN_RES, t + "_ffn1_down")
        weights.append(([wgu0, wgu1], [wd0, wd1], win, wout))
        saved.append((hb, g0, u0, a0, y1, h1b, proj, mix_saved, catb, y2, h2b, g1, u1, a1, y3))
        h, hb = h3, h3b

    tgt_p = jnp.pad(tgt, ((N_META, tp - t_real), (0, 0)))
    lossv, dh = _loss_grad(h, tgt_p, t_real, "loss")
    loss = jnp.sum(lossv)

    gs = {k: [None] * depth for k in MIXER_PARAMS}
    d_ln_g = [[None] * 3 for _ in range(depth)]
    d_ln_b = [[None] * 3 for _ in range(depth)]
    slabs = [{} for _ in range(depth)]
    send = (lambda k, a: a) if shards is None else (lambda k, a: _enqueue_matrix(k, a, True))

    def ffn_bwd(l, j, xb_in, g, u, a, y, dout):
        t = f"l{l}_ffn{j}"
        wgu, wd = weights[l][0][j], weights[l][1][j]
        dres, dyb, dgam, dbet = _ln_bwd(y, dout, ln_g[l, 2 * j][None], FFN_RES, t + "_ln_bwd")
        d_ln_g[l][2 * j], d_ln_b[l][2 * j] = dgam[0], dbet[0]
        slabs[l][f"down{j}"] = send("down", _row_slabs(_mm_tn(a, dyb, t + "_dwd")))
        dg, du = _ffn_dact(dyb, wd, g, u, t + "_dact")
        slabs[l][f"gate{j}"] = send("gate", _col_slabs(_mm_tn(xb_in, dg, t + "_dwg")))
        slabs[l][f"up{j}"] = send("up", _col_slabs(_mm_tn(xb_in, du, t + "_dwu")))
        return _ffn_dx(dg, du, *wgu, dres, t + "_dx")

    for l in reversed(range(depth)):
        hb_in, g0, u0, a0, y1, h1b, proj, mix_saved, catb, y2, h2b, g1, u1, a1, y3 = saved[l]
        lp = {k: small[k][l] for k in MIXER_PARAMS}
        t = f"l{l}"
        _, _, win, wout = weights[l]
        dh2 = ffn_bwd(l, 1, h2b, g1, u1, a1, y3, dh)
        dres, dyb, dgam, dbet = _ln_bwd(y2, dh2, ln_g[l, 1][None], 1.0, t + "_mix_ln_bwd")
        d_ln_g[l][1], d_ln_b[l][1] = dgam[0], dbet[0]
        slabs[l]["wout"] = send("wout", _row_slabs(_mm_tn(catb, dyb, t + "_dwout")))
        dcat = _mm_nt_add(dyb, wout, None, t + "_dcat")
        dproj, mg = _mixers_bwd(proj, lp, mix_saved, dcat, t)
        for k in MIXER_PARAMS:
            gs[k][l] = mg[k]
        dprojb = dproj.astype(BF16)
        slabs[l]["win"] = send("win", _col_slabs(_proj_cols_inv(_mm_tn(h1b, dprojb, t + "_dwin"))))
        dh1 = _mm_nt_add(dprojb, win, dres, t + "_dh1")
        dh = ffn_bwd(l, 0, hb_in, g0, u0, a0, y1, dh1)

    small_grads = {k: jnp.stack(v) for k, v in gs.items()}
    small_grads["ln_g"] = jnp.stack([jnp.stack(r) for r in d_ln_g])
    small_grads["ln_b"] = jnp.stack([jnp.stack(r) for r in d_ln_b])
    small_grads["meta"] = dh[:N_META]
    return loss, dh[N_META:t_real], small_grads, slabs


def _mesh_pos():
    return lax.axis_index("x"), lax.axis_index("y"), lax.axis_index("c")


def _flip(pos, k):
    x, y, c = pos
    return (1 - x if k & 4 else x, 1 - y if k & 2 else y, 1 - c if k & 1 else c)


def _flat(pos):
    return 4 * pos[0] + 2 * pos[1] + pos[2]


def _exchange_copies(ins, outs, slabs, send, recv, loc):
    pos = _mesh_pos()
    me = _flat(pos)
    local, sends, receives = [], [], []
    for a in range(len(ins)):
        local.append(pltpu.make_async_copy(ins[a].at[me] if slabs[a] else ins[a], outs[a].at[me], loc.at[a]))
    for k in range(1, N_DEV):
        peer = _flip(pos, k)
        for a in range(len(ins)):
            if not slabs[a] and k not in GATHER_FIRST:
                continue
            sem = dict(send_sem=send.at[a, k - 1], recv_sem=recv.at[a, k - 1], device_id=peer, device_id_type=MESH_IDS)
            sends.append(pltpu.make_async_remote_copy(
                src_ref=ins[a].at[_flat(peer)] if slabs[a] else ins[a], dst_ref=outs[a].at[me], **sem))
            receives.append(pltpu.make_async_remote_copy(
                src_ref=ins[a].at[me] if slabs[a] else ins[a], dst_ref=outs[a].at[_flat(peer)], **sem))
    return local, sends, receives


GATHER_FIRST = (1, 2, 4, 6)


def _gather_forwards(ins, outs, slabs, send, recv):
    pos = _mesh_pos()
    sibling = _flip(pos, 1)
    forwards, arrivals = [], []
    for a in range(len(ins)):
        if slabs[a]:
            continue
        for k in GATHER_FIRST[1:]:
            sem = dict(send_sem=send.at[a, k], recv_sem=recv.at[a, k], device_id=sibling, device_id_type=MESH_IDS)
            mine, theirs = _flat(_flip(pos, k)), _flat(_flip(pos, k + 1))
            forwards.append(pltpu.make_async_remote_copy(src_ref=outs[a].at[mine], dst_ref=outs[a].at[mine], **sem))
            arrivals.append(pltpu.make_async_remote_copy(src_ref=outs[a].at[theirs], dst_ref=outs[a].at[theirs], **sem))
    return forwards, arrivals


def _exchange_start(ins, outs, slabs, send, recv, loc):
    local, sends, _ = _exchange_copies(ins, outs, slabs, send, recv, loc)
    for cp in local + sends:
        cp.start()


def _exchange_wait(ins, outs, slabs, send, recv, loc):
    local, sends, receives = _exchange_copies(ins, outs, slabs, send, recv, loc)
    forwards, arrivals = _gather_forwards(ins, outs, slabs, send, recv)
    for cp in receives:
        cp.wait_recv()
    for cp in forwards:
        cp.start()
    for cp in arrivals:
        cp.wait_recv()
    for cp in sends + forwards:
        cp.wait_send()
    for cp in local:
        cp.wait()


def _exchange_shapes(arrs, slabs):
    return tuple(_sds(a.shape if s else (N_DEV,) + a.shape, a.dtype) for a, s in zip(arrs, slabs))


def _exchange_sems(n):
    return [pltpu.SemaphoreType.DMA((n, N_DEV - 1)), pltpu.SemaphoreType.DMA((n, N_DEV - 1)),
            pltpu.SemaphoreType.DMA((n,))]


def _exchange(arrs, slabs, name):
    n = len(arrs)

    def body(*refs):
        ins, outs, sems = refs[:n], refs[n:2 * n], refs[2 * n:]
        _exchange_start(ins, outs, slabs, *sems)
        _exchange_wait(ins, outs, slabs, *sems)

    hbm = pl.BlockSpec(memory_space=pl.ANY)
    return pl.pallas_call(
        body, name=name, out_shape=_exchange_shapes(arrs, slabs), in_specs=[hbm] * n, out_specs=[hbm] * n,
        scratch_shapes=_exchange_sems(n),
    )(*arrs)


def _adam_math(w, g, m, v):
    m = ADAM_B1 * m + (1.0 - ADAM_B1) * g
    v = ADAM_B2 * v + (1.0 - ADAM_B2) * (g * g)
    m_hat = m / (1.0 - ADAM_B1 ** ADAM_STEP)
    v_hat = v / (1.0 - ADAM_B2 ** ADAM_STEP)
    delta = -ADAM_LR * (m_hat / (jnp.sqrt(v_hat) + ADAM_EPS) + ADAM_WD * w)
    return delta, m, v


def _adam(w, g, m, v, name):
    r, c = w.shape
    parts = g.ndim == 3
    tr = _tile(r, 512, 8)

    def body(w_ref, g_ref, m_ref, v_ref, go_ref, d_ref, mo_ref, vo_ref):
        if parts:
            gv = g_ref[0].astype(F32)
            for d in range(1, N_DEV):
                gv = gv + g_ref[d].astype(F32)
        else:
            gv = g_ref[...]
        delta, mn, vn = _adam_math(w_ref[...], gv, m_ref[...], v_ref[...])
        go_ref[...] = gv
        d_ref[...] = delta
        mo_ref[...] = mn
        vo_ref[...] = vn

    blk = pl.BlockSpec((tr, c), lambda i: (i, 0))
    gblk = pl.BlockSpec((N_DEV, tr, c), lambda i: (0, i, 0)) if parts else blk
    out = _sds((r, c), F32)
    return pl.pallas_call(
        body, name=name, out_shape=(out,) * 4, grid=(r // tr,),
        in_specs=[blk, gblk, blk, blk], out_specs=[blk] * 4,
        compiler_params=_params(("arbitrary",)),
    )(w, g, m, v)


def _sum_parts(parts, name):
    _, r, c = parts.shape

    def body(p_ref, o_ref):
        acc = p_ref[0]
        for d in range(1, N_DEV):
            acc = acc + p_ref[d]
        o_ref[...] = acc

    return pl.pallas_call(body, name=name, out_shape=_sds((r, c), F32), compiler_params=_params())(parts)


SMALL_SHARD_AXIS = {
    "meta": 1, "ln_g": 2, "ln_b": 2, "lru_conv_w": 2, "lru_conv_b": None, "lru_w_a": None, "lru_b_a": None,
    "lru_w_x": None, "lru_b_x": None, "lru_lambda": None, "ssd_conv_w": 2, "ssd_conv_b": None, "ssd_dt_bias": None,
    "ssd_a_log": None, "ssd_d": None, "ssd_norm_w": None, "dn_conv_w": 2, "dn_a_log": None, "dn_dt_bias": None,
    "dn_norm_w": None,
}
BIG = ("ffn_w_gate", "ffn_w_up", "ffn_w_down", "w_in", "w_out")
WEIGHT_ORDER = ("meta", "ln_g", "ln_b", "ffn_w_gate", "ffn_w_up", "ffn_w_down", "w_in", "lru_conv_w", "lru_conv_b",
                "lru_w_a", "lru_b_a", "lru_w_x", "lru_b_x", "lru_lambda", "ssd_conv_w", "ssd_conv_b", "ssd_dt_bias",
                "ssd_a_log", "ssd_d", "ssd_norm_w", "dn_conv_w", "dn_a_log", "dn_dt_bias", "dn_norm_w", "w_out")


def _pack(arrs):
    flat = jnp.concatenate([a.reshape(-1) for a in arrs])
    rows = -(-flat.shape[0] // (8 * LANE)) * 8
    return jnp.pad(flat, (0, rows * LANE - flat.shape[0])).reshape(rows, LANE)


def _unpack(buf, shapes, lead=()):
    flat = buf.reshape(lead + (-1,))
    out, off = [], 0
    for s in shapes:
        n = math.prod(s)
        out.append(flat[..., off:off + n].reshape(lead + tuple(s)))
        off += n
    return out


def _proj_cols(w):
    pad = jnp.zeros(w.shape[:-1] + (PROJ_W - D_IN,), w.dtype)
    return jnp.concatenate([w[..., 0:256], w[..., 1024:1792], w[..., 1800:2568], w[..., 256:512], w[..., 512:1024],
                            w[..., 2568:2824], w[..., 1792:1800], w[..., 2824:2832], pad], axis=-1)


def _proj_cols_inv(w):
    return jnp.concatenate([w[..., 0:256], w[..., 1792:2048], w[..., 2048:2560], w[..., 256:1024], w[..., 2816:2824],
                            w[..., 1024:1792], w[..., 2560:2816], w[..., 2824:2832]], axis=-1)


def kernel(x, meta, ln_g, ln_b, ffn_w_gate, ffn_w_up, ffn_w_down, w_in, lru_conv_w, lru_conv_b, lru_w_a, lru_b_a, lru_w_x, lru_b_x, lru_lambda, ssd_conv_w, ssd_conv_b, ssd_dt_bias, ssd_a_log, ssd_d, ssd_norm_w, dn_conv_w, dn_a_log, dn_dt_bias, dn_norm_w, w_out, loss_target, m_meta, m_ln_g, m_ln_b, m_ffn_w_gate, m_ffn_w_up, m_ffn_w_down, m_w_in, m_lru_conv_w, m_lru_conv_b, m_lru_w_a, m_lru_b_a, m_lru_w_x, m_lru_b_x, m_lru_lambda, m_ssd_conv_w, m_ssd_conv_b, m_ssd_dt_bias, m_ssd_a_log, m_ssd_d, m_ssd_norm_w, m_dn_conv_w, m_dn_a_log, m_dn_dt_bias, m_dn_norm_w, m_w_out, v_meta, v_ln_g, v_ln_b, v_ffn_w_gate, v_ffn_w_up, v_ffn_w_down, v_w_in, v_lru_conv_w, v_lru_conv_b, v_lru_w_a, v_lru_b_a, v_lru_w_x, v_lru_b_x, v_lru_lambda, v_ssd_conv_w, v_ssd_conv_b, v_ssd_dt_bias, v_ssd_a_log, v_ssd_d, v_ssd_norm_w, v_dn_conv_w, v_dn_a_log, v_dn_dt_bias, v_dn_norm_w, v_w_out):
    w = dict(meta=meta, ln_g=ln_g, ln_b=ln_b, ffn_w_gate=ffn_w_gate, ffn_w_up=ffn_w_up, ffn_w_down=ffn_w_down, w_in=w_in,
             lru_conv_w=lru_conv_w, lru_conv_b=lru_conv_b, lru_w_a=lru_w_a, lru_b_a=lru_b_a, lru_w_x=lru_w_x,
             lru_b_x=lru_b_x, lru_lambda=lru_lambda, ssd_conv_w=ssd_conv_w, ssd_conv_b=ssd_conv_b, ssd_dt_bias=ssd_dt_bias,
             ssd_a_log=ssd_a_log, ssd_d=ssd_d, ssd_norm_w=ssd_norm_w, dn_conv_w=dn_conv_w, dn_a_log=dn_a_log,
             dn_dt_bias=dn_dt_bias, dn_norm_w=dn_norm_w, w_out=w_out)
    m = dict(meta=m_meta, ln_g=m_ln_g, ln_b=m_ln_b, ffn_w_gate=m_ffn_w_gate, ffn_w_up=m_ffn_w_up, ffn_w_down=m_ffn_w_down,
             w_in=m_w_in, lru_conv_w=m_lru_conv_w, lru_conv_b=m_lru_conv_b, lru_w_a=m_lru_w_a, lru_b_a=m_lru_b_a,
             lru_w_x=m_lru_w_x, lru_b_x=m_lru_b_x, lru_lambda=m_lru_lambda, ssd_conv_w=m_ssd_conv_w, ssd_conv_b=m_ssd_conv_b,
             ssd_dt_bias=m_ssd_dt_bias, ssd_a_log=m_ssd_a_log, ssd_d=m_ssd_d, ssd_norm_w=m_ssd_norm_w, dn_conv_w=m_dn_conv_w,
             dn_a_log=m_dn_a_log, dn_dt_bias=m_dn_dt_bias, dn_norm_w=m_dn_norm_w, w_out=m_w_out)
    v = dict(meta=v_meta, ln_g=v_ln_g, ln_b=v_ln_b, ffn_w_gate=v_ffn_w_gate, ffn_w_up=v_ffn_w_up, ffn_w_down=v_ffn_w_down,
             w_in=v_w_in, lru_conv_w=v_lru_conv_w, lru_conv_b=v_lru_conv_b, lru_w_a=v_lru_w_a, lru_b_a=v_lru_b_a,
             lru_w_x=v_lru_w_x, lru_b_x=v_lru_b_x, lru_lambda=v_lru_lambda, ssd_conv_w=v_ssd_conv_w, ssd_conv_b=v_ssd_conv_b,
             ssd_dt_bias=v_ssd_dt_bias, ssd_a_log=v_ssd_a_log, ssd_d=v_ssd_d, ssd_norm_w=v_ssd_norm_w, dn_conv_w=v_dn_conv_w,
             dn_a_log=v_dn_a_log, dn_dt_bias=v_dn_dt_bias, dn_norm_w=v_dn_norm_w, w_out=v_w_out)
    depth = ln_g.shape[0]
    me = _flat(_mesh_pos())
    small_names = tuple(SMALL_SHARD_AXIS)
    sharded = tuple(k for k in small_names if SMALL_SHARD_AXIS[k] is not None)

    del _QUEUE[:]
    _STANDALONE[0] = 0
    shards = [_layer_shards(w, l) for l in range(depth)]
    small_unit = _enqueue(_pack([w[k] for k in sharded]), False)
    first = _enqueue_layer(shards[0])
    g_small = _collect(small_unit)
    small = {k: w[k] for k in small_names if SMALL_SHARD_AXIS[k] is None}
    for k, piece in zip(sharded, _unpack(g_small, [w[k].shape for k in sharded], lead=(N_DEV,))):
        ax = SMALL_SHARD_AXIS[k]
        full = jnp.moveaxis(piece, 0, ax)
        small[k] = full.reshape(full.shape[:ax] + (N_DEV * w[k].shape[ax],) + full.shape[ax + 2:])

    loss, dx, small_grads, slabs = _local_step(x[0], loss_target[0], small, first, shards)

    r_small = _collect(_enqueue(_pack([small_grads[k] for k in small_names]), False))
    got = [{k: _collect_halves(g) for k, g in layer.items()} for layer in slabs]

    outs = {}
    ffn = lambda name: jnp.stack([jnp.stack([got[l][f"{name}{j}"] for j in range(2)], axis=1) for l in range(depth)], axis=1)
    one = lambda name: jnp.stack([got[l][name] for l in range(depth)], axis=1)
    parts_of = {"ffn_w_gate": ffn("gate"), "ffn_w_up": ffn("up"), "ffn_w_down": ffn("down"), "w_in": one("win"),
                "w_out": one("wout")}
    for k in BIG:
        shp = w[k].shape
        two = lambda a: a.reshape(-1, shp[-1])
        res = _adam(two(w[k]), parts_of[k].reshape(N_DEV, -1, shp[-1]), two(m[k]), two(v[k]), "adam_" + k)
        outs[k] = [a.reshape(shp) for a in res]
    g_full = _unpack(_sum_parts(r_small, "sum_small_grads"), [small[k].shape for k in small_names])
    g_loc = {}
    for k, gf in zip(small_names, g_full):
        ax = SMALL_SHARD_AXIS[k]
        g_loc[k] = gf if ax is None else lax.dynamic_slice_in_dim(gf, me * w[k].shape[ax], w[k].shape[ax], axis=ax)
    res = _adam(_pack([w[k] for k in small_names]), _pack([g_loc[k] for k in small_names]),
                _pack([m[k] for k in small_names]), _pack([v[k] for k in small_names]), "adam_small")
    shapes = [w[k].shape for k in small_names]
    for i, k in enumerate(small_names):
        outs[k] = [_unpack(r, shapes)[i] for r in res]

    loss = lax.psum(loss, ("x", "y", "c"))
    return (loss, dx[None], *[outs[k][0] for k in WEIGHT_ORDER], *[outs[k][1] for k in WEIGHT_ORDER],
            *[outs[k][2] for k in WEIGHT_ORDER], *[outs[k][3] for k in WEIGHT_ORDER])
```

```python
import functools
import math

import jax
import jax.numpy as jnp
from jax import lax
from jax.experimental import pallas as pl
from jax.experimental.pallas import tpu as pltpu

F32 = jnp.float32
BF16 = jnp.bfloat16
HI = lax.Precision.HIGHEST
MESH_IDS = pl.DeviceIdType.MESH

N_DEV = 8
D_MODEL = 1024
DEPTH = 4
N_META = 16
CHUNK = 64
CONV_K = 4
D_FF = 2816
LRU_WIDTH = 256
LRU_HEADS = 4
LRU_C = 8.0
SSD_HEADS = 8
SSD_GROUPS = 2
SSD_HPG = 4
SSD_INNER = 512
DN_HEADS = 4
HEAD = 64
CONV_W = 1792
PROJ_W = 2944
OFF_LRU_Y, OFF_SSD_Z, OFF_DN_GATE, OFF_SMALL = 1792, 2048, 2560, 2816
D_IN = 2832
ALPHA = (2 * DEPTH) ** 0.25
FFN_RES = 0.5
LN_EPS = 1e-5
RMS_EPS = 1e-6
ADAM_LR, ADAM_B1, ADAM_B2, ADAM_EPS, ADAM_WD, ADAM_STEP = 0.001, 0.9, 0.999, 1e-08, 0.01, 10

VMEM_LIMIT = 56 * 1024 * 1024
ROW_TILE_CAP = 832
LANE = 128


def _tile(n, cap, mult=16):
    best = None
    for t in range(mult, min(n, cap) + 1, mult):
        if n % t == 0:
            best = t
    assert best is not None, (n, cap, mult)
    return best


def _params(sem=None):
    return pltpu.CompilerParams(dimension_semantics=sem, vmem_limit_bytes=VMEM_LIMIT)


def _resident(shape, index_map):
    return pl.BlockSpec(shape, index_map, pipeline_mode=pl.Buffered(1))


def _dot(a, b):
    return jnp.dot(a, b, preferred_element_type=F32)


def _dot_nt(a, b):
    return lax.dot_general(a, b, (((1,), (1,)), ((), ())), preferred_element_type=F32)


def _dot_tn(a, b):
    return lax.dot_general(a, b, (((0,), (0,)), ((), ())), preferred_element_type=F32)


def _sds(shape, dtype):
    return jax.ShapeDtypeStruct(shape, dtype)


US_PER_MB = 92.5


class _Unit:
    def __init__(self, arr, slab):
        self.arr, self.slab, self.out = arr, slab, None
        per_peer = arr.size * arr.dtype.itemsize / (N_DEV if slab else 1)
        self.us = per_peer / 1e6 * US_PER_MB * (1.0 if slab else 0.5)


_QUEUE = []
_STANDALONE = [0]


def _enqueue(arr, slab):
    unit = _Unit(arr, slab)
    _QUEUE.append(unit)
    return unit


def _enqueue_halves(arr, slab, axis):
    half = arr.shape[axis] // 2
    parts = (lax.slice_in_dim(arr, 0, half, axis=axis), lax.slice_in_dim(arr, half, arr.shape[axis], axis=axis))
    return [_enqueue(p, slab) for p in parts], axis + (0 if slab else 1)


def _collect_halves(group):
    units, axis = group
    return jnp.concatenate([_collect(u) for u in units], axis=axis)


def _take_units(host_us):
    units = []
    while _QUEUE and host_us >= 0.5 * _QUEUE[0].us:
        host_us -= _QUEUE[0].us
        units.append(_QUEUE.pop(0))
    return units


def _collect(unit):
    if unit.out is None:
        n = _QUEUE.index(unit) + 1
        units = [_QUEUE.pop(0) for _ in range(n)]
        _STANDALONE[0] += 1
        res = _exchange([u.arr for u in units], [u.slab for u in units], f"exchange_{_STANDALONE[0]}")
        for u, r in zip(units, res):
            u.out = r
    return unit.out


def _call(body, host_us, *, name, out_shape, in_specs, out_specs, grid=(), scratch_shapes=(), compiler_params=None):
    units = _take_units(0.5 * host_us)
    kw = dict(name=name, grid=grid, compiler_params=compiler_params)
    if not units:
        return pl.pallas_call(body, out_shape=out_shape, in_specs=in_specs, out_specs=out_specs,
                              scratch_shapes=list(scratch_shapes), **kw)
    single = not isinstance(out_shape, (tuple, list))
    outs = (out_shape,) if single else tuple(out_shape)
    ospecs = [out_specs] if single else list(out_specs)
    nin, nout, nscr, ncm = len(in_specs), len(outs), len(scratch_shapes), len(units)
    slabs = [u.slab for u in units]

    def hosted(*refs):
        ins, c_in = refs[:nin], refs[nin:nin + ncm]
        o, c_out = refs[nin + ncm:nin + ncm + nout], refs[nin + ncm + nout:nin + 2 * ncm + nout]
        scr, sems = refs[nin + 2 * ncm + nout:nin + 2 * ncm + nout + nscr], refs[nin + 2 * ncm + nout + nscr:]
        ids = [pl.program_id(d) for d in range(len(grid))]
        first = functools.reduce(jnp.logical_and, [i == 0 for i in ids])
        last = functools.reduce(jnp.logical_and, [i == g - 1 for i, g in zip(ids, grid)])

        @pl.when(first)
        def _():
            _exchange_start(c_in, c_out, slabs, *sems)
        body(*ins, *o, *scr)

        @pl.when(last)
        def _():
            _exchange_wait(c_in, c_out, slabs, *sems)

    hbm = pl.BlockSpec(memory_space=pl.ANY)
    fn = pl.pallas_call(
        hosted, out_shape=outs + _exchange_shapes([u.arr for u in units], slabs), in_specs=list(in_specs) + [hbm] * ncm,
        out_specs=ospecs + [hbm] * ncm, scratch_shapes=list(scratch_shapes) + _exchange_sems(ncm), **kw)

    def run(*args):
        res = fn(*args, *[u.arr for u in units])
        for u, r in zip(units, res[nout:]):
            u.out = r
        return res[0] if single else tuple(res[:nout])

    return run


def _ffn_up(xb, wg, wu, name):
    tp, d = xb.shape
    f = wg.shape[1]
    tm = _tile(tp, ROW_TILE_CAP)
    tn = _tile(f, 1408, LANE)
    nj = f // tn

    def body(x_ref, wg_ref, wu_ref, g_ref, u_ref, a_ref):
        x = x_ref[...]
        g = _dot(x, wg_ref[...])
        u = _dot(x, wu_ref[...])
        g_ref[...] = g.astype(BF16)
        u_ref[...] = u.astype(BF16)
        a_ref[...] = (g * jax.nn.sigmoid(g) * u).astype(BF16)

    out = _sds((tp, f), BF16)
    return _call(
        body, 60, name=name, out_shape=(out, out, out), grid=(nj, tp // tm),
        in_specs=[pl.BlockSpec((tm, d), lambda j, i: (i, 0)),
                  pl.BlockSpec((d, tn), lambda j, i: (0, j)),
                  pl.BlockSpec((d, tn), lambda j, i: (0, j))],
        out_specs=[pl.BlockSpec((tm, tn), lambda j, i: (i, j))] * 3,
        compiler_params=_params(("arbitrary", "arbitrary")),
    )(xb, wg, wu)


def _mm_resid_ln(a, w, h, gamma, beta, scale, name):
    tp, k = a.shape
    d = w.shape[1]
    tm = _tile(tp, ROW_TILE_CAP)

    def body(a_ref, w_ref, h_ref, g_ref, b_ref, y_ref, o_ref, ob_ref):
        y = ALPHA * h_ref[...] + scale * _dot(a_ref[...], w_ref[...])
        mu = jnp.mean(y, axis=-1, keepdims=True)
        yc = y - mu
        var = jnp.mean(yc * yc, axis=-1, keepdims=True)
        o = yc * lax.rsqrt(var + LN_EPS) * g_ref[...] + b_ref[...]
        y_ref[...] = y
        o_ref[...] = o
        ob_ref[...] = o.astype(BF16)

    row = lambda i: (i, 0)
    fix = lambda i: (0, 0)
    return _call(
        body, 30, name=name, out_shape=(_sds((tp, d), F32), _sds((tp, d), F32), _sds((tp, d), BF16)),
        grid=(tp // tm,),
        in_specs=[pl.BlockSpec((tm, k), row), _resident((k, d), fix), pl.BlockSpec((tm, d), row),
                  pl.BlockSpec((1, d), fix), pl.BlockSpec((1, d), fix)],
        out_specs=[pl.BlockSpec((tm, d), row)] * 3,
        compiler_params=_params(("arbitrary",)),
    )(a, w, h, gamma, beta)


def _mm_nn(xb, w, name):
    tp, k = xb.shape
    n = w.shape[1]
    tm = _tile(tp, ROW_TILE_CAP)

    def body(x_ref, w_ref, o_ref):
        o_ref[...] = _dot(x_ref[...], w_ref[...])

    return _call(
        body, 30, name=name, out_shape=_sds((tp, n), F32), grid=(tp // tm,),
        in_specs=[pl.BlockSpec((tm, k), lambda i: (i, 0)), _resident((k, n), lambda i: (0, 0))],
        out_specs=pl.BlockSpec((tm, n), lambda i: (i, 0)),
        compiler_params=_params(("arbitrary",)),
    )(xb, w)


def _mm_nt_add(a, w, resid, name):
    tp, k = a.shape
    n = w.shape[0]
    tm = _tile(tp, ROW_TILE_CAP)
    has_resid = resid is not None

    def body(*refs):
        if has_resid:
            a_ref, w_ref, r_ref, o_ref = refs
            o_ref[...] = r_ref[...] + _dot_nt(a_ref[...], w_ref[...])
        else:
            a_ref, w_ref, o_ref = refs
            o_ref[...] = _dot_nt(a_ref[...], w_ref[...])

    in_specs = [pl.BlockSpec((tm, k), lambda i: (i, 0)), _resident((n, k), lambda i: (0, 0))]
    args = [a, w]
    if has_resid:
        in_specs.append(pl.BlockSpec((tm, n), lambda i: (i, 0)))
        args.append(resid)
    return _call(
        body, 25, name=name, out_shape=_sds((tp, n), F32), grid=(tp // tm,),
        in_specs=in_specs, out_specs=pl.BlockSpec((tm, n), lambda i: (i, 0)),
        compiler_params=_params(("arbitrary",)),
    )(*args)


def _ffn_dx(dg, du, wg, wu, resid, name):
    tp, f = dg.shape
    d = wg.shape[0]
    tm = _tile(tp, ROW_TILE_CAP)

    def body(dg_ref, du_ref, wg_ref, wu_ref, r_ref, o_ref):
        acc = r_ref[...] + _dot_nt(dg_ref[...], wg_ref[...])
        o_ref[...] = acc + _dot_nt(du_ref[...], wu_ref[...])

    row = lambda i: (i, 0)
    fix = lambda i: (0, 0)
    return _call(
        body, 55, name=name, out_shape=_sds((tp, d), F32), grid=(tp // tm,),
        in_specs=[pl.BlockSpec((tm, f), row), pl.BlockSpec((tm, f), row), _resident((d, f), fix), _resident((d, f), fix),
                  pl.BlockSpec((tm, d), row)],
        out_specs=pl.BlockSpec((tm, d), row),
        compiler_params=_params(("arbitrary",)),
    )(dg, du, wg, wu, resid)


def _mm_tn(a, b, name):
    tp, ka = a.shape
    nb = b.shape[1]
    tt = _tile(tp, ROW_TILE_CAP)
    tk = _tile(ka, 1408, LANE)
    tn = _tile(nb, 1536, LANE)
    if tn < 512:
        tn = nb
    nt = tp // tt

    def body(a_ref, b_ref, o_ref):
        @pl.when(pl.program_id(2) == 0)
        def _():
            o_ref[...] = jnp.zeros_like(o_ref)
        o_ref[...] += _dot_tn(a_ref[...], b_ref[...])

    return _call(
        body, 40, name=name, out_shape=_sds((ka, nb), F32), grid=(ka // tk, nb // tn, nt),
        in_specs=[pl.BlockSpec((tt, tk), lambda i, j, t: (t, i)), pl.BlockSpec((tt, tn), lambda i, j, t: (t, j))],
        out_specs=pl.BlockSpec((tk, tn), lambda i, j, t: (i, j)),
        compiler_params=_params(("arbitrary", "arbitrary", "arbitrary")),
    )(a, b)


def _ln_bwd(y, dout, gamma, scale, name):
    tp, d = y.shape
    tm = _tile(tp, ROW_TILE_CAP)

    def body(y_ref, do_ref, g_ref, dres_ref, dyb_ref, dg_ref, db_ref):
        @pl.when(pl.program_id(0) == 0)
        def _():
            dg_ref[...] = jnp.zeros_like(dg_ref)
            db_ref[...] = jnp.zeros_like(db_ref)
        yv = y_ref[...]
        do = do_ref[...]
        mu = jnp.mean(yv, axis=-1, keepdims=True)
        yc = yv - mu
        var = jnp.mean(yc * yc, axis=-1, keepdims=True)
        rstd = lax.rsqrt(var + LN_EPS)
        xhat = yc * rstd
        dxh = do * g_ref[...]
        m1 = jnp.mean(dxh, axis=-1, keepdims=True)
        m2 = jnp.mean(dxh * xhat, axis=-1, keepdims=True)
        dy = rstd * (dxh - m1 - xhat * m2)
        dres_ref[...] = ALPHA * dy
        dyb_ref[...] = (scale * dy).astype(BF16)
        dg_ref[...] += jnp.sum(do * xhat, axis=0, keepdims=True)
        db_ref[...] += jnp.sum(do, axis=0, keepdims=True)

    row = lambda i: (i, 0)
    fix = lambda i: (0, 0)
    return _call(
        body, 22, name=name,
        out_shape=(_sds((tp, d), F32), _sds((tp, d), BF16), _sds((1, d), F32), _sds((1, d), F32)),
        grid=(tp // tm,),
        in_specs=[pl.BlockSpec((tm, d), row), pl.BlockSpec((tm, d), row), pl.BlockSpec((1, d), fix)],
        out_specs=[pl.BlockSpec((tm, d), row), pl.BlockSpec((tm, d), row), pl.BlockSpec((1, d), fix),
                   pl.BlockSpec((1, d), fix)],
        compiler_params=_params(("arbitrary",)),
    )(y, dout, gamma)


def _ffn_dact(dyb, wd, g, u, name):
    tp, d = dyb.shape
    f = wd.shape[0]
    tm = _tile(tp, ROW_TILE_CAP)
    tn = _tile(f, 1408, LANE)

    def body(dy_ref, w_ref, g_ref, u_ref, dg_ref, du_ref):
        dact = _dot_nt(dy_ref[...], w_ref[...])
        gv = g_ref[...].astype(F32)
        uv = u_ref[...].astype(F32)
        sg = jax.nn.sigmoid(gv)
        dg_ref[...] = (dact * uv * (sg * (1.0 + gv * (1.0 - sg)))).astype(BF16)
        du_ref[...] = (dact * (gv * sg)).astype(BF16)

    out = _sds((tp, f), BF16)
    blk = pl.BlockSpec((tm, tn), lambda j, i: (i, j))
    return _call(
        body, 55, name=name, out_shape=(out, out), grid=(f // tn, tp // tm),
        in_specs=[pl.BlockSpec((tm, d), lambda j, i: (i, 0)), pl.BlockSpec((tn, d), lambda j, i: (j, 0)), blk, blk],
        out_specs=[blk, blk],
        compiler_params=_params(("arbitrary", "arbitrary")),
    )(dyb, wd, g, u)


def _loss_grad(o, tgt, t_real, name):
    tp, d = o.shape
    tm = _tile(tp, ROW_TILE_CAP)

    def body(o_ref, t_ref, l_ref, d_ref):
        i = pl.program_id(0)

        @pl.when(i == 0)
        def _():
            l_ref[...] = jnp.zeros_like(l_ref)
        rows = i * tm + lax.broadcasted_iota(jnp.int32, (tm, 1), 0)
        real = jnp.logical_and(rows >= N_META, rows < t_real)
        err = jnp.where(real, o_ref[...] - t_ref[...], 0.0)
        d_ref[...] = err * (1.0 / d)
        l_ref[...] += jnp.sum(err * err, axis=0, keepdims=True) * (0.5 / d)

    row = lambda i: (i, 0)
    return pl.pallas_call(
        body, name=name, out_shape=(_sds((1, d), F32), _sds((tp, d), F32)), grid=(tp // tm,),
        in_specs=[pl.BlockSpec((tm, d), row), pl.BlockSpec((tm, d), row)],
        out_specs=[pl.BlockSpec((1, d), lambda i: (0, 0)), pl.BlockSpec((tm, d), row)],
        compiler_params=_params(("arbitrary",)),
    )(o, tgt)


CONV_TC = 256


def _silu_grad(y):
    s = jax.nn.sigmoid(y)
    return s * (1.0 + y * (1.0 - s))


def _conv_taps(x_ref, w, r0, rb):
    cur = x_ref[r0:r0 + rb, :]
    prev = x_ref[r0 - 8:r0, :] if r0 > 0 else jnp.zeros((8, cur.shape[1]), F32)
    xcat = jnp.concatenate([prev, cur], axis=0)
    taps = [xcat[5 + j:5 + j + rb] for j in range(CONV_K - 1)] + [cur]
    y = w[0:1] * taps[0]
    for j in range(1, CONV_K):
        y = y + w[j:j + 1] * taps[j]
    return y, taps


def _conv_fwd(proj, w, b, name):
    tp = proj.shape[0]
    rb = _tile(tp, ROW_TILE_CAP, 8)

    def body(x_ref, w_ref, b_ref, o_ref):
        gated = pl.program_id(0) > 0
        wv = w_ref[...]
        bv = b_ref[...]
        for r0 in range(0, tp, rb):
            y, _ = _conv_taps(x_ref, wv, r0, rb)
            y = y + bv
            o_ref[r0:r0 + rb, :] = jnp.where(gated, y * jax.nn.sigmoid(y), y)

    col = lambda j: (0, j)
    return _call(
        body, 25, name=name, out_shape=_sds((tp, CONV_W), F32), grid=(CONV_W // CONV_TC,),
        in_specs=[pl.BlockSpec((tp, CONV_TC), col), pl.BlockSpec((CONV_K, CONV_TC), col), pl.BlockSpec((1, CONV_TC), col)],
        out_specs=pl.BlockSpec((tp, CONV_TC), col),
        compiler_params=_params(("arbitrary",)),
    )(proj, w, b)


def _conv_bwd(proj, w, b, dxc, name):
    tp = proj.shape[0]
    rb = _tile(tp, ROW_TILE_CAP, 8)

    def body(x_ref, w_ref, b_ref, d_ref, dx_ref, dw_ref, db_ref, dy_scr):
        gated = pl.program_id(0) > 0
        wv = w_ref[...]
        bv = b_ref[...]
        dw = [jnp.zeros((1, CONV_TC), F32) for _ in range(CONV_K)]
        db = jnp.zeros((1, CONV_TC), F32)
        for r0 in range(0, tp, rb):
            y, taps = _conv_taps(x_ref, wv, r0, rb)
            y = y + bv
            d = d_ref[r0:r0 + rb, :]
            dy = jnp.where(gated, d * _silu_grad(y), d)
            dy_scr[r0:r0 + rb, :] = dy
            for j in range(CONV_K):
                dw[j] = dw[j] + jnp.sum(dy * taps[j], axis=0, keepdims=True)
            db = db + jnp.sum(dy, axis=0, keepdims=True)
        for j in range(CONV_K):
            dw_ref[j:j + 1, :] = dw[j]
        db_ref[...] = db
        for r0 in range(0, tp, rb):
            cur = dy_scr[r0:r0 + rb, :]
            nxt = dy_scr[r0 + rb:r0 + rb + 8, :] if r0 + rb < tp else jnp.zeros((8, CONV_TC), F32)
            dcat = jnp.concatenate([cur, nxt], axis=0)
            dx = wv[3:4] * cur
            for s in range(1, CONV_K):
                dx = dx + wv[3 - s:4 - s] * dcat[s:s + rb]
            dx_ref[r0:r0 + rb, :] = dx.astype(BF16)

    col = lambda j: (0, j)
    return _call(
        body, 70, name=name,
        out_shape=(_sds((tp, CONV_W), BF16), _sds((CONV_K, CONV_W), F32), _sds((1, CONV_W), F32)),
        grid=(CONV_W // CONV_TC,),
        in_specs=[pl.BlockSpec((tp, CONV_TC), col), pl.BlockSpec((CONV_K, CONV_TC), col), pl.BlockSpec((1, CONV_TC), col),
                  pl.BlockSpec((tp, CONV_TC), col)],
        out_specs=[pl.BlockSpec((tp, CONV_TC), col), pl.BlockSpec((CONV_K, CONV_TC), col), pl.BlockSpec((1, CONV_TC), col)],
        scratch_shapes=[pltpu.VMEM((tp, CONV_TC), F32)],
        compiler_params=_params(("arbitrary",)),
    )(proj, w, b, dxc)


def _neg_expm1(x):
    series = -x * (1.0 + x * (0.5 + x * (1.0 / 6.0 + x * (1.0 / 24.0))))
    return jnp.where(jnp.abs(x) < 0.03, series, 1.0 - jnp.exp(x))


def _lru_gates(u, wa, ba, wx, bx, lam):
    r = jax.nn.sigmoid(jnp.dot(u, wa, precision=HI, preferred_element_type=F32) + ba)
    i = jax.nn.sigmoid(jnp.dot(u, wx, precision=HI, preferred_element_type=F32) + bx)
    log_a = -LRU_C * r * jax.nn.softplus(-lam)
    a = jnp.exp(log_a)
    b = jnp.sqrt(_neg_expm1(2.0 * log_a)) * (i * u)
    return a, b


def _lru_specs(tm, flip, n_tiles):
    idx = (lambda i: (n_tiles - 1 - i, 0)) if flip else (lambda i: (i, 0))
    return idx, lambda i: (0, 0)


def _lru_fwd(xc, proj, wa, ba, wx, bx, lam, name):
    tp = xc.shape[0]
    w = LRU_WIDTH
    tm = _tile(tp, ROW_TILE_CAP, 8)
    ycol = OFF_LRU_Y // w

    def body(u_ref, y_ref, wa_ref, ba_ref, wx_ref, bx_ref, lam_ref, o_ref, h_ref, a_scr, b_scr, carry):
        @pl.when(pl.program_id(0) == 0)
        def _():
            carry[...] = jnp.zeros_like(carry)
        a, b = _lru_gates(u_ref[...], wa_ref[...], ba_ref[...], wx_ref[...], bx_ref[...], lam_ref[...])
        a_scr[...] = a
        b_scr[...] = b

        def step(t, h):
            h = a_scr[pl.ds(t, 1), :] * h + b_scr[pl.ds(t, 1), :]
            h_ref[pl.ds(t, 1), :] = h
            return h

        carry[0:1, :] = lax.fori_loop(0, tm, step, carry[0:1, :])
        o_ref[...] = (h_ref[...] * jax.nn.gelu(y_ref[...])).astype(BF16)

    fix = lambda i: (0, 0)
    return _call(
        body, 40, name=name, out_shape=(_sds((tp, w), BF16), _sds((tp, w), F32)), grid=(tp // tm,),
        in_specs=[pl.BlockSpec((tm, w), lambda i: (i, 0)), pl.BlockSpec((tm, w), lambda i: (i, ycol)),
                  pl.BlockSpec((w, w), fix), pl.BlockSpec((1, w), fix), pl.BlockSpec((w, w), fix),
                  pl.BlockSpec((1, w), fix), pl.BlockSpec((1, w), fix)],
        out_specs=[pl.BlockSpec((tm, w), lambda i: (i, 0))] * 2,
        scratch_shapes=[pltpu.VMEM((tm, w), F32), pltpu.VMEM((tm, w), F32), pltpu.VMEM((8, w), F32)],
        compiler_params=_params(("arbitrary",)),
    )(xc, proj, wa, ba, wx, bx, lam)


def _lru_bwd(xc, proj, h, dout, wa, ba, wx, bx, lam, name):
    tp = xc.shape[0]
    w = LRU_WIDTH
    tm = _tile(tp, ROW_TILE_CAP, 8)
    nt = tp // tm
    ycol = OFF_LRU_Y // w
    rev = lambda i: (nt - 1 - i, 0)
    prev8 = lambda i: (jnp.maximum((nt - 1 - i) * (tm // 8) - 1, 0), 0)

    def body(u_ref, y_ref, h_ref, hp_ref, do_ref, wa_ref, ba_ref, wx_ref, bx_ref, lam_ref,
             du_ref, dy_ref, dwa_ref, dba_ref, dwx_ref, dbx_ref, dlam_ref,
             a_scr, dh_scr, g_scr, da_scr, hext, carry):
        i = pl.program_id(0)

        @pl.when(i == 0)
        def _():
            carry[...] = jnp.zeros_like(carry)
            for r in (dwa_ref, dba_ref, dwx_ref, dbx_ref, dlam_ref):
                r[...] = jnp.zeros_like(r)
        params = (wa_ref[...], ba_ref[...], wx_ref[...], bx_ref[...], lam_ref[...])
        (a, _), gates_vjp = jax.vjp(_lru_gates, u_ref[...], *params)
        gel, gelu_vjp = jax.vjp(jax.nn.gelu, y_ref[...])
        do = do_ref[...]
        hv = h_ref[...]
        dy_ref[...] = gelu_vjp(do * hv)[0].astype(BF16)
        a_scr[...] = a
        dh_scr[...] = do * gel
        hext[0:8, :] = jnp.where(i == nt - 1, 0.0, hp_ref[...])
        hext[8:8 + tm, :] = hv

        def step(s, c):
            t = tm - 1 - s
            g = dh_scr[pl.ds(t, 1), :] + c
            g_scr[pl.ds(t, 1), :] = g
            da_scr[pl.ds(t, 1), :] = g * hext[pl.ds(t + 7, 1), :]
            return a_scr[pl.ds(t, 1), :] * g

        carry[0:1, :] = lax.fori_loop(0, tm, step, carry[0:1, :])
        du, dwa, dba, dwx, dbx, dlam = gates_vjp((da_scr[...], g_scr[...]))
        du_ref[...] = du
        dwa_ref[...] += dwa
        dba_ref[...] += dba
        dwx_ref[...] += dwx
        dbx_ref[...] += dbx
        dlam_ref[...] += dlam

    fix = lambda i: (0, 0)
    tile = pl.BlockSpec((tm, w), rev)
    mat = pl.BlockSpec((w, w), fix)
    vec = pl.BlockSpec((1, w), fix)
    return _call(
        body, 90, name=name,
        out_shape=(_sds((tp, w), F32), _sds((tp, w), BF16), _sds((w, w), F32), _sds((1, w), F32), _sds((w, w), F32),
                   _sds((1, w), F32), _sds((1, w), F32)),
        grid=(nt,),
        in_specs=[tile, pl.BlockSpec((tm, w), lambda i: (nt - 1 - i, ycol)), tile, pl.BlockSpec((8, w), prev8), tile,
                  mat, vec, mat, vec, vec],
        out_specs=[tile, tile, mat, vec, mat, vec, vec],
        scratch_shapes=[pltpu.VMEM((tm, w), F32)] * 4 + [pltpu.VMEM((tm + 8, w), F32), pltpu.VMEM((8, w), F32)],
        compiler_params=_params(("arbitrary",)),
    )(xc, proj, h, h, dout, wa, ba, wx, bx, lam)


def _dot3(a, b, dims):
    ah = a.astype(BF16)
    al = (a - ah.astype(F32)).astype(BF16)
    bh = b.astype(BF16)
    bl = (b - bh.astype(F32)).astype(BF16)
    dot = lambda x, y: lax.dot_general(x, y, (dims, ((0,), (0,))), preferred_element_type=F32)
    return dot(ah, bh) + (dot(ah, bl) + dot(al, bh))


@jax.custom_vjp
def _bmm(a, b):
    return _dot3(a, b, ((2,), (1,)))


@jax.custom_vjp
def _bmm_nt(a, b):
    return _dot3(a, b, ((2,), (2,)))


@jax.custom_vjp
def _bmm_tn(a, b):
    return _dot3(a, b, ((1,), (1,)))


_bmm.defvjp(lambda a, b: (_bmm(a, b), (a, b)), lambda r, g: (_bmm_nt(g, r[1]), _bmm_tn(r[0], g)))
_bmm_nt.defvjp(lambda a, b: (_bmm_nt(a, b), (a, b)), lambda r, g: (_bmm(g, r[1]), _bmm_tn(g, r[0])))
_bmm_tn.defvjp(lambda a, b: (_bmm_tn(a, b), (a, b)), lambda r, g: (_bmm_nt(r[1], g), _bmm(r[0], g)))


def _dot1(a, b, dims):
    return lax.dot_general(a.astype(BF16), b.astype(BF16), (dims, ((0,), (0,))), preferred_element_type=F32)


@jax.custom_vjp
def _bmm1(a, b):
    return _dot1(a, b, ((2,), (1,)))


@jax.custom_vjp
def _bmm1_nt(a, b):
    return _dot1(a, b, ((2,), (2,)))


@jax.custom_vjp
def _bmm1_tn(a, b):
    return _dot1(a, b, ((1,), (1,)))


_bmm1.defvjp(lambda a, b: (_bmm1(a, b), (a, b)), lambda r, g: (_bmm1_nt(g, r[1]), _bmm1_tn(r[0], g)))
_bmm1_nt.defvjp(lambda a, b: (_bmm1_nt(a, b), (a, b)), lambda r, g: (_bmm1(g, r[1]), _bmm1_tn(g, r[0])))
_bmm1_tn.defvjp(lambda a, b: (_bmm1_tn(a, b), (a, b)), lambda r, g: (_bmm1_nt(r[1], g), _bmm1(r[0], g)))


def _chunk_masks(nh):
    r = lax.broadcasted_iota(jnp.int32, (CHUNK, CHUNK), 0)
    c = lax.broadcasted_iota(jnp.int32, (CHUNK, CHUNK), 1)
    full = lambda m: jnp.broadcast_to(m[None], (nh, CHUNK, CHUNK))
    return r, c, full


def _decay_terms(g, nh):
    r, c, full = _chunk_masks(nh)
    incl = r >= c
    cs = _bmm(full(incl.astype(F32)), g)
    cs_t = jnp.swapaxes(cs, 1, 2)
    tot = jnp.broadcast_to(jnp.sum(g, axis=1, keepdims=True), g.shape)
    m = full(incl)
    decay = jnp.where(m, jnp.exp(jnp.where(m, cs - cs_t, 0.0)), 0.0)
    return cs, decay, tot


def _rep_groups(x):
    return jnp.concatenate([jnp.broadcast_to(x[g:g + 1], (SSD_HPG,) + x.shape[1:]) for g in range(SSD_GROUPS)], axis=0)


def _ssd_chunk(xs, bm, cm, dtr, z, p_dtb, p_alog, p_d, p_nw, state, saved=None):
    nh = SSD_HEADS
    dt = jax.nn.softplus(dtr + p_dtb)
    a = dt * (-jnp.exp(p_alog))
    x = xs * dt
    cs, decay, tot = _decay_terms(a, nh)
    b8 = _rep_groups(bm)
    c8 = _rep_groups(cm)
    y = _bmm1(_rep_groups(_bmm1_nt(cm, bm)) * decay, x)
    y = y + _bmm1_nt(c8, state) * jnp.exp(cs)
    new_state = state * jnp.exp(tot) + _bmm1_tn(x * jnp.exp(tot - cs), b8)
    y = y + p_d * xs
    y = y * (z * jax.nn.sigmoid(z))
    ss = jnp.sum(y * y, axis=-1, keepdims=True)
    ssg = jnp.concatenate(
        [jnp.broadcast_to(jnp.sum(ss[g * SSD_HPG:(g + 1) * SSD_HPG], axis=0, keepdims=True), (SSD_HPG, CHUNK, 1))
         for g in range(SSD_GROUPS)], axis=0)
    y = y * lax.rsqrt(ssg * (1.0 / (SSD_HPG * HEAD)) + RMS_EPS) * p_nw
    return y, new_state, None


@jax.custom_vjp
def _unit_lower_inverse(m):
    r, c, full = _chunk_masks(m.shape[0])
    eye = full((r == c).astype(F32))
    md = jnp.where(full((r // 16) == (c // 16)), m, 0.0)
    mo = m - md
    x = eye - md
    p = _bmm(md, md)
    x = x + _bmm(x, p)
    p = _bmm(p, p)
    x = x + _bmm(x, p)
    p = _bmm(p, p)
    x = x + _bmm(x, p)
    n = _bmm(x, mo)
    y = x - _bmm(n, x)
    return y + _bmm(_bmm(n, n), y)


def _unit_lower_inverse_fwd(m):
    t = _unit_lower_inverse(m)
    return t, t


_unit_lower_inverse.defvjp(_unit_lower_inverse_fwd, lambda t, g: (-_bmm_nt(_bmm_tn(t, g), t),))


@jax.custom_vjp
def _saved_inverse(m, t):
    return t


_saved_inverse.defvjp(lambda m, t: (t, t), lambda t, g: (-_bmm_nt(_bmm_tn(t, g), t), jnp.zeros_like(t)))


def _dn_chunk(q, k, v, gate, braw, araw, p_alog, p_dtb, p_nw, state, saved=None):
    nh = DN_HEADS
    r, c, full = _chunk_masks(nh)
    q = q * lax.rsqrt(jnp.sum(q * q, axis=-1, keepdims=True) + RMS_EPS) * (HEAD ** -0.5)
    k = k * lax.rsqrt(jnp.sum(k * k, axis=-1, keepdims=True) + RMS_EPS)
    beta = jax.nn.sigmoid(braw)
    g = -jnp.exp(p_alog) * jax.nn.softplus(araw + p_dtb)
    gcs, decay, tot = _decay_terms(g, nh)
    kb = k * beta
    vb = v * beta
    m = jnp.where(full(r > c), _bmm1_nt(kb, k) * decay, 0.0)
    t = _unit_lower_inverse(m) if saved is None else _saved_inverse(m, saved)
    egcs = jnp.exp(gcs)
    u = _bmm(t, vb)
    w = _bmm(t, kb * egcs)
    attn = _bmm1_nt(q, k) * decay
    v_new = u - _bmm1(w, state)
    out = _bmm1(q * egcs, state) + _bmm1(attn, v_new)
    new_state = state * jnp.exp(tot) + _bmm1_tn(k * jnp.exp(tot - gcs), v_new)
    out = out * lax.rsqrt(jnp.mean(out * out, axis=-1, keepdims=True) + RMS_EPS) * p_nw
    return out * (gate * jax.nn.sigmoid(gate)), new_state, t


def _chunks_per_step(nc):
    return max(n for n in range(1, 6) if nc % n == 0)


def _heads(t, n):
    return jnp.stack([t[:, HEAD * h:HEAD * (h + 1)] for h in range(n)])


def _unheads(a):
    return jnp.concatenate([a[h] for h in range(a.shape[0])], axis=-1)


def _head_scalars(t, off, n):
    return jnp.stack([jnp.broadcast_to(t[:, off + h:off + h + 1], (CHUNK, HEAD)) for h in range(n)])


def _unhead_scalars(d, off):
    red = jnp.sum(d, axis=-1, keepdims=True)
    lane = lax.broadcasted_iota(jnp.int32, (CHUNK, LANE), 1)
    out = jnp.zeros((CHUNK, LANE), F32)
    for h in range(d.shape[0]):
        out = out + jnp.where(lane == off + h, red[h], 0.0)
    return out


def _chunk_scan_fwd(chunk_fn, srcs, read, params, nh, keeps, host_us, name):
    tp = srcs[0][0].shape[0]
    nc = tp // CHUNK
    nb = _chunks_per_step(nc)
    rows = nb * CHUNK
    ns, npar = len(srcs), len(params)

    def body(*refs):
        s_refs, p_refs = refs[:ns], refs[ns:ns + npar]
        y_ref, st_ref = refs[ns + npar:ns + npar + 2]
        keep_ref = refs[ns + npar + 2] if keeps else None
        state = refs[-1]

        @pl.when(pl.program_id(0) == 0)
        def _():
            state[...] = jnp.zeros_like(state)
        par = [r[...] for r in p_refs]
        st = state[...]
        for k in range(nb):
            sl = slice(k * CHUNK, (k + 1) * CHUNK)
            st_ref[k] = st
            y, st, kept = chunk_fn(*read(s_refs, sl), *par, st)
            y_ref[sl, :] = _unheads(y).astype(BF16)
            if keeps:
                keep_ref[k] = kept
        state[...] = st

    src_spec = lambda s: pl.BlockSpec((rows, s[1]), lambda c: (c, s[2]))
    par_spec = lambda a: pl.BlockSpec(a.shape, lambda c: (0, 0, 0))
    per_chunk = (_sds((nc, nh, HEAD, HEAD), F32), pl.BlockSpec((nb, nh, HEAD, HEAD), lambda c: (c, 0, 0, 0)))
    n_chunk_outs = 2 if keeps else 1
    return _call(
        body, host_us, name=name,
        out_shape=(_sds((tp, nh * HEAD), BF16),) + (per_chunk[0],) * n_chunk_outs,
        grid=(nc // nb,),
        in_specs=[src_spec(s) for s in srcs] + [par_spec(a) for a in params],
        out_specs=[pl.BlockSpec((rows, nh * HEAD), lambda c: (c, 0))] + [per_chunk[1]] * n_chunk_outs,
        scratch_shapes=[pltpu.VMEM((nh, HEAD, HEAD), F32)],
        compiler_params=_params(("arbitrary",)),
    )(*[s[0] for s in srcs], *params)


def _chunk_scan_bwd(chunk_fn, srcs, read, params, states, kept, dsrc, write, out_widths, nh, host_us, name):
    tp = srcs[0][0].shape[0]
    nc = tp // CHUNK
    nb = _chunks_per_step(nc)
    rows = nb * CHUNK
    steps = nc // nb
    ns, npar, nout = len(srcs), len(params), len(out_widths)
    d_arr, d_off = dsrc
    per_chunk = [states] if kept is None else [states, kept]
    nin = ns + npar + len(per_chunk) + 1

    def body(*refs):
        s_refs, p_refs = refs[:ns], refs[ns:ns + npar]
        st_ref, dy_ref = refs[ns + npar], refs[nin - 1]
        kept_ref = None if kept is None else refs[ns + npar + 1]
        o_refs = refs[nin:nin + nout]
        dp_refs = refs[nin + nout:nin + nout + npar]
        dstate = refs[-1]

        @pl.when(pl.program_id(0) == 0)
        def _():
            dstate[...] = jnp.zeros_like(dstate)
            for r in dp_refs:
                r[...] = jnp.zeros_like(r)
        par = [r[...] for r in p_refs]
        dst = dstate[...]
        dpar = None
        for k in reversed(range(nb)):
            sl = slice(k * CHUNK, (k + 1) * CHUNK)
            seqs = read(s_refs, sl)
            saved = None if kept is None else kept_ref[k]
            _, vjp = jax.vjp(lambda *a: chunk_fn(*a, saved=saved)[:2], *seqs, *par, st_ref[k])
            grads = vjp((_heads(dy_ref[sl, d_off:d_off + nh * HEAD], nh), dst))
            for r, tile in zip(o_refs, write(*grads[:len(seqs)])):
                r[sl, :] = tile.astype(r.dtype)
            gp = grads[len(seqs):len(seqs) + npar]
            dpar = gp if dpar is None else [a + b for a, b in zip(dpar, gp)]
            dst = grads[-1]
        for r, gr in zip(dp_refs, dpar):
            r[...] += gr
        dstate[...] = dst

    rev = lambda c: steps - 1 - c
    src_spec = lambda s: pl.BlockSpec((rows, s[1]), lambda c: (rev(c), s[2]))
    par_spec = lambda a: pl.BlockSpec(a.shape, lambda c: (0, 0, 0))
    out_spec = lambda w: pl.BlockSpec((rows, w), lambda c: (rev(c), 0))
    return tuple(_call(
        body, host_us, name=name,
        out_shape=tuple(_sds((tp, w), dt) for w, dt in out_widths) + tuple(_sds(a.shape, F32) for a in params),
        grid=(steps,),
        in_specs=[src_spec(s) for s in srcs] + [par_spec(a) for a in params]
        + [pl.BlockSpec((nb, nh, HEAD, HEAD), lambda c: (rev(c), 0, 0, 0))] * len(per_chunk) + [out_spec(d_arr.shape[1])],
        out_specs=[out_spec(w) for w, _ in out_widths] + [par_spec(a) for a in params],
        scratch_shapes=[pltpu.VMEM((nh, HEAD, HEAD), F32)],
        compiler_params=_params(("arbitrary",)),
    )(*[s[0] for s in srcs], *params, *per_chunk, d_arr))


def _lane_param(p):
    return jnp.broadcast_to(p[:, None, None], (p.shape[0], 1, HEAD))


def _block_diag(w):
    out = jnp.zeros((LRU_WIDTH, LRU_WIDTH), F32)
    for h in range(LRU_HEADS):
        out = out.at[h * HEAD:(h + 1) * HEAD, h * HEAD:(h + 1) * HEAD].set(w[h])
    return out


def _block_diag_inv(w):
    return jnp.stack([w[h * HEAD:(h + 1) * HEAD, h * HEAD:(h + 1) * HEAD] for h in range(LRU_HEADS)])


def _ssd_inputs(xc, proj, lp):
    srcs = ((xc, CONV_W, 0), (proj, SSD_INNER, OFF_SSD_Z // SSD_INNER), (proj, LANE, OFF_SMALL // LANE))
    params = (_lane_param(lp["ssd_dt_bias"]), _lane_param(lp["ssd_a_log"]), _lane_param(lp["ssd_d"]),
              lp["ssd_norm_w"].reshape(SSD_HEADS, 1, HEAD))
    return srcs, params


def _ssd_read(refs, rows):
    xc, z, sm = refs
    return (_heads(xc[rows, 256:768], SSD_HEADS), _heads(xc[rows, 768:896], SSD_GROUPS),
            _heads(xc[rows, 896:1024], SSD_GROUPS), _head_scalars(sm[rows, :], 0, SSD_HEADS), _heads(z[rows, :], SSD_HEADS))


def _ssd_write(dxs, dbm, dcm, ddtr, dz):
    return (jnp.concatenate([_unheads(dxs), _unheads(dbm), _unheads(dcm)], axis=-1), _unheads(dz), _unhead_scalars(ddtr, 0))


def _dn_inputs(xc, proj, lp):
    srcs = ((xc, CONV_W, 0), (proj, 256, OFF_DN_GATE // 256), (proj, LANE, OFF_SMALL // LANE))
    params = (_lane_param(lp["dn_a_log"]), _lane_param(lp["dn_dt_bias"]),
              jnp.broadcast_to(lp["dn_norm_w"][None, None, :], (DN_HEADS, 1, HEAD)))
    return srcs, params


def _dn_read(refs, rows):
    xc, gate, sm = refs
    return (_heads(xc[rows, 1024:1280], DN_HEADS), _heads(xc[rows, 1280:1536], DN_HEADS),
            _heads(xc[rows, 1536:1792], DN_HEADS), _heads(gate[rows, :], DN_HEADS),
            _head_scalars(sm[rows, :], 8, DN_HEADS), _head_scalars(sm[rows, :], 12, DN_HEADS))


def _dn_write(dq, dk, dv, dgate, dbraw, daraw):
    return (jnp.concatenate([_unheads(dq), _unheads(dk), _unheads(dv)], axis=-1), _unheads(dgate),
            _unhead_scalars(dbraw, 8) + _unhead_scalars(daraw, 12))


def _lru_params(lp):
    return (_block_diag(lp["lru_w_a"]), lp["lru_b_a"][None], _block_diag(lp["lru_w_x"]), lp["lru_b_x"][None],
            lp["lru_lambda"][None])


def _conv_params(lp):
    w = jnp.concatenate([lp["lru_conv_w"], lp["ssd_conv_w"], lp["dn_conv_w"]], axis=1)
    b = jnp.concatenate([lp["lru_conv_b"], lp["ssd_conv_b"], jnp.zeros((768,), F32)])[None]
    return w, b


def _mixers_fwd(proj, lp, tag):
    cw, cb = _conv_params(lp)
    xc = _conv_fwd(proj, cw, cb, tag + "_conv")
    o_lru, h_lru = _lru_fwd(xc, proj, *_lru_params(lp), name=tag + "_lru")
    s_srcs, s_par = _ssd_inputs(xc, proj, lp)
    o_ssd, s_states = _chunk_scan_fwd(_ssd_chunk, s_srcs, _ssd_read, s_par, SSD_HEADS, False, 70, tag + "_ssd")
    d_srcs, d_par = _dn_inputs(xc, proj, lp)
    o_dn, d_states, d_inv = _chunk_scan_fwd(_dn_chunk, d_srcs, _dn_read, d_par, DN_HEADS, True, 140, tag + "_dn")
    cat = jnp.concatenate([o_lru, o_ssd, o_dn], axis=1)
    return cat, (xc, h_lru, s_states, d_states, d_inv)


def _mixers_bwd(proj, lp, saved, dcat, tag):
    xc, h_lru, s_states, d_states, d_inv = saved
    tp = proj.shape[0]
    cw, cb = _conv_params(lp)
    g = {}
    du, dyraw, dwa, dba, dwx, dbx, dlam = _lru_bwd(xc, proj, h_lru, dcat, *_lru_params(lp), name=tag + "_lru_bwd")
    g["lru_w_a"], g["lru_b_a"], g["lru_w_x"], g["lru_b_x"], g["lru_lambda"] = (
        _block_diag_inv(dwa), dba[0], _block_diag_inv(dwx), dbx[0], dlam[0])

    s_srcs, s_par = _ssd_inputs(xc, proj, lp)
    dxbc, dz, dsm_ssd, dp_dtb, dp_alog, dp_d, dp_nw = _chunk_scan_bwd(
        _ssd_chunk, s_srcs, _ssd_read, s_par, s_states, None, (dcat, 256), _ssd_write, ((768, F32), (SSD_INNER, BF16), (LANE, F32)), SSD_HEADS, 125,
        tag + "_ssd_bwd")
    g["ssd_dt_bias"], g["ssd_a_log"], g["ssd_d"] = (jnp.sum(p, axis=(1, 2)) for p in (dp_dtb, dp_alog, dp_d))
    g["ssd_norm_w"] = dp_nw.reshape(SSD_INNER)

    d_srcs, d_par = _dn_inputs(xc, proj, lp)
    dqkv, dgate, dsm_dn, dq_alog, dq_dtb, dq_nw = _chunk_scan_bwd(
        _dn_chunk, d_srcs, _dn_read, d_par, d_states, d_inv, (dcat, 768), _dn_write, ((768, F32), (256, BF16), (LANE, F32)), DN_HEADS, 300,
        tag + "_dn_bwd")
    g["dn_a_log"], g["dn_dt_bias"] = (jnp.sum(p, axis=(1, 2)) for p in (dq_alog, dq_dtb))
    g["dn_norm_w"] = jnp.sum(dq_nw, axis=(0, 1))

    dconv, dcw, dcb = _conv_bwd(proj, cw, cb, jnp.concatenate([du, dxbc, dqkv], axis=1), tag + "_conv_bwd")
    g["lru_conv_w"], g["ssd_conv_w"], g["dn_conv_w"] = dcw[:, :256], dcw[:, 256:1024], dcw[:, 1024:]
    g["lru_conv_b"], g["ssd_conv_b"] = dcb[0, :256], dcb[0, 256:1024]
    dproj = jnp.concatenate([dconv, dyraw, dz, dgate, (dsm_ssd + dsm_dn).astype(BF16)], axis=1)
    return dproj, g


MIXER_PARAMS = ("lru_conv_w", "lru_conv_b", "lru_w_a", "lru_b_a", "lru_w_x", "lru_b_x", "lru_lambda",
                "ssd_conv_w", "ssd_conv_b", "ssd_dt_bias", "ssd_a_log", "ssd_d", "ssd_norm_w",
                "dn_conv_w", "dn_a_log", "dn_dt_bias", "dn_norm_w")


UNITS = ("gate0", "up0", "down0", "win", "wout", "gate1", "up1", "down1")


def _layer_shards(w, l):
    out = {"win": w["w_in"][l], "wout": w["w_out"][l]}
    for j in range(2):
        out[f"gate{j}"], out[f"up{j}"], out[f"down{j}"] = w["ffn_w_gate"][l, j], w["ffn_w_up"][l, j], w["ffn_w_down"][l, j]
    return {k: a.astype(BF16) for k, a in out.items()}


def _enqueue_matrix(name, arr, slab):
    if name == "wout" or not slab:
        return [_enqueue(arr, slab)], 1
    return _enqueue_halves(arr, slab, 1)


def _enqueue_layer(shards):
    return {k: _enqueue_matrix(k, shards[k], False) for k in UNITS}


def _gathered_cols(a):
    return a.transpose(1, 0, 2).reshape(a.shape[1], -1)


def _col_slabs(a):
    return a.reshape(a.shape[0], N_DEV, -1).transpose(1, 0, 2).astype(BF16)


def _row_slabs(a):
    return a.reshape(N_DEV, -1, a.shape[1]).astype(BF16)


def _local_step(x, tgt, small, first, shards):
    s = x.shape[0]
    t_real = N_META + s
    tp = -(-t_real // CHUNK) * CHUNK
    depth = len(first) if shards is None else len(shards)
    h = jnp.concatenate([small["meta"], x, jnp.zeros((tp - t_real, D_MODEL), F32)], axis=0)
    hb = h.astype(BF16)
    ln_g, ln_b = small["ln_g"], small["ln_b"]
    saved, weights = [], []
    queued = first
    for l in range(depth):
        lp = {k: small[k][l] for k in MIXER_PARAMS}
        t = f"l{l}"
        if shards is None:
            get = lambda k, g=first[l]: g[k]
        else:
            get = lambda k, g=queued: _collect_halves(g[k])
            if l + 1 < depth:
                queued = _enqueue_layer(shards[l + 1])

        def ffn_weights(j):
            up = _gathered_cols(get(f"up{j}"))
            return _gathered_cols(get(f"gate{j}")), up

        wgu0 = ffn_weights(0)
        g0, u0, a0 = _ffn_up(hb, *wgu0, t + "_ffn0_up")
        wd0 = get("down0").reshape(-1, D_MODEL)
        y1, h1, h1b = _mm_resid_ln(a0, wd0, h, ln_g[l, 0][None], ln_b[l, 0][None], FFN_RES, t + "_ffn0_down")
        win = _proj_cols(_gathered_cols(get("win")))
        proj = _mm_nn(h1b, win, t + "_in_proj")
        cat, mix_saved = _mixers_fwd(proj, lp, t)
        catb = cat.astype(BF16)
        wout = get("wout").reshape(D_MODEL, D_MODEL)
        y2, h2, h2b = _mm_resid_ln(catb, wout, h1, ln_g[l, 1][None], ln_b[l, 1][None], 1.0, t + "_out_proj")
        wgu1 = ffn_weights(1)
        g1, u1, a1 = _ffn_up(h2b, *wgu1, t + "_ffn1_up")
        wd1 = get("down1").reshape(-1, D_MODEL)
        y3, h3, h3b = _mm_resid_ln(a1, wd1, h2, ln_g[l, 2][None], ln_b[l, 2][None], FFN_RES, t + "_ffn1_down")
        weights.append(([wgu0, wgu1], [wd0, wd1], win, wout))
        saved.append((hb, g0, u0, a0, y1, h1b, proj, mix_saved, catb, y2, h2b, g1, u1, a1, y3))
        h, hb = h3, h3b

    tgt_p = jnp.pad(tgt, ((N_META, tp - t_real), (0, 0)))
    lossv, dh = _loss_grad(h, tgt_p, t_real, "loss")
    loss = jnp.sum(lossv)

    gs = {k: [None] * depth for k in MIXER_PARAMS}
    d_ln_g = [[None] * 3 for _ in range(depth)]
    d_ln_b = [[None] * 3 for _ in range(depth)]
    slabs = [{} for _ in range(depth)]
    send = (lambda k, a: a) if shards is None else (lambda k, a: _enqueue_matrix(k, a, True))

    def ffn_bwd(l, j, xb_in, g, u, a, y, dout):
        t = f"l{l}_ffn{j}"
        wgu, wd = weights[l][0][j], weights[l][1][j]
        dres, dyb, dgam, dbet = _ln_bwd(y, dout, ln_g[l, 2 * j][None], FFN_RES, t + "_ln_bwd")
        d_ln_g[l][2 * j], d_ln_b[l][2 * j] = dgam[0], dbet[0]
        slabs[l][f"down{j}"] = send("down", _row_slabs(_mm_tn(a, dyb, t + "_dwd")))
        dg, du = _ffn_dact(dyb, wd, g, u, t + "_dact")
        slabs[l][f"gate{j}"] = send("gate", _col_slabs(_mm_tn(xb_in, dg, t + "_dwg")))
        slabs[l][f"up{j}"] = send("up", _col_slabs(_mm_tn(xb_in, du, t + "_dwu")))
        return _ffn_dx(dg, du, *wgu, dres, t + "_dx")

    for l in reversed(range(depth)):
        hb_in, g0, u0, a0, y1, h1b, proj, mix_saved, catb, y2, h2b, g1, u1, a1, y3 = saved[l]
        lp = {k: small[k][l] for k in MIXER_PARAMS}
        t = f"l{l}"
        _, _, win, wout = weights[l]
        dh2 = ffn_bwd(l, 1, h2b, g1, u1, a1, y3, dh)
        dres, dyb, dgam, dbet = _ln_bwd(y2, dh2, ln_g[l, 1][None], 1.0, t + "_mix_ln_bwd")
        d_ln_g[l][1], d_ln_b[l][1] = dgam[0], dbet[0]
        slabs[l]["wout"] = send("wout", _row_slabs(_mm_tn(catb, dyb, t + "_dwout")))
        dcat = _mm_nt_add(dyb, wout, None, t + "_dcat")
        dproj, mg = _mixers_bwd(proj, lp, mix_saved, dcat, t)
        for k in MIXER_PARAMS:
            gs[k][l] = mg[k]
        dprojb = dproj.astype(BF16)
        slabs[l]["win"] = send("win", _col_slabs(_proj_cols_inv(_mm_tn(h1b, dprojb, t + "_dwin"))))
        dh1 = _mm_nt_add(dprojb, win, dres, t + "_dh1")
        dh = ffn_bwd(l, 0, hb_in, g0, u0, a0, y1, dh1)

    small_grads = {k: jnp.stack(v) for k, v in gs.items()}
    small_grads["ln_g"] = jnp.stack([jnp.stack(r) for r in d_ln_g])
    small_grads["ln_b"] = jnp.stack([jnp.stack(r) for r in d_ln_b])
    small_grads["meta"] = dh[:N_META]
    return loss, dh[N_META:t_real], small_grads, slabs


def _mesh_pos():
    return lax.axis_index("x"), lax.axis_index("y"), lax.axis_index("c")


def _flip(pos, k):
    x, y, c = pos
    return (1 - x if k & 4 else x, 1 - y if k & 2 else y, 1 - c if k & 1 else c)


def _flat(pos):
    return 4 * pos[0] + 2 * pos[1] + pos[2]


def _exchange_copies(ins, outs, slabs, send, recv, loc):
    pos = _mesh_pos()
    me = _flat(pos)
    local, sends, receives = [], [], []
    for a in range(len(ins)):
        local.append(pltpu.make_async_copy(ins[a].at[me] if slabs[a] else ins[a], outs[a].at[me], loc.at[a]))
    for k in range(1, N_DEV):
        peer = _flip(pos, k)
        for a in range(len(ins)):
            if not slabs[a] and k not in GATHER_FIRST:
                continue
            sem = dict(send_sem=send.at[a, k - 1], recv_sem=recv.at[a, k - 1], device_id=peer, device_id_type=MESH_IDS)
            sends.append(pltpu.make_async_remote_copy(
                src_ref=ins[a].at[_flat(peer)] if slabs[a] else ins[a], dst_ref=outs[a].at[me], **sem))
            receives.append(pltpu.make_async_remote_copy(
                src_ref=ins[a].at[me] if slabs[a] else ins[a], dst_ref=outs[a].at[_flat(peer)], **sem))
    return local, sends, receives


GATHER_FIRST = (1, 2, 4, 6)


def _gather_forwards(ins, outs, slabs, send, recv):
    pos = _mesh_pos()
    sibling = _flip(pos, 1)
    forwards, arrivals = [], []
    for a in range(len(ins)):
        if slabs[a]:
            continue
        for k in GATHER_FIRST[1:]:
            sem = dict(send_sem=send.at[a, k], recv_sem=recv.at[a, k], device_id=sibling, device_id_type=MESH_IDS)
            mine, theirs = _flat(_flip(pos, k)), _flat(_flip(pos, k + 1))
            forwards.append(pltpu.make_async_remote_copy(src_ref=outs[a].at[mine], dst_ref=outs[a].at[mine], **sem))
            arrivals.append(pltpu.make_async_remote_copy(src_ref=outs[a].at[theirs], dst_ref=outs[a].at[theirs], **sem))
    return forwards, arrivals


def _exchange_start(ins, outs, slabs, send, recv, loc):
    local, sends, _ = _exchange_copies(ins, outs, slabs, send, recv, loc)
    for cp in local + sends:
        cp.start()


def _exchange_wait(ins, outs, slabs, send, recv, loc):
    local, sends, receives = _exchange_copies(ins, outs, slabs, send, recv, loc)
    forwards, arrivals = _gather_forwards(ins, outs, slabs, send, recv)
    for cp in receives:
        cp.wait_recv()
    for cp in forwards:
        cp.start()
    for cp in arrivals:
        cp.wait_recv()
    for cp in sends + forwards:
        cp.wait_send()
    for cp in local:
        cp.wait()


def _exchange_shapes(arrs, slabs):
    return tuple(_sds(a.shape if s else (N_DEV,) + a.shape, a.dtype) for a, s in zip(arrs, slabs))


def _exchange_sems(n):
    return [pltpu.SemaphoreType.DMA((n, N_DEV - 1)), pltpu.SemaphoreType.DMA((n, N_DEV - 1)),
            pltpu.SemaphoreType.DMA((n,))]


def _exchange(arrs, slabs, name):
    n = len(arrs)

    def body(*refs):
        ins, outs, sems = refs[:n], refs[n:2 * n], refs[2 * n:]
        _exchange_start(ins, outs, slabs, *sems)
        _exchange_wait(ins, outs, slabs, *sems)

    hbm = pl.BlockSpec(memory_space=pl.ANY)
    return pl.pallas_call(
        body, name=name, out_shape=_exchange_shapes(arrs, slabs), in_specs=[hbm] * n, out_specs=[hbm] * n,
        scratch_shapes=_exchange_sems(n),
    )(*arrs)


def _adam_math(w, g, m, v):
    m = ADAM_B1 * m + (1.0 - ADAM_B1) * g
    v = ADAM_B2 * v + (1.0 - ADAM_B2) * (g * g)
    m_hat = m / (1.0 - ADAM_B1 ** ADAM_STEP)
    v_hat = v / (1.0 - ADAM_B2 ** ADAM_STEP)
    delta = -ADAM_LR * (m_hat / (jnp.sqrt(v_hat) + ADAM_EPS) + ADAM_WD * w)
    return delta, m, v


def _adam(w, g, m, v, name):
    r, c = w.shape
    parts = g.ndim == 3
    tr = _tile(r, 512, 8)

    def body(w_ref, g_ref, m_ref, v_ref, go_ref, d_ref, mo_ref, vo_ref):
        if parts:
            gv = g_ref[0].astype(F32)
            for d in range(1, N_DEV):
                gv = gv + g_ref[d].astype(F32)
        else:
            gv = g_ref[...]
        delta, mn, vn = _adam_math(w_ref[...], gv, m_ref[...], v_ref[...])
        go_ref[...] = gv
        d_ref[...] = delta
        mo_ref[...] = mn
        vo_ref[...] = vn

    blk = pl.BlockSpec((tr, c), lambda i: (i, 0))
    gblk = pl.BlockSpec((N_DEV, tr, c), lambda i: (0, i, 0)) if parts else blk
    out = _sds((r, c), F32)
    return pl.pallas_call(
        body, name=name, out_shape=(out,) * 4, grid=(r // tr,),
        in_specs=[blk, gblk, blk, blk], out_specs=[blk] * 4,
        compiler_params=_params(("arbitrary",)),
    )(w, g, m, v)


def _sum_parts(parts, name):
    _, r, c = parts.shape

    def body(p_ref, o_ref):
        acc = p_ref[0]
        for d in range(1, N_DEV):
            acc = acc + p_ref[d]
        o_ref[...] = acc

    return pl.pallas_call(body, name=name, out_shape=_sds((r, c), F32), compiler_params=_params())(parts)


SMALL_SHARD_AXIS = {
    "meta": 1, "ln_g": 2, "ln_b": 2, "lru_conv_w": 2, "lru_conv_b": None, "lru_w_a": None, "lru_b_a": None,
    "lru_w_x": None, "lru_b_x": None, "lru_lambda": None, "ssd_conv_w": 2, "ssd_conv_b": None, "ssd_dt_bias": None,
    "ssd_a_log": None, "ssd_d": None, "ssd_norm_w": None, "dn_conv_w": 2, "dn_a_log": None, "dn_dt_bias": None,
    "dn_norm_w": None,
}
BIG = ("ffn_w_gate", "ffn_w_up", "ffn_w_down", "w_in", "w_out")
WEIGHT_ORDER = ("meta", "ln_g", "ln_b", "ffn_w_gate", "ffn_w_up", "ffn_w_down", "w_in", "lru_conv_w", "lru_conv_b",
                "lru_w_a", "lru_b_a", "lru_w_x", "lru_b_x", "lru_lambda", "ssd_conv_w", "ssd_conv_b", "ssd_dt_bias",
                "ssd_a_log", "ssd_d", "ssd_norm_w", "dn_conv_w", "dn_a_log", "dn_dt_bias", "dn_norm_w", "w_out")


def _pack(arrs):
    flat = jnp.concatenate([a.reshape(-1) for a in arrs])
    rows = -(-flat.shape[0] // (8 * LANE)) * 8
    return jnp.pad(flat, (0, rows * LANE - flat.shape[0])).reshape(rows, LANE)


def _unpack(buf, shapes, lead=()):
    flat = buf.reshape(lead + (-1,))
    out, off = [], 0
    for s in shapes:
        n = math.prod(s)
        out.append(flat[..., off:off + n].reshape(lead + tuple(s)))
        off += n
    return out


def _proj_cols(w):
    pad = jnp.zeros(w.shape[:-1] + (PROJ_W - D_IN,), w.dtype)
    return jnp.concatenate([w[..., 0:256], w[..., 1024:1792], w[..., 1800:2568], w[..., 256:512], w[..., 512:1024],
                            w[..., 2568:2824], w[..., 1792:1800], w[..., 2824:2832], pad], axis=-1)


def _proj_cols_inv(w):
    return jnp.concatenate([w[..., 0:256], w[..., 1792:2048], w[..., 2048:2560], w[..., 256:1024], w[..., 2816:2824],
                            w[..., 1024:1792], w[..., 2560:2816], w[..., 2824:2832]], axis=-1)


def kernel(x, meta, ln_g, ln_b, ffn_w_gate, ffn_w_up, ffn_w_down, w_in, lru_conv_w, lru_conv_b, lru_w_a, lru_b_a, lru_w_x, lru_b_x, lru_lambda, ssd_conv_w, ssd_conv_b, ssd_dt_bias, ssd_a_log, ssd_d, ssd_norm_w, dn_conv_w, dn_a_log, dn_dt_bias, dn_norm_w, w_out, loss_target, m_meta, m_ln_g, m_ln_b, m_ffn_w_gate, m_ffn_w_up, m_ffn_w_down, m_w_in, m_lru_conv_w, m_lru_conv_b, m_lru_w_a, m_lru_b_a, m_lru_w_x, m_lru_b_x, m_lru_lambda, m_ssd_conv_w, m_ssd_conv_b, m_ssd_dt_bias, m_ssd_a_log, m_ssd_d, m_ssd_norm_w, m_dn_conv_w, m_dn_a_log, m_dn_dt_bias, m_dn_norm_w, m_w_out, v_meta, v_ln_g, v_ln_b, v_ffn_w_gate, v_ffn_w_up, v_ffn_w_down, v_w_in, v_lru_conv_w, v_lru_conv_b, v_lru_w_a, v_lru_b_a, v_lru_w_x, v_lru_b_x, v_lru_lambda, v_ssd_conv_w, v_ssd_conv_b, v_ssd_dt_bias, v_ssd_a_log, v_ssd_d, v_ssd_norm_w, v_dn_conv_w, v_dn_a_log, v_dn_dt_bias, v_dn_norm_w, v_w_out):
    w = dict(meta=meta, ln_g=ln_g, ln_b=ln_b, ffn_w_gate=ffn_w_gate, ffn_w_up=ffn_w_up, ffn_w_down=ffn_w_down, w_in=w_in,
             lru_conv_w=lru_conv_w, lru_conv_b=lru_conv_b, lru_w_a=lru_w_a, lru_b_a=lru_b_a, lru_w_x=lru_w_x,
             lru_b_x=lru_b_x, lru_lambda=lru_lambda, ssd_conv_w=ssd_conv_w, ssd_conv_b=ssd_conv_b, ssd_dt_bias=ssd_dt_bias,
             ssd_a_log=ssd_a_log, ssd_d=ssd_d, ssd_norm_w=ssd_norm_w, dn_conv_w=dn_conv_w, dn_a_log=dn_a_log,
             dn_dt_bias=dn_dt_bias, dn_norm_w=dn_norm_w, w_out=w_out)
    m = dict(meta=m_meta, ln_g=m_ln_g, ln_b=m_ln_b, ffn_w_gate=m_ffn_w_gate, ffn_w_up=m_ffn_w_up, ffn_w_down=m_ffn_w_down,
             w_in=m_w_in, lru_conv_w=m_lru_conv_w, lru_conv_b=m_lru_conv_b, lru_w_a=m_lru_w_a, lru_b_a=m_lru_b_a,
             lru_w_x=m_lru_w_x, lru_b_x=m_lru_b_x, lru_lambda=m_lru_lambda, ssd_conv_w=m_ssd_conv_w, ssd_conv_b=m_ssd_conv_b,
             ssd_dt_bias=m_ssd_dt_bias, ssd_a_log=m_ssd_a_log, ssd_d=m_ssd_d, ssd_norm_w=m_ssd_norm_w, dn_conv_w=m_dn_conv_w,
             dn_a_log=m_dn_a_log, dn_dt_bias=m_dn_dt_bias, dn_norm_w=m_dn_norm_w, w_out=m_w_out)
    v = dict(meta=v_meta, ln_g=v_ln_g, ln_b=v_ln_b, ffn_w_gate=v_ffn_w_gate, ffn_w_up=v_ffn_w_up, ffn_w_down=v_ffn_w_down,
             w_in=v_w_in, lru_conv_w=v_lru_conv_w, lru_conv_b=v_lru_conv_b, lru_w_a=v_lru_w_a, lru_b_a=v_lru_b_a,
             lru_w_x=v_lru_w_x, lru_b_x=v_lru_b_x, lru_lambda=v_lru_lambda, ssd_conv_w=v_ssd_conv_w, ssd_conv_b=v_ssd_conv_b,
             ssd_dt_bias=v_ssd_dt_bias, ssd_a_log=v_ssd_a_log, ssd_d=v_ssd_d, ssd_norm_w=v_ssd_norm_w, dn_conv_w=v_dn_conv_w,
             dn_a_log=v_dn_a_log, dn_dt_bias=v_dn_dt_bias, dn_norm_w=v_dn_norm_w, w_out=v_w_out)
    depth = ln_g.shape[0]
    me = _flat(_mesh_pos())
    small_names = tuple(SMALL_SHARD_AXIS)
    sharded = tuple(k for k in small_names if SMALL_SHARD_AXIS[k] is not None)

    del _QUEUE[:]
    _STANDALONE[0] = 0
    shards = [_layer_shards(w, l) for l in range(depth)]
    small_unit = _enqueue(_pack([w[k] for k in sharded]), False)
    first = _enqueue_layer(shards[0])
    g_small = _collect(small_unit)
    small = {k: w[k] for k in small_names if SMALL_SHARD_AXIS[k] is None}
    for k, piece in zip(sharded, _unpack(g_small, [w[k].shape for k in sharded], lead=(N_DEV,))):
        ax = SMALL_SHARD_AXIS[k]
        full = jnp.moveaxis(piece, 0, ax)
        small[k] = full.reshape(full.shape[:ax] + (N_DEV * w[k].shape[ax],) + full.shape[ax + 2:])

    loss, dx, small_grads, slabs = _local_step(x[0], loss_target[0], small, first, shards)

    r_small = _collect(_enqueue(_pack([small_grads[k] for k in small_names]), False))
    got = [{k: _collect_halves(g) for k, g in layer.items()} for layer in slabs]

    outs = {}
    ffn = lambda name: jnp.stack([jnp.stack([got[l][f"{name}{j}"] for j in range(2)], axis=1) for l in range(depth)], axis=1)
    one = lambda name: jnp.stack([got[l][name] for l in range(depth)], axis=1)
    parts_of = {"ffn_w_gate": ffn("gate"), "ffn_w_up": ffn("up"), "ffn_w_down": ffn("down"), "w_in": one("win"),
                "w_out": one("wout")}
    for k in BIG:
        shp = w[k].shape
        two = lambda a: a.reshape(-1, shp[-1])
        res = _adam(two(w[k]), parts_of[k].reshape(N_DEV, -1, shp[-1]), two(m[k]), two(v[k]), "adam_" + k)
        outs[k] = [a.reshape(shp) for a in res]
    g_full = _unpack(_sum_parts(r_small, "sum_small_grads"), [small[k].shape for k in small_names])
    g_loc = {}
    for k, gf in zip(small_names, g_full):
        ax = SMALL_SHARD_AXIS[k]
        g_loc[k] = gf if ax is None else lax.dynamic_slice_in_dim(gf, me * w[k].shape[ax], w[k].shape[ax], axis=ax)
    res = _adam(_pack([w[k] for k in small_names]), _pack([g_loc[k] for k in small_names]),
                _pack([m[k] for k in small_names]), _pack([v[k] for k in small_names]), "adam_small")
    shapes = [w[k].shape for k in small_names]
    for i, k in enumerate(small_names):
        outs[k] = [_unpack(r, shapes)[i] for r in res]

    loss = lax.psum(loss, ("x", "y", "c"))
    return (loss, dx[None], *[outs[k][0] for k in WEIGHT_ORDER], *[outs[k][1] for k in WEIGHT_ORDER],
            *[outs[k][2] for k in WEIGHT_ORDER], *[outs[k][3] for k in WEIGHT_ORDER])
```
